```python
import math
import jax, jax.numpy as jnp
from jax import lax
import numpy as np

D_MODEL = 2048
BATCH = 2
SEQ = 4096
DEPTH = 2
DEC_BATCH = 32
DEC_SEQ = 8
PAST_LEN = 8192
PAGE_SIZE = 128

HEAD_DIM = 128
GLA_HEADS = 4
GLA_DK = 64
GLA_DV = 128
GLA_GATE_RANK = 16
GLA_TAU = 16.0
GLA_CHUNK = 64
NSA_HEADS = 8
NSA_KV_HEADS = 2
NSA_CMP_BLOCK = 32
NSA_CMP_STRIDE = 16
NSA_SEL_BLOCK = 64
NSA_TOP_N = 16
NSA_WINDOW = 512
NSA_FORCE_BONUS = 1.0e4
DIFF_HEADS = 4
DIFF_QK = 64
DIFF_V = 2 * DIFF_QK
MIX_WIDTH = GLA_HEADS * GLA_DV + NSA_HEADS * HEAD_DIM + DIFF_HEADS * DIFF_V
D_FF = ((8 * D_MODEL + 3 * 256 - 1) // (3 * 256)) * 256
IN_SPLITS = (GLA_HEADS * GLA_DK, GLA_HEADS * GLA_DK, GLA_HEADS * GLA_DV, GLA_HEADS * GLA_DV, GLA_GATE_RANK,
             NSA_HEADS * HEAD_DIM, NSA_KV_HEADS * HEAD_DIM, NSA_KV_HEADS * HEAD_DIM, NSA_KV_HEADS * HEAD_DIM,
             NSA_KV_HEADS * HEAD_DIM, NSA_KV_HEADS * HEAD_DIM, NSA_KV_HEADS * HEAD_DIM, NSA_HEADS * 3,
             DIFF_HEADS * 2 * DIFF_QK, DIFF_HEADS * 2 * DIFF_QK, DIFF_HEADS * DIFF_V)
IN_WIDTH = sum(IN_SPLITS)
ROPE_THETA = 10000.0
NORM_EPS = 1e-6
Q_BLOCK = 128
NEG = -1.0e30

kernel_name = "hymba_gla_nsa_diff_sandwich_adaln_step"


def rmsnorm(x, w):
    xf = x.astype(jnp.float32)
    y = xf * lax.rsqrt(jnp.mean(xf * xf, axis=-1, keepdims=True) + NORM_EPS) * w.astype(jnp.float32)
    return y.astype(x.dtype)


def masked_softmax(s, mask):
    s = jnp.where(mask, s.astype(jnp.float32), NEG)
    m = jnp.max(s, axis=-1, keepdims=True)
    e = jnp.where(mask, jnp.exp(s - m), 0.0)
    den = jnp.sum(e, axis=-1, keepdims=True)
    return e / jnp.where(den > 0, den, 1.0)


def rope(x, pos):
    half = x.shape[-1] // 2
    inv = ROPE_THETA ** (-jnp.arange(half, dtype=jnp.float32) / half)
    ang = pos.astype(jnp.float32)[:, None] * inv[None, :]
    cos = jnp.cos(ang)[:, None, :]
    sin = jnp.sin(ang)[:, None, :]
    xf = x.astype(jnp.float32)
    x1, x2 = xf[..., :half], xf[..., half:]
    return jnp.concatenate([x1 * cos - x2 * sin, x2 * cos + x1 * sin], axis=-1).astype(x.dtype)


def query_block(L):
    qb = min(Q_BLOCK, L)
    return qb if L % qb == 0 else L


def gla_chunk_scan(q, k, v, log_a, s0):
    B, L, H, K = q.shape
    C = GLA_CHUNK if L % GLA_CHUNK == 0 else L
    n = L // C

    def chunks(t):
        return t.reshape(B, n, C, H, t.shape[-1]).transpose(1, 0, 3, 2, 4)

    causal = jnp.tril(jnp.ones((C, C), dtype=bool))

    def step(S, inp):
        qc, kc, vc, gc = inp
        b = jnp.cumsum(gc, axis=2)
        rel = jnp.where(causal[None, None, :, :, None], b[:, :, :, None, :] - b[:, :, None, :, :], -jnp.inf)
        attn = jnp.einsum('bhtk,bhsk,bhtsk->bhts', qc, kc, jnp.exp(rel))
        o = jnp.einsum('bhts,bhsv->bhtv', attn, vc) + jnp.einsum('bhtk,bhkv->bhtv', qc * jnp.exp(b), S)
        b_end = b[:, :, -1:, :]
        S = jnp.exp(b_end[:, :, 0, :, None]) * S + jnp.einsum('bhsk,bhsv->bhkv', kc * jnp.exp(b_end - b), vc)
        return S, o

    S, o = lax.scan(step, s0, (chunks(q), chunks(k), chunks(v), chunks(log_a)))
    return o.transpose(1, 0, 3, 2, 4).reshape(B, L, H, v.shape[-1]), S


def gla_mixer(zq, zk, zv, zr, za, s0, gate_w, gate_b, norm_w):
    B, L, _ = zq.shape
    f32 = jnp.float32
    q = zq.reshape(B, L, GLA_HEADS, GLA_DK).astype(f32) * (GLA_DK ** -0.5)
    k = zk.reshape(B, L, GLA_HEADS, GLA_DK).astype(f32)
    v = zv.reshape(B, L, GLA_HEADS, GLA_DV).astype(f32)
    log_a = (jax.nn.log_sigmoid((za @ gate_w + gate_b).astype(f32)) / GLA_TAU).reshape(B, L, GLA_HEADS, GLA_DK)
    o, S = gla_chunk_scan(q, k, v, log_a, s0.astype(f32))
    o = rmsnorm(o, norm_w) * jax.nn.silu(zr.reshape(B, L, GLA_HEADS, GLA_DV).astype(f32))
    return o.reshape(B, L, GLA_HEADS * GLA_DV).astype(zq.dtype), S


def nsa_mixer(q, gates, cmp_full, slc_full, win_all, cmp_pe, cmp_w, qpos0):
    B, L, H, D = q.shape
    T, G = cmp_full.shape[1], cmp_full.shape[3]
    HPG = H // G
    S, LB, SB, W = NSA_CMP_STRIDE, NSA_CMP_BLOCK, NSA_SEL_BLOCK, NSA_WINDOW
    scale = D ** -0.5
    t_seg = -(-T // S)
    seg = jnp.pad(cmp_full, ((0, 0), (0, t_seg * S - T), (0, 0), (0, 0), (0, 0))).reshape(B, t_seg, S, 2, G, D)
    span = LB // S
    n_cmp = t_seg - span + 1
    blocks = jnp.concatenate([seg[:, j:j + n_cmp] for j in range(span)], axis=2)
    pe = cmp_pe.transpose(1, 0, 2)[:, :, None, :]
    comp = jnp.einsum('bnjcgd,cjde->bncge', blocks + pe, cmp_w.reshape(2, LB, D, D))
    ck, cv = comp[:, :, 0], comp[:, :, 1]
    cmp_end = jnp.arange(n_cmp) * S + LB - 1
    n_slc = -(-T // SB)
    sel = jnp.pad(slc_full, ((0, 0), (0, n_slc * SB - T), (0, 0), (0, 0), (0, 0)))
    sel = sel.reshape(B, n_slc, SB, 2, G, D).transpose(0, 4, 1, 2, 3, 5)
    n_sel = min(NSA_TOP_N, n_slc)
    ratio = SB // S
    n_off = ratio + span - 1
    front = span - 1
    back = ratio * n_slc + n_off - 1 - front - n_cmp
    blk_ids = jnp.arange(n_slc)
    QB = query_block(L)
    n_qb = L // QB
    qs = q.reshape(B, n_qb, QB, G, HPG, D).transpose(1, 0, 2, 3, 4, 5)
    gs = gates.reshape(B, n_qb, QB, G, HPG, 3).transpose(1, 0, 2, 3, 4, 5)

    def per_block(args):
        i, qb, gb = args
        qpos = qpos0 + i * QB + jnp.arange(QB)
        p_c = masked_softmax(jnp.einsum('bqgnd,bkgd->bgnqk', qb, ck) * scale, cmp_end[None, :] <= qpos[:, None])
        o_c = jnp.einsum('bgnqk,bkgd->bqgnd', p_c, cv)
        imp = jnp.pad(p_c.sum(axis=2), ((0, 0), (0, 0), (0, 0), (front, back)))
        p_slc = sum(imp[..., o:o + ratio * n_slc].reshape(B, G, QB, n_slc, ratio)[..., 0] for o in range(n_off))
        cur = qpos // SB
        valid = blk_ids[None, :] * SB <= qpos[:, None]
        forced = (blk_ids[None, :] == 0) | (blk_ids[None, :] == cur[:, None]) | (blk_ids[None, :] == cur[:, None] - 1)
        score = jnp.where(valid, p_slc + jnp.where(forced, NSA_FORCE_BONUS, 0.0), NEG)
        top_s, idx = lax.top_k(score, n_sel)
        picked = jax.vmap(jax.vmap(lambda blocks_g, ii: blocks_g[ii]))(sel, idx)
        kpos = idx[..., None] * SB + jnp.arange(SB)
        mask_s = (top_s > 0.5 * NEG)[..., None] & (kpos <= qpos[None, None, :, None, None])
        s_s = jnp.einsum('bqgnd,bgqksd->bgnqks', qb, picked[..., 0, :]) * scale
        p_s = masked_softmax(s_s.reshape(B, G, HPG, QB, n_sel * SB), mask_s.reshape(B, G, 1, QB, n_sel * SB))
        o_s = jnp.einsum('bgnqm,bgqmd->bqgnd', p_s, picked[..., 1, :].reshape(B, G, QB, n_sel * SB, D))
        kw = lax.dynamic_slice_in_dim(win_all, i * QB, QB + W, axis=1)
        wpos = qpos0 - W + i * QB + jnp.arange(QB + W)
        mask_w = (wpos[None, :] <= qpos[:, None]) & (wpos[None, :] > qpos[:, None] - W) & (wpos[None, :] >= 0)
        p_w = masked_softmax(jnp.einsum('bqgnd,bkgd->bgnqk', qb, kw[:, :, 0]) * scale, mask_w)
        o_w = jnp.einsum('bgnqk,bkgd->bqgnd', p_w, kw[:, :, 1])
        o = gb[..., 0:1] * o_c + gb[..., 1:2] * o_s + gb[..., 2:3] * o_w
        return o.reshape(B, QB, H * D).astype(q.dtype)

    out = lax.map(per_block, (jnp.arange(n_qb), qs, gs))
    return out.transpose(1, 0, 2, 3).reshape(B, L, H * D)


def diff_attention(q1, q2, k1, k2, v, lam, qpos0):
    B, L, H, dq = q1.shape
    T = k1.shape[1]
    scale = dq ** -0.5
    QB = query_block(L)
    n = L // QB
    kpos = jnp.arange(T)

    def per_block(args):
        i, a1, a2 = args
        qpos = qpos0 + i * QB + jnp.arange(QB)
        mask = kpos[None, :] <= qpos[:, None]
        p1 = masked_softmax(jnp.einsum('bqhd,bkhd->bhqk', a1, k1) * scale, mask)
        p2 = masked_softmax(jnp.einsum('bqhd,bkhd->bhqk', a2, k2) * scale, mask)
        return jnp.einsum('bhqk,bkhd->bqhd', p1 - lam * p2, v).astype(v.dtype)

    q1b = q1.reshape(B, n, QB, H, dq).transpose(1, 0, 2, 3, 4)
    q2b = q2.reshape(B, n, QB, H, dq).transpose(1, 0, 2, 3, 4)
    out = lax.map(per_block, (jnp.arange(n), q1b, q2b))
    return out.transpose(1, 0, 2, 3, 4).reshape(B, L, H, v.shape[-1])


def trunk_layer(x, c, qpos0, past_cmp, past_slc, past_diff, win_buf, gla_s0, lp, lam_init):
    B, L, _ = x.shape
    dt = x.dtype
    pos = qpos0 + jnp.arange(L)
    mod = jax.nn.silu(c) @ lp['ada_w'] + lp['ada_b']
    sh1, sc1, g1, sh2, sc2, g2 = jnp.split(mod[:, None, :].astype(dt), 6, axis=-1)
    h = rmsnorm(x, lp['norm'][0]) * (1 + sc1) + sh1
    z = h @ lp['w_in']
    split_at = [int(s) for s in np.cumsum(IN_SPLITS)[:-1]]
    (gq, gk, gv, gr, ga, nq, nck, ncv, nsk, nsv, nwk, nwv, ngate, dq, dk, dv) = jnp.split(z, split_at, axis=-1)
    o_gla, gla_state = gla_mixer(gq, gk, gv, gr, ga, gla_s0, lp['gla_gate_w'], lp['gla_gate_b'], lp['gla_norm'])
    kvs = (B, L, NSA_KV_HEADS, HEAD_DIM)
    q_n = rope(nq.reshape(B, L, NSA_HEADS, HEAD_DIM), pos)
    cmp_new = jnp.stack([rope(nck.reshape(kvs), pos), ncv.reshape(kvs)], axis=2)
    slc_new = jnp.stack([rope(nsk.reshape(kvs), pos), nsv.reshape(kvs)], axis=2)
    win_new = jnp.stack([rope(nwk.reshape(kvs), pos), nwv.reshape(kvs)], axis=2)
    cmp_full = jnp.concatenate([past_cmp.astype(dt), cmp_new], axis=1)
    slc_full = jnp.concatenate([past_slc.astype(dt), slc_new], axis=1)
    wb = win_buf.shape[1]
    win_buf = win_buf.astype(dt)
    win_all = jnp.concatenate([jnp.zeros((B, NSA_WINDOW - wb) + win_buf.shape[2:], dt), win_buf, win_new], axis=1)
    win_state = jnp.concatenate([win_buf, win_new], axis=1)[:, -wb:]
    gates = jax.nn.sigmoid(ngate.astype(jnp.float32)).reshape(B, L, NSA_HEADS, 3)
    o_nsa = nsa_mixer(q_n, gates, cmp_full, slc_full, win_all, lp['nsa_cmp_pe'], lp['nsa_cmp_w'], qpos0)
    dqr = dq.reshape(B, L, DIFF_HEADS, 2 * DIFF_QK)
    dkr = dk.reshape(B, L, DIFF_HEADS, 2 * DIFF_QK)
    q1, q2 = rope(dqr[..., :DIFF_QK], pos), rope(dqr[..., DIFF_QK:], pos)
    k_new = jnp.concatenate([rope(dkr[..., :DIFF_QK], pos), rope(dkr[..., DIFF_QK:], pos)], axis=-1)
    diff_new = jnp.stack([k_new, dv.reshape(B, L, DIFF_HEADS, DIFF_V)], axis=2)
    diff_full = jnp.concatenate([past_diff.astype(dt), diff_new], axis=1)
    lam_p = lp['diff_lambda'].astype(jnp.float32)
    lam = jnp.exp(jnp.sum(lam_p[0] * lam_p[1])) - jnp.exp(jnp.sum(lam_p[2] * lam_p[3])) + lam_init
    o_d = diff_attention(q1, q2, diff_full[:, :, 0, :, :DIFF_QK], diff_full[:, :, 0, :, DIFF_QK:], diff_full[:, :, 1], lam, qpos0)
    o_d = (rmsnorm(o_d, lp['diff_norm']) * (1.0 - lam_init)).reshape(B, L, DIFF_HEADS * DIFF_V)
    mix = jnp.concatenate([o_gla, o_nsa, o_d.astype(dt)], axis=-1) @ lp['w_out']
    x = x + g1 * rmsnorm(mix, lp['norm'][1])
    h2 = rmsnorm(x, lp['norm'][2]) * (1 + sc2) + sh2
    gate, up = jnp.split(h2 @ lp['ffn_w_in'], 2, axis=-1)
    f = (jax.nn.silu(gate) * up) @ lp['ffn_w_out']
    x = x + g2 * rmsnorm(f, lp['norm'][3])
    return x, cmp_new, slc_new, diff_new, win_state, gla_state


def setup_inputs(seed: int = 0) -> dict:
    key = jax.random.key(seed)
    keys = iter(jax.random.split(key, 32))

    def nrm(shape, scale=1.0):
        a = jax.random.normal(next(keys), shape, jnp.float32)
        return a if scale == 1.0 else a * scale

    n_pages = PAST_LEN // PAGE_SIZE
    n_used = DEC_BATCH * n_pages
    n_pool = n_used + n_used // 4
    win_rows = min(NSA_WINDOW, PAST_LEN)
    page_table = jax.random.permutation(next(keys), n_pool)[:n_used].reshape(DEC_BATCH, n_pages).astype(jnp.int32)
    return {
        'x_prompt': nrm((BATCH, SEQ, D_MODEL)),
        'x_sample': nrm((DEC_BATCH, DEC_SEQ, D_MODEL)),
        'c_prompt': nrm((BATCH, D_MODEL)),
        'c_sample': nrm((DEC_BATCH, D_MODEL)),
        'cache_nsa_cmp_kv': nrm((DEPTH, n_pool, PAGE_SIZE, 2, NSA_KV_HEADS, HEAD_DIM)),
        'cache_nsa_slc_kv': nrm((DEPTH, n_pool, PAGE_SIZE, 2, NSA_KV_HEADS, HEAD_DIM)),
        'cache_diff_kv': nrm((DEPTH, n_pool, PAGE_SIZE, 2, DIFF_HEADS, DIFF_V)),
        'state_nsa_win_kv': nrm((DEPTH, DEC_BATCH, win_rows, 2, NSA_KV_HEADS, HEAD_DIM)),
        'state_gla': nrm((DEPTH, DEC_BATCH, GLA_HEADS, GLA_DK, GLA_DV), 0.5),
        'page_table': page_table,
        'ada_w': nrm((DEPTH, D_MODEL, 6 * D_MODEL), 0.3 * D_MODEL ** -0.5),
        'ada_b': nrm((DEPTH, 6 * D_MODEL), 0.01),
        'norm_w': 1.0 + nrm((DEPTH, 4, D_MODEL), 0.05),
        'w_in': nrm((DEPTH, D_MODEL, IN_WIDTH), D_MODEL ** -0.5),
        'gla_gate_w': nrm((DEPTH, GLA_GATE_RANK, GLA_HEADS * GLA_DK), GLA_GATE_RANK ** -0.5),
        'gla_gate_b': nrm((DEPTH, GLA_HEADS * GLA_DK), 0.1),
        'gla_norm': 1.0 + nrm((DEPTH, GLA_DV), 0.05),
        'nsa_cmp_pe': nrm((DEPTH, 2, NSA_CMP_BLOCK, HEAD_DIM), 0.1),
        'nsa_cmp_w': nrm((DEPTH, 2, NSA_CMP_BLOCK * HEAD_DIM, HEAD_DIM), (NSA_CMP_BLOCK * HEAD_DIM) ** -0.5),
        'diff_lambda': nrm((DEPTH, 4, DIFF_QK), 0.1),
        'diff_norm': 1.0 + nrm((DEPTH, DIFF_V), 0.05),
        'w_out': nrm((DEPTH, MIX_WIDTH, D_MODEL), MIX_WIDTH ** -0.5),
        'ffn_w_in': nrm((DEPTH, D_MODEL, 2 * D_FF), D_MODEL ** -0.5),
        'ffn_w_out': nrm((DEPTH, D_FF, D_MODEL), D_FF ** -0.5),
    }


def reference(x_prompt, x_sample, c_prompt, c_sample, cache_nsa_cmp_kv, cache_nsa_slc_kv, cache_diff_kv,
              state_nsa_win_kv, state_gla, page_table, ada_w, ada_b, norm_w, w_in, gla_gate_w, gla_gate_b,
              gla_norm, nsa_cmp_pe, nsa_cmp_w, diff_lambda, diff_norm, w_out, ffn_w_in, ffn_w_out):
    B = x_prompt.shape[0]
    DB, n_pages = page_table.shape
    past = n_pages * PAGE_SIZE
    wb = state_nsa_win_kv.shape[2]
    dt = x_prompt.dtype

    def gather_pages(pool):
        return pool[page_table].reshape((DB, past) + pool.shape[2:])

    empty_nsa = jnp.zeros((B, 0, 2, NSA_KV_HEADS, HEAD_DIM), dt)
    empty_diff = jnp.zeros((B, 0, 2, DIFF_HEADS, DIFF_V), dt)
    win0 = jnp.zeros((B, wb, 2, NSA_KV_HEADS, HEAD_DIM), dt)
    gla0 = jnp.zeros((B, GLA_HEADS, GLA_DK, GLA_DV), jnp.float32)
    xp, xs = x_prompt, x_sample
    cmp_p, cmp_s, slc_p, slc_s, diff_p, diff_s, win_p, win_s, gla_p, gla_s = ([] for _ in range(10))
    for l in range(DEPTH):
        lp = {'ada_w': ada_w[l], 'ada_b': ada_b[l], 'norm': norm_w[l], 'w_in': w_in[l],
              'gla_gate_w': gla_gate_w[l], 'gla_gate_b': gla_gate_b[l], 'gla_norm': gla_norm[l],
              'nsa_cmp_pe': nsa_cmp_pe[l], 'nsa_cmp_w': nsa_cmp_w[l], 'diff_lambda': diff_lambda[l],
              'diff_norm': diff_norm[l], 'w_out': w_out[l], 'ffn_w_in': ffn_w_in[l], 'ffn_w_out': ffn_w_out[l]}
        lam_init = 0.8 - 0.6 * math.exp(-0.3 * l)
        xp, a1, a2, a3, a4, a5 = trunk_layer(xp, c_prompt, 0, empty_nsa, empty_nsa, empty_diff, win0, gla0, lp, lam_init)
        xs, b1, b2, b3, b4, b5 = trunk_layer(xs, c_sample, past, gather_pages(cache_nsa_cmp_kv[l]),
                                             gather_pages(cache_nsa_slc_kv[l]), gather_pages(cache_diff_kv[l]),
                                             state_nsa_win_kv[l], state_gla[l], lp, lam_init)
        cmp_p.append(a1); slc_p.append(a2); diff_p.append(a3); win_p.append(a4); gla_p.append(a5)
        cmp_s.append(b1); slc_s.append(b2); diff_s.append(b3); win_s.append(b4); gla_s.append(b5)
    return (xp, xs, jnp.stack(cmp_p), jnp.stack(cmp_s), jnp.stack(slc_p), jnp.stack(slc_s),
            jnp.stack(diff_p), jnp.stack(diff_s), jnp.stack(win_p), jnp.stack(win_s),
            jnp.stack(gla_p), jnp.stack(gla_s))
```

```python
import functools
import math

import numpy as np
import jax
import jax.numpy as jnp
from jax import lax
from jax.experimental import pallas as pl
from jax.experimental.pallas import tpu as pltpu

F32 = jnp.float32
BF16 = jnp.bfloat16

PAGE_SIZE = 128
HEAD_DIM = 128
GLA_HEADS, GLA_DK, GLA_DV, GLA_GATE_RANK, GLA_TAU = 4, 64, 128, 16, 16.0
NSA_HEADS, NSA_KV_HEADS = 8, 2
NSA_CMP_BLOCK, NSA_CMP_STRIDE, NSA_SEL_BLOCK, NSA_TOP_N, NSA_WINDOW = 32, 16, 64, 16, 512
NSA_FORCE_BONUS = 1.0e4
DIFF_HEADS, DIFF_QK, DIFF_V = 4, 64, 128
ROPE_THETA = 10000.0
NORM_EPS = 1e-6
NEG = -1.0e30

LANES = 128
SUBLANES = 8
BF16_ROWS = 16
VMEM_LIMIT_BYTES = 56 * 1024 * 1024

Z_NQ, Z_GQ, Z_GK, Z_GV, Z_GR = 0, 1024, 1280, 1536, 2048
Z_CMP, Z_SLC, Z_WIN, Z_DQ, Z_DK, Z_DV, Z_SMALL = 2560, 3072, 3584, 4096, 4608, 5120, 5632
Z_WIDTH = 5760
SMALL_GA, SMALL_GATE = 0, 16


def _cparams(sem):
    return pltpu.CompilerParams(dimension_semantics=sem, vmem_limit_bytes=VMEM_LIMIT_BYTES)


def _dot(a, b):
    return jnp.dot(a, b, preferred_element_type=F32)


def _dot_nt(a, b):
    return lax.dot_general(a, b, (((1,), (1,)), ((), ())), preferred_element_type=F32)


def _dot_tn(a, b):
    return lax.dot_general(a, b, (((0,), (0,)), ((), ())), preferred_element_type=F32)


def _split3(a):
    hi = a.astype(BF16)
    r = a - hi.astype(F32)
    mid = r.astype(BF16)
    lo = (r - mid.astype(F32)).astype(BF16)
    return hi, mid, lo


def _dot3_rhs_exact(a, b):
    hi, mid, lo = _split3(a)
    return _dot(hi, b) + _dot(mid, b) + _dot(lo, b)


def _dot3_lhs_exact(a, b):
    hi, mid, lo = _split3(b)
    return _dot(a, hi) + _dot(a, mid) + _dot(a, lo)


def _sigmoid(x):
    return 1.0 / (1.0 + jnp.exp(-x))


def _masked_softmax(s, mask):
    s = jnp.where(mask, s, NEG)
    m = jnp.max(s, axis=-1, keepdims=True)
    e = jnp.where(mask, jnp.exp(s - m), 0.0)
    den = jnp.sum(e, axis=-1, keepdims=True)
    return e / jnp.where(den > 0, den, 1.0)


def _mod_kernel(c_ref, w_ref, b_ref, o_ref):
    c = c_ref[...]
    a = (c * _sigmoid(c)).astype(BF16)
    o_ref[0] = _dot(a, w_ref[0].astype(BF16)) + b_ref[0]


def _modulation(c_all, ada_w, ada_b):
    depth, d, n6 = ada_w.shape
    rows = c_all.shape[0]
    tn = max(t for t in range(LANES, min(1024, n6) + 1, LANES) if n6 % t == 0)
    return pl.pallas_call(
        _mod_kernel,
        out_shape=jax.ShapeDtypeStruct((depth, rows, n6), F32),
        grid=(depth, n6 // tn),
        in_specs=[pl.BlockSpec((rows, d), lambda l, j: (0, 0)),
                  pl.BlockSpec((1, d, tn), lambda l, j: (l, 0, j)),
                  pl.BlockSpec((1, 1, tn), lambda l, j: (l, 0, j))],
        out_specs=pl.BlockSpec((1, rows, tn), lambda l, j: (l, 0, j)),
        compiler_params=_cparams(("arbitrary", "arbitrary")),
        name="adaln_modulation",
    )(c_all, ada_w, ada_b.reshape(depth, 1, n6))


def _norm_mod(x, nw, sc, sh):
    ms = jnp.mean(x * x, axis=-1, keepdims=True)
    return x * lax.rsqrt(ms + NORM_EPS) * nw * (1.0 + sc) + sh


def _in_proj_kernel(x_ref, nw_ref, sc_ref, sh_ref, w_ref, o_ref, h_ref):
    @pl.when(pl.program_id(1) == 0)
    def _():
        h_ref[...] = _norm_mod(x_ref[...], nw_ref[...], sc_ref[0], sh_ref[0]).astype(BF16)

    o_ref[...] = _dot(h_ref[...], w_ref[...])


def _mod_spec(seq_len, tm, d):
    if seq_len % tm == 0:
        per = seq_len // tm
        return pl.BlockSpec((1, 1, d), lambda i, *_: (i // per, 0, 0))
    return pl.BlockSpec((1, tm, d), lambda i, *_: (0, i, 0))


def _expand_mod(m, seq_len, tm):
    if seq_len % tm == 0:
        return m[:, None, :]
    return jnp.repeat(m, seq_len, axis=0)[None]


def _in_proj(x2, nw, sc, sh, w, seq_len, tm, tn):
    m, d = x2.shape
    n = w.shape[1]
    return pl.pallas_call(
        _in_proj_kernel,
        out_shape=jax.ShapeDtypeStruct((m, n), F32),
        grid=(m // tm, n // tn),
        in_specs=[pl.BlockSpec((tm, d), lambda i, j: (i, 0)),
                  pl.BlockSpec((1, d), lambda i, j: (0, 0)),
                  _mod_spec(seq_len, tm, d), _mod_spec(seq_len, tm, d),
                  pl.BlockSpec((d, tn), lambda i, j: (0, j))],
        out_specs=pl.BlockSpec((tm, tn), lambda i, j: (i, j)),
        scratch_shapes=[pltpu.VMEM((tm, d), BF16)],
        compiler_params=_cparams(("arbitrary", "arbitrary")),
        name="in_proj",
    )(x2, nw, _expand_mod(sc, seq_len, tm), _expand_mod(sh, seq_len, tm), w)


def _rope_tables(pos):
    pos = pos.astype(F32)[:, None]

    def tab(half, reps):
        inv = ROPE_THETA ** (-jnp.arange(half, dtype=F32) / half)
        ang = pos * inv[None, :]
        c, s = jnp.cos(ang), jnp.sin(ang)
        return jnp.tile(jnp.concatenate([c, c], -1), (1, reps)), jnp.tile(jnp.concatenate([-s, s], -1), (1, reps))

    c128, s128 = tab(HEAD_DIM // 2, 1)
    c64, s64 = tab(DIFF_QK // 2, 2)
    return jnp.concatenate([c128, s128, c64, s64], axis=-1)


def _rope128(x, cos, sin):
    return x * cos + pltpu.roll(x, 64, 1) * sin


def _rope64(x, cos, sin, first_half):
    partner = jnp.where(first_half, pltpu.roll(x, 96, 1), pltpu.roll(x, 32, 1))
    return x * cos + partner * sin


def _rope_kernel(nq_ref, cmp_ref, slc_ref, win_ref, dq_ref, dk_ref, dv_ref, tab_ref,
                 qn_o, cmp_o, cmpb_o, slc_o, slcb_o, win_o, winb_o, dq_o, dkv_o, dkvb_o):
    tab = tab_ref[...]
    c128, s128, c64, s64 = (tab[:, i * LANES:(i + 1) * LANES] for i in range(4))
    lane = lax.broadcasted_iota(jnp.int32, (1, LANES), 1)
    first_half = (lane % DIFF_QK) < (DIFF_QK // 2)
    nsa_scale = HEAD_DIM ** -0.5
    diff_scale = DIFF_QK ** -0.5

    for h in range(NSA_HEADS):
        sl = slice(h * LANES, (h + 1) * LANES)
        qn_o[:, sl] = (_rope128(nq_ref[:, sl], c128, s128) * nsa_scale).astype(qn_o.dtype)

    for src, dst, dstb in ((cmp_ref, cmp_o, cmpb_o), (slc_ref, slc_o, slcb_o), (win_ref, win_o, winb_o)):
        for g in range(2 * NSA_KV_HEADS):
            sl = slice(g * LANES, (g + 1) * LANES)
            v = src[:, sl]
            if g < NSA_KV_HEADS:
                v = _rope128(v, c128, s128)
            dst[:, sl] = v
            dstb[:, sl] = v.astype(BF16)

    for h in range(DIFF_HEADS):
        sl = slice(h * LANES, (h + 1) * LANES)
        dq_o[:, sl] = (_rope64(dq_ref[:, sl], c64, s64, first_half) * diff_scale).astype(dq_o.dtype)
        k = _rope64(dk_ref[:, sl], c64, s64, first_half)
        dkv_o[:, sl] = k
        dkvb_o[:, sl] = k.astype(BF16)
        sv = slice((DIFF_HEADS + h) * LANES, (DIFF_HEADS + h + 1) * LANES)
        v = dv_ref[:, sl]
        dkv_o[:, sv] = v
        dkvb_o[:, sv] = v.astype(BF16)


def _rope_split(z, tab, seq_len, tm, act_dt):
    m = z.shape[0]
    per = max(seq_len // tm, 1)
    tab_spec = (pl.BlockSpec((tm, 4 * LANES), lambda i: (i % per, 0)) if seq_len % tm == 0
                else pl.BlockSpec((tm, 4 * LANES), lambda i: (i, 0)))

    def zs(width, col):
        return pl.BlockSpec((tm, width), lambda i: (i, col // width))

    def os(width):
        return pl.BlockSpec((tm, width), lambda i: (i, 0))

    outs = [(1024, act_dt), (512, F32), (512, BF16), (512, F32), (512, BF16), (512, F32), (512, BF16),
            (512, act_dt), (1024, F32), (1024, BF16)]
    return pl.pallas_call(
        _rope_kernel,
        out_shape=[jax.ShapeDtypeStruct((m, w), dt) for w, dt in outs],
        grid=(m // tm,),
        in_specs=[zs(1024, Z_NQ), zs(512, Z_CMP), zs(512, Z_SLC), zs(512, Z_WIN),
                  zs(512, Z_DQ), zs(512, Z_DK), zs(512, Z_DV), tab_spec],
        out_specs=[os(w) for w, _ in outs],
        compiler_params=_cparams(("arbitrary",)),
        name="rope_split",
    )(z, z, z, z, z, z, z, tab)


def _gather_kernel(pages_per_step, pt_ref, *refs):
    page_refs, new_ref, o_ref = refs[:pages_per_step], refs[pages_per_step], refs[pages_per_step + 1]
    j = pl.program_id(1)
    last = pl.num_programs(1) - 1

    @pl.when(j < last)
    def _():
        for p in range(pages_per_step):
            o_ref[0, p * PAGE_SIZE:(p + 1) * PAGE_SIZE, :] = page_refs[p][0].astype(BF16)

    @pl.when(j == last)
    def _():
        new = new_ref[0]
        pad = jnp.zeros((o_ref.shape[1] - new.shape[0], new.shape[1]), new.dtype)
        o_ref[0] = jnp.concatenate([new, pad], axis=0).astype(BF16)


def _gather_pages(pool, page_table, new_rows, pages_per_step):
    b, n_pages = page_table.shape
    w = pool.shape[-1]
    steps = n_pages // pages_per_step
    rows = pages_per_step * PAGE_SIZE
    n_new = new_rows.shape[1]

    def page_spec(p):
        return pl.BlockSpec((1, PAGE_SIZE, w),
                            lambda bi, j, pt: (pt[bi, jnp.minimum(j, steps - 1) * pages_per_step + p], 0, 0))

    return pl.pallas_call(
        functools.partial(_gather_kernel, pages_per_step),
        out_shape=jax.ShapeDtypeStruct((b, (steps + 1) * rows, w), BF16),
        grid_spec=pltpu.PrefetchScalarGridSpec(
            num_scalar_prefetch=1,
            grid=(b, steps + 1),
            in_specs=[page_spec(p) for p in range(pages_per_step)]
                     + [pl.BlockSpec((1, n_new, w), lambda bi, j, pt: (bi, 0, 0))],
            out_specs=pl.BlockSpec((1, rows, w), lambda bi, j, pt: (bi, j, 0))),
        compiler_params=_cparams(("arbitrary", "arbitrary")),
        name="gather_pages",
    )(page_table, *([pool] * pages_per_step), new_rows)


def _gla_constants(c):
    t = np.arange(c)
    sizes = []
    s = c // 2
    while s >= 1:
        sizes.append(s)
        s //= 2
    sel, masks = [], []
    for sz in sizes:
        ref_row = (t // (2 * sz)) * (2 * sz) + sz - 1
        sel.append((t[None, :] <= ref_row[:, None]).astype(np.float32))
        same = (t[:, None] // (2 * sz)) == (t[None, :] // (2 * sz))
        masks.append((same & ((t[:, None] // sz) % 2 == 1) & ((t[None, :] // sz) % 2 == 0)).astype(np.float32))
    sel.append((t[None, :] <= t[:, None]).astype(np.float32))
    masks.append(np.eye(c, dtype=np.float32))
    return np.concatenate(sel, 0), np.stack(masks, 0)


def _gla_kernel(chunk, levels, q_ref, k_ref, v_ref, r_ref, sm_ref, gw_ref, gb_ref, nw_ref, s0_ref,
                sel_ref, mask_ref, o_ref, st_ref, state_ref):
    c = chunk
    c_in = q_ref.shape[0]
    hk = GLA_HEADS * GLA_DK
    ci = pl.program_id(1)

    @pl.when(ci == 0)
    def _():
        state_ref[...] = s0_ref[0].T

    def rows(x):
        if c_in == c:
            return x
        return jnp.concatenate([x, jnp.zeros((c - c_in, x.shape[1]), x.dtype)], axis=0)

    ga = rows(sm_ref[:, SMALL_GA:SMALL_GA + GLA_GATE_RANK])
    pre = _dot3_lhs_exact_both(ga, gw_ref[...]) + gb_ref[...]
    log_a = (jnp.minimum(pre, 0.0) - jnp.log(1.0 + jnp.exp(-jnp.abs(pre)))) / GLA_TAU
    if c_in != c:
        log_a = jnp.where(lax.broadcasted_iota(jnp.int32, (c, 1), 0) < c_in, log_a, 0.0)
    refs = _dot3_lhs_exact(sel_ref[...], log_a)
    b = refs[(levels - 1) * c:levels * c]
    q = rows(q_ref[...]) * (GLA_DK ** -0.5)
    k = rows(k_ref[...])
    v_all = rows(v_ref[...])
    lane = lax.broadcasted_iota(jnp.int32, (1, hk), 1)
    head_of_lane = lane // GLA_DK

    def stack_heads(x):
        return jnp.concatenate([jnp.where(head_of_lane == h, x, 0.0) for h in range(GLA_HEADS)], axis=0).astype(BF16)

    attn = jnp.zeros((GLA_HEADS, c, c), F32)
    for lv in range(levels):
        r = refs[lv * c:(lv + 1) * c]
        qd = stack_heads(q * jnp.exp(jnp.minimum(b - r, 0.0)))
        kd = (k * jnp.exp(jnp.minimum(r - b, 0.0))).astype(BF16)
        attn = attn + _dot_nt(qd, kd).reshape(GLA_HEADS, c, c) * mask_ref[lv][None]
    state = state_ref[...]
    inter = _dot_nt(stack_heads(q * jnp.exp(b)), state.astype(BF16))

    nw = nw_ref[...]
    for h in range(GLA_HEADS):
        vh = v_all[:, h * GLA_DV:(h + 1) * GLA_DV]
        o = (_dot(attn[h].astype(BF16), vh.astype(BF16)) + inter[h * c:(h + 1) * c])[0:c_in]
        y = o * lax.rsqrt(jnp.mean(o * o, axis=-1, keepdims=True) + NORM_EPS) * nw
        rh = r_ref[:, h * GLA_DV:(h + 1) * GLA_DV]
        o_ref[:, h * GLA_DV:(h + 1) * GLA_DV] = (y * (rh * _sigmoid(rh))).astype(o_ref.dtype)

    b_end = b[c - 1:c]
    kd = (k * jnp.exp(b_end - b)).astype(BF16)
    new_state = state * jnp.exp(b_end)
    for h in range(GLA_HEADS):
        vh = v_all[:, h * GLA_DV:(h + 1) * GLA_DV].astype(BF16)
        new_state = new_state + jnp.where(head_of_lane == h, _dot_tn(vh, kd), 0.0)
    state_ref[...] = new_state

    @pl.when(ci == pl.num_programs(1) - 1)
    def _():
        st_ref[0] = new_state.T


def _dot3_lhs_exact_both(a, b):
    ah, am, al = _split3(a)
    bh, bm, bl = _split3(b)
    return (_dot(ah, bh) + _dot(ah, bm) + _dot(am, bh)) + (_dot(ah, bl) + _dot(am, bm) + _dot(al, bh))


def _gla(z, gate_w, gate_b, norm_w, s0, batch, seq_len, chunk, act_dt):
    m = z.shape[0]
    nc = seq_len // chunk
    hk, hv = GLA_HEADS * GLA_DK, GLA_HEADS * GLA_DV
    comp_rows = max(chunk, BF16_ROWS)
    sel, masks = _gla_constants(comp_rows)
    levels = masks.shape[0]

    def zs(width, col):
        return pl.BlockSpec((chunk, width), lambda b, c: (b * nc + c, col // width))

    const2 = lambda b, c: (0, 0)
    return pl.pallas_call(
        functools.partial(_gla_kernel, comp_rows, levels),
        out_shape=[jax.ShapeDtypeStruct((m, hv), act_dt), jax.ShapeDtypeStruct((batch, hk, GLA_DV), F32)],
        grid=(batch, nc),
        in_specs=[zs(hk, Z_GQ), zs(hk, Z_GK), zs(hv, Z_GV), zs(hv, Z_GR), zs(LANES, Z_SMALL),
                  pl.BlockSpec((GLA_GATE_RANK, hk), const2), pl.BlockSpec((1, hk), const2),
                  pl.BlockSpec((1, GLA_DV), const2),
                  pl.BlockSpec((1, hk, GLA_DV), lambda b, c: (b, 0, 0)),
                  pl.BlockSpec((levels * comp_rows, comp_rows), const2),
                  pl.BlockSpec((levels, comp_rows, comp_rows), lambda b, c: (0, 0, 0))],
        out_specs=[pl.BlockSpec((chunk, hv), lambda b, c: (b * nc + c, 0)),
                   pl.BlockSpec((1, hk, GLA_DV), lambda b, c: (b, 0, 0))],
        scratch_shapes=[pltpu.VMEM((GLA_DV, hk), F32)],
        compiler_params=_cparams(("arbitrary", "arbitrary")),
        name="gla",
    )(z, z, z, z, z, gate_w, gate_b.reshape(1, hk), norm_w.reshape(1, GLA_DV), s0,
      jnp.asarray(sel, BF16), jnp.asarray(masks, F32))


def _online_update(s, mask, v, m_ref, l_ref, acc_ref):
    if mask is not None:
        s = jnp.where(mask, s, NEG)
    m_prev = m_ref[...]
    m_new = jnp.maximum(m_prev, jnp.max(s, axis=-1, keepdims=True))
    p = jnp.exp(s - m_new)
    if mask is not None:
        p = jnp.where(mask, p, 0.0)
    alpha = jnp.exp(m_prev - m_new)
    l_ref[...] = alpha * l_ref[...] + jnp.sum(p, axis=-1, keepdims=True)
    acc_ref[...] = alpha * acc_ref[...] + _dot(p.astype(BF16), v)
    m_ref[...] = m_new


def _diff_kernel(tq, tk, qpos0, lam_init, q_ref, k_ref, v_ref, lam_ref, nw_ref, o_ref,
                 qq_ref, m_ref, l_ref, acc_ref):
    i, j = pl.program_id(2), pl.program_id(3)
    q_lo = qpos0 + i * tq
    last_j = (q_lo + tq - 1) // tk

    @pl.when(j == 0)
    def _():
        q = q_ref[...].astype(F32)
        lane = lax.broadcasted_iota(jnp.int32, (1, LANES), 1)
        qq_ref[...] = jnp.concatenate([jnp.where(lane < DIFF_QK, q, 0.0),
                                       jnp.where(lane >= DIFF_QK, q, 0.0)], axis=0).astype(BF16)
        m_ref[...] = jnp.full_like(m_ref, NEG)
        l_ref[...] = jnp.zeros_like(l_ref)
        acc_ref[...] = jnp.zeros_like(acc_ref)

    def step(masked):
        s = _dot_nt(qq_ref[...], k_ref[0])
        mask = None
        if masked:
            kpos = j * tk + lax.broadcasted_iota(jnp.int32, (1, tk), 1)
            qpos = q_lo + lax.broadcasted_iota(jnp.int32, (tq, 1), 0)
            mk = kpos <= qpos
            mask = jnp.concatenate([mk, mk], axis=0)
        _online_update(s, mask, v_ref[0], m_ref, l_ref, acc_ref)

    fully_visible = (j + 1) * tk - 1 <= q_lo

    @pl.when(fully_visible)
    def _():
        step(False)

    @pl.when(jnp.logical_and(jnp.logical_not(fully_visible), j <= last_j))
    def _():
        step(True)

    @pl.when(j == pl.num_programs(3) - 1)
    def _():
        lam_p = lam_ref[...]
        lam = (jnp.exp(jnp.sum(lam_p[0:1] * lam_p[1:2], axis=-1, keepdims=True))
               - jnp.exp(jnp.sum(lam_p[2:3] * lam_p[3:4], axis=-1, keepdims=True)) + lam_init)
        l = l_ref[...]
        o12 = acc_ref[...] / jnp.where(l > 0, l, 1.0)
        o = o12[0:tq] - lam * o12[tq:2 * tq]
        y = o * lax.rsqrt(jnp.mean(o * o, axis=-1, keepdims=True) + NORM_EPS) * nw_ref[...]
        o_ref[...] = (y * (1.0 - lam_init)).astype(o_ref.dtype)


def _diff_attention(dq, kv, lam_p, norm_w, batch, seq_len, qpos0, lam_init, tq, tk):
    m = dq.shape[0]
    nq = seq_len // tq
    nk = kv.shape[1] // tk

    def kv_idx(off):
        def f(b, h, i, j):
            last = (qpos0 + i * tq + tq - 1) // tk
            return (b, jnp.minimum(j, last), off + h)
        return f

    return pl.pallas_call(
        functools.partial(_diff_kernel, tq, tk, qpos0, lam_init),
        out_shape=jax.ShapeDtypeStruct((m, DIFF_HEADS * DIFF_V), dq.dtype),
        grid=(batch, DIFF_HEADS, nq, nk),
        in_specs=[pl.BlockSpec((tq, LANES), lambda b, h, i, j: (b * nq + i, h)),
                  pl.BlockSpec((1, tk, LANES), kv_idx(0)),
                  pl.BlockSpec((1, tk, LANES), kv_idx(DIFF_HEADS)),
                  pl.BlockSpec((4, DIFF_QK), lambda b, h, i, j: (0, 0)),
                  pl.BlockSpec((1, DIFF_V), lambda b, h, i, j: (0, 0))],
        out_specs=pl.BlockSpec((tq, LANES), lambda b, h, i, j: (b * nq + i, h)),
        scratch_shapes=[pltpu.VMEM((2 * tq, LANES), BF16), pltpu.VMEM((2 * tq, 1), F32),
                        pltpu.VMEM((2 * tq, 1), F32), pltpu.VMEM((2 * tq, DIFF_V), F32)],
        compiler_params=_cparams(("arbitrary",) * 4),
        name="diff_attention",
    )(dq, kv, kv, lam_p, norm_w.reshape(1, DIFF_V))


def _compress_kernel(x_ref, w_ref, pe_ref, o_ref):
    rows = x_ref.shape[1]
    half = NSA_CMP_STRIDE
    tok_w = 2 * NSA_KV_HEADS * HEAD_DIM
    for c in range(2):
        pe_term = _dot(jnp.broadcast_to(pe_ref[c], (BF16_ROWS, pe_ref.shape[2])).astype(BF16), w_ref[c])[0:1]
        for g in range(NSA_KV_HEADS):
            lo = jnp.zeros((rows, HEAD_DIM), F32)
            hi = jnp.zeros((rows, HEAD_DIM), F32)
            for j in range(half):
                col = j * tok_w + (c * NSA_KV_HEADS + g) * HEAD_DIM
                xj = x_ref[0, :, col:col + HEAD_DIM]
                lo = lo + _dot(xj, w_ref[c, j * HEAD_DIM:(j + 1) * HEAD_DIM, :])
                hi = hi + _dot(xj, w_ref[c, (half + j) * HEAD_DIM:(half + j + 1) * HEAD_DIM, :])
            o_ref[0, c, g] = (lo + pltpu.roll(hi, rows - 1, 0) + pe_term).astype(BF16)


def _compress(seg, w, pe):
    b, rows, width = seg.shape
    return pl.pallas_call(
        _compress_kernel,
        out_shape=jax.ShapeDtypeStruct((b, 2, NSA_KV_HEADS, rows, HEAD_DIM), BF16),
        grid=(b,),
        in_specs=[pl.BlockSpec((1, rows, width), lambda i: (i, 0, 0)),
                  pl.BlockSpec(w.shape, lambda i: (0, 0, 0)),
                  pl.BlockSpec(pe.shape, lambda i: (0, 0, 0))],
        out_specs=pl.BlockSpec((1, 2, NSA_KV_HEADS, rows, HEAD_DIM), lambda i: (i, 0, 0, 0, 0)),
        compiler_params=_cparams(("arbitrary",)),
        name="nsa_compress",
    )(seg, w, pe)


def _nsa_constants(n_cmp_rows, n_slc_pad, tk_pad):
    ratio = NSA_SEL_BLOCK // NSA_CMP_STRIDE
    span = NSA_CMP_BLOCK // NSA_CMP_STRIDE
    n = np.arange(n_cmp_rows)[:, None]
    j = np.arange(n_slc_pad)[None, :]
    band = ((n >= ratio * j - (span - 1)) & (n <= ratio * j + ratio - 1)).astype(np.float32)
    expand_t = (np.arange(tk_pad)[:, None] // NSA_SEL_BLOCK == j).astype(np.float32)
    return band, expand_t


def _nsa_kernel(tq, tk, qpos0, n_slc, q_ref, sm_ref, ck_ref, cv_ref, sk_ref, sv_ref, wk_ref, wv_ref,
                band_ref, exp_ref, o_ref, score_ref, m_ref, l_ref, acc_ref):
    i = pl.program_id(1)
    hpg = NSA_HEADS // NSA_KV_HEADS
    rows = hpg * tq
    n_cmp_rows = ck_ref.shape[3]
    nsp = band_ref.shape[1]
    tqp = score_ref.shape[1]
    q_lo = qpos0 + i * tq
    qpos = q_lo + lax.broadcasted_iota(jnp.int32, (tq, 1), 0)
    qpos_rows = jnp.concatenate([qpos] * hpg, axis=0)
    gates = _sigmoid(sm_ref[:, SMALL_GATE:SMALL_GATE + 3 * NSA_HEADS])

    for g in range(NSA_KV_HEADS):
        q = q_ref[:, g * hpg * LANES:(g + 1) * hpg * LANES]
        qs = jnp.concatenate([q[:, n * LANES:(n + 1) * LANES] for n in range(hpg)], axis=0).astype(BF16)

        cmp_end = lax.broadcasted_iota(jnp.int32, (1, n_cmp_rows), 1) * NSA_CMP_STRIDE + (NSA_CMP_BLOCK - 1)
        p_c = _masked_softmax(_dot_nt(qs, ck_ref[0, 0, g]), cmp_end <= qpos_rows)
        o_c = _dot(p_c.astype(BF16), cv_ref[0, 0, g])

        imp = p_c[0:tq]
        for n in range(1, hpg):
            imp = imp + p_c[n * tq:(n + 1) * tq]
        p_slc = _dot3_rhs_exact(imp, band_ref[...])
        blk = lax.broadcasted_iota(jnp.int32, (1, nsp), 1)
        cur = qpos // NSA_SEL_BLOCK
        valid = blk * NSA_SEL_BLOCK <= qpos
        forced = (blk == 0) | (blk == cur) | (blk == cur - 1)
        score = jnp.where(valid, p_slc + jnp.where(forced, NSA_FORCE_BONUS, 0.0), NEG)
        if tqp > tq:
            score = jnp.concatenate([score, jnp.full((tqp - tq, nsp), NEG, F32)], axis=0)
        score_t = score.T
        score_ref[...] = score_t
        blk_t = lax.broadcasted_iota(jnp.int32, (nsp, 1), 0)

        def count(jp, cnt):
            row = score_ref[pl.ds(jp, 1), :]
            beats = (row > score_t) | ((row == score_t) & (jp < blk_t))
            return cnt + jnp.where(beats, 1.0, 0.0)

        cnt = lax.fori_loop(0, n_slc, count, jnp.zeros((nsp, tqp), F32))
        sel_t = jnp.where((cnt < NSA_TOP_N) & (score_t > 0.5 * NEG), 1.0, 0.0)
        sel = sel_t.T[0:max(tq, BF16_ROWS)].astype(BF16)

        m_ref[...] = jnp.full_like(m_ref, NEG)
        l_ref[...] = jnp.zeros_like(l_ref)
        acc_ref[...] = jnp.zeros_like(acc_ref)

        def key_tile(kt, carry):
            start = pl.multiple_of(kt * tk, tk)
            k = sk_ref[0, pl.ds(start, tk), g * LANES:(g + 1) * LANES]
            v = sv_ref[0, pl.ds(start, tk), g * LANES:(g + 1) * LANES]
            picked = _dot_nt(sel, exp_ref[pl.ds(start, tk), :])[0:tq] > 0.5
            kpos = kt * tk + lax.broadcasted_iota(jnp.int32, (1, tk), 1)
            mk = picked & (kpos <= qpos)
            mask = jnp.concatenate([mk] * hpg, axis=0)
            _online_update(_dot_nt(qs, k), mask, v, m_ref, l_ref, acc_ref)
            return carry

        lax.fori_loop(0, (q_lo + tq - 1) // tk + 1, key_tile, 0)
        l = l_ref[...]
        o_s = acc_ref[...] / jnp.where(l > 0, l, 1.0)

        wrows = -(-(tq + NSA_WINDOW) // BF16_ROWS) * BF16_ROWS
        wstart = pl.multiple_of(i * tq, tq)
        kw = wk_ref[0, pl.ds(wstart, wrows), g * LANES:(g + 1) * LANES]
        vw = wv_ref[0, pl.ds(wstart, wrows), g * LANES:(g + 1) * LANES]
        wpos = q_lo - NSA_WINDOW + lax.broadcasted_iota(jnp.int32, (1, wrows), 1)
        mask_w = (wpos <= qpos_rows) & (wpos > qpos_rows - NSA_WINDOW) & (wpos >= 0)
        p_w = _masked_softmax(_dot_nt(qs, kw), mask_w)
        o_w = _dot(p_w.astype(BF16), vw)

        for n in range(hpg):
            h = g * hpg + n
            r = slice(n * tq, (n + 1) * tq)
            o = (gates[:, 3 * h:3 * h + 1] * o_c[r] + gates[:, 3 * h + 1:3 * h + 2] * o_s[r]
                 + gates[:, 3 * h + 2:3 * h + 3] * o_w[r])
            o_ref[:, h * LANES:(h + 1) * LANES] = o.astype(o_ref.dtype)


def _nsa_attention(qn, z, comp, slc_buf, win_all, batch, seq_len, t_valid, qpos0, tq, tk):
    m = qn.shape[0]
    nq = seq_len // tq
    tk_pad = slc_buf.shape[1]
    n_cmp_rows = comp.shape[3]
    n_slc = -(-t_valid // NSA_SEL_BLOCK)
    nsp = -(-n_slc // LANES) * LANES
    tqp = -(-tq // LANES) * LANES
    band, expand_t = _nsa_constants(n_cmp_rows, nsp, tk_pad)
    hw = NSA_HEADS * HEAD_DIM
    kvw = NSA_KV_HEADS * HEAD_DIM
    rows = (NSA_HEADS // NSA_KV_HEADS) * tq
    wlen = win_all.shape[1]
    return pl.pallas_call(
        functools.partial(_nsa_kernel, tq, tk, qpos0, n_slc),
        out_shape=jax.ShapeDtypeStruct((m, hw), qn.dtype),
        grid=(batch, nq),
        in_specs=[pl.BlockSpec((tq, hw), lambda b, i: (b * nq + i, 0)),
                  pl.BlockSpec((tq, LANES), lambda b, i: (b * nq + i, Z_SMALL // LANES)),
                  pl.BlockSpec((1, 1, NSA_KV_HEADS, n_cmp_rows, HEAD_DIM), lambda b, i: (b, 0, 0, 0, 0)),
                  pl.BlockSpec((1, 1, NSA_KV_HEADS, n_cmp_rows, HEAD_DIM), lambda b, i: (b, 1, 0, 0, 0)),
                  pl.BlockSpec((1, tk_pad, kvw), lambda b, i: (b, 0, 0)),
                  pl.BlockSpec((1, tk_pad, kvw), lambda b, i: (b, 0, 1)),
                  pl.BlockSpec((1, wlen, kvw), lambda b, i: (b, 0, 0)),
                  pl.BlockSpec((1, wlen, kvw), lambda b, i: (b, 0, 1)),
                  pl.BlockSpec((n_cmp_rows, nsp), lambda b, i: (0, 0)),
                  pl.BlockSpec((tk_pad, nsp), lambda b, i: (0, 0))],
        out_specs=pl.BlockSpec((tq, hw), lambda b, i: (b * nq + i, 0)),
        scratch_shapes=[pltpu.VMEM((nsp, tqp), F32), pltpu.VMEM((rows, 1), F32), pltpu.VMEM((rows, 1), F32),
                        pltpu.VMEM((rows, HEAD_DIM), F32)],
        compiler_params=_cparams(("arbitrary", "arbitrary")),
        name="nsa_attention",
    )(qn, z, comp, comp, slc_buf, slc_buf, win_all, win_all, jnp.asarray(band, BF16), jnp.asarray(expand_t, BF16))


def _out_proj_kernel(x_ref, og_ref, on_ref, od_ref, w_ref, nw_ref, g_ref, o_ref):
    a = jnp.concatenate([og_ref[...].astype(BF16), on_ref[...].astype(BF16), od_ref[...].astype(BF16)], axis=-1)
    mix = _dot(a, w_ref[...])
    y = mix * lax.rsqrt(jnp.mean(mix * mix, axis=-1, keepdims=True) + NORM_EPS) * nw_ref[...]
    o_ref[...] = x_ref[...] + g_ref[0] * y


def _out_proj(x2, o_gla, o_nsa, o_d, w, nw, gate, seq_len, tm):
    m, d = x2.shape
    row = lambda width: pl.BlockSpec((tm, width), lambda i: (i, 0))
    return pl.pallas_call(
        _out_proj_kernel,
        out_shape=jax.ShapeDtypeStruct((m, d), F32),
        grid=(m // tm,),
        in_specs=[row(d), row(o_gla.shape[1]), row(o_nsa.shape[1]), row(o_d.shape[1]),
                  pl.BlockSpec(w.shape, lambda i: (0, 0)), pl.BlockSpec((1, d), lambda i: (0, 0)),
                  _mod_spec(seq_len, tm, d)],
        out_specs=row(d),
        compiler_params=_cparams(("arbitrary",)),
        name="out_proj",
    )(x2, o_gla, o_nsa, o_d, w, nw, _expand_mod(gate, seq_len, tm))


def _ffn_kernel(x_ref, nw2_ref, sc_ref, sh_ref, wg_ref, wu_ref, wo_ref, nw3_ref, g_ref, o_ref, h_ref, acc_ref):
    j = pl.program_id(1)

    @pl.when(j == 0)
    def _():
        h_ref[...] = _norm_mod(x_ref[...], nw2_ref[...], sc_ref[0], sh_ref[0]).astype(BF16)
        acc_ref[...] = jnp.zeros_like(acc_ref)

    h = h_ref[...]
    gate = _dot(h, wg_ref[...])
    up = _dot(h, wu_ref[...])
    acc_ref[...] += _dot((gate * _sigmoid(gate) * up).astype(BF16), wo_ref[...])

    @pl.when(j == pl.num_programs(1) - 1)
    def _():
        f = acc_ref[...]
        y = f * lax.rsqrt(jnp.mean(f * f, axis=-1, keepdims=True) + NORM_EPS) * nw3_ref[...]
        o_ref[...] = x_ref[...] + g_ref[0] * y


def _ffn(x2, nw2, sc, sh, w_in, w_out, nw3, gate, seq_len, tm, tf):
    m, d = x2.shape
    d_ff = w_out.shape[0]
    nf = d_ff // tf
    row = pl.BlockSpec((tm, d), lambda i, j: (i, 0))
    vec = pl.BlockSpec((1, d), lambda i, j: (0, 0))
    return pl.pallas_call(
        _ffn_kernel,
        out_shape=jax.ShapeDtypeStruct((m, d), F32),
        grid=(m // tm, nf),
        in_specs=[row, vec, _mod_spec(seq_len, tm, d), _mod_spec(seq_len, tm, d),
                  pl.BlockSpec((d, tf), lambda i, j: (0, j)),
                  pl.BlockSpec((d, tf), lambda i, j: (0, nf + j)),
                  pl.BlockSpec((tf, d), lambda i, j: (j, 0)),
                  vec, _mod_spec(seq_len, tm, d)],
        out_specs=row,
        scratch_shapes=[pltpu.VMEM((tm, d), BF16), pltpu.VMEM((tm, d), F32)],
        compiler_params=_cparams(("arbitrary", "arbitrary")),
        name="ffn",
    )(x2, nw2, _expand_mod(sc, seq_len, tm), _expand_mod(sh, seq_len, tm), w_in, w_in, w_out, nw3,
      _expand_mod(gate, seq_len, tm))


def _pick(n, target):
    if n <= target:
        return n
    for t in range(target, 7, -1):
        if n % t == 0 and t % SUBLANES == 0:
            return t
    return n


def _permute_w_in(w):
    d = w.shape[0]
    hk, hv = GLA_HEADS * GLA_DK, GLA_HEADS * GLA_DV
    kvw = NSA_KV_HEADS * HEAD_DIM
    sizes = [hk, hk, hv, hv, GLA_GATE_RANK, NSA_HEADS * HEAD_DIM, kvw, kvw, kvw, kvw, kvw, kvw, NSA_HEADS * 3,
             DIFF_HEADS * 2 * DIFF_QK, DIFF_HEADS * 2 * DIFF_QK, DIFF_HEADS * DIFF_V]
    offs = np.concatenate([[0], np.cumsum(sizes)])
    part = lambda k: w[:, offs[k]:offs[k + 1]]
    order = [5, 0, 1, 2, 3, 6, 7, 8, 9, 10, 11, 13, 14, 15, 4, 12]
    cols = [part(k) for k in order]
    pad = Z_WIDTH - int(offs[-1])
    return jnp.concatenate(cols + [jnp.zeros((d, pad), w.dtype)], axis=1).astype(BF16)


def _layer(x, mod, qpos0, past, win_buf, gla_s0, lp, lam_init):
    b, seq_len, d = x.shape
    m = b * seq_len
    x2 = x.reshape(m, d)
    sh1, sc1, g1, sh2, sc2, g2 = jnp.split(mod, 6, axis=-1)
    tm = _pick(seq_len, 512) if seq_len >= 128 else m
    z = _in_proj(x2, lp['norm'][0:1], sc1, sh1, lp['w_in'], seq_len, tm, _pick(Z_WIDTH, 640))

    pos = qpos0 + jnp.arange(seq_len)
    tab = _rope_tables(pos)
    if seq_len % tm != 0:
        tab = jnp.tile(tab, (m // seq_len, 1))
    act_dt = BF16 if seq_len % BF16_ROWS == 0 else F32
    (qn, cmp_f, cmp_b, slc_f, slc_b, win_f, win_b, dq, dkv_f, dkv_b) = _rope_split(z, tab, seq_len, tm, act_dt)

    chunk = 64 if seq_len % 64 == 0 else seq_len
    o_gla, gla_state = _gla(z, lp['gla_gate_w'], lp['gla_gate_b'], lp['gla_norm'], gla_s0, b, seq_len, chunk, act_dt)

    kvw = 2 * NSA_KV_HEADS * HEAD_DIM
    dw = 2 * DIFF_HEADS * DIFF_V
    if past is None:
        t_valid = seq_len
        cmp_buf = cmp_b.reshape(b, seq_len, kvw)
        slc_buf = slc_b.reshape(b, seq_len, kvw)
        diff_buf = dkv_b.reshape(b, seq_len, dw)
        tk_attn = _pick(seq_len, 512)
    else:
        t_valid = qpos0 + seq_len
        pps = 4
        cmp_buf = _gather_pages(past['cmp'], past['page_table'], cmp_f.reshape(b, seq_len, kvw), pps)
        slc_buf = _gather_pages(past['slc'], past['page_table'], slc_f.reshape(b, seq_len, kvw), pps)
        diff_buf = _gather_pages(past['diff'], past['page_table'], dkv_f.reshape(b, seq_len, dw), pps)
        tk_attn = pps * PAGE_SIZE
    win_len = NSA_WINDOW + seq_len
    win_pad = jnp.zeros((b, -win_len % BF16_ROWS, kvw), BF16)
    win_all = jnp.concatenate([win_buf.astype(BF16), win_b.reshape(b, seq_len, kvw), win_pad], axis=1)

    seg = cmp_buf.reshape(b, cmp_buf.shape[1] // NSA_CMP_STRIDE, NSA_CMP_STRIDE * kvw)
    comp = _compress(seg, lp['nsa_cmp_w'], lp['nsa_cmp_pe'])
    tq_nsa = _pick(seq_len, 128)
    o_nsa = _nsa_attention(qn, z, comp, slc_buf, win_all, b, seq_len, t_valid, qpos0, tq_nsa, tk_attn)

    tq_d = _pick(seq_len, 256)
    o_d = _diff_attention(dq, diff_buf, lp['diff_lambda'], lp['diff_norm'], b, seq_len, qpos0, lam_init,
                          tq_d, tk_attn)

    x1 = _out_proj(x2, o_gla, o_nsa, o_d, lp['w_out'], lp['norm'][1:2], g1, seq_len, tm)
    d_ff = lp['ffn_w_out'].shape[0]
    x2n = _ffn(x1, lp['norm'][2:3], sc2, sh2, lp['ffn_w_in'], lp['ffn_w_out'], lp['norm'][3:4], g2, seq_len, tm,
               _pick(d_ff, 512) if d_ff % LANES == 0 else d_ff)

    return (x2n.reshape(b, seq_len, d), cmp_f, slc_f, dkv_f, win_f, gla_state)


def kernel(x_prompt, x_sample, c_prompt, c_sample, cache_nsa_cmp_kv, cache_nsa_slc_kv, cache_diff_kv,
           state_nsa_win_kv, state_gla, page_table, ada_w, ada_b, norm_w, w_in, gla_gate_w, gla_gate_b,
           gla_norm, nsa_cmp_pe, nsa_cmp_w, diff_lambda, diff_norm, w_out, ffn_w_in, ffn_w_out):
    depth = ada_w.shape[0]
    bp, lp_len, d = x_prompt.shape
    bs, ls_len, _ = x_sample.shape
    n_pool = cache_nsa_cmp_kv.shape[1]
    n_pages = page_table.shape[1]
    past_len = n_pages * PAGE_SIZE
    wb = state_nsa_win_kv.shape[2]
    kvw = 2 * NSA_KV_HEADS * HEAD_DIM
    dw = 2 * DIFF_HEADS * DIFF_V
    hk = GLA_HEADS * GLA_DK

    n_c = bp + bs
    rows = -(-n_c // SUBLANES) * SUBLANES
    c_all = jnp.concatenate([c_prompt, c_sample, jnp.zeros((rows - n_c, d), F32)], axis=0)
    mod_all = _modulation(c_all, ada_w, ada_b)

    xp, xs = x_prompt, x_sample
    outs = [[] for _ in range(10)]
    for l in range(depth):
        lp = {'norm': norm_w[l], 'w_in': _permute_w_in(w_in[l]),
              'gla_gate_w': gla_gate_w[l], 'gla_gate_b': gla_gate_b[l], 'gla_norm': gla_norm[l],
              'nsa_cmp_pe': nsa_cmp_pe[l].reshape(2, 1, NSA_CMP_BLOCK * HEAD_DIM),
              'nsa_cmp_w': nsa_cmp_w[l].astype(BF16), 'diff_lambda': diff_lambda[l], 'diff_norm': diff_norm[l],
              'w_out': w_out[l].astype(BF16), 'ffn_w_in': ffn_w_in[l].astype(BF16),
              'ffn_w_out': ffn_w_out[l].astype(BF16)}
        lam_init = 0.8 - 0.6 * math.exp(-0.3 * l)

        win0 = jnp.zeros((bp, NSA_WINDOW, kvw), BF16)
        gla0 = jnp.zeros((bp, hk, GLA_DV), F32)
        xp, cmp_p, slc_p, diff_p, win_p, gla_p = _layer(xp, mod_all[l, :bp], 0, None, win0, gla0, lp, lam_init)

        past = {'cmp': cache_nsa_cmp_kv[l].reshape(n_pool, PAGE_SIZE, kvw),
                'slc': cache_nsa_slc_kv[l].reshape(n_pool, PAGE_SIZE, kvw),
                'diff': cache_diff_kv[l].reshape(n_pool, PAGE_SIZE, dw), 'page_table': page_table}
        win_prev = state_nsa_win_kv[l].reshape(bs, wb, kvw)
        win_in = jnp.concatenate([jnp.zeros((bs, NSA_WINDOW - wb, kvw), F32), win_prev], axis=1)
        xs, cmp_s, slc_s, diff_s, win_s, gla_s = _layer(xs, mod_all[l, bp:bp + bs], past_len, past, win_in,
                                                        state_gla[l].reshape(bs, hk, GLA_DV), lp, lam_init)

        kv_shape = lambda b, n: (b, n, 2, NSA_KV_HEADS, HEAD_DIM)
        outs[0].append(cmp_p.reshape(kv_shape(bp, lp_len)))
        outs[1].append(cmp_s.reshape(kv_shape(bs, ls_len)))
        outs[2].append(slc_p.reshape(kv_shape(bp, lp_len)))
        outs[3].append(slc_s.reshape(kv_shape(bs, ls_len)))
        outs[4].append(diff_p.reshape(bp, lp_len, 2, DIFF_HEADS, DIFF_V))
        outs[5].append(diff_s.reshape(bs, ls_len, 2, DIFF_HEADS, DIFF_V))
        win_p3 = win_p.reshape(bp, lp_len, kvw)
        win_state_p = jnp.concatenate([jnp.zeros((bp, wb, kvw), F32), win_p3], axis=1)[:, -wb:]
        win_state_s = jnp.concatenate([win_prev, win_s.reshape(bs, ls_len, kvw)], axis=1)[:, -wb:]
        outs[6].append(win_state_p.reshape(kv_shape(bp, wb)))
        outs[7].append(win_state_s.reshape(kv_shape(bs, wb)))
        outs[8].append(gla_p.reshape(bp, GLA_HEADS, GLA_DK, GLA_DV))
        outs[9].append(gla_s.reshape(bs, GLA_HEADS, GLA_DK, GLA_DV))

    return (xp, xs) + tuple(jnp.stack(o) for o in outs)
```

```python
import functools
import math

import numpy as np
import jax
import jax.numpy as jnp
from jax import lax
from jax.experimental import pallas as pl
from jax.experimental.pallas import tpu as pltpu

F32 = jnp.float32
BF16 = jnp.bfloat16

PAGE_SIZE = 128
HEAD_DIM = 128
GLA_HEADS, GLA_DK, GLA_DV, GLA_GATE_RANK, GLA_TAU = 4, 64, 128, 16, 16.0
NSA_HEADS, NSA_KV_HEADS = 8, 2
NSA_CMP_BLOCK, NSA_CMP_STRIDE, NSA_SEL_BLOCK, NSA_TOP_N, NSA_WINDOW = 32, 16, 64, 16, 512
NSA_FORCE_BONUS = 1.0e4
DIFF_HEADS, DIFF_QK, DIFF_V = 4, 64, 128
ROPE_THETA = 10000.0
NORM_EPS = 1e-6
NEG = -1.0e30

LANES = 128
SUBLANES = 8
BF16_ROWS = 16
VMEM_LIMIT_BYTES = 56 * 1024 * 1024
PAGES_PER_STEP = 8

Z_NQ, Z_GQ, Z_GK, Z_GV, Z_GR = 0, 1024, 1280, 1536, 2048
Z_CMP, Z_SLC, Z_WIN, Z_DQ, Z_DK, Z_DV, Z_SMALL = 2560, 3072, 3584, 4096, 4608, 5120, 5632
Z_WIDTH = 5760
SMALL_GA, SMALL_GATE = 0, 16


def _cparams(sem):
    return pltpu.CompilerParams(dimension_semantics=sem, vmem_limit_bytes=VMEM_LIMIT_BYTES)


def _dot(a, b):
    return jnp.dot(a, b, preferred_element_type=F32)


def _dot_nt(a, b):
    return lax.dot_general(a, b, (((1,), (1,)), ((), ())), preferred_element_type=F32)


def _dot_tn(a, b):
    return lax.dot_general(a, b, (((0,), (0,)), ((), ())), preferred_element_type=F32)


def _split3(a):
    hi = a.astype(BF16)
    r = a - hi.astype(F32)
    mid = r.astype(BF16)
    lo = (r - mid.astype(F32)).astype(BF16)
    return hi, mid, lo


def _dot3_rhs_exact(a, b):
    hi, mid, lo = _split3(a)
    return _dot(hi, b) + _dot(mid, b) + _dot(lo, b)


def _dot3_lhs_exact(a, b):
    hi, mid, lo = _split3(b)
    return _dot(a, hi) + _dot(a, mid) + _dot(a, lo)


def _sigmoid(x):
    return 1.0 / (1.0 + jnp.exp(-x))


def _masked_softmax(s, mask):
    s = jnp.where(mask, s, NEG)
    m = jnp.max(s, axis=-1, keepdims=True)
    e = jnp.where(mask, jnp.exp(s - m), 0.0)
    den = jnp.sum(e, axis=-1, keepdims=True)
    return e / jnp.where(den > 0, den, 1.0)


def _mod_kernel(c_ref, w_ref, b_ref, o_ref):
    c = c_ref[...]
    a = (c * _sigmoid(c)).astype(BF16)
    o_ref[0] = _dot(a, w_ref[0].astype(BF16)) + b_ref[0]


def _modulation(c_all, ada_w, ada_b):
    depth, d, n6 = ada_w.shape
    rows = c_all.shape[0]
    tn = max(t for t in range(LANES, min(1024, n6) + 1, LANES) if n6 % t == 0)
    return pl.pallas_call(
        _mod_kernel,
        out_shape=jax.ShapeDtypeStruct((depth, rows, n6), F32),
        grid=(depth, n6 // tn),
        in_specs=[pl.BlockSpec((rows, d), lambda l, j: (0, 0)),
                  pl.BlockSpec((1, d, tn), lambda l, j: (l, 0, j)),
                  pl.BlockSpec((1, 1, tn), lambda l, j: (l, 0, j))],
        out_specs=pl.BlockSpec((1, rows, tn), lambda l, j: (l, 0, j)),
        compiler_params=_cparams(("arbitrary", "arbitrary")),
        name="adaln_modulation",
    )(c_all, ada_w, ada_b.reshape(depth, 1, n6))


def _norm_mod(x, nw, sc, sh):
    ms = jnp.mean(x * x, axis=-1, keepdims=True)
    return x * lax.rsqrt(ms + NORM_EPS) * nw * (1.0 + sc) + sh


def _in_proj_kernel(x_ref, nw_ref, sc_ref, sh_ref, w_ref, o_ref, h_ref):
    @pl.when(pl.program_id(1) == 0)
    def _():
        h_ref[...] = _norm_mod(x_ref[...], nw_ref[...], sc_ref[0], sh_ref[0]).astype(BF16)

    o_ref[...] = _dot(h_ref[...], w_ref[...])


def _mod_spec(seq_len, tm, d):
    if seq_len % tm == 0:
        per = seq_len // tm
        return pl.BlockSpec((1, 1, d), lambda i, *_: (i // per, 0, 0))
    return pl.BlockSpec((1, tm, d), lambda i, *_: (0, i, 0))


def _expand_mod(m, seq_len, tm):
    if seq_len % tm == 0:
        return m[:, None, :]
    return jnp.repeat(m, seq_len, axis=0)[None]


def _in_proj(x2, nw, sc, sh, w, seq_len, tm, tn):
    m, d = x2.shape
    n = w.shape[1]
    return pl.pallas_call(
        _in_proj_kernel,
        out_shape=jax.ShapeDtypeStruct((m, n), F32),
        grid=(m // tm, n // tn),
        in_specs=[pl.BlockSpec((tm, d), lambda i, j: (i, 0)),
                  pl.BlockSpec((1, d), lambda i, j: (0, 0)),
                  _mod_spec(seq_len, tm, d), _mod_spec(seq_len, tm, d),
                  pl.BlockSpec((d, tn), lambda i, j: (0, j))],
        out_specs=pl.BlockSpec((tm, tn), lambda i, j: (i, j)),
        scratch_shapes=[pltpu.VMEM((tm, d), BF16)],
        compiler_params=_cparams(("arbitrary", "arbitrary")),
        name="in_proj",
    )(x2, nw, _expand_mod(sc, seq_len, tm), _expand_mod(sh, seq_len, tm), w)


def _rope_tables(pos):
    pos = pos.astype(F32)[:, None]

    def tab(half, reps):
        inv = ROPE_THETA ** (-jnp.arange(half, dtype=F32) / half)
        ang = pos * inv[None, :]
        c, s = jnp.cos(ang), jnp.sin(ang)
        return jnp.tile(jnp.concatenate([c, c], -1), (1, reps)), jnp.tile(jnp.concatenate([-s, s], -1), (1, reps))

    c128, s128 = tab(HEAD_DIM // 2, 1)
    c64, s64 = tab(DIFF_QK // 2, 2)
    return jnp.concatenate([c128, s128, c64, s64], axis=-1)


def _rope128(x, cos, sin):
    return x * cos + pltpu.roll(x, 64, 1) * sin


def _rope64(x, cos, sin, first_half):
    partner = jnp.where(first_half, pltpu.roll(x, 96, 1), pltpu.roll(x, 32, 1))
    return x * cos + partner * sin


def _rope_kernel(nq_ref, cmp_ref, slc_ref, win_ref, dq_ref, dk_ref, dv_ref, tab_ref,
                 qn_o, cmp_o, cmpb_o, slc_o, slcb_o, win_o, winb_o, dq_o, dkv_o, dkvb_o):
    tab = tab_ref[...]
    c128, s128, c64, s64 = (tab[:, i * LANES:(i + 1) * LANES] for i in range(4))
    lane = lax.broadcasted_iota(jnp.int32, (1, LANES), 1)
    first_half = (lane % DIFF_QK) < (DIFF_QK // 2)
    nsa_scale = HEAD_DIM ** -0.5
    diff_scale = DIFF_QK ** -0.5

    for h in range(NSA_HEADS):
        sl = slice(h * LANES, (h + 1) * LANES)
        qn_o[:, sl] = (_rope128(nq_ref[:, sl], c128, s128) * nsa_scale).astype(qn_o.dtype)

    for src, dst, dstb in ((cmp_ref, cmp_o, cmpb_o), (slc_ref, slc_o, slcb_o), (win_ref, win_o, winb_o)):
        for g in range(2 * NSA_KV_HEADS):
            sl = slice(g * LANES, (g + 1) * LANES)
            v = src[:, sl]
            if g < NSA_KV_HEADS:
                v = _rope128(v, c128, s128)
            dst[:, sl] = v
            dstb[:, sl] = v.astype(BF16)

    for h in range(DIFF_HEADS):
        sl = slice(h * LANES, (h + 1) * LANES)
        dq_o[:, sl] = (_rope64(dq_ref[:, sl], c64, s64, first_half) * diff_scale).astype(dq_o.dtype)
        k = _rope64(dk_ref[:, sl], c64, s64, first_half)
        dkv_o[:, sl] = k
        dkvb_o[:, sl] = k.astype(BF16)
        sv = slice((DIFF_HEADS + h) * LANES, (DIFF_HEADS + h + 1) * LANES)
        v = dv_ref[:, sl]
        dkv_o[:, sv] = v
        dkvb_o[:, sv] = v.astype(BF16)


def _rope_split(z, tab, seq_len, tm, act_dt):
    m = z.shape[0]
    per = max(seq_len // tm, 1)
    tab_spec = (pl.BlockSpec((tm, 4 * LANES), lambda i: (i % per, 0)) if seq_len % tm == 0
                else pl.BlockSpec((tm, 4 * LANES), lambda i: (i, 0)))

    def zs(width, col):
        return pl.BlockSpec((tm, width), lambda i: (i, col // width))

    def os(width):
        return pl.BlockSpec((tm, width), lambda i: (i, 0))

    outs = [(1024, act_dt), (512, F32), (512, BF16), (512, F32), (512, BF16), (512, F32), (512, BF16),
            (512, act_dt), (1024, F32), (1024, BF16)]
    return pl.pallas_call(
        _rope_kernel,
        out_shape=[jax.ShapeDtypeStruct((m, w), dt) for w, dt in outs],
        grid=(m // tm,),
        in_specs=[zs(1024, Z_NQ), zs(512, Z_CMP), zs(512, Z_SLC), zs(512, Z_WIN),
                  zs(512, Z_DQ), zs(512, Z_DK), zs(512, Z_DV), tab_spec],
        out_specs=[os(w) for w, _ in outs],
        compiler_params=_cparams(("arbitrary",)),
        name="rope_split",
    )(z, z, z, z, z, z, z, tab)


def _page_view(cache):
    depth, n_pool, page, two, heads, width = cache.shape
    return cache.reshape(depth, n_pool, page * two * heads, width)


def _page_specs(layer, page_rows, pages_per_step):
    def spec(p):
        return pl.BlockSpec((1, 1, page_rows, LANES),
                            lambda b, j, pt: (layer, pt[b, j * pages_per_step + p], 0, 0))
    return [spec(p) for p in range(pages_per_step)]


def _page_slabs(page_refs, first_slot, n, n_slots):
    return jnp.concatenate(
        [jnp.concatenate([ref[0, 0, pl.ds(first_slot + s, PAGE_SIZE, stride=n_slots), :] for s in range(n)], axis=1)
         for ref in page_refs], axis=0).astype(BF16)


def _pad_rows(x, multiple):
    pad = -x.shape[0] % multiple
    return x if pad == 0 else jnp.concatenate([x, jnp.zeros((pad, x.shape[1]), x.dtype)], axis=0)


def _gla_constants(c):
    t = np.arange(c)
    sizes = []
    s = c // 2
    while s >= 1:
        sizes.append(s)
        s //= 2
    sel, masks = [], []
    for sz in sizes:
        ref_row = (t // (2 * sz)) * (2 * sz) + sz - 1
        sel.append((t[None, :] <= ref_row[:, None]).astype(np.float32))
        same = (t[:, None] // (2 * sz)) == (t[None, :] // (2 * sz))
        masks.append((same & ((t[:, None] // sz) % 2 == 1) & ((t[None, :] // sz) % 2 == 0)).astype(np.float32))
    sel.append((t[None, :] <= t[:, None]).astype(np.float32))
    masks.append(np.eye(c, dtype=np.float32))
    return np.concatenate(sel, 0), np.stack(masks, 0)


def _gla_kernel(chunk, levels, q_ref, k_ref, v_ref, r_ref, sm_ref, gw_ref, gb_ref, nw_ref, s0_ref,
                sel_ref, mask_ref, o_ref, st_ref, state_ref):
    c = chunk
    c_in = q_ref.shape[0]
    hk = GLA_HEADS * GLA_DK
    ci = pl.program_id(1)

    @pl.when(ci == 0)
    def _():
        state_ref[...] = s0_ref[0].T

    def rows(x):
        if c_in == c:
            return x
        return jnp.concatenate([x, jnp.zeros((c - c_in, x.shape[1]), x.dtype)], axis=0)

    ga = rows(sm_ref[:, SMALL_GA:SMALL_GA + GLA_GATE_RANK])
    pre = _dot3_lhs_exact_both(ga, gw_ref[...]) + gb_ref[...]
    log_a = (jnp.minimum(pre, 0.0) - jnp.log(1.0 + jnp.exp(-jnp.abs(pre)))) / GLA_TAU
    if c_in != c:
        log_a = jnp.where(lax.broadcasted_iota(jnp.int32, (c, 1), 0) < c_in, log_a, 0.0)
    refs = _dot3_lhs_exact(sel_ref[...], log_a)
    b = refs[(levels - 1) * c:levels * c]
    q = rows(q_ref[...]) * (GLA_DK ** -0.5)
    k = rows(k_ref[...])
    v_all = rows(v_ref[...])
    lane = lax.broadcasted_iota(jnp.int32, (1, hk), 1)
    head_of_lane = lane // GLA_DK

    def stack_heads(x):
        return jnp.concatenate([jnp.where(head_of_lane == h, x, 0.0) for h in range(GLA_HEADS)], axis=0).astype(BF16)

    attn = jnp.zeros((GLA_HEADS, c, c), F32)
    for lv in range(levels):
        r = refs[lv * c:(lv + 1) * c]
        qd = stack_heads(q * jnp.exp(jnp.minimum(b - r, 0.0)))
        kd = (k * jnp.exp(jnp.minimum(r - b, 0.0))).astype(BF16)
        attn = attn + _dot_nt(qd, kd).reshape(GLA_HEADS, c, c) * mask_ref[lv][None]
    state = state_ref[...]
    inter = _dot_nt(stack_heads(q * jnp.exp(b)), state.astype(BF16))

    nw = nw_ref[...]
    for h in range(GLA_HEADS):
        vh = v_all[:, h * GLA_DV:(h + 1) * GLA_DV]
        o = (_dot(attn[h].astype(BF16), vh.astype(BF16)) + inter[h * c:(h + 1) * c])[0:c_in]
        y = o * lax.rsqrt(jnp.mean(o * o, axis=-1, keepdims=True) + NORM_EPS) * nw
        rh = r_ref[:, h * GLA_DV:(h + 1) * GLA_DV]
        o_ref[:, h * GLA_DV:(h + 1) * GLA_DV] = (y * (rh * _sigmoid(rh))).astype(o_ref.dtype)

    b_end = b[c - 1:c]
    kd = (k * jnp.exp(b_end - b)).astype(BF16)
    new_state = state * jnp.exp(b_end)
    for h in range(GLA_HEADS):
        vh = v_all[:, h * GLA_DV:(h + 1) * GLA_DV].astype(BF16)
        new_state = new_state + jnp.where(head_of_lane == h, _dot_tn(vh, kd), 0.0)
    state_ref[...] = new_state

    @pl.when(ci == pl.num_programs(1) - 1)
    def _():
        st_ref[0] = new_state.T


def _dot3_lhs_exact_both(a, b):
    ah, am, al = _split3(a)
    bh, bm, bl = _split3(b)
    return (_dot(ah, bh) + _dot(ah, bm) + _dot(am, bh)) + (_dot(ah, bl) + _dot(am, bm) + _dot(al, bh))


def _gla(z, gate_w, gate_b, norm_w, s0, batch, seq_len, chunk, act_dt):
    m = z.shape[0]
    nc = seq_len // chunk
    hk, hv = GLA_HEADS * GLA_DK, GLA_HEADS * GLA_DV
    comp_rows = max(chunk, BF16_ROWS)
    sel, masks = _gla_constants(comp_rows)
    levels = masks.shape[0]

    def zs(width, col):
        return pl.BlockSpec((chunk, width), lambda b, c: (b * nc + c, col // width))

    const2 = lambda b, c: (0, 0)
    return pl.pallas_call(
        functools.partial(_gla_kernel, comp_rows, levels),
        out_shape=[jax.ShapeDtypeStruct((m, hv), act_dt), jax.ShapeDtypeStruct((batch, hk, GLA_DV), F32)],
        grid=(batch, nc),
        in_specs=[zs(hk, Z_GQ), zs(hk, Z_GK), zs(hv, Z_GV), zs(hv, Z_GR), zs(LANES, Z_SMALL),
                  pl.BlockSpec((GLA_GATE_RANK, hk), const2), pl.BlockSpec((1, hk), const2),
                  pl.BlockSpec((1, GLA_DV), const2),
                  pl.BlockSpec((1, hk, GLA_DV), lambda b, c: (b, 0, 0)),
                  pl.BlockSpec((levels * comp_rows, comp_rows), const2),
                  pl.BlockSpec((levels, comp_rows, comp_rows), lambda b, c: (0, 0, 0))],
        out_specs=[pl.BlockSpec((chunk, hv), lambda b, c: (b * nc + c, 0)),
                   pl.BlockSpec((1, hk, GLA_DV), lambda b, c: (b, 0, 0))],
        scratch_shapes=[pltpu.VMEM((GLA_DV, hk), F32)],
        compiler_params=_cparams(("arbitrary", "arbitrary")),
        name="gla",
    )(z, z, z, z, z, gate_w, gate_b.reshape(1, hk), norm_w.reshape(1, GLA_DV), s0,
      jnp.asarray(sel, BF16), jnp.asarray(masks, F32))


def _online_update(s, mask, v, m_ref, l_ref, acc_ref):
    if mask is not None:
        s = jnp.where(mask, s, NEG)
    m_prev = m_ref[...]
    m_new = jnp.maximum(m_prev, jnp.max(s, axis=-1, keepdims=True))
    p = jnp.exp(s - m_new)
    if mask is not None:
        p = jnp.where(mask, p, 0.0)
    alpha = jnp.exp(m_prev - m_new)
    l_ref[...] = alpha * l_ref[...] + jnp.sum(p, axis=-1, keepdims=True)
    acc_ref[...] = alpha * acc_ref[...] + _dot(p.astype(BF16), v)
    m_ref[...] = m_new


def _diff_kernel(tq, tk, qpos0, lam_init, q_ref, k_ref, v_ref, lam_ref, nw_ref, o_ref,
                 qq_ref, m_ref, l_ref, acc_ref):
    i, j = pl.program_id(2), pl.program_id(3)
    q_lo = qpos0 + i * tq
    last_j = (q_lo + tq - 1) // tk

    @pl.when(j == 0)
    def _():
        q = q_ref[...].astype(F32)
        lane = lax.broadcasted_iota(jnp.int32, (1, LANES), 1)
        qq_ref[...] = jnp.concatenate([jnp.where(lane < DIFF_QK, q, 0.0),
                                       jnp.where(lane >= DIFF_QK, q, 0.0)], axis=0).astype(BF16)
        m_ref[...] = jnp.full_like(m_ref, NEG)
        l_ref[...] = jnp.zeros_like(l_ref)
        acc_ref[...] = jnp.zeros_like(acc_ref)

    def step(masked):
        s = _dot_nt(qq_ref[...], k_ref[0])
        mask = None
        if masked:
            kpos = j * tk + lax.broadcasted_iota(jnp.int32, (1, tk), 1)
            qpos = q_lo + lax.broadcasted_iota(jnp.int32, (tq, 1), 0)
            mk = kpos <= qpos
            mask = jnp.concatenate([mk, mk], axis=0)
        _online_update(s, mask, v_ref[0], m_ref, l_ref, acc_ref)

    fully_visible = (j + 1) * tk - 1 <= q_lo

    @pl.when(fully_visible)
    def _():
        step(False)

    @pl.when(jnp.logical_and(jnp.logical_not(fully_visible), j <= last_j))
    def _():
        step(True)

    @pl.when(j == pl.num_programs(3) - 1)
    def _():
        o_ref[...] = _diff_finish(acc_ref[...], l_ref[...], lam_ref, nw_ref, lam_init).astype(o_ref.dtype)


def _diff_attention(dq, kv, lam_p, norm_w, batch, seq_len, qpos0, lam_init, tq, tk):
    m = dq.shape[0]
    nq = seq_len // tq
    nk = kv.shape[1] // tk

    def kv_idx(off):
        def f(b, h, i, j):
            last = (qpos0 + i * tq + tq - 1) // tk
            return (b, jnp.minimum(j, last), off + h)
        return f

    return pl.pallas_call(
        functools.partial(_diff_kernel, tq, tk, qpos0, lam_init),
        out_shape=jax.ShapeDtypeStruct((m, DIFF_HEADS * DIFF_V), dq.dtype),
        grid=(batch, DIFF_HEADS, nq, nk),
        in_specs=[pl.BlockSpec((tq, LANES), lambda b, h, i, j: (b * nq + i, h)),
                  pl.BlockSpec((1, tk, LANES), kv_idx(0)),
                  pl.BlockSpec((1, tk, LANES), kv_idx(DIFF_HEADS)),
                  pl.BlockSpec((4, DIFF_QK), lambda b, h, i, j: (0, 0)),
                  pl.BlockSpec((1, DIFF_V), lambda b, h, i, j: (0, 0))],
        out_specs=pl.BlockSpec((tq, LANES), lambda b, h, i, j: (b * nq + i, h)),
        scratch_shapes=[pltpu.VMEM((2 * tq, LANES), BF16), pltpu.VMEM((2 * tq, 1), F32),
                        pltpu.VMEM((2 * tq, 1), F32), pltpu.VMEM((2 * tq, DIFF_V), F32)],
        compiler_params=_cparams(("arbitrary",) * 4),
        name="diff_attention",
    )(dq, kv, kv, lam_p, norm_w.reshape(1, DIFF_V))


def _diff_finish(acc, l, lam_ref, nw_ref, lam_init):
    half = acc.shape[0] // 2
    lam_p = lam_ref[...]
    lam = (jnp.exp(jnp.sum(lam_p[0:1] * lam_p[1:2], axis=-1, keepdims=True))
           - jnp.exp(jnp.sum(lam_p[2:3] * lam_p[3:4], axis=-1, keepdims=True)) + lam_init)
    o12 = acc / jnp.where(l > 0, l, 1.0)
    o = o12[0:half] - lam * o12[half:]
    y = o * lax.rsqrt(jnp.mean(o * o, axis=-1, keepdims=True) + NORM_EPS) * nw_ref[...]
    return y * (1.0 - lam_init)


def _diff_paged_kernel(pps, lam_init, pt_ref, q_ref, new_ref, lam_ref, nw_ref, *rest):
    page_refs, o_ref = rest[:pps], rest[pps]
    qbd_ref, m_ref, l_ref, acc_ref = rest[pps + 1:]
    j = pl.program_id(1)
    n_new = q_ref.shape[0]
    rows_h = 2 * n_new
    hw = DIFF_HEADS * LANES

    @pl.when(j == 0)
    def _():
        q = q_ref[...]
        lane = lax.broadcasted_iota(jnp.int32, (1, LANES), 1)
        blocks = []
        for h in range(DIFF_HEADS):
            qh = q[:, h * LANES:(h + 1) * LANES]
            q12 = jnp.concatenate([jnp.where(lane < DIFF_QK, qh, 0.0), jnp.where(lane >= DIFF_QK, qh, 0.0)], axis=0)
            blocks.append(jnp.concatenate(
                [q12 if hh == h else jnp.zeros((rows_h, LANES), F32) for hh in range(DIFF_HEADS)], axis=1))
        qbd_ref[...] = jnp.concatenate(blocks, axis=0).astype(BF16)
        m_ref[...] = jnp.full_like(m_ref, NEG)
        l_ref[...] = jnp.zeros_like(l_ref)
        acc_ref[...] = jnp.zeros_like(acc_ref)

    k = _page_slabs(page_refs, 0, DIFF_HEADS, 2 * DIFF_HEADS)
    v = _page_slabs(page_refs, DIFF_HEADS, DIFF_HEADS, 2 * DIFF_HEADS)
    _online_update(_dot_nt(qbd_ref[...], k), None, v, m_ref, l_ref, acc_ref)

    @pl.when(j == pl.num_programs(1) - 1)
    def _():
        new = new_ref[...]
        kn = _pad_rows(new[:, 0:hw], BF16_ROWS).astype(BF16)
        vn = _pad_rows(new[:, hw:2 * hw], BF16_ROWS).astype(BF16)
        t = lax.broadcasted_iota(jnp.int32, (DIFF_HEADS * rows_h, 1), 0) % n_new
        u = lax.broadcasted_iota(jnp.int32, (1, kn.shape[0]), 1)
        _online_update(_dot_nt(qbd_ref[...], kn), u <= t, vn, m_ref, l_ref, acc_ref)
        acc, l = acc_ref[...], l_ref[...]
        for h in range(DIFF_HEADS):
            r = slice(h * rows_h, (h + 1) * rows_h)
            o_ref[:, h * LANES:(h + 1) * LANES] = _diff_finish(acc[r, h * LANES:(h + 1) * LANES], l[r], lam_ref,
                                                               nw_ref, lam_init)


def _diff_attention_paged(dq, new_kv, cache_view, layer, page_table, lam_p, norm_w, lam_init, pps):
    b, n_pages = page_table.shape
    n_new = dq.shape[0] // b
    hw = DIFF_HEADS * LANES
    rows = DIFF_HEADS * 2 * n_new
    assert n_pages % pps == 0 and n_new % SUBLANES == 0
    const = lambda i, j, pt: (0, 0)
    return pl.pallas_call(
        functools.partial(_diff_paged_kernel, pps, lam_init),
        out_shape=jax.ShapeDtypeStruct(dq.shape, F32),
        grid_spec=pltpu.PrefetchScalarGridSpec(
            num_scalar_prefetch=1,
            grid=(b, n_pages // pps),
            in_specs=[pl.BlockSpec((n_new, hw), lambda i, j, pt: (i, 0)),
                      pl.BlockSpec((n_new, 2 * hw), lambda i, j, pt: (i, 0)),
                      pl.BlockSpec((4, DIFF_QK), const), pl.BlockSpec((1, DIFF_V), const)]
                     + _page_specs(layer, PAGE_SIZE * 2 * DIFF_HEADS, pps),
            out_specs=pl.BlockSpec((n_new, hw), lambda i, j, pt: (i, 0)),
            scratch_shapes=[pltpu.VMEM((rows, hw), BF16), pltpu.VMEM((rows, 1), F32), pltpu.VMEM((rows, 1), F32),
                            pltpu.VMEM((rows, hw), F32)]),
        compiler_params=_cparams(("arbitrary", "arbitrary")),
        name="diff_attention_paged",
    )(page_table, dq, new_kv, lam_p, norm_w.reshape(1, DIFF_V), *([cache_view] * pps))


def _compress_blocks(segment_rows, rows, w_ref, pe_ref, o_ref):
    half = NSA_CMP_STRIDE
    for c in range(2):
        pe_term = _dot(jnp.broadcast_to(pe_ref[c], (BF16_ROWS, pe_ref.shape[2])).astype(BF16), w_ref[c])[0:1]
        for g in range(NSA_KV_HEADS):
            lo = jnp.zeros((rows, HEAD_DIM), F32)
            hi = jnp.zeros((rows, HEAD_DIM), F32)
            for j in range(half):
                xj = segment_rows(c * NSA_KV_HEADS + g, j)
                lo = lo + _dot(xj, w_ref[c, j * HEAD_DIM:(j + 1) * HEAD_DIM, :])
                hi = hi + _dot(xj, w_ref[c, (half + j) * HEAD_DIM:(half + j + 1) * HEAD_DIM, :])
            o_ref[0, c, g] = (lo + pltpu.roll(hi, rows - 1, 0) + pe_term).astype(BF16)


def _compress_kernel(x_ref, w_ref, pe_ref, o_ref):
    tok_w = 2 * NSA_KV_HEADS * HEAD_DIM

    def segment_rows(slot, j):
        col = j * tok_w + slot * HEAD_DIM
        return x_ref[0, :, col:col + HEAD_DIM]

    _compress_blocks(segment_rows, x_ref.shape[1], w_ref, pe_ref, o_ref)


def _compress_paged_kernel(pps, n_past_seg, pt_ref, new_ref, w_ref, pe_ref, *rest):
    page_refs, o_ref, x_ref = rest[:pps], rest[pps], rest[pps + 1]
    j = pl.program_id(1)
    slots = 2 * NSA_KV_HEADS
    seg_per_page = PAGE_SIZE // NSA_CMP_STRIDE
    rows = x_ref.shape[2]

    for pp in range(0, pps, 2):
        row0 = pl.multiple_of((j * pps + pp) * seg_per_page, 2 * seg_per_page)
        for slot in range(slots):
            for t in range(NSA_CMP_STRIDE):
                pair = [page_refs[pp + q][0, 0, pl.ds(t * slots + slot, seg_per_page, stride=NSA_CMP_STRIDE * slots), :]
                        for q in range(2)]
                x_ref[slot, t, pl.ds(row0, 2 * seg_per_page), :] = jnp.concatenate(pair, axis=0).astype(BF16)

    @pl.when(j == pl.num_programs(1) - 1)
    def _():
        new = new_ref[...]
        first = lax.broadcasted_iota(jnp.int32, (rows - n_past_seg, 1), 0) == 0
        for slot in range(slots):
            for t in range(NSA_CMP_STRIDE):
                if t < new.shape[0]:
                    tail = jnp.where(first, new[t:t + 1, slot * HEAD_DIM:(slot + 1) * HEAD_DIM], 0.0)
                else:
                    tail = jnp.zeros((rows - n_past_seg, HEAD_DIM), F32)
                x_ref[slot, t, n_past_seg:rows, :] = tail.astype(BF16)
        _compress_blocks(lambda slot, t: x_ref[slot, t], rows, w_ref, pe_ref, o_ref)


def _compress_paged(cache_view, layer, page_table, new_rows, w, pe, pps):
    b, n_pages = page_table.shape
    n_new = new_rows.shape[0] // b
    slots = 2 * NSA_KV_HEADS
    seg_per_page = PAGE_SIZE // NSA_CMP_STRIDE
    n_past_seg = n_pages * seg_per_page
    rows = n_past_seg + 2 * BF16_ROWS
    assert n_new <= NSA_CMP_STRIDE and pps % 2 == 0 and n_pages % pps == 0
    return pl.pallas_call(
        functools.partial(_compress_paged_kernel, pps, n_past_seg),
        out_shape=jax.ShapeDtypeStruct((b, 2, NSA_KV_HEADS, rows, HEAD_DIM), BF16),
        grid_spec=pltpu.PrefetchScalarGridSpec(
            num_scalar_prefetch=1,
            grid=(b, n_pages // pps),
            in_specs=[pl.BlockSpec((n_new, slots * HEAD_DIM), lambda i, j, pt: (i, 0)),
                      pl.BlockSpec(w.shape, lambda i, j, pt: (0, 0, 0)),
                      pl.BlockSpec(pe.shape, lambda i, j, pt: (0, 0, 0))]
                     + _page_specs(layer, PAGE_SIZE * slots, pps),
            out_specs=pl.BlockSpec((1, 2, NSA_KV_HEADS, rows, HEAD_DIM), lambda i, j, pt: (i, 0, 0, 0, 0)),
            scratch_shapes=[pltpu.VMEM((slots, NSA_CMP_STRIDE, rows, HEAD_DIM), BF16)]),
        compiler_params=_cparams(("arbitrary", "arbitrary")),
        name="nsa_compress_paged",
    )(page_table, new_rows, w, pe, *([cache_view] * pps))


def _compress(seg, w, pe):
    b, rows, width = seg.shape
    return pl.pallas_call(
        _compress_kernel,
        out_shape=jax.ShapeDtypeStruct((b, 2, NSA_KV_HEADS, rows, HEAD_DIM), BF16),
        grid=(b,),
        in_specs=[pl.BlockSpec((1, rows, width), lambda i: (i, 0, 0)),
                  pl.BlockSpec(w.shape, lambda i: (0, 0, 0)),
                  pl.BlockSpec(pe.shape, lambda i: (0, 0, 0))],
        out_specs=pl.BlockSpec((1, 2, NSA_KV_HEADS, rows, HEAD_DIM), lambda i: (i, 0, 0, 0, 0)),
        compiler_params=_cparams(("arbitrary",)),
        name="nsa_compress",
    )(seg, w, pe)


def _nsa_constants(n_cmp_rows, n_slc_pad, tk_pad):
    ratio = NSA_SEL_BLOCK // NSA_CMP_STRIDE
    span = NSA_CMP_BLOCK // NSA_CMP_STRIDE
    n = np.arange(n_cmp_rows)[:, None]
    j = np.arange(n_slc_pad)[None, :]
    band = ((n >= ratio * j - (span - 1)) & (n <= ratio * j + ratio - 1)).astype(np.float32)
    expand_t = (np.arange(tk_pad)[:, None] // NSA_SEL_BLOCK == j).astype(np.float32)
    return band, expand_t


def _nsa_compressed(qs, qpos_rows, ck, cv):
    cmp_end = lax.broadcasted_iota(jnp.int32, (1, ck.shape[0]), 1) * NSA_CMP_STRIDE + (NSA_CMP_BLOCK - 1)
    p_c = _masked_softmax(_dot_nt(qs, ck), cmp_end <= qpos_rows)
    return p_c, _dot(p_c.astype(BF16), cv)


def _nsa_block_scores(p_c, qpos, band_ref):
    tq = qpos.shape[0]
    imp = p_c[0:tq]
    for n in range(1, p_c.shape[0] // tq):
        imp = imp + p_c[n * tq:(n + 1) * tq]
    p_slc = _dot3_rhs_exact(_pad_rows(imp, BF16_ROWS), band_ref[...])[0:tq]
    blk = lax.broadcasted_iota(jnp.int32, (1, band_ref.shape[1]), 1)
    cur = qpos // NSA_SEL_BLOCK
    valid = blk * NSA_SEL_BLOCK <= qpos
    forced = (blk == 0) | (blk == cur) | (blk == cur - 1)
    return jnp.where(valid, p_slc + jnp.where(forced, NSA_FORCE_BONUS, 0.0), NEG)


def _nsa_select(score, score_ref, n_slc):
    score_t = score.T
    score_ref[...] = score_t
    blk_t = lax.broadcasted_iota(jnp.int32, (score_t.shape[0], 1), 0)

    def count(jp, cnt):
        row = score_ref[pl.ds(jp, 1), :]
        beats = (row > score_t) | ((row == score_t) & (jp < blk_t))
        return cnt + jnp.where(beats, 1.0, 0.0)

    cnt = lax.fori_loop(0, n_slc, count, jnp.zeros(score_t.shape, F32))
    return jnp.where((cnt < NSA_TOP_N) & (score_t > 0.5 * NEG), 1.0, 0.0).T


def _nsa_window(qs, qpos_rows, kw, vw, first_pos):
    wpos = first_pos + lax.broadcasted_iota(jnp.int32, (1, kw.shape[0]), 1)
    mask_w = (wpos <= qpos_rows) & (wpos > qpos_rows - NSA_WINDOW) & (wpos >= 0)
    return _dot(_masked_softmax(_dot_nt(qs, kw), mask_w).astype(BF16), vw)


def _nsa_kernel(tq, tk, qpos0, n_slc, q_ref, sm_ref, ck_ref, cv_ref, sk_ref, sv_ref, wk_ref, wv_ref,
                band_ref, exp_ref, o_ref, score_ref, m_ref, l_ref, acc_ref):
    i = pl.program_id(1)
    hpg = NSA_HEADS // NSA_KV_HEADS
    rows = hpg * tq
    nsp = band_ref.shape[1]
    tqp = score_ref.shape[1]
    q_lo = qpos0 + i * tq
    qpos = q_lo + lax.broadcasted_iota(jnp.int32, (tq, 1), 0)
    qpos_rows = jnp.concatenate([qpos] * hpg, axis=0)
    gates = _sigmoid(sm_ref[:, SMALL_GATE:SMALL_GATE + 3 * NSA_HEADS])

    for g in range(NSA_KV_HEADS):
        q = q_ref[:, g * hpg * LANES:(g + 1) * hpg * LANES]
        qs = jnp.concatenate([q[:, n * LANES:(n + 1) * LANES] for n in range(hpg)], axis=0).astype(BF16)

        p_c, o_c = _nsa_compressed(qs, qpos_rows, ck_ref[0, 0, g], cv_ref[0, 0, g])
        score = _nsa_block_scores(p_c, qpos, band_ref)
        if tqp > tq:
            score = jnp.concatenate([score, jnp.full((tqp - tq, nsp), NEG, F32)], axis=0)
        sel = _nsa_select(score, score_ref, n_slc)[0:max(tq, BF16_ROWS)].astype(BF16)

        m_ref[...] = jnp.full_like(m_ref, NEG)
        l_ref[...] = jnp.zeros_like(l_ref)
        acc_ref[...] = jnp.zeros_like(acc_ref)

        def key_tile(kt, carry):
            start = pl.multiple_of(kt * tk, tk)
            k = sk_ref[0, pl.ds(start, tk), g * LANES:(g + 1) * LANES]
            v = sv_ref[0, pl.ds(start, tk), g * LANES:(g + 1) * LANES]
            picked = _dot_nt(sel, exp_ref[pl.ds(start, tk), :])[0:tq] > 0.5
            kpos = kt * tk + lax.broadcasted_iota(jnp.int32, (1, tk), 1)
            mk = picked & (kpos <= qpos)
            mask = jnp.concatenate([mk] * hpg, axis=0)
            _online_update(_dot_nt(qs, k), mask, v, m_ref, l_ref, acc_ref)
            return carry

        lax.fori_loop(0, (q_lo + tq - 1) // tk + 1, key_tile, 0)
        l = l_ref[...]
        o_s = acc_ref[...] / jnp.where(l > 0, l, 1.0)

        wrows = -(-(tq + NSA_WINDOW) // BF16_ROWS) * BF16_ROWS
        wstart = pl.multiple_of(i * tq, tq)
        o_w = _nsa_window(qs, qpos_rows, wk_ref[0, pl.ds(wstart, wrows), g * LANES:(g + 1) * LANES],
                          wv_ref[0, pl.ds(wstart, wrows), g * LANES:(g + 1) * LANES], q_lo - NSA_WINDOW)

        for n in range(hpg):
            h = g * hpg + n
            r = slice(n * tq, (n + 1) * tq)
            o = (gates[:, 3 * h:3 * h + 1] * o_c[r] + gates[:, 3 * h + 1:3 * h + 2] * o_s[r]
                 + gates[:, 3 * h + 2:3 * h + 3] * o_w[r])
            o_ref[:, h * LANES:(h + 1) * LANES] = o.astype(o_ref.dtype)


def _nsa_attention(qn, z, comp, slc_buf, win_all, batch, seq_len, t_valid, qpos0, tq, tk):
    m = qn.shape[0]
    nq = seq_len // tq
    tk_pad = slc_buf.shape[1]
    n_cmp_rows = comp.shape[3]
    n_slc = -(-t_valid // NSA_SEL_BLOCK)
    nsp = -(-n_slc // LANES) * LANES
    tqp = -(-tq // LANES) * LANES
    band, expand_t = _nsa_constants(n_cmp_rows, nsp, tk_pad)
    hw = NSA_HEADS * HEAD_DIM
    kvw = NSA_KV_HEADS * HEAD_DIM
    rows = (NSA_HEADS // NSA_KV_HEADS) * tq
    wlen = win_all.shape[1]
    return pl.pallas_call(
        functools.partial(_nsa_kernel, tq, tk, qpos0, n_slc),
        out_shape=jax.ShapeDtypeStruct((m, hw), qn.dtype),
        grid=(batch, nq),
        in_specs=[pl.BlockSpec((tq, hw), lambda b, i: (b * nq + i, 0)),
                  pl.BlockSpec((tq, LANES), lambda b, i: (b * nq + i, Z_SMALL // LANES)),
                  pl.BlockSpec((1, 1, NSA_KV_HEADS, n_cmp_rows, HEAD_DIM), lambda b, i: (b, 0, 0, 0, 0)),
                  pl.BlockSpec((1, 1, NSA_KV_HEADS, n_cmp_rows, HEAD_DIM), lambda b, i: (b, 1, 0, 0, 0)),
                  pl.BlockSpec((1, tk_pad, kvw), lambda b, i: (b, 0, 0)),
                  pl.BlockSpec((1, tk_pad, kvw), lambda b, i: (b, 0, 1)),
                  pl.BlockSpec((1, wlen, kvw), lambda b, i: (b, 0, 0)),
                  pl.BlockSpec((1, wlen, kvw), lambda b, i: (b, 0, 1)),
                  pl.BlockSpec((n_cmp_rows, nsp), lambda b, i: (0, 0)),
                  pl.BlockSpec((tk_pad, nsp), lambda b, i: (0, 0))],
        out_specs=pl.BlockSpec((tq, hw), lambda b, i: (b * nq + i, 0)),
        scratch_shapes=[pltpu.VMEM((nsp, tqp), F32), pltpu.VMEM((rows, 1), F32), pltpu.VMEM((rows, 1), F32),
                        pltpu.VMEM((rows, HEAD_DIM), F32)],
        compiler_params=_cparams(("arbitrary", "arbitrary")),
        name="nsa_attention",
    )(qn, z, comp, comp, slc_buf, slc_buf, win_all, win_all, jnp.asarray(band, BF16), jnp.asarray(expand_t, BF16))


def _nsa_paged_kernel(pps, qpos0, n_slc, pt_ref, q_ref, sm_ref, ck_ref, cv_ref, new_ref, wk_ref, wv_ref, band_ref,
                      *rest):
    page_refs, o_ref = rest[:pps], rest[pps]
    score_ref, qbd_ref, sel_ref, m_ref, l_ref, acc_ref, oc_ref, ow_ref = rest[pps + 1:]
    j = pl.program_id(1)
    n_new = q_ref.shape[0]
    groups = NSA_KV_HEADS
    hpg = NSA_HEADS // groups
    rows_g = hpg * n_new
    gw = groups * LANES
    nsp = band_ref.shape[1]
    qpos = qpos0 + lax.broadcasted_iota(jnp.int32, (n_new, 1), 0)

    @pl.when(j == 0)
    def _():
        q = q_ref[...]
        qpos_rows = jnp.concatenate([qpos] * hpg, axis=0)
        scores, blocks = [], []
        for g in range(groups):
            qf = jnp.concatenate([q[:, (g * hpg + n) * LANES:(g * hpg + n + 1) * LANES] for n in range(hpg)], axis=0)
            qs = qf.astype(BF16)
            p_c, o_c = _nsa_compressed(qs, qpos_rows, ck_ref[0, 0, g], cv_ref[0, 0, g])
            oc_ref[g * rows_g:(g + 1) * rows_g] = o_c
            scores.append(_nsa_block_scores(p_c, qpos, band_ref))
            ow_ref[g * rows_g:(g + 1) * rows_g] = _nsa_window(
                qs, qpos_rows, wk_ref[0, :, g * LANES:(g + 1) * LANES], wv_ref[0, :, g * LANES:(g + 1) * LANES],
                qpos0 - NSA_WINDOW)
            blocks.append(jnp.concatenate(
                [qf if gg == g else jnp.zeros((rows_g, LANES), F32) for gg in range(groups)], axis=1))
        qbd_ref[...] = jnp.concatenate(blocks, axis=0).astype(BF16)
        pad = jnp.full((score_ref.shape[1] - groups * n_new, nsp), NEG, F32)
        sel = _nsa_select(jnp.concatenate(scores + [pad], axis=0), score_ref, n_slc)
        sel_ref[...] = jnp.concatenate(
            [sel[g * n_new:(g + 1) * n_new] for g in range(groups) for _ in range(hpg)], axis=0).astype(BF16)
        m_ref[...] = jnp.full_like(m_ref, NEG)
        l_ref[...] = jnp.zeros_like(l_ref)
        acc_ref[...] = jnp.zeros_like(acc_ref)

    k = _page_slabs(page_refs, 0, groups, 2 * groups)
    v = _page_slabs(page_refs, groups, groups, 2 * groups)
    n_keys = pps * PAGE_SIZE
    key_blk = (j * n_keys + lax.broadcasted_iota(jnp.int32, (1, n_keys), 1)) // NSA_SEL_BLOCK
    expand = jnp.where(lax.broadcasted_iota(jnp.int32, (nsp, 1), 0) == key_blk, 1.0, 0.0).astype(BF16)
    picked = _dot(sel_ref[...], expand) > 0.5
    _online_update(_dot_nt(qbd_ref[...], k), picked, v, m_ref, l_ref, acc_ref)

    @pl.when(j == pl.num_programs(1) - 1)
    def _():
        new = new_ref[...]
        kn = _pad_rows(new[:, 0:gw], BF16_ROWS).astype(BF16)
        vn = _pad_rows(new[:, gw:2 * gw], BF16_ROWS).astype(BF16)
        new_blk = qpos0 // NSA_SEL_BLOCK
        t = lax.broadcasted_iota(jnp.int32, (groups * rows_g, 1), 0) % n_new
        u = lax.broadcasted_iota(jnp.int32, (1, kn.shape[0]), 1)
        mask = (u <= t) & (sel_ref[:, new_blk:new_blk + 1] > 0.5)
        _online_update(_dot_nt(qbd_ref[...], kn), mask, vn, m_ref, l_ref, acc_ref)
        l = l_ref[...]
        o_sel = acc_ref[...] / jnp.where(l > 0, l, 1.0)
        gates = _sigmoid(sm_ref[:, SMALL_GATE:SMALL_GATE + 3 * NSA_HEADS])
        for g in range(groups):
            for n in range(hpg):
                h = g * hpg + n
                r = slice(g * rows_g + n * n_new, g * rows_g + (n + 1) * n_new)
                o_ref[:, h * LANES:(h + 1) * LANES] = (
                    gates[:, 3 * h:3 * h + 1] * oc_ref[r] + gates[:, 3 * h + 1:3 * h + 2] * o_sel[r, g * LANES:(g + 1) * LANES]
                    + gates[:, 3 * h + 2:3 * h + 3] * ow_ref[r])


def _nsa_attention_paged(qn, z, comp, new_kv, win_all, cache_view, layer, page_table, qpos0, pps):
    b, n_pages = page_table.shape
    n_new = qn.shape[0] // b
    groups = NSA_KV_HEADS
    hw = NSA_HEADS * HEAD_DIM
    gw = groups * HEAD_DIM
    rows = NSA_HEADS * n_new
    n_cmp_rows = comp.shape[3]
    n_slc = -(-(qpos0 + n_new) // NSA_SEL_BLOCK)
    nsp = -(-n_slc // LANES) * LANES
    wlen = win_all.shape[1]
    assert n_pages % pps == 0 and qpos0 == n_pages * PAGE_SIZE and n_new <= NSA_SEL_BLOCK
    assert qpos0 % NSA_SEL_BLOCK == 0 and groups * n_new <= LANES and n_new % SUBLANES == 0
    band, _ = _nsa_constants(n_cmp_rows, nsp, SUBLANES)
    const = lambda i, j, pt: (0, 0)
    return pl.pallas_call(
        functools.partial(_nsa_paged_kernel, pps, qpos0, n_slc),
        out_shape=jax.ShapeDtypeStruct(qn.shape, F32),
        grid_spec=pltpu.PrefetchScalarGridSpec(
            num_scalar_prefetch=1,
            grid=(b, n_pages // pps),
            in_specs=[pl.BlockSpec((n_new, hw), lambda i, j, pt: (i, 0)),
                      pl.BlockSpec((n_new, LANES), lambda i, j, pt: (i, Z_SMALL // LANES)),
                      pl.BlockSpec((1, 1, groups, n_cmp_rows, HEAD_DIM), lambda i, j, pt: (i, 0, 0, 0, 0)),
                      pl.BlockSpec((1, 1, groups, n_cmp_rows, HEAD_DIM), lambda i, j, pt: (i, 1, 0, 0, 0)),
                      pl.BlockSpec((n_new, 2 * gw), lambda i, j, pt: (i, 0)),
                      pl.BlockSpec((1, wlen, gw), lambda i, j, pt: (i, 0, 0)),
                      pl.BlockSpec((1, wlen, gw), lambda i, j, pt: (i, 0, 1)),
                      pl.BlockSpec((n_cmp_rows, nsp), const)]
                     + _page_specs(layer, PAGE_SIZE * 2 * groups, pps),
            out_specs=pl.BlockSpec((n_new, hw), lambda i, j, pt: (i, 0)),
            scratch_shapes=[pltpu.VMEM((nsp, LANES), F32), pltpu.VMEM((rows, gw), BF16), pltpu.VMEM((rows, nsp), BF16),
                            pltpu.VMEM((rows, 1), F32), pltpu.VMEM((rows, 1), F32), pltpu.VMEM((rows, gw), F32),
                            pltpu.VMEM((rows, HEAD_DIM), F32), pltpu.VMEM((rows, HEAD_DIM), F32)]),
        compiler_params=_cparams(("arbitrary", "arbitrary")),
        name="nsa_attention_paged",
    )(page_table, qn, z, comp, comp, new_kv, win_all, win_all, jnp.asarray(band, BF16), *([cache_view] * pps))


def _out_proj_kernel(x_ref, og_ref, on_ref, od_ref, w_ref, nw_ref, g_ref, o_ref):
    a = jnp.concatenate([og_ref[...].astype(BF16), on_ref[...].astype(BF16), od_ref[...].astype(BF16)], axis=-1)
    mix = _dot(a, w_ref[...])
    y = mix * lax.rsqrt(jnp.mean(mix * mix, axis=-1, keepdims=True) + NORM_EPS) * nw_ref[...]
    o_ref[...] = x_ref[...] + g_ref[0] * y


def _out_proj(x2, o_gla, o_nsa, o_d, w, nw, gate, seq_len, tm):
    m, d = x2.shape
    row = lambda width: pl.BlockSpec((tm, width), lambda i: (i, 0))
    return pl.pallas_call(
        _out_proj_kernel,
        out_shape=jax.ShapeDtypeStruct((m, d), F32),
        grid=(m // tm,),
        in_specs=[row(d), row(o_gla.shape[1]), row(o_nsa.shape[1]), row(o_d.shape[1]),
                  pl.BlockSpec(w.shape, lambda i: (0, 0)), pl.BlockSpec((1, d), lambda i: (0, 0)),
                  _mod_spec(seq_len, tm, d)],
        out_specs=row(d),
        compiler_params=_cparams(("arbitrary",)),
        name="out_proj",
    )(x2, o_gla, o_nsa, o_d, w, nw, _expand_mod(gate, seq_len, tm))


def _ffn_kernel(x_ref, nw2_ref, sc_ref, sh_ref, wg_ref, wu_ref, wo_ref, nw3_ref, g_ref, o_ref, h_ref, acc_ref):
    j = pl.program_id(1)

    @pl.when(j == 0)
    def _():
        h_ref[...] = _norm_mod(x_ref[...], nw2_ref[...], sc_ref[0], sh_ref[0]).astype(BF16)
        acc_ref[...] = jnp.zeros_like(acc_ref)

    h = h_ref[...]
    gate = _dot(h, wg_ref[...])
    up = _dot(h, wu_ref[...])
    acc_ref[...] += _dot((gate * _sigmoid(gate) * up).astype(BF16), wo_ref[...])

    @pl.when(j == pl.num_programs(1) - 1)
    def _():
        f = acc_ref[...]
        y = f * lax.rsqrt(jnp.mean(f * f, axis=-1, keepdims=True) + NORM_EPS) * nw3_ref[...]
        o_ref[...] = x_ref[...] + g_ref[0] * y


def _ffn(x2, nw2, sc, sh, w_in, w_out, nw3, gate, seq_len, tm, tf):
    m, d = x2.shape
    d_ff = w_out.shape[0]
    nf = d_ff // tf
    row = pl.BlockSpec((tm, d), lambda i, j: (i, 0))
    vec = pl.BlockSpec((1, d), lambda i, j: (0, 0))
    return pl.pallas_call(
        _ffn_kernel,
        out_shape=jax.ShapeDtypeStruct((m, d), F32),
        grid=(m // tm, nf),
        in_specs=[row, vec, _mod_spec(seq_len, tm, d), _mod_spec(seq_len, tm, d),
                  pl.BlockSpec((d, tf), lambda i, j: (0, j)),
                  pl.BlockSpec((d, tf), lambda i, j: (0, nf + j)),
                  pl.BlockSpec((tf, d), lambda i, j: (j, 0)),
                  vec, _mod_spec(seq_len, tm, d)],
        out_specs=row,
        scratch_shapes=[pltpu.VMEM((tm, d), BF16), pltpu.VMEM((tm, d), F32)],
        compiler_params=_cparams(("arbitrary", "arbitrary")),
        name="ffn",
    )(x2, nw2, _expand_mod(sc, seq_len, tm), _expand_mod(sh, seq_len, tm), w_in, w_in, w_out, nw3,
      _expand_mod(gate, seq_len, tm))


def _pick(n, target):
    if n <= target:
        return n
    for t in range(target, 7, -1):
        if n % t == 0 and t % SUBLANES == 0:
            return t
    return n


def _permute_w_in(w):
    d = w.shape[0]
    hk, hv = GLA_HEADS * GLA_DK, GLA_HEADS * GLA_DV
    kvw = NSA_KV_HEADS * HEAD_DIM
    sizes = [hk, hk, hv, hv, GLA_GATE_RANK, NSA_HEADS * HEAD_DIM, kvw, kvw, kvw, kvw, kvw, kvw, NSA_HEADS * 3,
             DIFF_HEADS * 2 * DIFF_QK, DIFF_HEADS * 2 * DIFF_QK, DIFF_HEADS * DIFF_V]
    offs = np.concatenate([[0], np.cumsum(sizes)])
    part = lambda k: w[:, offs[k]:offs[k + 1]]
    order = [5, 0, 1, 2, 3, 6, 7, 8, 9, 10, 11, 13, 14, 15, 4, 12]
    cols = [part(k) for k in order]
    pad = Z_WIDTH - int(offs[-1])
    return jnp.concatenate(cols + [jnp.zeros((d, pad), w.dtype)], axis=1).astype(BF16)


def _layer(x, mod, qpos0, past, win_buf, gla_s0, lp, lam_init):
    b, seq_len, d = x.shape
    m = b * seq_len
    x2 = x.reshape(m, d)
    sh1, sc1, g1, sh2, sc2, g2 = jnp.split(mod, 6, axis=-1)
    tm = _pick(seq_len, 512) if seq_len >= 128 else m
    z = _in_proj(x2, lp['norm'][0:1], sc1, sh1, lp['w_in'], seq_len, tm, _pick(Z_WIDTH, 640))

    pos = qpos0 + jnp.arange(seq_len)
    tab = _rope_tables(pos)
    if seq_len % tm != 0:
        tab = jnp.tile(tab, (m // seq_len, 1))
    act_dt = BF16 if seq_len % BF16_ROWS == 0 else F32
    (qn, cmp_f, cmp_b, slc_f, slc_b, win_f, win_b, dq, dkv_f, dkv_b) = _rope_split(z, tab, seq_len, tm, act_dt)

    chunk = 64 if seq_len % 64 == 0 else seq_len
    o_gla, gla_state = _gla(z, lp['gla_gate_w'], lp['gla_gate_b'], lp['gla_norm'], gla_s0, b, seq_len, chunk, act_dt)

    kvw = 2 * NSA_KV_HEADS * HEAD_DIM
    dw = 2 * DIFF_HEADS * DIFF_V
    win_len = NSA_WINDOW + seq_len
    win_pad = jnp.zeros((b, -win_len % BF16_ROWS, kvw), BF16)
    win_all = jnp.concatenate([win_buf.astype(BF16), win_b.reshape(b, seq_len, kvw), win_pad], axis=1)

    if past is None:
        tk_attn = _pick(seq_len, 512)
        seg = cmp_b.reshape(b, seq_len // NSA_CMP_STRIDE, NSA_CMP_STRIDE * kvw)
        comp = _compress(seg, lp['nsa_cmp_w'], lp['nsa_cmp_pe'])
        o_nsa = _nsa_attention(qn, z, comp, slc_b.reshape(b, seq_len, kvw), win_all, b, seq_len, seq_len, qpos0,
                               _pick(seq_len, 128), tk_attn)
        o_d = _diff_attention(dq, dkv_b.reshape(b, seq_len, dw), lp['diff_lambda'], lp['diff_norm'], b, seq_len,
                              qpos0, lam_init, _pick(seq_len, 512), tk_attn)
    else:
        pt, layer = past['page_table'], past['layer']
        pps = PAGES_PER_STEP if pt.shape[1] % PAGES_PER_STEP == 0 else 2
        comp = _compress_paged(past['cmp'], layer, pt, cmp_f, lp['nsa_cmp_w'], lp['nsa_cmp_pe'], pps)
        o_nsa = _nsa_attention_paged(qn, z, comp, slc_f, win_all, past['slc'], layer, pt, qpos0, pps)
        o_d = _diff_attention_paged(dq, dkv_f, past['diff'], layer, pt, lp['diff_lambda'], lp['diff_norm'],
                                    lam_init, pps)

    x1 = _out_proj(x2, o_gla, o_nsa, o_d, lp['w_out'], lp['norm'][1:2], g1, seq_len, tm)
    d_ff = lp['ffn_w_out'].shape[0]
    x2n = _ffn(x1, lp['norm'][2:3], sc2, sh2, lp['ffn_w_in'], lp['ffn_w_out'], lp['norm'][3:4], g2, seq_len, tm,
               _pick(d_ff, 512) if d_ff % LANES == 0 else d_ff)

    return (x2n.reshape(b, seq_len, d), cmp_f, slc_f, dkv_f, win_f, gla_state)


def kernel(x_prompt, x_sample, c_prompt, c_sample, cache_nsa_cmp_kv, cache_nsa_slc_kv, cache_diff_kv,
           state_nsa_win_kv, state_gla, page_table, ada_w, ada_b, norm_w, w_in, gla_gate_w, gla_gate_b,
           gla_norm, nsa_cmp_pe, nsa_cmp_w, diff_lambda, diff_norm, w_out, ffn_w_in, ffn_w_out):
    depth = ada_w.shape[0]
    bp, lp_len, d = x_prompt.shape
    bs, ls_len, _ = x_sample.shape
    n_pages = page_table.shape[1]
    past_len = n_pages * PAGE_SIZE
    wb = state_nsa_win_kv.shape[2]
    kvw = 2 * NSA_KV_HEADS * HEAD_DIM
    hk = GLA_HEADS * GLA_DK

    n_c = bp + bs
    rows = -(-n_c // SUBLANES) * SUBLANES
    c_all = jnp.concatenate([c_prompt, c_sample, jnp.zeros((rows - n_c, d), F32)], axis=0)
    mod_all = _modulation(c_all, ada_w, ada_b)

    xp, xs = x_prompt, x_sample
    outs = [[] for _ in range(10)]
    for l in range(depth):
        lp = {'norm': norm_w[l], 'w_in': _permute_w_in(w_in[l]),
              'gla_gate_w': gla_gate_w[l], 'gla_gate_b': gla_gate_b[l], 'gla_norm': gla_norm[l],
              'nsa_cmp_pe': nsa_cmp_pe[l].reshape(2, 1, NSA_CMP_BLOCK * HEAD_DIM),
              'nsa_cmp_w': nsa_cmp_w[l].astype(BF16), 'diff_lambda': diff_lambda[l], 'diff_norm': diff_norm[l],
              'w_out': w_out[l].astype(BF16), 'ffn_w_in': ffn_w_in[l].astype(BF16),
              'ffn_w_out': ffn_w_out[l].astype(BF16)}
        lam_init = 0.8 - 0.6 * math.exp(-0.3 * l)

        win0 = jnp.zeros((bp, NSA_WINDOW, kvw), BF16)
        gla0 = jnp.zeros((bp, hk, GLA_DV), F32)
        xp, cmp_p, slc_p, diff_p, win_p, gla_p = _layer(xp, mod_all[l, :bp], 0, None, win0, gla0, lp, lam_init)

        past = {'cmp': _page_view(cache_nsa_cmp_kv), 'slc': _page_view(cache_nsa_slc_kv),
                'diff': _page_view(cache_diff_kv), 'layer': l, 'page_table': page_table}
        win_prev = state_nsa_win_kv[l].reshape(bs, wb, kvw)
        win_in = jnp.concatenate([jnp.zeros((bs, NSA_WINDOW - wb, kvw), F32), win_prev], axis=1)
        xs, cmp_s, slc_s, diff_s, win_s, gla_s = _layer(xs, mod_all[l, bp:bp + bs], past_len, past, win_in,
                                                        state_gla[l].reshape(bs, hk, GLA_DV), lp, lam_init)

        kv_shape = lambda b, n: (b, n, 2, NSA_KV_HEADS, HEAD_DIM)
        outs[0].append(cmp_p.reshape(kv_shape(bp, lp_len)))
        outs[1].append(cmp_s.reshape(kv_shape(bs, ls_len)))
        outs[2].append(slc_p.reshape(kv_shape(bp, lp_len)))
        outs[3].append(slc_s.reshape(kv_shape(bs, ls_len)))
        outs[4].append(diff_p.reshape(bp, lp_len, 2, DIFF_HEADS, DIFF_V))
        outs[5].append(diff_s.reshape(bs, ls_len, 2, DIFF_HEADS, DIFF_V))
        win_p3 = win_p.reshape(bp, lp_len, kvw)
        win_state_p = jnp.concatenate([jnp.zeros((bp, wb, kvw), F32), win_p3], axis=1)[:, -wb:]
        win_state_s = jnp.concatenate([win_prev, win_s.reshape(bs, ls_len, kvw)], axis=1)[:, -wb:]
        outs[6].append(win_state_p.reshape(kv_shape(bp, wb)))
        outs[7].append(win_state_s.reshape(kv_shape(bs, wb)))
        outs[8].append(gla_p.reshape(bp, GLA_HEADS, GLA_DK, GLA_DV))
        outs[9].append(gla_s.reshape(bs, GLA_HEADS, GLA_DK, GLA_DV))

    return (xp, xs) + tuple(jnp.stack(o) for o in outs)
```

```python
import functools
import math

import numpy as np
import jax
import jax.numpy as jnp
from jax import lax
from jax.experimental import pallas as pl
from jax.experimental.pallas import tpu as pltpu

F32 = jnp.float32
BF16 = jnp.bfloat16

PAGE_SIZE = 128
HEAD_DIM = 128
GLA_HEADS, GLA_DK, GLA_DV, GLA_GATE_RANK, GLA_TAU = 4, 64, 128, 16, 16.0
NSA_HEADS, NSA_KV_HEADS = 8, 2
NSA_CMP_BLOCK, NSA_CMP_STRIDE, NSA_SEL_BLOCK, NSA_TOP_N, NSA_WINDOW = 32, 16, 64, 16, 512
NSA_FORCE_BONUS = 1.0e4
DIFF_HEADS, DIFF_QK, DIFF_V = 4, 64, 128
ROPE_THETA = 10000.0
NORM_EPS = 1e-6
NEG = -1.0e30

LANES = 128
SUBLANES = 8
BF16_ROWS = 16
VMEM_LIMIT_BYTES = 56 * 1024 * 1024
PAGES_PER_STEP = 8

Z_NQ, Z_GQ, Z_GK, Z_GV, Z_GR = 0, 1024, 1280, 1536, 2048
Z_CMP, Z_SLC, Z_WIN, Z_DQ, Z_DK, Z_DV, Z_WIDTH = 2560, 3072, 3584, 4096, 4608, 5120, 5632
SMALL_GA, SMALL_GATE = 0, 16


def _cparams(sem):
    return pltpu.CompilerParams(dimension_semantics=sem, vmem_limit_bytes=VMEM_LIMIT_BYTES)


def _dot(a, b):
    return jnp.dot(a, b, preferred_element_type=F32)


def _dot_nt(a, b):
    return lax.dot_general(a, b, (((1,), (1,)), ((), ())), preferred_element_type=F32)


def _dot_tn(a, b):
    return lax.dot_general(a, b, (((0,), (0,)), ((), ())), preferred_element_type=F32)


def _split3(a):
    hi = a.astype(BF16)
    r = a - hi.astype(F32)
    mid = r.astype(BF16)
    lo = (r - mid.astype(F32)).astype(BF16)
    return hi, mid, lo


def _dot3_rhs_exact(a, b):
    hi, mid, lo = _split3(a)
    return _dot(hi, b) + _dot(mid, b) + _dot(lo, b)


def _dot3_lhs_exact(a, b):
    hi, mid, lo = _split3(b)
    return _dot(a, hi) + _dot(a, mid) + _dot(a, lo)


def _sigmoid(x):
    return 1.0 / (1.0 + jnp.exp(-x))


def _masked_softmax(s, mask):
    s = jnp.where(mask, s, NEG)
    m = jnp.max(s, axis=-1, keepdims=True)
    e = jnp.where(mask, jnp.exp(s - m), 0.0)
    den = jnp.sum(e, axis=-1, keepdims=True)
    return e / jnp.where(den > 0, den, 1.0)


def _mod_kernel(c_ref, w_ref, b_ref, o_ref):
    c = c_ref[...]
    a = (c * _sigmoid(c)).astype(BF16)
    o_ref[0] = _dot(a, w_ref[0].astype(BF16)) + b_ref[0]


def _modulation(c_all, ada_w, ada_b):
    depth, d, n6 = ada_w.shape
    rows = c_all.shape[0]
    tn = max(t for t in range(LANES, min(1024, n6) + 1, LANES) if n6 % t == 0)
    return pl.pallas_call(
        _mod_kernel,
        out_shape=jax.ShapeDtypeStruct((depth, rows, n6), F32),
        grid=(depth, n6 // tn),
        in_specs=[pl.BlockSpec((rows, d), lambda l, j: (0, 0)),
                  pl.BlockSpec((1, d, tn), lambda l, j: (l, 0, j)),
                  pl.BlockSpec((1, 1, tn), lambda l, j: (l, 0, j))],
        out_specs=pl.BlockSpec((1, rows, tn), lambda l, j: (l, 0, j)),
        compiler_params=_cparams(("arbitrary", "arbitrary")),
        name="adaln_modulation",
    )(c_all, ada_w, ada_b.reshape(depth, 1, n6))


def _norm_mod(x, nw, sc, sh):
    ms = jnp.mean(x * x, axis=-1, keepdims=True)
    return x * lax.rsqrt(ms + NORM_EPS) * nw * (1.0 + sc) + sh


def _in_proj_kernel(x_ref, nw_ref, sc_ref, sh_ref, w_ref, ws_ref, o_ref, os_ref, h_ref):
    @pl.when(pl.program_id(1) == 0)
    def _():
        h = _norm_mod(x_ref[...], nw_ref[...], sc_ref[0], sh_ref[0]).astype(BF16)
        h_ref[...] = h
        os_ref[...] = _dot(h, ws_ref[...])

    o_ref[...] = _dot(h_ref[...], w_ref[...])


def _mod_spec(seq_len, tm, d):
    if seq_len % tm == 0:
        per = seq_len // tm
        return pl.BlockSpec((1, 1, d), lambda i, *_: (i // per, 0, 0))
    return pl.BlockSpec((1, tm, d), lambda i, *_: (0, i, 0))


def _expand_mod(m, seq_len, tm):
    if seq_len % tm == 0:
        return m[:, None, :]
    return jnp.repeat(m, seq_len, axis=0)[None]


def _in_proj(x2, nw, sc, sh, w, w_small, seq_len, tm, tn):
    m, d = x2.shape
    n = w.shape[1]
    return pl.pallas_call(
        _in_proj_kernel,
        out_shape=[jax.ShapeDtypeStruct((m, n), F32), jax.ShapeDtypeStruct((m, LANES), F32)],
        grid=(m // tm, n // tn),
        in_specs=[pl.BlockSpec((tm, d), lambda i, j: (i, 0)),
                  pl.BlockSpec((1, d), lambda i, j: (0, 0)),
                  _mod_spec(seq_len, tm, d), _mod_spec(seq_len, tm, d),
                  pl.BlockSpec((d, tn), lambda i, j: (0, j)),
                  pl.BlockSpec((d, LANES), lambda i, j: (0, 0))],
        out_specs=[pl.BlockSpec((tm, tn), lambda i, j: (i, j)), pl.BlockSpec((tm, LANES), lambda i, j: (i, 0))],
        scratch_shapes=[pltpu.VMEM((tm, d), BF16)],
        compiler_params=_cparams(("arbitrary", "arbitrary")),
        name="in_proj",
    )(x2, nw, _expand_mod(sc, seq_len, tm), _expand_mod(sh, seq_len, tm), w, w_small)


def _rope_tables(pos):
    pos = pos.astype(F32)[:, None]

    def tab(half, reps):
        inv = ROPE_THETA ** (-jnp.arange(half, dtype=F32) / half)
        ang = pos * inv[None, :]
        c, s = jnp.cos(ang), jnp.sin(ang)
        return jnp.tile(jnp.concatenate([c, c], -1), (1, reps)), jnp.tile(jnp.concatenate([-s, s], -1), (1, reps))

    c128, s128 = tab(HEAD_DIM // 2, 1)
    c64, s64 = tab(DIFF_QK // 2, 2)
    return jnp.concatenate([c128, s128, c64, s64], axis=-1)


def _rope128(x, cos, sin):
    return x * cos + pltpu.roll(x, 64, 1) * sin


def _rope64(x, cos, sin, first_half):
    partner = jnp.where(first_half, pltpu.roll(x, 96, 1), pltpu.roll(x, 32, 1))
    return x * cos + partner * sin


def _rope_kernel(nq_ref, cmp_ref, slc_ref, win_ref, dq_ref, dk_ref, dv_ref, tab_ref,
                 qn_o, cmp_o, cmpb_o, slc_o, slcb_o, win_o, winb_o, dq_o, dkv_o, dkvb_o, *vt_outs):
    if vt_outs:
        slc_vt_o, diff_vt_o = vt_outs
        for g in range(NSA_KV_HEADS):
            slc_vt_o[0, g, 0] = slc_ref[:, (NSA_KV_HEADS + g) * LANES:(NSA_KV_HEADS + g + 1) * LANES].T.astype(BF16)
        for h in range(DIFF_HEADS):
            diff_vt_o[0, h, 0] = dv_ref[:, h * LANES:(h + 1) * LANES].T.astype(BF16)
    tab = tab_ref[...]
    c128, s128, c64, s64 = (tab[:, i * LANES:(i + 1) * LANES] for i in range(4))
    lane = lax.broadcasted_iota(jnp.int32, (1, LANES), 1)
    first_half = (lane % DIFF_QK) < (DIFF_QK // 2)
    nsa_scale = HEAD_DIM ** -0.5
    diff_scale = DIFF_QK ** -0.5

    for h in range(NSA_HEADS):
        sl = slice(h * LANES, (h + 1) * LANES)
        qn_o[:, sl] = (_rope128(nq_ref[:, sl], c128, s128) * nsa_scale).astype(qn_o.dtype)

    for src, dst, dstb in ((cmp_ref, cmp_o, cmpb_o), (slc_ref, slc_o, slcb_o), (win_ref, win_o, winb_o)):
        for g in range(2 * NSA_KV_HEADS):
            sl = slice(g * LANES, (g + 1) * LANES)
            v = src[:, sl]
            if g < NSA_KV_HEADS:
                v = _rope128(v, c128, s128)
            dst[:, sl] = v
            dstb[:, sl] = v.astype(BF16)

    for h in range(DIFF_HEADS):
        sl = slice(h * LANES, (h + 1) * LANES)
        dq_o[:, sl] = (_rope64(dq_ref[:, sl], c64, s64, first_half) * diff_scale).astype(dq_o.dtype)
        k = _rope64(dk_ref[:, sl], c64, s64, first_half)
        dkv_o[:, sl] = k
        dkvb_o[:, sl] = k.astype(BF16)
        sv = slice((DIFF_HEADS + h) * LANES, (DIFF_HEADS + h + 1) * LANES)
        v = dv_ref[:, sl]
        dkv_o[:, sv] = v
        dkvb_o[:, sv] = v.astype(BF16)


def _rope_split(z, tab, seq_len, tm, act_dt, transposed_values):
    m = z.shape[0]
    per = max(seq_len // tm, 1)
    tab_spec = (pl.BlockSpec((tm, 4 * LANES), lambda i: (i % per, 0)) if seq_len % tm == 0
                else pl.BlockSpec((tm, 4 * LANES), lambda i: (i, 0)))

    def zs(width, col):
        return pl.BlockSpec((tm, width), lambda i: (i, col // width))

    def os(width):
        return pl.BlockSpec((tm, width), lambda i: (i, 0))

    outs = [(1024, act_dt), (512, F32), (512, BF16), (512, F32), (512, BF16), (512, F32), (512, BF16),
            (512, act_dt), (1024, F32), (1024, BF16)]
    out_shape = [jax.ShapeDtypeStruct((m, w), dt) for w, dt in outs]
    out_specs = [os(w) for w, _ in outs]
    if transposed_values:
        assert seq_len % tm == 0
        for heads in (NSA_KV_HEADS, DIFF_HEADS):
            out_shape.append(jax.ShapeDtypeStruct((m // seq_len, heads, per, LANES, tm), BF16))
            out_specs.append(pl.BlockSpec((1, heads, 1, LANES, tm), lambda i: (i // per, 0, i % per, 0, 0)))
    return pl.pallas_call(
        _rope_kernel,
        out_shape=out_shape,
        grid=(m // tm,),
        in_specs=[zs(1024, Z_NQ), zs(512, Z_CMP), zs(512, Z_SLC), zs(512, Z_WIN),
                  zs(512, Z_DQ), zs(512, Z_DK), zs(512, Z_DV), tab_spec],
        out_specs=out_specs,
        compiler_params=_cparams(("arbitrary",)),
        name="rope_split",
    )(z, z, z, z, z, z, z, tab)


def _page_view(cache):
    depth, n_pool, page, two, heads, width = cache.shape
    return cache.reshape(depth, n_pool, page * two * heads, width)


def _page_specs(layer, page_rows, pages_per_step):
    def spec(p):
        return pl.BlockSpec((1, 1, page_rows, LANES),
                            lambda b, j, pt: (layer, pt[b, j * pages_per_step + p], 0, 0))
    return [spec(p) for p in range(pages_per_step)]


def _page_slabs(page_refs, first_slot, n, n_slots):
    return jnp.concatenate(
        [jnp.concatenate([ref[0, 0, pl.ds(first_slot + s, PAGE_SIZE, stride=n_slots), :] for s in range(n)], axis=1)
         for ref in page_refs], axis=0).astype(BF16)


def _pad_rows(x, multiple):
    pad = -x.shape[0] % multiple
    return x if pad == 0 else jnp.concatenate([x, jnp.zeros((pad, x.shape[1]), x.dtype)], axis=0)


def _gla_constants(c):
    t = np.arange(c)
    sizes = []
    s = c // 2
    while s >= 1:
        sizes.append(s)
        s //= 2
    sel, masks = [], []
    for sz in sizes:
        ref_row = (t // (2 * sz)) * (2 * sz) + sz - 1
        sel.append((t[None, :] <= ref_row[:, None]).astype(np.float32))
        same = (t[:, None] // (2 * sz)) == (t[None, :] // (2 * sz))
        masks.append((same & ((t[:, None] // sz) % 2 == 1) & ((t[None, :] // sz) % 2 == 0)).astype(np.float32))
    sel.append((t[None, :] <= t[:, None]).astype(np.float32))
    masks.append(np.eye(c, dtype=np.float32))
    return np.concatenate(sel, 0), np.stack(masks, 0)


def _gla_kernel(chunk, levels, q_ref, k_ref, v_ref, r_ref, sm_ref, gw_ref, gb_ref, nw_ref, s0_ref,
                sel_ref, mask_ref, o_ref, st_ref, state_ref):
    c = chunk
    c_in = q_ref.shape[0]
    hk = GLA_HEADS * GLA_DK
    ci = pl.program_id(1)

    @pl.when(ci == 0)
    def _():
        state_ref[...] = s0_ref[0].T

    def rows(x):
        if c_in == c:
            return x
        return jnp.concatenate([x, jnp.zeros((c - c_in, x.shape[1]), x.dtype)], axis=0)

    ga = rows(sm_ref[:, SMALL_GA:SMALL_GA + GLA_GATE_RANK])
    pre = _dot3_lhs_exact_both(ga, gw_ref[...]) + gb_ref[...]
    log_a = (jnp.minimum(pre, 0.0) - jnp.log(1.0 + jnp.exp(-jnp.abs(pre)))) / GLA_TAU
    if c_in != c:
        log_a = jnp.where(lax.broadcasted_iota(jnp.int32, (c, 1), 0) < c_in, log_a, 0.0)
    refs = _dot3_lhs_exact(sel_ref[...], log_a)
    b = refs[(levels - 1) * c:levels * c]
    q = rows(q_ref[...]) * (GLA_DK ** -0.5)
    k = rows(k_ref[...])
    v_all = rows(v_ref[...])
    lane = lax.broadcasted_iota(jnp.int32, (1, hk), 1)
    head_of_lane = lane // GLA_DK

    def stack_heads(x):
        return jnp.concatenate([jnp.where(head_of_lane == h, x, 0.0) for h in range(GLA_HEADS)], axis=0).astype(BF16)

    attn = jnp.zeros((GLA_HEADS, c, c), F32)
    for lv in range(levels):
        r = refs[lv * c:(lv + 1) * c]
        qd = stack_heads(q * jnp.exp(jnp.minimum(b - r, 0.0)))
        kd = (k * jnp.exp(jnp.minimum(r - b, 0.0))).astype(BF16)
        attn = attn + _dot_nt(qd, kd).reshape(GLA_HEADS, c, c) * mask_ref[lv][None]
    state = state_ref[...]
    inter = _dot_nt(stack_heads(q * jnp.exp(b)), state.astype(BF16))

    nw = nw_ref[...]
    for h in range(GLA_HEADS):
        vh = v_all[:, h * GLA_DV:(h + 1) * GLA_DV]
        o = (_dot(attn[h].astype(BF16), vh.astype(BF16)) + inter[h * c:(h + 1) * c])[0:c_in]
        y = o * lax.rsqrt(jnp.mean(o * o, axis=-1, keepdims=True) + NORM_EPS) * nw
        rh = r_ref[:, h * GLA_DV:(h + 1) * GLA_DV]
        o_ref[:, h * GLA_DV:(h + 1) * GLA_DV] = (y * (rh * _sigmoid(rh))).astype(o_ref.dtype)

    b_end = b[c - 1:c]
    kd = (k * jnp.exp(b_end - b)).astype(BF16)
    new_state = state * jnp.exp(b_end)
    for h in range(GLA_HEADS):
        vh = v_all[:, h * GLA_DV:(h + 1) * GLA_DV].astype(BF16)
        new_state = new_state + jnp.where(head_of_lane == h, _dot_tn(vh, kd), 0.0)
    state_ref[...] = new_state

    @pl.when(ci == pl.num_programs(1) - 1)
    def _():
        st_ref[0] = new_state.T


def _dot3_lhs_exact_both(a, b):
    ah, am, al = _split3(a)
    bh, bm, bl = _split3(b)
    return (_dot(ah, bh) + _dot(ah, bm) + _dot(am, bh)) + (_dot(ah, bl) + _dot(am, bm) + _dot(al, bh))


def _gla(z, z_small, gate_w, gate_b, norm_w, s0, batch, seq_len, chunk, act_dt):
    m = z.shape[0]
    nc = seq_len // chunk
    hk, hv = GLA_HEADS * GLA_DK, GLA_HEADS * GLA_DV
    comp_rows = max(chunk, BF16_ROWS)
    sel, masks = _gla_constants(comp_rows)
    levels = masks.shape[0]

    def zs(width, col):
        return pl.BlockSpec((chunk, width), lambda b, c: (b * nc + c, col // width))

    const2 = lambda b, c: (0, 0)
    return pl.pallas_call(
        functools.partial(_gla_kernel, comp_rows, levels),
        out_shape=[jax.ShapeDtypeStruct((m, hv), act_dt), jax.ShapeDtypeStruct((batch, hk, GLA_DV), F32)],
        grid=(batch, nc),
        in_specs=[zs(hk, Z_GQ), zs(hk, Z_GK), zs(hv, Z_GV), zs(hv, Z_GR), zs(LANES, 0),
                  pl.BlockSpec((GLA_GATE_RANK, hk), const2), pl.BlockSpec((1, hk), const2),
                  pl.BlockSpec((1, GLA_DV), const2),
                  pl.BlockSpec((1, hk, GLA_DV), lambda b, c: (b, 0, 0)),
                  pl.BlockSpec((levels * comp_rows, comp_rows), const2),
                  pl.BlockSpec((levels, comp_rows, comp_rows), lambda b, c: (0, 0, 0))],
        out_specs=[pl.BlockSpec((chunk, hv), lambda b, c: (b * nc + c, 0)),
                   pl.BlockSpec((1, hk, GLA_DV), lambda b, c: (b, 0, 0))],
        scratch_shapes=[pltpu.VMEM((GLA_DV, hk), F32)],
        compiler_params=_cparams(("arbitrary", "arbitrary")),
        name="gla",
    )(z, z, z, z, z_small, gate_w, gate_b.reshape(1, hk), norm_w.reshape(1, GLA_DV), s0,
      jnp.asarray(sel, BF16), jnp.asarray(masks, F32))


def _online_update(s, mask, v, m_ref, l_ref, acc_ref):
    if mask is not None:
        s = jnp.where(mask, s, NEG)
    m_prev = m_ref[...]
    m_new = jnp.maximum(m_prev, jnp.max(s, axis=-1, keepdims=True))
    p = jnp.exp(s - m_new)
    if mask is not None:
        p = jnp.where(mask, p, 0.0)
    alpha = jnp.exp(m_prev - m_new)
    l_ref[...] = alpha * l_ref[...] + jnp.sum(p, axis=-1, keepdims=True)
    acc_ref[...] = alpha * acc_ref[...] + _dot(p.astype(BF16), v)
    m_ref[...] = m_new


def _online_update_keymajor(s, mask, vt, m_ref, l_ref, acc_ref):
    if mask is not None:
        s = jnp.where(mask, s, NEG)
    m_prev = m_ref[...]
    m_new = jnp.maximum(m_prev, jnp.max(s, axis=0, keepdims=True))
    p = jnp.exp(s - m_new)
    if mask is not None:
        p = jnp.where(mask, p, 0.0)
    alpha = jnp.exp(m_prev - m_new)
    l_ref[...] = alpha * l_ref[...] + jnp.sum(p, axis=0, keepdims=True)
    acc_ref[...] = alpha * acc_ref[...] + _dot(vt, p.astype(BF16))
    m_ref[...] = m_new


def _diff_kernel(tq, tk, qpos0, lam_init, q_ref, k_ref, vt_ref, lam_ref, nw_ref, o_ref,
                 qq_ref, m_ref, l_ref, acc_ref):
    i, j = pl.program_id(2), pl.program_id(3)
    q_lo = qpos0 + i * tq
    last_j = (q_lo + tq - 1) // tk

    @pl.when(j == 0)
    def _():
        q = q_ref[...].astype(F32)
        lane = lax.broadcasted_iota(jnp.int32, (1, LANES), 1)
        qq_ref[...] = jnp.concatenate([jnp.where(lane < DIFF_QK, q, 0.0),
                                       jnp.where(lane >= DIFF_QK, q, 0.0)], axis=0).astype(BF16)
        m_ref[...] = jnp.full_like(m_ref, NEG)
        l_ref[...] = jnp.zeros_like(l_ref)
        acc_ref[...] = jnp.zeros_like(acc_ref)

    def step(masked):
        s = _dot_nt(k_ref[0], qq_ref[...])
        mask = None
        if masked:
            kpos = j * tk + lax.broadcasted_iota(jnp.int32, (tk, 1), 0)
            qpos = q_lo + lax.broadcasted_iota(jnp.int32, (1, tq), 1)
            mk = kpos <= qpos
            mask = jnp.concatenate([mk, mk], axis=1)
        _online_update_keymajor(s, mask, vt_ref[0, 0, 0], m_ref, l_ref, acc_ref)

    fully_visible = (j + 1) * tk - 1 <= q_lo

    @pl.when(fully_visible)
    def _():
        step(False)

    @pl.when(jnp.logical_and(jnp.logical_not(fully_visible), j <= last_j))
    def _():
        step(True)

    @pl.when(j == pl.num_programs(3) - 1)
    def _():
        l = l_ref[...]
        o12 = (acc_ref[...] / jnp.where(l > 0, l, 1.0)).T
        o_ref[...] = _diff_finish(o12, lam_ref, nw_ref, lam_init).astype(o_ref.dtype)


def _diff_attention(dq, kv, vt, lam_p, norm_w, batch, seq_len, qpos0, lam_init, tq, tk):
    m = dq.shape[0]
    nq = seq_len // tq
    nk = kv.shape[1] // tk
    assert vt.shape[2] == nk and vt.shape[4] == tk

    def last_tile(i):
        return (qpos0 + i * tq + tq - 1) // tk

    return pl.pallas_call(
        functools.partial(_diff_kernel, tq, tk, qpos0, lam_init),
        out_shape=jax.ShapeDtypeStruct((m, DIFF_HEADS * DIFF_V), dq.dtype),
        grid=(batch, DIFF_HEADS, nq, nk),
        in_specs=[pl.BlockSpec((tq, LANES), lambda b, h, i, j: (b * nq + i, h)),
                  pl.BlockSpec((1, tk, LANES), lambda b, h, i, j: (b, jnp.minimum(j, last_tile(i)), h)),
                  pl.BlockSpec((1, 1, 1, LANES, tk), lambda b, h, i, j: (b, h, jnp.minimum(j, last_tile(i)), 0, 0)),
                  pl.BlockSpec((4, DIFF_QK), lambda b, h, i, j: (0, 0)),
                  pl.BlockSpec((1, DIFF_V), lambda b, h, i, j: (0, 0))],
        out_specs=pl.BlockSpec((tq, LANES), lambda b, h, i, j: (b * nq + i, h)),
        scratch_shapes=[pltpu.VMEM((2 * tq, LANES), BF16), pltpu.VMEM((1, 2 * tq), F32),
                        pltpu.VMEM((1, 2 * tq), F32), pltpu.VMEM((DIFF_V, 2 * tq), F32)],
        compiler_params=_cparams(("arbitrary",) * 4),
        name="diff_attention",
    )(dq, kv, vt, lam_p, norm_w.reshape(1, DIFF_V))


def _diff_finish(o12, lam_ref, nw_ref, lam_init):
    half = o12.shape[0] // 2
    lam_p = lam_ref[...]
    lam = (jnp.exp(jnp.sum(lam_p[0:1] * lam_p[1:2], axis=-1, keepdims=True))
           - jnp.exp(jnp.sum(lam_p[2:3] * lam_p[3:4], axis=-1, keepdims=True)) + lam_init)
    o = o12[0:half] - lam * o12[half:]
    y = o * lax.rsqrt(jnp.mean(o * o, axis=-1, keepdims=True) + NORM_EPS) * nw_ref[...]
    return y * (1.0 - lam_init)


def _diff_paged_kernel(pps, lam_init, pt_ref, q_ref, new_ref, lam_ref, nw_ref, *rest):
    page_refs, o_ref = rest[:pps], rest[pps]
    qbd_ref, m_ref, l_ref, acc_ref = rest[pps + 1:]
    j = pl.program_id(1)
    n_new = q_ref.shape[0]
    rows_h = 2 * n_new
    hw = DIFF_HEADS * LANES

    @pl.when(j == 0)
    def _():
        q = q_ref[...]
        lane = lax.broadcasted_iota(jnp.int32, (1, LANES), 1)
        blocks = []
        for h in range(DIFF_HEADS):
            qh = q[:, h * LANES:(h + 1) * LANES]
            q12 = jnp.concatenate([jnp.where(lane < DIFF_QK, qh, 0.0), jnp.where(lane >= DIFF_QK, qh, 0.0)], axis=0)
            blocks.append(jnp.concatenate(
                [q12 if hh == h else jnp.zeros((rows_h, LANES), F32) for hh in range(DIFF_HEADS)], axis=1))
        qbd_ref[...] = jnp.concatenate(blocks, axis=0).astype(BF16)
        m_ref[...] = jnp.full_like(m_ref, NEG)
        l_ref[...] = jnp.zeros_like(l_ref)
        acc_ref[...] = jnp.zeros_like(acc_ref)

    k = _page_slabs(page_refs, 0, DIFF_HEADS, 2 * DIFF_HEADS)
    v = _page_slabs(page_refs, DIFF_HEADS, DIFF_HEADS, 2 * DIFF_HEADS)
    _online_update(_dot_nt(qbd_ref[...], k), None, v, m_ref, l_ref, acc_ref)

    @pl.when(j == pl.num_programs(1) - 1)
    def _():
        new = new_ref[...]
        kn = _pad_rows(new[:, 0:hw], BF16_ROWS).astype(BF16)
        vn = _pad_rows(new[:, hw:2 * hw], BF16_ROWS).astype(BF16)
        t = lax.broadcasted_iota(jnp.int32, (DIFF_HEADS * rows_h, 1), 0) % n_new
        u = lax.broadcasted_iota(jnp.int32, (1, kn.shape[0]), 1)
        _online_update(_dot_nt(qbd_ref[...], kn), u <= t, vn, m_ref, l_ref, acc_ref)
        l = l_ref[...]
        o12 = acc_ref[...] / jnp.where(l > 0, l, 1.0)
        for h in range(DIFF_HEADS):
            r = slice(h * rows_h, (h + 1) * rows_h)
            o_ref[:, h * LANES:(h + 1) * LANES] = _diff_finish(o12[r, h * LANES:(h + 1) * LANES], lam_ref, nw_ref,
                                                               lam_init)


def _diff_attention_paged(dq, new_kv, cache_view, layer, page_table, lam_p, norm_w, lam_init, pps):
    b, n_pages = page_table.shape
    n_new = dq.shape[0] // b
    hw = DIFF_HEADS * LANES
    rows = DIFF_HEADS * 2 * n_new
    assert n_pages % pps == 0 and n_new % SUBLANES == 0
    const = lambda i, j, pt: (0, 0)
    return pl.pallas_call(
        functools.partial(_diff_paged_kernel, pps, lam_init),
        out_shape=jax.ShapeDtypeStruct(dq.shape, F32),
        grid_spec=pltpu.PrefetchScalarGridSpec(
            num_scalar_prefetch=1,
            grid=(b, n_pages // pps),
            in_specs=[pl.BlockSpec((n_new, hw), lambda i, j, pt: (i, 0)),
                      pl.BlockSpec((n_new, 2 * hw), lambda i, j, pt: (i, 0)),
                      pl.BlockSpec((4, DIFF_QK), const), pl.BlockSpec((1, DIFF_V), const)]
                     + _page_specs(layer, PAGE_SIZE * 2 * DIFF_HEADS, pps),
            out_specs=pl.BlockSpec((n_new, hw), lambda i, j, pt: (i, 0)),
            scratch_shapes=[pltpu.VMEM((rows, hw), BF16), pltpu.VMEM((rows, 1), F32), pltpu.VMEM((rows, 1), F32),
                            pltpu.VMEM((rows, hw), F32)]),
        compiler_params=_cparams(("arbitrary", "arbitrary")),
        name="diff_attention_paged",
    )(page_table, dq, new_kv, lam_p, norm_w.reshape(1, DIFF_V), *([cache_view] * pps))


def _compress_weights(w, pe):
    k = NSA_CMP_STRIDE * HEAD_DIM
    return (jnp.concatenate([w[:, :k], w[:, k:]], axis=2).astype(BF16),
            pe.reshape(2, NSA_CMP_BLOCK // NSA_CMP_STRIDE, k))


def _compress_blocks(segments, rows, w_ref, pe_ref, o_ref):
    for c in range(2):
        pe = _dot(_pad_rows(pe_ref[c], BF16_ROWS).astype(BF16), w_ref[c])
        pe_term = pe[0:1, 0:HEAD_DIM] + pe[1:2, HEAD_DIM:2 * HEAD_DIM]
        for g in range(NSA_KV_HEADS):
            both = _dot(segments(c * NSA_KV_HEADS + g), w_ref[c])
            o_ref[0, c, g] = (both[:, 0:HEAD_DIM] + pltpu.roll(both[:, HEAD_DIM:2 * HEAD_DIM], rows - 1, 0)
                              + pe_term).astype(BF16)


def _compress_kernel(x_ref, w_ref, pe_ref, o_ref):
    tok_w = 2 * NSA_KV_HEADS * HEAD_DIM

    def segments(slot):
        return jnp.concatenate([x_ref[0, :, t * tok_w + slot * HEAD_DIM:t * tok_w + (slot + 1) * HEAD_DIM]
                                for t in range(NSA_CMP_STRIDE)], axis=1)

    _compress_blocks(segments, x_ref.shape[1], w_ref, pe_ref, o_ref)


def _compress_paged_kernel(pps, n_past_seg, pt_ref, new_ref, w_ref, pe_ref, *rest):
    page_refs, o_ref, x_ref = rest[:pps], rest[pps], rest[pps + 1]
    j = pl.program_id(1)
    slots = 2 * NSA_KV_HEADS
    seg_per_page = PAGE_SIZE // NSA_CMP_STRIDE
    rows = x_ref.shape[1]

    for pp in range(0, pps, 2):
        row0 = pl.multiple_of((j * pps + pp) * seg_per_page, 2 * seg_per_page)
        for slot in range(slots):
            for t in range(NSA_CMP_STRIDE):
                pair = [page_refs[pp + q][0, 0, pl.ds(t * slots + slot, seg_per_page, stride=NSA_CMP_STRIDE * slots), :]
                        for q in range(2)]
                x_ref[slot, pl.ds(row0, 2 * seg_per_page), t * HEAD_DIM:(t + 1) * HEAD_DIM] = (
                    jnp.concatenate(pair, axis=0).astype(BF16))

    @pl.when(j == pl.num_programs(1) - 1)
    def _():
        new = new_ref[...]
        first = lax.broadcasted_iota(jnp.int32, (rows - n_past_seg, 1), 0) == 0
        for slot in range(slots):
            for t in range(NSA_CMP_STRIDE):
                if t < new.shape[0]:
                    tail = jnp.where(first, new[t:t + 1, slot * HEAD_DIM:(slot + 1) * HEAD_DIM], 0.0)
                else:
                    tail = jnp.zeros((rows - n_past_seg, HEAD_DIM), F32)
                x_ref[slot, n_past_seg:rows, t * HEAD_DIM:(t + 1) * HEAD_DIM] = tail.astype(BF16)
        _compress_blocks(lambda slot: x_ref[slot], rows, w_ref, pe_ref, o_ref)


def _compress_paged(cache_view, layer, page_table, new_rows, w, pe, pps):
    b, n_pages = page_table.shape
    n_new = new_rows.shape[0] // b
    slots = 2 * NSA_KV_HEADS
    seg_per_page = PAGE_SIZE // NSA_CMP_STRIDE
    n_past_seg = n_pages * seg_per_page
    rows = n_past_seg + 2 * BF16_ROWS
    assert n_new <= NSA_CMP_STRIDE and pps % 2 == 0 and n_pages % pps == 0
    return pl.pallas_call(
        functools.partial(_compress_paged_kernel, pps, n_past_seg),
        out_shape=jax.ShapeDtypeStruct((b, 2, NSA_KV_HEADS, rows, HEAD_DIM), BF16),
        grid_spec=pltpu.PrefetchScalarGridSpec(
            num_scalar_prefetch=1,
            grid=(b, n_pages // pps),
            in_specs=[pl.BlockSpec((n_new, slots * HEAD_DIM), lambda i, j, pt: (i, 0)),
                      pl.BlockSpec(w.shape, lambda i, j, pt: (0, 0, 0)),
                      pl.BlockSpec(pe.shape, lambda i, j, pt: (0, 0, 0))]
                     + _page_specs(layer, PAGE_SIZE * slots, pps),
            out_specs=pl.BlockSpec((1, 2, NSA_KV_HEADS, rows, HEAD_DIM), lambda i, j, pt: (i, 0, 0, 0, 0)),
            scratch_shapes=[pltpu.VMEM((slots, rows, NSA_CMP_STRIDE * HEAD_DIM), BF16)]),
        compiler_params=_cparams(("arbitrary", "arbitrary")),
        name="nsa_compress_paged",
    )(page_table, new_rows, w, pe, *([cache_view] * pps))


def _compress(seg, w, pe):
    b, rows, width = seg.shape
    return pl.pallas_call(
        _compress_kernel,
        out_shape=jax.ShapeDtypeStruct((b, 2, NSA_KV_HEADS, rows, HEAD_DIM), BF16),
        grid=(b,),
        in_specs=[pl.BlockSpec((1, rows, width), lambda i: (i, 0, 0)),
                  pl.BlockSpec(w.shape, lambda i: (0, 0, 0)),
                  pl.BlockSpec(pe.shape, lambda i: (0, 0, 0))],
        out_specs=pl.BlockSpec((1, 2, NSA_KV_HEADS, rows, HEAD_DIM), lambda i: (i, 0, 0, 0, 0)),
        compiler_params=_cparams(("arbitrary",)),
        name="nsa_compress",
    )(seg, w, pe)


def _nsa_band(n_cmp_rows, n_slc_pad):
    ratio = NSA_SEL_BLOCK // NSA_CMP_STRIDE
    span = NSA_CMP_BLOCK // NSA_CMP_STRIDE
    n = np.arange(n_cmp_rows)[:, None]
    j = np.arange(n_slc_pad)[None, :]
    return ((n >= ratio * j - (span - 1)) & (n <= ratio * j + ratio - 1)).astype(np.float32)


def _nsa_compressed(qs, qpos_rows, ck, cv):
    cmp_end = lax.broadcasted_iota(jnp.int32, (1, ck.shape[0]), 1) * NSA_CMP_STRIDE + (NSA_CMP_BLOCK - 1)
    p_c = _masked_softmax(_dot_nt(qs, ck), cmp_end <= qpos_rows)
    return p_c, _dot(p_c.astype(BF16), cv)


def _nsa_block_scores(p_c, qpos, band_ref):
    tq = qpos.shape[0]
    imp = p_c[0:tq]
    for n in range(1, p_c.shape[0] // tq):
        imp = imp + p_c[n * tq:(n + 1) * tq]
    p_slc = _dot3_rhs_exact(_pad_rows(imp, BF16_ROWS), band_ref[...])[0:tq]
    blk = lax.broadcasted_iota(jnp.int32, (1, band_ref.shape[1]), 1)
    cur = qpos // NSA_SEL_BLOCK
    valid = blk * NSA_SEL_BLOCK <= qpos
    forced = (blk == 0) | (blk == cur) | (blk == cur - 1)
    return jnp.where(valid, p_slc + jnp.where(forced, NSA_FORCE_BONUS, 0.0), NEG)


def _nsa_select(score, score_ref, n_slc):
    score_t = score.T
    score_ref[...] = score_t
    blk_t = lax.broadcasted_iota(jnp.int32, (score_t.shape[0], 1), 0)

    def count(jp, cnt):
        row = score_ref[pl.ds(jp, 1), :]
        beats = (row > score_t) | ((row == score_t) & (jp < blk_t))
        return cnt + jnp.where(beats, 1.0, 0.0)

    cnt = lax.fori_loop(0, n_slc, count, jnp.zeros(score_t.shape, F32))
    return jnp.where((cnt < NSA_TOP_N) & (score_t > 0.5 * NEG), 1.0, 0.0)


def _nsa_window(qs, qpos_rows, kw, vw, first_pos):
    wpos = first_pos + lax.broadcasted_iota(jnp.int32, (1, kw.shape[0]), 1)
    mask_w = (wpos <= qpos_rows) & (wpos > qpos_rows - NSA_WINDOW) & (wpos >= 0)
    return _dot(_masked_softmax(_dot_nt(qs, kw), mask_w).astype(BF16), vw)


def _nsa_kernel(tq, tk, qpos0, n_slc, q_ref, sm_ref, ck_ref, cv_ref, sk_ref, svt_ref, wk_ref, wv_ref,
                band_ref, o_ref, score_ref, sel_ref, m_ref, l_ref, acc_ref):
    i = pl.program_id(1)
    hpg = NSA_HEADS // NSA_KV_HEADS
    q_lo = qpos0 + i * tq
    qpos = q_lo + lax.broadcasted_iota(jnp.int32, (tq, 1), 0)
    qpos_lanes = q_lo + lax.broadcasted_iota(jnp.int32, (1, tq), 1)
    qpos_rows = jnp.concatenate([qpos] * hpg, axis=0)
    gates = _sigmoid(sm_ref[:, SMALL_GATE:SMALL_GATE + 3 * NSA_HEADS])
    blocks_per_tile = tk // NSA_SEL_BLOCK

    for g in range(NSA_KV_HEADS):
        q = q_ref[:, g * hpg * LANES:(g + 1) * hpg * LANES]
        qs = jnp.concatenate([q[:, n * LANES:(n + 1) * LANES] for n in range(hpg)], axis=0).astype(BF16)

        p_c, o_c = _nsa_compressed(qs, qpos_rows, ck_ref[0, 0, g], cv_ref[0, 0, g])
        sel_ref[...] = _nsa_select(_nsa_block_scores(p_c, qpos, band_ref), score_ref, n_slc)

        m_ref[...] = jnp.full_like(m_ref, NEG)
        l_ref[...] = jnp.zeros_like(l_ref)
        acc_ref[...] = jnp.zeros_like(acc_ref)

        def key_tile(kt, carry):
            start = pl.multiple_of(kt * tk, tk)
            k = sk_ref[0, pl.ds(start, tk), g * LANES:(g + 1) * LANES]
            blocks = sel_ref[pl.ds(pl.multiple_of(kt * blocks_per_tile, blocks_per_tile), blocks_per_tile), :]
            picked = jnp.concatenate([jnp.broadcast_to(blocks[r:r + 1], (NSA_SEL_BLOCK, tq))
                                      for r in range(blocks_per_tile)], axis=0) > 0.5
            kpos = kt * tk + lax.broadcasted_iota(jnp.int32, (tk, 1), 0)
            mk = picked & (kpos <= qpos_lanes)
            mask = jnp.concatenate([mk] * hpg, axis=1)
            _online_update_keymajor(_dot_nt(k, qs), mask, svt_ref[0, g, kt], m_ref, l_ref, acc_ref)
            return carry

        lax.fori_loop(0, (q_lo + tq - 1) // tk + 1, key_tile, 0)
        l = l_ref[...]
        o_st = acc_ref[...] / jnp.where(l > 0, l, 1.0)
        o_s = jnp.concatenate([o_st[:, n * tq:(n + 1) * tq].T for n in range(hpg)], axis=0)

        wrows = -(-(tq + NSA_WINDOW) // BF16_ROWS) * BF16_ROWS
        wstart = pl.multiple_of(i * tq, tq)
        o_w = _nsa_window(qs, qpos_rows, wk_ref[0, pl.ds(wstart, wrows), g * LANES:(g + 1) * LANES],
                          wv_ref[0, pl.ds(wstart, wrows), g * LANES:(g + 1) * LANES], q_lo - NSA_WINDOW)

        for n in range(hpg):
            h = g * hpg + n
            r = slice(n * tq, (n + 1) * tq)
            o = (gates[:, 3 * h:3 * h + 1] * o_c[r] + gates[:, 3 * h + 1:3 * h + 2] * o_s[r]
                 + gates[:, 3 * h + 2:3 * h + 3] * o_w[r])
            o_ref[:, h * LANES:(h + 1) * LANES] = o.astype(o_ref.dtype)


def _nsa_attention(qn, z_small, comp, slc_buf, slc_vt, win_all, batch, seq_len, qpos0, tq, tk):
    m = qn.shape[0]
    nq = seq_len // tq
    t_keys = slc_buf.shape[1]
    n_cmp_rows = comp.shape[3]
    n_slc = -(-(qpos0 + seq_len) // NSA_SEL_BLOCK)
    nsp = -(-n_slc // LANES) * LANES
    assert tq % LANES == 0 and tk % NSA_SEL_BLOCK == 0 and slc_vt.shape[2] * tk == t_keys and nsp * NSA_SEL_BLOCK >= t_keys
    band = _nsa_band(n_cmp_rows, nsp)
    hw = NSA_HEADS * HEAD_DIM
    kvw = NSA_KV_HEADS * HEAD_DIM
    rows = (NSA_HEADS // NSA_KV_HEADS) * tq
    wlen = win_all.shape[1]
    return pl.pallas_call(
        functools.partial(_nsa_kernel, tq, tk, qpos0, n_slc),
        out_shape=jax.ShapeDtypeStruct((m, hw), qn.dtype),
        grid=(batch, nq),
        in_specs=[pl.BlockSpec((tq, hw), lambda b, i: (b * nq + i, 0)),
                  pl.BlockSpec((tq, LANES), lambda b, i: (b * nq + i, 0)),
                  pl.BlockSpec((1, 1, NSA_KV_HEADS, n_cmp_rows, HEAD_DIM), lambda b, i: (b, 0, 0, 0, 0)),
                  pl.BlockSpec((1, 1, NSA_KV_HEADS, n_cmp_rows, HEAD_DIM), lambda b, i: (b, 1, 0, 0, 0)),
                  pl.BlockSpec((1, t_keys, kvw), lambda b, i: (b, 0, 0)),
                  pl.BlockSpec((1,) + slc_vt.shape[1:], lambda b, i: (b, 0, 0, 0, 0)),
                  pl.BlockSpec((1, wlen, kvw), lambda b, i: (b, 0, 0)),
                  pl.BlockSpec((1, wlen, kvw), lambda b, i: (b, 0, 1)),
                  pl.BlockSpec((n_cmp_rows, nsp), lambda b, i: (0, 0))],
        out_specs=pl.BlockSpec((tq, hw), lambda b, i: (b * nq + i, 0)),
        scratch_shapes=[pltpu.VMEM((nsp, tq), F32), pltpu.VMEM((nsp, tq), F32), pltpu.VMEM((1, rows), F32),
                        pltpu.VMEM((1, rows), F32), pltpu.VMEM((HEAD_DIM, rows), F32)],
        compiler_params=_cparams(("arbitrary", "arbitrary")),
        name="nsa_attention",
    )(qn, z_small, comp, comp, slc_buf, slc_vt, win_all, win_all, jnp.asarray(band, BF16))


def _nsa_paged_kernel(pps, qpos0, n_slc, pt_ref, q_ref, sm_ref, ck_ref, cv_ref, new_ref, wk_ref, wv_ref, band_ref,
                      *rest):
    page_refs, o_ref = rest[:pps], rest[pps]
    score_ref, qbd_ref, sel_ref, m_ref, l_ref, acc_ref, oc_ref, ow_ref = rest[pps + 1:]
    j = pl.program_id(1)
    n_new = q_ref.shape[0]
    groups = NSA_KV_HEADS
    hpg = NSA_HEADS // groups
    rows_g = hpg * n_new
    gw = groups * LANES
    nsp = band_ref.shape[1]
    qpos = qpos0 + lax.broadcasted_iota(jnp.int32, (n_new, 1), 0)

    @pl.when(j == 0)
    def _():
        q = q_ref[...]
        qpos_rows = jnp.concatenate([qpos] * hpg, axis=0)
        scores, blocks = [], []
        for g in range(groups):
            qf = jnp.concatenate([q[:, (g * hpg + n) * LANES:(g * hpg + n + 1) * LANES] for n in range(hpg)], axis=0)
            qs = qf.astype(BF16)
            p_c, o_c = _nsa_compressed(qs, qpos_rows, ck_ref[0, 0, g], cv_ref[0, 0, g])
            oc_ref[g * rows_g:(g + 1) * rows_g] = o_c
            scores.append(_nsa_block_scores(p_c, qpos, band_ref))
            ow_ref[g * rows_g:(g + 1) * rows_g] = _nsa_window(
                qs, qpos_rows, wk_ref[0, :, g * LANES:(g + 1) * LANES], wv_ref[0, :, g * LANES:(g + 1) * LANES],
                qpos0 - NSA_WINDOW)
            blocks.append(jnp.concatenate(
                [qf if gg == g else jnp.zeros((rows_g, LANES), F32) for gg in range(groups)], axis=1))
        qbd_ref[...] = jnp.concatenate(blocks, axis=0).astype(BF16)
        pad = jnp.full((score_ref.shape[1] - groups * n_new, nsp), NEG, F32)
        sel = _nsa_select(jnp.concatenate(scores + [pad], axis=0), score_ref, n_slc).T
        sel_ref[...] = jnp.concatenate(
            [sel[g * n_new:(g + 1) * n_new] for g in range(groups) for _ in range(hpg)], axis=0).astype(BF16)
        m_ref[...] = jnp.full_like(m_ref, NEG)
        l_ref[...] = jnp.zeros_like(l_ref)
        acc_ref[...] = jnp.zeros_like(acc_ref)

    k = _page_slabs(page_refs, 0, groups, 2 * groups)
    v = _page_slabs(page_refs, groups, groups, 2 * groups)
    n_keys = pps * PAGE_SIZE
    key_blk = (j * n_keys + lax.broadcasted_iota(jnp.int32, (1, n_keys), 1)) // NSA_SEL_BLOCK
    expand = jnp.where(lax.broadcasted_iota(jnp.int32, (nsp, 1), 0) == key_blk, 1.0, 0.0).astype(BF16)
    picked = _dot(sel_ref[...], expand) > 0.5
    _online_update(_dot_nt(qbd_ref[...], k), picked, v, m_ref, l_ref, acc_ref)

    @pl.when(j == pl.num_programs(1) - 1)
    def _():
        new = new_ref[...]
        kn = _pad_rows(new[:, 0:gw], BF16_ROWS).astype(BF16)
        vn = _pad_rows(new[:, gw:2 * gw], BF16_ROWS).astype(BF16)
        new_blk = qpos0 // NSA_SEL_BLOCK
        t = lax.broadcasted_iota(jnp.int32, (groups * rows_g, 1), 0) % n_new
        u = lax.broadcasted_iota(jnp.int32, (1, kn.shape[0]), 1)
        mask = (u <= t) & (sel_ref[:, new_blk:new_blk + 1] > 0.5)
        _online_update(_dot_nt(qbd_ref[...], kn), mask, vn, m_ref, l_ref, acc_ref)
        l = l_ref[...]
        o_sel = acc_ref[...] / jnp.where(l > 0, l, 1.0)
        gates = _sigmoid(sm_ref[:, SMALL_GATE:SMALL_GATE + 3 * NSA_HEADS])
        for g in range(groups):
            for n in range(hpg):
                h = g * hpg + n
                r = slice(g * rows_g + n * n_new, g * rows_g + (n + 1) * n_new)
                o_ref[:, h * LANES:(h + 1) * LANES] = (
                    gates[:, 3 * h:3 * h + 1] * oc_ref[r] + gates[:, 3 * h + 1:3 * h + 2] * o_sel[r, g * LANES:(g + 1) * LANES]
                    + gates[:, 3 * h + 2:3 * h + 3] * ow_ref[r])


def _nsa_attention_paged(qn, z_small, comp, new_kv, win_all, cache_view, layer, page_table, qpos0, pps):
    b, n_pages = page_table.shape
    n_new = qn.shape[0] // b
    groups = NSA_KV_HEADS
    hw = NSA_HEADS * HEAD_DIM
    gw = groups * HEAD_DIM
    rows = NSA_HEADS * n_new
    n_cmp_rows = comp.shape[3]
    n_slc = -(-(qpos0 + n_new) // NSA_SEL_BLOCK)
    nsp = -(-n_slc // LANES) * LANES
    wlen = win_all.shape[1]
    assert n_pages % pps == 0 and qpos0 == n_pages * PAGE_SIZE and n_new <= NSA_SEL_BLOCK
    assert qpos0 % NSA_SEL_BLOCK == 0 and groups * n_new <= LANES and n_new % SUBLANES == 0
    band = _nsa_band(n_cmp_rows, nsp)
    const = lambda i, j, pt: (0, 0)
    return pl.pallas_call(
        functools.partial(_nsa_paged_kernel, pps, qpos0, n_slc),
        out_shape=jax.ShapeDtypeStruct(qn.shape, F32),
        grid_spec=pltpu.PrefetchScalarGridSpec(
            num_scalar_prefetch=1,
            grid=(b, n_pages // pps),
            in_specs=[pl.BlockSpec((n_new, hw), lambda i, j, pt: (i, 0)),
                      pl.BlockSpec((n_new, LANES), lambda i, j, pt: (i, 0)),
                      pl.BlockSpec((1, 1, groups, n_cmp_rows, HEAD_DIM), lambda i, j, pt: (i, 0, 0, 0, 0)),
                      pl.BlockSpec((1, 1, groups, n_cmp_rows, HEAD_DIM), lambda i, j, pt: (i, 1, 0, 0, 0)),
                      pl.BlockSpec((n_new, 2 * gw), lambda i, j, pt: (i, 0)),
                      pl.BlockSpec((1, wlen, gw), lambda i, j, pt: (i, 0, 0)),
                      pl.BlockSpec((1, wlen, gw), lambda i, j, pt: (i, 0, 1)),
                      pl.BlockSpec((n_cmp_rows, nsp), const)]
                     + _page_specs(layer, PAGE_SIZE * 2 * groups, pps),
            out_specs=pl.BlockSpec((n_new, hw), lambda i, j, pt: (i, 0)),
            scratch_shapes=[pltpu.VMEM((nsp, LANES), F32), pltpu.VMEM((rows, gw), BF16), pltpu.VMEM((rows, nsp), BF16),
                            pltpu.VMEM((rows, 1), F32), pltpu.VMEM((rows, 1), F32), pltpu.VMEM((rows, gw), F32),
                            pltpu.VMEM((rows, HEAD_DIM), F32), pltpu.VMEM((rows, HEAD_DIM), F32)]),
        compiler_params=_cparams(("arbitrary", "arbitrary")),
        name="nsa_attention_paged",
    )(page_table, qn, z_small, comp, comp, new_kv, win_all, win_all, jnp.asarray(band, BF16),
      *([cache_view] * pps))


def _out_proj_kernel(x_ref, og_ref, on_ref, od_ref, w_ref, nw_ref, g_ref, o_ref):
    a = jnp.concatenate([og_ref[...].astype(BF16), on_ref[...].astype(BF16), od_ref[...].astype(BF16)], axis=-1)
    mix = _dot(a, w_ref[...])
    y = mix * lax.rsqrt(jnp.mean(mix * mix, axis=-1, keepdims=True) + NORM_EPS) * nw_ref[...]
    o_ref[...] = x_ref[...] + g_ref[0] * y


def _out_proj(x2, o_gla, o_nsa, o_d, w, nw, gate, seq_len, tm):
    m, d = x2.shape
    row = lambda width: pl.BlockSpec((tm, width), lambda i: (i, 0))
    return pl.pallas_call(
        _out_proj_kernel,
        out_shape=jax.ShapeDtypeStruct((m, d), F32),
        grid=(m // tm,),
        in_specs=[row(d), row(o_gla.shape[1]), row(o_nsa.shape[1]), row(o_d.shape[1]),
                  pl.BlockSpec(w.shape, lambda i: (0, 0)), pl.BlockSpec((1, d), lambda i: (0, 0)),
                  _mod_spec(seq_len, tm, d)],
        out_specs=row(d),
        compiler_params=_cparams(("arbitrary",)),
        name="out_proj",
    )(x2, o_gla, o_nsa, o_d, w, nw, _expand_mod(gate, seq_len, tm))


def _ffn_kernel(x_ref, nw2_ref, sc_ref, sh_ref, wg_ref, wu_ref, wo_ref, nw3_ref, g_ref, o_ref, h_ref, acc_ref):
    j = pl.program_id(1)

    @pl.when(j == 0)
    def _():
        h_ref[...] = _norm_mod(x_ref[...], nw2_ref[...], sc_ref[0], sh_ref[0]).astype(BF16)
        acc_ref[...] = jnp.zeros_like(acc_ref)

    h = h_ref[...]
    gate = _dot(h, wg_ref[...])
    up = _dot(h, wu_ref[...])
    acc_ref[...] += _dot((gate * _sigmoid(gate) * up).astype(BF16), wo_ref[...])

    @pl.when(j == pl.num_programs(1) - 1)
    def _():
        f = acc_ref[...]
        y = f * lax.rsqrt(jnp.mean(f * f, axis=-1, keepdims=True) + NORM_EPS) * nw3_ref[...]
        o_ref[...] = x_ref[...] + g_ref[0] * y


def _ffn(x2, nw2, sc, sh, w_in, w_out, nw3, gate, seq_len, tm, tf):
    m, d = x2.shape
    d_ff = w_out.shape[0]
    nf = d_ff // tf
    row = pl.BlockSpec((tm, d), lambda i, j: (i, 0))
    vec = pl.BlockSpec((1, d), lambda i, j: (0, 0))
    return pl.pallas_call(
        _ffn_kernel,
        out_shape=jax.ShapeDtypeStruct((m, d), F32),
        grid=(m // tm, nf),
        in_specs=[row, vec, _mod_spec(seq_len, tm, d), _mod_spec(seq_len, tm, d),
                  pl.BlockSpec((d, tf), lambda i, j: (0, j)),
                  pl.BlockSpec((d, tf), lambda i, j: (0, nf + j)),
                  pl.BlockSpec((tf, d), lambda i, j: (j, 0)),
                  vec, _mod_spec(seq_len, tm, d)],
        out_specs=row,
        scratch_shapes=[pltpu.VMEM((tm, d), BF16), pltpu.VMEM((tm, d), F32)],
        compiler_params=_cparams(("arbitrary", "arbitrary")),
        name="ffn",
    )(x2, nw2, _expand_mod(sc, seq_len, tm), _expand_mod(sh, seq_len, tm), w_in, w_in, w_out, nw3,
      _expand_mod(gate, seq_len, tm))


def _pick(n, target):
    if n <= target:
        return n
    for t in range(target, 7, -1):
        if n % t == 0 and t % SUBLANES == 0:
            return t
    return n


def _permute_w_in(w):
    d = w.shape[0]
    hk, hv = GLA_HEADS * GLA_DK, GLA_HEADS * GLA_DV
    kvw = NSA_KV_HEADS * HEAD_DIM
    sizes = [hk, hk, hv, hv, GLA_GATE_RANK, NSA_HEADS * HEAD_DIM, kvw, kvw, kvw, kvw, kvw, kvw, NSA_HEADS * 3,
             DIFF_HEADS * 2 * DIFF_QK, DIFF_HEADS * 2 * DIFF_QK, DIFF_HEADS * DIFF_V]
    offs = np.concatenate([[0], np.cumsum(sizes)])
    part = lambda k: w[:, offs[k]:offs[k + 1]]
    main = jnp.concatenate([part(k) for k in (5, 0, 1, 2, 3, 6, 7, 8, 9, 10, 11, 13, 14, 15)], axis=1)
    pad = LANES - GLA_GATE_RANK - NSA_HEADS * 3
    small = jnp.concatenate([part(4), part(12), jnp.zeros((d, pad), w.dtype)], axis=1)
    return main.astype(BF16), small.astype(BF16)


def _layer(x, mod, qpos0, past, win_buf, gla_s0, lp, lam_init):
    b, seq_len, d = x.shape
    m = b * seq_len
    x2 = x.reshape(m, d)
    sh1, sc1, g1, sh2, sc2, g2 = jnp.split(mod, 6, axis=-1)
    tm = _pick(seq_len, 512) if seq_len >= 128 else m
    z, z_small = _in_proj(x2, lp['norm'][0:1], sc1, sh1, *lp['w_in'], seq_len, tm, _pick(Z_WIDTH, 512))

    pos = qpos0 + jnp.arange(seq_len)
    tab = _rope_tables(pos)
    if seq_len % tm != 0:
        tab = jnp.tile(tab, (m // seq_len, 1))
    act_dt = BF16 if seq_len % BF16_ROWS == 0 else F32
    (qn, cmp_f, cmp_b, slc_f, slc_b, win_f, win_b, dq, dkv_f, dkv_b, *vts) = _rope_split(
        z, tab, seq_len, tm, act_dt, transposed_values=past is None)

    chunk = 64 if seq_len % 64 == 0 else seq_len
    o_gla, gla_state = _gla(z, z_small, lp['gla_gate_w'], lp['gla_gate_b'], lp['gla_norm'], gla_s0, b, seq_len,
                            chunk, act_dt)

    kvw = 2 * NSA_KV_HEADS * HEAD_DIM
    dw = 2 * DIFF_HEADS * DIFF_V
    win_len = NSA_WINDOW + seq_len
    win_pad = jnp.zeros((b, -win_len % BF16_ROWS, kvw), BF16)
    win_all = jnp.concatenate([win_buf.astype(BF16), win_b.reshape(b, seq_len, kvw), win_pad], axis=1)

    if past is None:
        slc_vt, diff_vt = vts
        seg = cmp_b.reshape(b, seq_len // NSA_CMP_STRIDE, NSA_CMP_STRIDE * kvw)
        comp = _compress(seg, lp['nsa_cmp_w'], lp['nsa_cmp_pe'])
        o_nsa = _nsa_attention(qn, z_small, comp, slc_b.reshape(b, seq_len, kvw), slc_vt, win_all, b, seq_len, qpos0,
                               _pick(seq_len, 128), tm)
        o_d = _diff_attention(dq, dkv_b.reshape(b, seq_len, dw), diff_vt, lp['diff_lambda'], lp['diff_norm'], b,
                              seq_len, qpos0, lam_init, _pick(seq_len, 512), tm)
    else:
        pt, layer = past['page_table'], past['layer']
        pps = PAGES_PER_STEP if pt.shape[1] % PAGES_PER_STEP == 0 else 2
        comp = _compress_paged(past['cmp'], layer, pt, cmp_f, lp['nsa_cmp_w'], lp['nsa_cmp_pe'], pps)
        o_nsa = _nsa_attention_paged(qn, z_small, comp, slc_f, win_all, past['slc'], layer, pt, qpos0, pps)
        o_d = _diff_attention_paged(dq, dkv_f, past['diff'], layer, pt, lp['diff_lambda'], lp['diff_norm'],
                                    lam_init, pps)

    x1 = _out_proj(x2, o_gla, o_nsa, o_d, lp['w_out'], lp['norm'][1:2], g1, seq_len, tm)
    d_ff = lp['ffn_w_out'].shape[0]
    x2n = _ffn(x1, lp['norm'][2:3], sc2, sh2, lp['ffn_w_in'], lp['ffn_w_out'], lp['norm'][3:4], g2, seq_len, tm,
               _pick(d_ff, 512) if d_ff % LANES == 0 else d_ff)

    return (x2n.reshape(b, seq_len, d), cmp_f, slc_f, dkv_f, win_f, gla_state)


def kernel(x_prompt, x_sample, c_prompt, c_sample, cache_nsa_cmp_kv, cache_nsa_slc_kv, cache_diff_kv,
           state_nsa_win_kv, state_gla, page_table, ada_w, ada_b, norm_w, w_in, gla_gate_w, gla_gate_b,
           gla_norm, nsa_cmp_pe, nsa_cmp_w, diff_lambda, diff_norm, w_out, ffn_w_in, ffn_w_out):
    depth = ada_w.shape[0]
    bp, lp_len, d = x_prompt.shape
    bs, ls_len, _ = x_sample.shape
    n_pages = page_table.shape[1]
    past_len = n_pages * PAGE_SIZE
    wb = state_nsa_win_kv.shape[2]
    kvw = 2 * NSA_KV_HEADS * HEAD_DIM
    hk = GLA_HEADS * GLA_DK

    n_c = bp + bs
    rows = -(-n_c // SUBLANES) * SUBLANES
    c_all = jnp.concatenate([c_prompt, c_sample, jnp.zeros((rows - n_c, d), F32)], axis=0)
    mod_all = _modulation(c_all, ada_w, ada_b)

    xp, xs = x_prompt, x_sample
    outs = [[] for _ in range(10)]
    for l in range(depth):
        cmp_w, cmp_pe = _compress_weights(nsa_cmp_w[l], nsa_cmp_pe[l])
        lp = {'norm': norm_w[l], 'w_in': _permute_w_in(w_in[l]),
              'gla_gate_w': gla_gate_w[l], 'gla_gate_b': gla_gate_b[l], 'gla_norm': gla_norm[l],
              'nsa_cmp_pe': cmp_pe, 'nsa_cmp_w': cmp_w, 'diff_lambda': diff_lambda[l], 'diff_norm': diff_norm[l],
              'w_out': w_out[l].astype(BF16), 'ffn_w_in': ffn_w_in[l].astype(BF16),
              'ffn_w_out': ffn_w_out[l].astype(BF16)}
        lam_init = 0.8 - 0.6 * math.exp(-0.3 * l)

        win0 = jnp.zeros((bp, NSA_WINDOW, kvw), BF16)
        gla0 = jnp.zeros((bp, hk, GLA_DV), F32)
        xp, cmp_p, slc_p, diff_p, win_p, gla_p = _layer(xp, mod_all[l, :bp], 0, None, win0, gla0, lp, lam_init)

        past = {'cmp': _page_view(cache_nsa_cmp_kv), 'slc': _page_view(cache_nsa_slc_kv),
                'diff': _page_view(cache_diff_kv), 'layer': l, 'page_table': page_table}
        win_prev = state_nsa_win_kv[l].reshape(bs, wb, kvw)
        win_in = jnp.concatenate([jnp.zeros((bs, NSA_WINDOW - wb, kvw), F32), win_prev], axis=1)
        xs, cmp_s, slc_s, diff_s, win_s, gla_s = _layer(xs, mod_all[l, bp:bp + bs], past_len, past, win_in,
                                                        state_gla[l].reshape(bs, hk, GLA_DV), lp, lam_init)

        kv_shape = lambda b, n: (b, n, 2, NSA_KV_HEADS, HEAD_DIM)
        outs[0].append(cmp_p.reshape(kv_shape(bp, lp_len)))
        outs[1].append(cmp_s.reshape(kv_shape(bs, ls_len)))
        outs[2].append(slc_p.reshape(kv_shape(bp, lp_len)))
        outs[3].append(slc_s.reshape(kv_shape(bs, ls_len)))
        outs[4].append(diff_p.reshape(bp, lp_len, 2, DIFF_HEADS, DIFF_V))
        outs[5].append(diff_s.reshape(bs, ls_len, 2, DIFF_HEADS, DIFF_V))
        win_p3 = win_p.reshape(bp, lp_len, kvw)
        win_state_p = jnp.concatenate([jnp.zeros((bp, wb, kvw), F32), win_p3], axis=1)[:, -wb:]
        win_state_s = jnp.concatenate([win_prev, win_s.reshape(bs, ls_len, kvw)], axis=1)[:, -wb:]
        outs[6].append(win_state_p.reshape(kv_shape(bp, wb)))
        outs[7].append(win_state_s.reshape(kv_shape(bs, wb)))
        outs[8].append(gla_p.reshape(bp, GLA_HEADS, GLA_DK, GLA_DV))
        outs[9].append(gla_s.reshape(bs, GLA_HEADS, GLA_DK, GLA_DV))

    return (xp, xs) + tuple(jnp.stack(o) for o in outs)
```

```python
import functools
import math

import numpy as np
import jax
import jax.numpy as jnp
from jax import lax
from jax.experimental import pallas as pl
from jax.experimental.pallas import tpu as pltpu

F32 = jnp.float32
BF16 = jnp.bfloat16

PAGE_SIZE = 128
HEAD_DIM = 128
GLA_HEADS, GLA_DK, GLA_DV, GLA_GATE_RANK, GLA_TAU = 4, 64, 128, 16, 16.0
NSA_HEADS, NSA_KV_HEADS = 8, 2
NSA_CMP_BLOCK, NSA_CMP_STRIDE, NSA_SEL_BLOCK, NSA_TOP_N, NSA_WINDOW = 32, 16, 64, 16, 512
NSA_FORCE_BONUS = 1.0e4
DIFF_HEADS, DIFF_QK, DIFF_V = 4, 64, 128
ROPE_THETA = 10000.0
NORM_EPS = 1e-6
NEG = -1.0e30

LANES = 128
SUBLANES = 8
BF16_ROWS = 16
VMEM_LIMIT_BYTES = 56 * 1024 * 1024
PAGES_PER_STEP = 16

Z_NQ, Z_GQ, Z_GK, Z_GV, Z_GR = 0, 1024, 1280, 1536, 2048
Z_CMP, Z_SLC, Z_WIN, Z_DQ, Z_DK, Z_DV, Z_WIDTH = 2560, 3072, 3584, 4096, 4608, 5120, 5632
SMALL_GA, SMALL_GATE = 0, 16


def _cparams(sem):
    return pltpu.CompilerParams(dimension_semantics=sem, vmem_limit_bytes=VMEM_LIMIT_BYTES)


def _dot(a, b):
    return jnp.dot(a, b, preferred_element_type=F32)


def _dot_nt(a, b):
    return lax.dot_general(a, b, (((1,), (1,)), ((), ())), preferred_element_type=F32)


def _dot_tn(a, b):
    return lax.dot_general(a, b, (((0,), (0,)), ((), ())), preferred_element_type=F32)


def _split3(a):
    hi = a.astype(BF16)
    r = a - hi.astype(F32)
    mid = r.astype(BF16)
    lo = (r - mid.astype(F32)).astype(BF16)
    return hi, mid, lo


def _dot3_rhs_exact(a, b):
    hi, mid, lo = _split3(a)
    return _dot(hi, b) + _dot(mid, b) + _dot(lo, b)


def _dot3_lhs_exact(a, b):
    hi, mid, lo = _split3(b)
    return _dot(a, hi) + _dot(a, mid) + _dot(a, lo)


def _sigmoid(x):
    return 1.0 / (1.0 + jnp.exp(-x))


def _masked_softmax(s, mask):
    s = jnp.where(mask, s, NEG)
    m = jnp.max(s, axis=-1, keepdims=True)
    e = jnp.where(mask, jnp.exp(s - m), 0.0)
    den = jnp.sum(e, axis=-1, keepdims=True)
    return e / jnp.where(den > 0, den, 1.0)


def _mod_kernel(c_ref, w_ref, b_ref, o_ref):
    c = c_ref[...]
    a = (c * _sigmoid(c)).astype(BF16)
    o_ref[0] = _dot(a, w_ref[0].astype(BF16)) + b_ref[0]


def _modulation(c_all, ada_w, ada_b):
    depth, d, n6 = ada_w.shape
    rows = c_all.shape[0]
    tn = max(t for t in range(LANES, min(1024, n6) + 1, LANES) if n6 % t == 0)
    return pl.pallas_call(
        _mod_kernel,
        out_shape=jax.ShapeDtypeStruct((depth, rows, n6), F32),
        grid=(depth, n6 // tn),
        in_specs=[pl.BlockSpec((rows, d), lambda l, j: (0, 0)),
                  pl.BlockSpec((1, d, tn), lambda l, j: (l, 0, j)),
                  pl.BlockSpec((1, 1, tn), lambda l, j: (l, 0, j))],
        out_specs=pl.BlockSpec((1, rows, tn), lambda l, j: (l, 0, j)),
        compiler_params=_cparams(("arbitrary", "arbitrary")),
        name="adaln_modulation",
    )(c_all, ada_w, ada_b.reshape(depth, 1, n6))


def _norm_mod(x, nw, sc, sh):
    ms = jnp.mean(x * x, axis=-1, keepdims=True)
    return x * lax.rsqrt(ms + NORM_EPS) * nw * (1.0 + sc) + sh


def _in_proj_kernel(x_ref, nw_ref, sc_ref, sh_ref, w_ref, ws_ref, o_ref, os_ref, h_ref):
    @pl.when(pl.program_id(1) == 0)
    def _():
        h = _norm_mod(x_ref[...], nw_ref[...], sc_ref[0], sh_ref[0]).astype(BF16)
        h_ref[...] = h
        os_ref[...] = _dot(h, ws_ref[0])

    o_ref[...] = _dot(h_ref[...], w_ref[0])


def _mod_spec(seq_len, tm, d):
    if seq_len % tm == 0:
        per = seq_len // tm
        return pl.BlockSpec((1, 1, d), lambda i, *_: (i // per, 0, 0))
    return pl.BlockSpec((1, tm, d), lambda i, *_: (0, i, 0))


def _expand_mod(m, seq_len, tm):
    if seq_len % tm == 0:
        return m[:, None, :]
    return jnp.repeat(m, seq_len, axis=0)[None]


def _in_proj(x2, nw, sc, sh, w, w_small, layer, seq_len, tm, tn):
    m, d = x2.shape
    n = w.shape[2]
    return pl.pallas_call(
        _in_proj_kernel,
        out_shape=[jax.ShapeDtypeStruct((m, n), F32), jax.ShapeDtypeStruct((m, LANES), F32)],
        grid=(m // tm, n // tn),
        in_specs=[pl.BlockSpec((tm, d), lambda i, j: (i, 0)),
                  pl.BlockSpec((1, d), lambda i, j: (0, 0)),
                  _mod_spec(seq_len, tm, d), _mod_spec(seq_len, tm, d),
                  pl.BlockSpec((1, d, tn), lambda i, j: (layer, 0, j)),
                  pl.BlockSpec((1, d, LANES), lambda i, j: (layer, 0, 0))],
        out_specs=[pl.BlockSpec((tm, tn), lambda i, j: (i, j)), pl.BlockSpec((tm, LANES), lambda i, j: (i, 0))],
        scratch_shapes=[pltpu.VMEM((tm, d), BF16)],
        compiler_params=_cparams(("arbitrary", "arbitrary")),
        name="in_proj",
    )(x2, nw, _expand_mod(sc, seq_len, tm), _expand_mod(sh, seq_len, tm), w, w_small)


def _rope_tables(pos):
    pos = np.asarray(pos, np.float64)[:, None]

    def tab(half, reps):
        inv = ROPE_THETA ** (-np.arange(half, dtype=np.float64) / half)
        ang = pos * inv[None, :]
        c, s = np.cos(ang), np.sin(ang)
        return np.tile(np.concatenate([c, c], -1), (1, reps)), np.tile(np.concatenate([-s, s], -1), (1, reps))

    c128, s128 = tab(HEAD_DIM // 2, 1)
    c64, s64 = tab(DIFF_QK // 2, 2)
    return np.concatenate([c128, s128, c64, s64], axis=-1).astype(np.float32)


def _rope128(x, cos, sin):
    return x * cos + pltpu.roll(x, 64, 1) * sin


def _rope64(x, cos, sin, first_half):
    partner = jnp.where(first_half, pltpu.roll(x, 96, 1), pltpu.roll(x, 32, 1))
    return x * cos + partner * sin


def _rope_kernel(nq_ref, cmp_ref, slc_ref, win_ref, dq_ref, dk_ref, dv_ref, tab_ref,
                 qn_o, cmp_o, cmpb_o, slc_o, slcb_o, win_o, winb_o, dq_o, dkv_o, dkvb_o, *vt_outs):
    if vt_outs:
        slc_vt_o, diff_vt_o = vt_outs
        for g in range(NSA_KV_HEADS):
            slc_vt_o[0, g, 0] = slc_ref[:, (NSA_KV_HEADS + g) * LANES:(NSA_KV_HEADS + g + 1) * LANES].T.astype(BF16)
        for h in range(DIFF_HEADS):
            diff_vt_o[0, h, 0] = dv_ref[:, h * LANES:(h + 1) * LANES].T.astype(BF16)
    tab = tab_ref[...]
    c128, s128, c64, s64 = (tab[:, i * LANES:(i + 1) * LANES] for i in range(4))
    lane = lax.broadcasted_iota(jnp.int32, (1, LANES), 1)
    first_half = (lane % DIFF_QK) < (DIFF_QK // 2)
    nsa_scale = HEAD_DIM ** -0.5
    diff_scale = DIFF_QK ** -0.5

    for h in range(NSA_HEADS):
        sl = slice(h * LANES, (h + 1) * LANES)
        qn_o[:, sl] = (_rope128(nq_ref[:, sl], c128, s128) * nsa_scale).astype(qn_o.dtype)

    for src, dst, dstb in ((cmp_ref, cmp_o, cmpb_o), (slc_ref, slc_o, slcb_o), (win_ref, win_o, winb_o)):
        for g in range(2 * NSA_KV_HEADS):
            sl = slice(g * LANES, (g + 1) * LANES)
            v = src[:, sl]
            if g < NSA_KV_HEADS:
                v = _rope128(v, c128, s128)
            dst[:, sl] = v
            dstb[:, sl] = v.astype(BF16)

    for h in range(DIFF_HEADS):
        sl = slice(h * LANES, (h + 1) * LANES)
        dq_o[:, sl] = (_rope64(dq_ref[:, sl], c64, s64, first_half) * diff_scale).astype(dq_o.dtype)
        k = _rope64(dk_ref[:, sl], c64, s64, first_half)
        dkv_o[:, sl] = k
        dkvb_o[:, sl] = k.astype(BF16)
        sv = slice((DIFF_HEADS + h) * LANES, (DIFF_HEADS + h + 1) * LANES)
        v = dv_ref[:, sl]
        dkv_o[:, sv] = v
        dkvb_o[:, sv] = v.astype(BF16)


def _rope_split(z, tab, seq_len, tm, act_dt, transposed_values):
    m = z.shape[0]
    per = max(seq_len // tm, 1)
    tab_spec = (pl.BlockSpec((tm, 4 * LANES), lambda i: (i % per, 0)) if seq_len % tm == 0
                else pl.BlockSpec((tm, 4 * LANES), lambda i: (i, 0)))

    def zs(width, col):
        return pl.BlockSpec((tm, width), lambda i: (i, col // width))

    def os(width):
        return pl.BlockSpec((tm, width), lambda i: (i, 0))

    outs = [(1024, act_dt), (512, F32), (512, BF16), (512, F32), (512, BF16), (512, F32), (512, BF16),
            (512, act_dt), (1024, F32), (1024, BF16)]
    out_shape = [jax.ShapeDtypeStruct((m, w), dt) for w, dt in outs]
    out_specs = [os(w) for w, _ in outs]
    if transposed_values:
        assert seq_len % tm == 0
        for heads in (NSA_KV_HEADS, DIFF_HEADS):
            out_shape.append(jax.ShapeDtypeStruct((m // seq_len, heads, per, LANES, tm), BF16))
            out_specs.append(pl.BlockSpec((1, heads, 1, LANES, tm), lambda i: (i // per, 0, i % per, 0, 0)))
    return pl.pallas_call(
        _rope_kernel,
        out_shape=out_shape,
        grid=(m // tm,),
        in_specs=[zs(1024, Z_NQ), zs(512, Z_CMP), zs(512, Z_SLC), zs(512, Z_WIN),
                  zs(512, Z_DQ), zs(512, Z_DK), zs(512, Z_DV), tab_spec],
        out_specs=out_specs,
        compiler_params=_cparams(("arbitrary",)),
        name="rope_split",
    )(z, z, z, z, z, z, z, tab)


def _page_view(cache):
    depth, n_pool, page, two, heads, width = cache.shape
    return cache.reshape(depth, n_pool, page * two * heads, width)


def _page_specs(layer, page_rows, pages_per_step):
    def spec(p):
        return pl.BlockSpec((1, 1, page_rows, LANES),
                            lambda b, j, pt: (layer, pt[b, j * pages_per_step + p], 0, 0))
    return [spec(p) for p in range(pages_per_step)]


def _page_slabs(page_refs, first_slot, n, n_slots):
    return jnp.concatenate(
        [jnp.concatenate([ref[0, 0, pl.ds(first_slot + s, PAGE_SIZE, stride=n_slots), :] for s in range(n)], axis=1)
         for ref in page_refs], axis=0).astype(BF16)


def _pad_rows(x, multiple):
    pad = -x.shape[0] % multiple
    return x if pad == 0 else jnp.concatenate([x, jnp.zeros((pad, x.shape[1]), x.dtype)], axis=0)


def _gla_constants(c):
    t = np.arange(c)
    sizes = []
    s = c // 2
    while s >= 1:
        sizes.append(s)
        s //= 2
    sel, masks = [], []
    for sz in sizes:
        ref_row = (t // (2 * sz)) * (2 * sz) + sz - 1
        sel.append((t[None, :] <= ref_row[:, None]).astype(np.float32))
        same = (t[:, None] // (2 * sz)) == (t[None, :] // (2 * sz))
        masks.append((same & ((t[:, None] // sz) % 2 == 1) & ((t[None, :] // sz) % 2 == 0)).astype(np.float32))
    sel.append((t[None, :] <= t[:, None]).astype(np.float32))
    masks.append(np.eye(c, dtype=np.float32))
    return np.concatenate(sel, 0), np.stack(masks, 0)


def _gla_kernel(chunk, levels, q_ref, k_ref, v_ref, r_ref, sm_ref, gw_ref, gb_ref, nw_ref, s0_ref,
                sel_ref, mask_ref, o_ref, st_ref, state_ref):
    c = chunk
    c_in = q_ref.shape[0]
    hk = GLA_HEADS * GLA_DK
    ci = pl.program_id(1)

    @pl.when(ci == 0)
    def _():
        state_ref[...] = s0_ref[0].T

    def rows(x):
        if c_in == c:
            return x
        return jnp.concatenate([x, jnp.zeros((c - c_in, x.shape[1]), x.dtype)], axis=0)

    ga = rows(sm_ref[:, SMALL_GA:SMALL_GA + GLA_GATE_RANK])
    pre = _dot3_lhs_exact_both(ga, gw_ref[...]) + gb_ref[...]
    log_a = (jnp.minimum(pre, 0.0) - jnp.log(1.0 + jnp.exp(-jnp.abs(pre)))) / GLA_TAU
    if c_in != c:
        log_a = jnp.where(lax.broadcasted_iota(jnp.int32, (c, 1), 0) < c_in, log_a, 0.0)
    refs = _dot3_lhs_exact(sel_ref[...], log_a)
    b = refs[(levels - 1) * c:levels * c]
    q = rows(q_ref[...]) * (GLA_DK ** -0.5)
    k = rows(k_ref[...])
    v_all = rows(v_ref[...])
    lane = lax.broadcasted_iota(jnp.int32, (1, hk), 1)
    head_of_lane = lane // GLA_DK

    def stack_heads(x):
        return jnp.concatenate([jnp.where(head_of_lane == h, x, 0.0) for h in range(GLA_HEADS)], axis=0).astype(BF16)

    attn = jnp.zeros((GLA_HEADS, c, c), F32)
    for lv in range(levels):
        r = refs[lv * c:(lv + 1) * c]
        qd = stack_heads(q * jnp.exp(jnp.minimum(b - r, 0.0)))
        kd = (k * jnp.exp(jnp.minimum(r - b, 0.0))).astype(BF16)
        attn = attn + _dot_nt(qd, kd).reshape(GLA_HEADS, c, c) * mask_ref[lv][None]
    state = state_ref[...]
    inter = _dot_nt(stack_heads(q * jnp.exp(b)), state.astype(BF16))

    nw = nw_ref[...]
    for h in range(GLA_HEADS):
        vh = v_all[:, h * GLA_DV:(h + 1) * GLA_DV]
        o = (_dot(attn[h].astype(BF16), vh.astype(BF16)) + inter[h * c:(h + 1) * c])[0:c_in]
        y = o * lax.rsqrt(jnp.mean(o * o, axis=-1, keepdims=True) + NORM_EPS) * nw
        rh = r_ref[:, h * GLA_DV:(h + 1) * GLA_DV]
        o_ref[:, h * GLA_DV:(h + 1) * GLA_DV] = (y * (rh * _sigmoid(rh))).astype(o_ref.dtype)

    b_end = b[c - 1:c]
    kd = (k * jnp.exp(b_end - b)).astype(BF16)
    new_state = state * jnp.exp(b_end)
    for h in range(GLA_HEADS):
        vh = v_all[:, h * GLA_DV:(h + 1) * GLA_DV].astype(BF16)
        new_state = new_state + jnp.where(head_of_lane == h, _dot_tn(vh, kd), 0.0)
    state_ref[...] = new_state

    @pl.when(ci == pl.num_programs(1) - 1)
    def _():
        st_ref[0] = new_state.T


def _dot3_lhs_exact_both(a, b):
    ah, am, al = _split3(a)
    bh, bm, bl = _split3(b)
    return (_dot(ah, bh) + _dot(ah, bm) + _dot(am, bh)) + (_dot(ah, bl) + _dot(am, bm) + _dot(al, bh))


def _gla(z, z_small, gate_w, gate_b, norm_w, s0, batch, seq_len, chunk, act_dt):
    m = z.shape[0]
    nc = seq_len // chunk
    hk, hv = GLA_HEADS * GLA_DK, GLA_HEADS * GLA_DV
    comp_rows = max(chunk, BF16_ROWS)
    sel, masks = _gla_constants(comp_rows)
    levels = masks.shape[0]

    def zs(width, col):
        return pl.BlockSpec((chunk, width), lambda b, c: (b * nc + c, col // width))

    const2 = lambda b, c: (0, 0)
    return pl.pallas_call(
        functools.partial(_gla_kernel, comp_rows, levels),
        out_shape=[jax.ShapeDtypeStruct((m, hv), act_dt), jax.ShapeDtypeStruct((batch, hk, GLA_DV), F32)],
        grid=(batch, nc),
        in_specs=[zs(hk, Z_GQ), zs(hk, Z_GK), zs(hv, Z_GV), zs(hv, Z_GR), zs(LANES, 0),
                  pl.BlockSpec((GLA_GATE_RANK, hk), const2), pl.BlockSpec((1, hk), const2),
                  pl.BlockSpec((1, GLA_DV), const2),
                  pl.BlockSpec((1, hk, GLA_DV), lambda b, c: (b, 0, 0)),
                  pl.BlockSpec((levels * comp_rows, comp_rows), const2),
                  pl.BlockSpec((levels, comp_rows, comp_rows), lambda b, c: (0, 0, 0))],
        out_specs=[pl.BlockSpec((chunk, hv), lambda b, c: (b * nc + c, 0)),
                   pl.BlockSpec((1, hk, GLA_DV), lambda b, c: (b, 0, 0))],
        scratch_shapes=[pltpu.VMEM((GLA_DV, hk), F32)],
        compiler_params=_cparams(("arbitrary", "arbitrary")),
        name="gla",
    )(z, z, z, z, z_small, gate_w, gate_b.reshape(1, hk), norm_w.reshape(1, GLA_DV), s0,
      jnp.asarray(sel, BF16), jnp.asarray(masks, F32))


def _online_update(s, mask, v, m_ref, l_ref, acc_ref):
    if mask is not None:
        s = jnp.where(mask, s, NEG)
    m_prev = m_ref[...]
    m_new = jnp.maximum(m_prev, jnp.max(s, axis=-1, keepdims=True))
    p = jnp.exp(s - m_new)
    if mask is not None:
        p = jnp.where(mask, p, 0.0)
    alpha = jnp.exp(m_prev - m_new)
    l_ref[...] = alpha * l_ref[...] + jnp.sum(p, axis=-1, keepdims=True)
    acc_ref[...] = alpha * acc_ref[...] + _dot(p.astype(BF16), v)
    m_ref[...] = m_new


def _online_update_keymajor(s, mask, vt, m_ref, l_ref, acc_ref):
    if mask is not None:
        s = jnp.where(mask, s, NEG)
    m_prev = m_ref[...]
    m_new = jnp.maximum(m_prev, jnp.max(s, axis=0, keepdims=True))
    p = jnp.exp(s - m_new)
    if mask is not None:
        p = jnp.where(mask, p, 0.0)
    alpha = jnp.exp(m_prev - m_new)
    l_ref[...] = alpha * l_ref[...] + jnp.sum(p, axis=0, keepdims=True)
    acc_ref[...] = alpha * acc_ref[...] + _dot(vt, p.astype(BF16))
    m_ref[...] = m_new


def _diff_kernel(tq, tk, qpos0, lam_init, q_ref, k_ref, vt_ref, lam_ref, nw_ref, o_ref,
                 qq_ref, m_ref, l_ref, acc_ref):
    i, j = pl.program_id(2), pl.program_id(3)
    q_lo = qpos0 + i * tq
    last_j = (q_lo + tq - 1) // tk

    @pl.when(j == 0)
    def _():
        q = q_ref[...].astype(F32)
        lane = lax.broadcasted_iota(jnp.int32, (1, LANES), 1)
        qq_ref[...] = jnp.concatenate([jnp.where(lane < DIFF_QK, q, 0.0),
                                       jnp.where(lane >= DIFF_QK, q, 0.0)], axis=0).astype(BF16)
        m_ref[...] = jnp.full_like(m_ref, NEG)
        l_ref[...] = jnp.zeros_like(l_ref)
        acc_ref[...] = jnp.zeros_like(acc_ref)

    def step(masked):
        s = _dot_nt(k_ref[0], qq_ref[...])
        mask = None
        if masked:
            kpos = j * tk + lax.broadcasted_iota(jnp.int32, (tk, 1), 0)
            qpos = q_lo + lax.broadcasted_iota(jnp.int32, (1, tq), 1)
            mk = kpos <= qpos
            mask = jnp.concatenate([mk, mk], axis=1)
        _online_update_keymajor(s, mask, vt_ref[0, 0, 0], m_ref, l_ref, acc_ref)

    fully_visible = (j + 1) * tk - 1 <= q_lo

    @pl.when(fully_visible)
    def _():
        step(False)

    @pl.when(jnp.logical_and(jnp.logical_not(fully_visible), j <= last_j))
    def _():
        step(True)

    @pl.when(j == pl.num_programs(3) - 1)
    def _():
        l = l_ref[...]
        o12 = (acc_ref[...] / jnp.where(l > 0, l, 1.0)).T
        o_ref[...] = _diff_finish(o12, lam_ref, nw_ref, lam_init).astype(o_ref.dtype)


def _diff_attention(dq, kv, vt, lam_p, norm_w, batch, seq_len, qpos0, lam_init, tq, tk):
    m = dq.shape[0]
    nq = seq_len // tq
    nk = kv.shape[1] // tk
    assert vt.shape[2] == nk and vt.shape[4] == tk

    def last_tile(i):
        return (qpos0 + i * tq + tq - 1) // tk

    return pl.pallas_call(
        functools.partial(_diff_kernel, tq, tk, qpos0, lam_init),
        out_shape=jax.ShapeDtypeStruct((m, DIFF_HEADS * DIFF_V), dq.dtype),
        grid=(batch, DIFF_HEADS, nq, nk),
        in_specs=[pl.BlockSpec((tq, LANES), lambda b, h, i, j: (b * nq + i, h)),
                  pl.BlockSpec((1, tk, LANES), lambda b, h, i, j: (b, jnp.minimum(j, last_tile(i)), h)),
                  pl.BlockSpec((1, 1, 1, LANES, tk), lambda b, h, i, j: (b, h, jnp.minimum(j, last_tile(i)), 0, 0)),
                  pl.BlockSpec((4, DIFF_QK), lambda b, h, i, j: (0, 0)),
                  pl.BlockSpec((1, DIFF_V), lambda b, h, i, j: (0, 0))],
        out_specs=pl.BlockSpec((tq, LANES), lambda b, h, i, j: (b * nq + i, h)),
        scratch_shapes=[pltpu.VMEM((2 * tq, LANES), BF16), pltpu.VMEM((1, 2 * tq), F32),
                        pltpu.VMEM((1, 2 * tq), F32), pltpu.VMEM((DIFF_V, 2 * tq), F32)],
        compiler_params=_cparams(("arbitrary",) * 4),
        name="diff_attention",
    )(dq, kv, vt, lam_p, norm_w.reshape(1, DIFF_V))


def _diff_finish(o12, lam_ref, nw_ref, lam_init):
    half = o12.shape[0] // 2
    lam_p = lam_ref[...]
    lam = (jnp.exp(jnp.sum(lam_p[0:1] * lam_p[1:2], axis=-1, keepdims=True))
           - jnp.exp(jnp.sum(lam_p[2:3] * lam_p[3:4], axis=-1, keepdims=True)) + lam_init)
    o = o12[0:half] - lam * o12[half:]
    y = o * lax.rsqrt(jnp.mean(o * o, axis=-1, keepdims=True) + NORM_EPS) * nw_ref[...]
    return y * (1.0 - lam_init)


def _diff_paged_kernel(pps, lam_init, pt_ref, q_ref, new_ref, lam_ref, nw_ref, *rest):
    page_refs, o_ref = rest[:pps], rest[pps]
    qbd_ref, m_ref, l_ref, acc_ref = rest[pps + 1:]
    j = pl.program_id(1)
    n_new = q_ref.shape[0]
    rows_h = 2 * n_new
    hw = DIFF_HEADS * LANES

    @pl.when(j == 0)
    def _():
        q = q_ref[...]
        lane = lax.broadcasted_iota(jnp.int32, (1, LANES), 1)
        blocks = []
        for h in range(DIFF_HEADS):
            qh = q[:, h * LANES:(h + 1) * LANES]
            q12 = jnp.concatenate([jnp.where(lane < DIFF_QK, qh, 0.0), jnp.where(lane >= DIFF_QK, qh, 0.0)], axis=0)
            blocks.append(jnp.concatenate(
                [q12 if hh == h else jnp.zeros((rows_h, LANES), F32) for hh in range(DIFF_HEADS)], axis=1))
        qbd_ref[...] = jnp.concatenate(blocks, axis=0).astype(BF16)
        m_ref[...] = jnp.full_like(m_ref, NEG)
        l_ref[...] = jnp.zeros_like(l_ref)
        acc_ref[...] = jnp.zeros_like(acc_ref)

    k = _page_slabs(page_refs, 0, DIFF_HEADS, 2 * DIFF_HEADS)
    v = _page_slabs(page_refs, DIFF_HEADS, DIFF_HEADS, 2 * DIFF_HEADS)
    _online_update(_dot_nt(qbd_ref[...], k), None, v, m_ref, l_ref, acc_ref)

    @pl.when(j == pl.num_programs(1) - 1)
    def _():
        new = new_ref[...]
        kn = _pad_rows(new[:, 0:hw], BF16_ROWS).astype(BF16)
        vn = _pad_rows(new[:, hw:2 * hw], BF16_ROWS).astype(BF16)
        t = lax.broadcasted_iota(jnp.int32, (DIFF_HEADS * rows_h, 1), 0) % n_new
        u = lax.broadcasted_iota(jnp.int32, (1, kn.shape[0]), 1)
        _online_update(_dot_nt(qbd_ref[...], kn), u <= t, vn, m_ref, l_ref, acc_ref)
        l = l_ref[...]
        o12 = acc_ref[...] / jnp.where(l > 0, l, 1.0)
        for h in range(DIFF_HEADS):
            r = slice(h * rows_h, (h + 1) * rows_h)
            o_ref[:, h * LANES:(h + 1) * LANES] = _diff_finish(o12[r, h * LANES:(h + 1) * LANES], lam_ref, nw_ref,
                                                               lam_init)


def _diff_attention_paged(dq, new_kv, cache_view, layer, page_table, lam_p, norm_w, lam_init, pps):
    b, n_pages = page_table.shape
    n_new = dq.shape[0] // b
    hw = DIFF_HEADS * LANES
    rows = DIFF_HEADS * 2 * n_new
    assert n_pages % pps == 0 and n_new % SUBLANES == 0
    const = lambda i, j, pt: (0, 0)
    return pl.pallas_call(
        functools.partial(_diff_paged_kernel, pps, lam_init),
        out_shape=jax.ShapeDtypeStruct(dq.shape, F32),
        grid_spec=pltpu.PrefetchScalarGridSpec(
            num_scalar_prefetch=1,
            grid=(b, n_pages // pps),
            in_specs=[pl.BlockSpec((n_new, hw), lambda i, j, pt: (i, 0)),
                      pl.BlockSpec((n_new, 2 * hw), lambda i, j, pt: (i, 0)),
                      pl.BlockSpec((4, DIFF_QK), const), pl.BlockSpec((1, DIFF_V), const)]
                     + _page_specs(layer, PAGE_SIZE * 2 * DIFF_HEADS, pps),
            out_specs=pl.BlockSpec((n_new, hw), lambda i, j, pt: (i, 0)),
            scratch_shapes=[pltpu.VMEM((rows, hw), BF16), pltpu.VMEM((rows, 1), F32), pltpu.VMEM((rows, 1), F32),
                            pltpu.VMEM((rows, hw), F32)]),
        compiler_params=_cparams(("arbitrary", "arbitrary")),
        name="diff_attention_paged",
    )(page_table, dq, new_kv, lam_p, norm_w.reshape(1, DIFF_V), *([cache_view] * pps))


def _compress_weights(w, pe):
    k = NSA_CMP_STRIDE * HEAD_DIM
    return (jnp.concatenate([w[:, :k], w[:, k:]], axis=2).astype(BF16),
            pe.reshape(2, NSA_CMP_BLOCK // NSA_CMP_STRIDE, k))


def _compress_blocks(segments, rows, w_ref, pe_ref, o_ref):
    for c in range(2):
        pe = _dot(_pad_rows(pe_ref[c], BF16_ROWS).astype(BF16), w_ref[c])
        pe_term = pe[0:1, 0:HEAD_DIM] + pe[1:2, HEAD_DIM:2 * HEAD_DIM]
        for g in range(NSA_KV_HEADS):
            both = _dot(segments(c * NSA_KV_HEADS + g), w_ref[c])
            o_ref[0, c, g] = (both[:, 0:HEAD_DIM] + pltpu.roll(both[:, HEAD_DIM:2 * HEAD_DIM], rows - 1, 0)
                              + pe_term).astype(BF16)


def _compress_kernel(x_ref, w_ref, pe_ref, o_ref):
    tok_w = 2 * NSA_KV_HEADS * HEAD_DIM

    def segments(slot):
        return jnp.concatenate([x_ref[0, :, t * tok_w + slot * HEAD_DIM:t * tok_w + (slot + 1) * HEAD_DIM]
                                for t in range(NSA_CMP_STRIDE)], axis=1)

    _compress_blocks(segments, x_ref.shape[1], w_ref, pe_ref, o_ref)


def _compress_paged_kernel(pps, n_past_seg, pt_ref, new_ref, w_ref, pe_ref, *rest):
    page_refs, o_ref, x_ref = rest[:pps], rest[pps], rest[pps + 1]
    j = pl.program_id(1)
    slots = 2 * NSA_KV_HEADS
    seg_per_page = PAGE_SIZE // NSA_CMP_STRIDE
    rows = x_ref.shape[1]

    for pp in range(0, pps, 2):
        row0 = pl.multiple_of((j * pps + pp) * seg_per_page, 2 * seg_per_page)
        for slot in range(slots):
            for t in range(NSA_CMP_STRIDE):
                pair = [page_refs[pp + q][0, 0, pl.ds(t * slots + slot, seg_per_page, stride=NSA_CMP_STRIDE * slots), :]
                        for q in range(2)]
                x_ref[slot, pl.ds(row0, 2 * seg_per_page), t * HEAD_DIM:(t + 1) * HEAD_DIM] = (
                    jnp.concatenate(pair, axis=0).astype(BF16))

    @pl.when(j == pl.num_programs(1) - 1)
    def _():
        new = new_ref[...]
        first = lax.broadcasted_iota(jnp.int32, (rows - n_past_seg, 1), 0) == 0
        for slot in range(slots):
            for t in range(NSA_CMP_STRIDE):
                if t < new.shape[0]:
                    tail = jnp.where(first, new[t:t + 1, slot * HEAD_DIM:(slot + 1) * HEAD_DIM], 0.0)
                else:
                    tail = jnp.zeros((rows - n_past_seg, HEAD_DIM), F32)
                x_ref[slot, n_past_seg:rows, t * HEAD_DIM:(t + 1) * HEAD_DIM] = tail.astype(BF16)
        _compress_blocks(lambda slot: x_ref[slot], rows, w_ref, pe_ref, o_ref)


def _compress_paged(cache_view, layer, page_table, new_rows, w, pe, pps):
    b, n_pages = page_table.shape
    n_new = new_rows.shape[0] // b
    slots = 2 * NSA_KV_HEADS
    seg_per_page = PAGE_SIZE // NSA_CMP_STRIDE
    n_past_seg = n_pages * seg_per_page
    rows = n_past_seg + 2 * BF16_ROWS
    assert n_new <= NSA_CMP_STRIDE and pps % 2 == 0 and n_pages % pps == 0
    return pl.pallas_call(
        functools.partial(_compress_paged_kernel, pps, n_past_seg),
        out_shape=jax.ShapeDtypeStruct((b, 2, NSA_KV_HEADS, rows, HEAD_DIM), BF16),
        grid_spec=pltpu.PrefetchScalarGridSpec(
            num_scalar_prefetch=1,
            grid=(b, n_pages // pps),
            in_specs=[pl.BlockSpec((n_new, slots * HEAD_DIM), lambda i, j, pt: (i, 0)),
                      pl.BlockSpec(w.shape, lambda i, j, pt: (0, 0, 0)),
                      pl.BlockSpec(pe.shape, lambda i, j, pt: (0, 0, 0))]
                     + _page_specs(layer, PAGE_SIZE * slots, pps),
            out_specs=pl.BlockSpec((1, 2, NSA_KV_HEADS, rows, HEAD_DIM), lambda i, j, pt: (i, 0, 0, 0, 0)),
            scratch_shapes=[pltpu.VMEM((slots, rows, NSA_CMP_STRIDE * HEAD_DIM), BF16)]),
        compiler_params=_cparams(("arbitrary", "arbitrary")),
        name="nsa_compress_paged",
    )(page_table, new_rows, w, pe, *([cache_view] * pps))


def _compress(seg, w, pe):
    b, rows, width = seg.shape
    return pl.pallas_call(
        _compress_kernel,
        out_shape=jax.ShapeDtypeStruct((b, 2, NSA_KV_HEADS, rows, HEAD_DIM), BF16),
        grid=(b,),
        in_specs=[pl.BlockSpec((1, rows, width), lambda i: (i, 0, 0)),
                  pl.BlockSpec(w.shape, lambda i: (0, 0, 0)),
                  pl.BlockSpec(pe.shape, lambda i: (0, 0, 0))],
        out_specs=pl.BlockSpec((1, 2, NSA_KV_HEADS, rows, HEAD_DIM), lambda i: (i, 0, 0, 0, 0)),
        compiler_params=_cparams(("arbitrary",)),
        name="nsa_compress",
    )(seg, w, pe)


def _nsa_band(n_cmp_rows, n_slc_pad):
    ratio = NSA_SEL_BLOCK // NSA_CMP_STRIDE
    span = NSA_CMP_BLOCK // NSA_CMP_STRIDE
    n = np.arange(n_cmp_rows)[:, None]
    j = np.arange(n_slc_pad)[None, :]
    return ((n >= ratio * j - (span - 1)) & (n <= ratio * j + ratio - 1)).astype(np.float32)


def _nsa_compressed(qs, qpos_rows, ck, cv):
    cmp_end = lax.broadcasted_iota(jnp.int32, (1, ck.shape[0]), 1) * NSA_CMP_STRIDE + (NSA_CMP_BLOCK - 1)
    p_c = _masked_softmax(_dot_nt(qs, ck), cmp_end <= qpos_rows)
    return p_c, _dot(p_c.astype(BF16), cv)


def _nsa_block_scores(p_c, qpos, band_ref):
    tq = qpos.shape[0]
    imp = p_c[0:tq]
    for n in range(1, p_c.shape[0] // tq):
        imp = imp + p_c[n * tq:(n + 1) * tq]
    p_slc = _dot3_rhs_exact(_pad_rows(imp, BF16_ROWS), band_ref[...])[0:tq]
    blk = lax.broadcasted_iota(jnp.int32, (1, band_ref.shape[1]), 1)
    cur = qpos // NSA_SEL_BLOCK
    valid = blk * NSA_SEL_BLOCK <= qpos
    forced = (blk == 0) | (blk == cur) | (blk == cur - 1)
    return jnp.where(valid, p_slc + jnp.where(forced, NSA_FORCE_BONUS, 0.0), NEG)


def _nsa_select(score, score_ref, n_slc):
    score_t = score.T
    score_ref[...] = score_t
    blk_t = lax.broadcasted_iota(jnp.int32, (score_t.shape[0], 1), 0)

    def count(jp, cnt):
        row = score_ref[pl.ds(jp, 1), :]
        beats = (row > score_t) | ((row == score_t) & (jp < blk_t))
        return cnt + jnp.where(beats, 1.0, 0.0)

    cnt = lax.fori_loop(0, n_slc, count, jnp.zeros(score_t.shape, F32))
    return jnp.where((cnt < NSA_TOP_N) & (score_t > 0.5 * NEG), 1.0, 0.0)


def _nsa_window(qs, qpos_rows, kw, vw, first_pos):
    wpos = first_pos + lax.broadcasted_iota(jnp.int32, (1, kw.shape[0]), 1)
    mask_w = (wpos <= qpos_rows) & (wpos > qpos_rows - NSA_WINDOW) & (wpos >= 0)
    return _dot(_masked_softmax(_dot_nt(qs, kw), mask_w).astype(BF16), vw)


def _nsa_kernel(tq, tk, qpos0, n_slc, q_ref, sm_ref, ck_ref, cv_ref, sk_ref, svt_ref, wk_ref, wv_ref,
                band_ref, o_ref, score_ref, sel_ref, m_ref, l_ref, acc_ref):
    i = pl.program_id(1)
    hpg = NSA_HEADS // NSA_KV_HEADS
    q_lo = qpos0 + i * tq
    qpos = q_lo + lax.broadcasted_iota(jnp.int32, (tq, 1), 0)
    qpos_lanes = q_lo + lax.broadcasted_iota(jnp.int32, (1, tq), 1)
    qpos_rows = jnp.concatenate([qpos] * hpg, axis=0)
    gates = _sigmoid(sm_ref[:, SMALL_GATE:SMALL_GATE + 3 * NSA_HEADS])
    blocks_per_tile = tk // NSA_SEL_BLOCK

    for g in range(NSA_KV_HEADS):
        q = q_ref[:, g * hpg * LANES:(g + 1) * hpg * LANES]
        qs = jnp.concatenate([q[:, n * LANES:(n + 1) * LANES] for n in range(hpg)], axis=0).astype(BF16)

        p_c, o_c = _nsa_compressed(qs, qpos_rows, ck_ref[0, 0, g], cv_ref[0, 0, g])
        n_rank = jnp.minimum((q_lo + tq - 1) // NSA_SEL_BLOCK + 1, n_slc)
        sel_ref[...] = _nsa_select(_nsa_block_scores(p_c, qpos, band_ref), score_ref, n_rank)

        m_ref[...] = jnp.full_like(m_ref, NEG)
        l_ref[...] = jnp.zeros_like(l_ref)
        acc_ref[...] = jnp.zeros_like(acc_ref)

        def key_tile(kt, carry):
            start = pl.multiple_of(kt * tk, tk)
            k = sk_ref[0, pl.ds(start, tk), g * LANES:(g + 1) * LANES]
            blocks = sel_ref[pl.ds(pl.multiple_of(kt * blocks_per_tile, blocks_per_tile), blocks_per_tile), :]
            picked = jnp.concatenate([jnp.broadcast_to(blocks[r:r + 1], (NSA_SEL_BLOCK, tq))
                                      for r in range(blocks_per_tile)], axis=0) > 0.5
            kpos = kt * tk + lax.broadcasted_iota(jnp.int32, (tk, 1), 0)
            mk = picked & (kpos <= qpos_lanes)
            mask = jnp.concatenate([mk] * hpg, axis=1)
            _online_update_keymajor(_dot_nt(k, qs), mask, svt_ref[0, g, kt], m_ref, l_ref, acc_ref)
            return carry

        lax.fori_loop(0, (q_lo + tq - 1) // tk + 1, key_tile, 0)
        l = l_ref[...]
        o_st = acc_ref[...] / jnp.where(l > 0, l, 1.0)
        o_s = jnp.concatenate([o_st[:, n * tq:(n + 1) * tq].T for n in range(hpg)], axis=0)

        wrows = -(-(tq + NSA_WINDOW) // BF16_ROWS) * BF16_ROWS
        wstart = pl.multiple_of(i * tq, tq)
        o_w = _nsa_window(qs, qpos_rows, wk_ref[0, pl.ds(wstart, wrows), g * LANES:(g + 1) * LANES],
                          wv_ref[0, pl.ds(wstart, wrows), g * LANES:(g + 1) * LANES], q_lo - NSA_WINDOW)

        for n in range(hpg):
            h = g * hpg + n
            r = slice(n * tq, (n + 1) * tq)
            o = (gates[:, 3 * h:3 * h + 1] * o_c[r] + gates[:, 3 * h + 1:3 * h + 2] * o_s[r]
                 + gates[:, 3 * h + 2:3 * h + 3] * o_w[r])
            o_ref[:, h * LANES:(h + 1) * LANES] = o.astype(o_ref.dtype)


def _nsa_attention(qn, z_small, comp, slc_buf, slc_vt, win_all, batch, seq_len, qpos0, tq, tk):
    m = qn.shape[0]
    nq = seq_len // tq
    t_keys = slc_buf.shape[1]
    n_cmp_rows = comp.shape[3]
    n_slc = -(-(qpos0 + seq_len) // NSA_SEL_BLOCK)
    nsp = -(-n_slc // LANES) * LANES
    assert tq % LANES == 0 and tk % NSA_SEL_BLOCK == 0 and slc_vt.shape[2] * tk == t_keys and nsp * NSA_SEL_BLOCK >= t_keys
    band = _nsa_band(n_cmp_rows, nsp)
    hw = NSA_HEADS * HEAD_DIM
    kvw = NSA_KV_HEADS * HEAD_DIM
    rows = (NSA_HEADS // NSA_KV_HEADS) * tq
    wlen = win_all.shape[1]
    return pl.pallas_call(
        functools.partial(_nsa_kernel, tq, tk, qpos0, n_slc),
        out_shape=jax.ShapeDtypeStruct((m, hw), qn.dtype),
        grid=(batch, nq),
        in_specs=[pl.BlockSpec((tq, hw), lambda b, i: (b * nq + i, 0)),
                  pl.BlockSpec((tq, LANES), lambda b, i: (b * nq + i, 0)),
                  pl.BlockSpec((1, 1, NSA_KV_HEADS, n_cmp_rows, HEAD_DIM), lambda b, i: (b, 0, 0, 0, 0)),
                  pl.BlockSpec((1, 1, NSA_KV_HEADS, n_cmp_rows, HEAD_DIM), lambda b, i: (b, 1, 0, 0, 0)),
                  pl.BlockSpec((1, t_keys, kvw), lambda b, i: (b, 0, 0)),
                  pl.BlockSpec((1,) + slc_vt.shape[1:], lambda b, i: (b, 0, 0, 0, 0)),
                  pl.BlockSpec((1, wlen, kvw), lambda b, i: (b, 0, 0)),
                  pl.BlockSpec((1, wlen, kvw), lambda b, i: (b, 0, 1)),
                  pl.BlockSpec((n_cmp_rows, nsp), lambda b, i: (0, 0))],
        out_specs=pl.BlockSpec((tq, hw), lambda b, i: (b * nq + i, 0)),
        scratch_shapes=[pltpu.VMEM((nsp, tq), F32), pltpu.VMEM((nsp, tq), F32), pltpu.VMEM((1, rows), F32),
                        pltpu.VMEM((1, rows), F32), pltpu.VMEM((HEAD_DIM, rows), F32)],
        compiler_params=_cparams(("arbitrary", "arbitrary")),
        name="nsa_attention",
    )(qn, z_small, comp, comp, slc_buf, slc_vt, win_all, win_all, jnp.asarray(band, BF16))


def _nsa_paged_kernel(pps, qpos0, n_slc, pt_ref, q_ref, sm_ref, ck_ref, cv_ref, new_ref, wk_ref, wv_ref, band_ref,
                      *rest):
    page_refs, o_ref = rest[:pps], rest[pps]
    score_ref, qbd_ref, sel_ref, m_ref, l_ref, acc_ref, oc_ref, ow_ref = rest[pps + 1:]
    j = pl.program_id(1)
    n_new = q_ref.shape[0]
    groups = NSA_KV_HEADS
    hpg = NSA_HEADS // groups
    rows_g = hpg * n_new
    gw = groups * LANES
    nsp = band_ref.shape[1]
    qpos = qpos0 + lax.broadcasted_iota(jnp.int32, (n_new, 1), 0)

    @pl.when(j == 0)
    def _():
        q = q_ref[...]
        qpos_rows = jnp.concatenate([qpos] * hpg, axis=0)
        scores, blocks = [], []
        for g in range(groups):
            qf = jnp.concatenate([q[:, (g * hpg + n) * LANES:(g * hpg + n + 1) * LANES] for n in range(hpg)], axis=0)
            qs = qf.astype(BF16)
            p_c, o_c = _nsa_compressed(qs, qpos_rows, ck_ref[0, 0, g], cv_ref[0, 0, g])
            oc_ref[g * rows_g:(g + 1) * rows_g] = o_c
            scores.append(_nsa_block_scores(p_c, qpos, band_ref))
            ow_ref[g * rows_g:(g + 1) * rows_g] = _nsa_window(
                qs, qpos_rows, wk_ref[0, :, g * LANES:(g + 1) * LANES], wv_ref[0, :, g * LANES:(g + 1) * LANES],
                qpos0 - NSA_WINDOW)
            blocks.append(jnp.concatenate(
                [qf if gg == g else jnp.zeros((rows_g, LANES), F32) for gg in range(groups)], axis=1))
        qbd_ref[...] = jnp.concatenate(blocks, axis=0).astype(BF16)
        pad = jnp.full((score_ref.shape[1] - groups * n_new, nsp), NEG, F32)
        sel = _nsa_select(jnp.concatenate(scores + [pad], axis=0), score_ref, n_slc).T
        sel_ref[...] = jnp.concatenate(
            [sel[g * n_new:(g + 1) * n_new] for g in range(groups) for _ in range(hpg)], axis=0).astype(BF16)
        m_ref[...] = jnp.full_like(m_ref, NEG)
        l_ref[...] = jnp.zeros_like(l_ref)
        acc_ref[...] = jnp.zeros_like(acc_ref)

    k = _page_slabs(page_refs, 0, groups, 2 * groups)
    v = _page_slabs(page_refs, groups, groups, 2 * groups)
    n_keys = pps * PAGE_SIZE
    key_blk = (j * n_keys + lax.broadcasted_iota(jnp.int32, (1, n_keys), 1)) // NSA_SEL_BLOCK
    expand = jnp.where(lax.broadcasted_iota(jnp.int32, (nsp, 1), 0) == key_blk, 1.0, 0.0).astype(BF16)
    picked = _dot(sel_ref[...], expand) > 0.5
    _online_update(_dot_nt(qbd_ref[...], k), picked, v, m_ref, l_ref, acc_ref)

    @pl.when(j == pl.num_programs(1) - 1)
    def _():
        new = new_ref[...]
        kn = _pad_rows(new[:, 0:gw], BF16_ROWS).astype(BF16)
        vn = _pad_rows(new[:, gw:2 * gw], BF16_ROWS).astype(BF16)
        new_blk = qpos0 // NSA_SEL_BLOCK
        t = lax.broadcasted_iota(jnp.int32, (groups * rows_g, 1), 0) % n_new
        u = lax.broadcasted_iota(jnp.int32, (1, kn.shape[0]), 1)
        mask = (u <= t) & (sel_ref[:, new_blk:new_blk + 1] > 0.5)
        _online_update(_dot_nt(qbd_ref[...], kn), mask, vn, m_ref, l_ref, acc_ref)
        l = l_ref[...]
        o_sel = acc_ref[...] / jnp.where(l > 0, l, 1.0)
        gates = _sigmoid(sm_ref[:, SMALL_GATE:SMALL_GATE + 3 * NSA_HEADS])
        for g in range(groups):
            for n in range(hpg):
                h = g * hpg + n
                r = slice(g * rows_g + n * n_new, g * rows_g + (n + 1) * n_new)
                o_ref[:, h * LANES:(h + 1) * LANES] = (
                    gates[:, 3 * h:3 * h + 1] * oc_ref[r] + gates[:, 3 * h + 1:3 * h + 2] * o_sel[r, g * LANES:(g + 1) * LANES]
                    + gates[:, 3 * h + 2:3 * h + 3] * ow_ref[r])


def _nsa_attention_paged(qn, z_small, comp, new_kv, win_all, cache_view, layer, page_table, qpos0, pps):
    b, n_pages = page_table.shape
    n_new = qn.shape[0] // b
    groups = NSA_KV_HEADS
    hw = NSA_HEADS * HEAD_DIM
    gw = groups * HEAD_DIM
    rows = NSA_HEADS * n_new
    n_cmp_rows = comp.shape[3]
    n_slc = -(-(qpos0 + n_new) // NSA_SEL_BLOCK)
    nsp = -(-n_slc // LANES) * LANES
    wlen = win_all.shape[1]
    assert n_pages % pps == 0 and qpos0 == n_pages * PAGE_SIZE and n_new <= NSA_SEL_BLOCK
    assert qpos0 % NSA_SEL_BLOCK == 0 and groups * n_new <= LANES and n_new % SUBLANES == 0
    band = _nsa_band(n_cmp_rows, nsp)
    const = lambda i, j, pt: (0, 0)
    return pl.pallas_call(
        functools.partial(_nsa_paged_kernel, pps, qpos0, n_slc),
        out_shape=jax.ShapeDtypeStruct(qn.shape, F32),
        grid_spec=pltpu.PrefetchScalarGridSpec(
            num_scalar_prefetch=1,
            grid=(b, n_pages // pps),
            in_specs=[pl.BlockSpec((n_new, hw), lambda i, j, pt: (i, 0)),
                      pl.BlockSpec((n_new, LANES), lambda i, j, pt: (i, 0)),
                      pl.BlockSpec((1, 1, groups, n_cmp_rows, HEAD_DIM), lambda i, j, pt: (i, 0, 0, 0, 0)),
                      pl.BlockSpec((1, 1, groups, n_cmp_rows, HEAD_DIM), lambda i, j, pt: (i, 1, 0, 0, 0)),
                      pl.BlockSpec((n_new, 2 * gw), lambda i, j, pt: (i, 0)),
                      pl.BlockSpec((1, wlen, gw), lambda i, j, pt: (i, 0, 0)),
                      pl.BlockSpec((1, wlen, gw), lambda i, j, pt: (i, 0, 1)),
                      pl.BlockSpec((n_cmp_rows, nsp), const)]
                     + _page_specs(layer, PAGE_SIZE * 2 * groups, pps),
            out_specs=pl.BlockSpec((n_new, hw), lambda i, j, pt: (i, 0)),
            scratch_shapes=[pltpu.VMEM((nsp, LANES), F32), pltpu.VMEM((rows, gw), BF16), pltpu.VMEM((rows, nsp), BF16),
                            pltpu.VMEM((rows, 1), F32), pltpu.VMEM((rows, 1), F32), pltpu.VMEM((rows, gw), F32),
                            pltpu.VMEM((rows, HEAD_DIM), F32), pltpu.VMEM((rows, HEAD_DIM), F32)]),
        compiler_params=_cparams(("arbitrary", "arbitrary")),
        name="nsa_attention_paged",
    )(page_table, qn, z_small, comp, comp, new_kv, win_all, win_all, jnp.asarray(band, BF16),
      *([cache_view] * pps))


def _out_proj_kernel(x_ref, og_ref, on_ref, od_ref, w_ref, nw_ref, g_ref, o_ref):
    a = jnp.concatenate([og_ref[...].astype(BF16), on_ref[...].astype(BF16), od_ref[...].astype(BF16)], axis=-1)
    mix = _dot(a, w_ref[0])
    y = mix * lax.rsqrt(jnp.mean(mix * mix, axis=-1, keepdims=True) + NORM_EPS) * nw_ref[...]
    o_ref[...] = x_ref[...] + g_ref[0] * y


def _out_proj(x2, o_gla, o_nsa, o_d, w, layer, nw, gate, seq_len, tm):
    m, d = x2.shape
    row = lambda width: pl.BlockSpec((tm, width), lambda i: (i, 0))
    return pl.pallas_call(
        _out_proj_kernel,
        out_shape=jax.ShapeDtypeStruct((m, d), F32),
        grid=(m // tm,),
        in_specs=[row(d), row(o_gla.shape[1]), row(o_nsa.shape[1]), row(o_d.shape[1]),
                  pl.BlockSpec((1,) + w.shape[1:], lambda i: (layer, 0, 0)), pl.BlockSpec((1, d), lambda i: (0, 0)),
                  _mod_spec(seq_len, tm, d)],
        out_specs=row(d),
        compiler_params=_cparams(("arbitrary",)),
        name="out_proj",
    )(x2, o_gla, o_nsa, o_d, w, nw, _expand_mod(gate, seq_len, tm))


def _ffn_kernel(x_ref, nw2_ref, sc_ref, sh_ref, wg_ref, wu_ref, wo_ref, nw3_ref, g_ref, o_ref, h_ref, acc_ref):
    j = pl.program_id(1)

    @pl.when(j == 0)
    def _():
        h_ref[...] = _norm_mod(x_ref[...], nw2_ref[...], sc_ref[0], sh_ref[0]).astype(BF16)
        acc_ref[...] = jnp.zeros_like(acc_ref)

    h = h_ref[...]
    gate = _dot(h, wg_ref[0])
    up = _dot(h, wu_ref[0])
    acc_ref[...] += _dot((gate * _sigmoid(gate) * up).astype(BF16), wo_ref[0])

    @pl.when(j == pl.num_programs(1) - 1)
    def _():
        f = acc_ref[...]
        y = f * lax.rsqrt(jnp.mean(f * f, axis=-1, keepdims=True) + NORM_EPS) * nw3_ref[...]
        o_ref[...] = x_ref[...] + g_ref[0] * y


def _ffn(x2, nw2, sc, sh, w_in, w_out, layer, nw3, gate, seq_len, tm, tf):
    m, d = x2.shape
    d_ff = w_out.shape[1]
    nf = d_ff // tf
    row = pl.BlockSpec((tm, d), lambda i, j: (i, 0))
    vec = pl.BlockSpec((1, d), lambda i, j: (0, 0))
    return pl.pallas_call(
        _ffn_kernel,
        out_shape=jax.ShapeDtypeStruct((m, d), F32),
        grid=(m // tm, nf),
        in_specs=[row, vec, _mod_spec(seq_len, tm, d), _mod_spec(seq_len, tm, d),
                  pl.BlockSpec((1, d, tf), lambda i, j: (layer, 0, j)),
                  pl.BlockSpec((1, d, tf), lambda i, j: (layer, 0, nf + j)),
                  pl.BlockSpec((1, tf, d), lambda i, j: (layer, j, 0)),
                  vec, _mod_spec(seq_len, tm, d)],
        out_specs=row,
        scratch_shapes=[pltpu.VMEM((tm, d), BF16), pltpu.VMEM((tm, d), F32)],
        compiler_params=_cparams(("arbitrary", "arbitrary")),
        name="ffn",
    )(x2, nw2, _expand_mod(sc, seq_len, tm), _expand_mod(sh, seq_len, tm), w_in, w_in, w_out, nw3,
      _expand_mod(gate, seq_len, tm))


def _pick(n, target):
    if n <= target:
        return n
    for t in range(target, 7, -1):
        if n % t == 0 and t % SUBLANES == 0:
            return t
    return n


def _permute_w_in(w):
    depth, d = w.shape[:2]
    hk, hv = GLA_HEADS * GLA_DK, GLA_HEADS * GLA_DV
    kvw = NSA_KV_HEADS * HEAD_DIM
    sizes = [hk, hk, hv, hv, GLA_GATE_RANK, NSA_HEADS * HEAD_DIM, kvw, kvw, kvw, kvw, kvw, kvw, NSA_HEADS * 3,
             DIFF_HEADS * 2 * DIFF_QK, DIFF_HEADS * 2 * DIFF_QK, DIFF_HEADS * DIFF_V]
    offs = np.concatenate([[0], np.cumsum(sizes)])
    part = lambda k: w[:, :, offs[k]:offs[k + 1]]
    main = jnp.concatenate([part(k) for k in (5, 0, 1, 2, 3, 6, 7, 8, 9, 10, 11, 13, 14, 15)], axis=2)
    pad = LANES - GLA_GATE_RANK - NSA_HEADS * 3
    small = jnp.concatenate([part(4), part(12), jnp.zeros((depth, d, pad), w.dtype)], axis=2)
    return main.astype(BF16), small.astype(BF16)


def _layer(x, mod, qpos0, past, win_buf, gla_s0, lp, lam_init):
    b, seq_len, d = x.shape
    m = b * seq_len
    x2 = x.reshape(m, d)
    sh1, sc1, g1, sh2, sc2, g2 = jnp.split(mod, 6, axis=-1)
    tm = _pick(seq_len, 512) if seq_len >= 128 else m
    tm_in = _pick(seq_len, 1024) if seq_len >= 128 else m
    layer = lp['layer']
    z, z_small = _in_proj(x2, lp['norm'][0:1], sc1, sh1, *lp['w_in'], layer, seq_len, tm_in, _pick(Z_WIDTH, 512))

    tab = _rope_tables(qpos0 + np.arange(seq_len))
    if seq_len % tm != 0:
        tab = np.tile(tab, (m // seq_len, 1))
    tab = jnp.asarray(tab)
    act_dt = BF16 if seq_len % BF16_ROWS == 0 else F32
    (qn, cmp_f, cmp_b, slc_f, slc_b, win_f, win_b, dq, dkv_f, dkv_b, *vts) = _rope_split(
        z, tab, seq_len, tm, act_dt, transposed_values=past is None)

    chunk = 64 if seq_len % 64 == 0 else seq_len
    o_gla, gla_state = _gla(z, z_small, lp['gla_gate_w'], lp['gla_gate_b'], lp['gla_norm'], gla_s0, b, seq_len,
                            chunk, act_dt)

    kvw = 2 * NSA_KV_HEADS * HEAD_DIM
    dw = 2 * DIFF_HEADS * DIFF_V
    win_len = NSA_WINDOW + seq_len
    win_pad = jnp.zeros((b, -win_len % BF16_ROWS, kvw), BF16)
    win_all = jnp.concatenate([win_buf.astype(BF16), win_b.reshape(b, seq_len, kvw), win_pad], axis=1)

    if past is None:
        slc_vt, diff_vt = vts
        seg = cmp_b.reshape(b, seq_len // NSA_CMP_STRIDE, NSA_CMP_STRIDE * kvw)
        comp = _compress(seg, lp['nsa_cmp_w'], lp['nsa_cmp_pe'])
        o_nsa = _nsa_attention(qn, z_small, comp, slc_b.reshape(b, seq_len, kvw), slc_vt, win_all, b, seq_len, qpos0,
                               _pick(seq_len, 128), tm)
        o_d = _diff_attention(dq, dkv_b.reshape(b, seq_len, dw), diff_vt, lp['diff_lambda'], lp['diff_norm'], b,
                              seq_len, qpos0, lam_init, _pick(seq_len, 512), tm)
    else:
        pt = past['page_table']
        pps = max(p for p in (PAGES_PER_STEP, 8, 4, 2) if pt.shape[1] % p == 0)
        comp = _compress_paged(past['cmp'], layer, pt, cmp_f, lp['nsa_cmp_w'], lp['nsa_cmp_pe'], pps)
        o_nsa = _nsa_attention_paged(qn, z_small, comp, slc_f, win_all, past['slc'], layer, pt, qpos0, pps)
        o_d = _diff_attention_paged(dq, dkv_f, past['diff'], layer, pt, lp['diff_lambda'], lp['diff_norm'],
                                    lam_init, pps)

    x1 = _out_proj(x2, o_gla, o_nsa, o_d, lp['w_out'], layer, lp['norm'][1:2], g1, seq_len, tm)
    d_ff = lp['ffn_w_out'].shape[1]
    x2n = _ffn(x1, lp['norm'][2:3], sc2, sh2, lp['ffn_w_in'], lp['ffn_w_out'], layer, lp['norm'][3:4], g2, seq_len,
               tm, _pick(d_ff, 512) if d_ff % LANES == 0 else d_ff)

    return (x2n.reshape(b, seq_len, d), cmp_f, slc_f, dkv_f, win_f, gla_state)


def kernel(x_prompt, x_sample, c_prompt, c_sample, cache_nsa_cmp_kv, cache_nsa_slc_kv, cache_diff_kv,
           state_nsa_win_kv, state_gla, page_table, ada_w, ada_b, norm_w, w_in, gla_gate_w, gla_gate_b,
           gla_norm, nsa_cmp_pe, nsa_cmp_w, diff_lambda, diff_norm, w_out, ffn_w_in, ffn_w_out):
    depth = ada_w.shape[0]
    bp, lp_len, d = x_prompt.shape
    bs, ls_len, _ = x_sample.shape
    n_pages = page_table.shape[1]
    past_len = n_pages * PAGE_SIZE
    wb = state_nsa_win_kv.shape[2]
    kvw = 2 * NSA_KV_HEADS * HEAD_DIM
    hk = GLA_HEADS * GLA_DK

    n_c = bp + bs
    rows = -(-n_c // SUBLANES) * SUBLANES
    c_all = jnp.concatenate([c_prompt, c_sample, jnp.zeros((rows - n_c, d), F32)], axis=0)
    mod_all = _modulation(c_all, ada_w, ada_b)

    w_in_b, w_out_b = _permute_w_in(w_in), w_out.astype(BF16)
    ffn_w_in_b, ffn_w_out_b = ffn_w_in.astype(BF16), ffn_w_out.astype(BF16)
    past_views = {'cmp': _page_view(cache_nsa_cmp_kv), 'slc': _page_view(cache_nsa_slc_kv),
                  'diff': _page_view(cache_diff_kv), 'page_table': page_table}

    xp, xs = x_prompt, x_sample
    outs = [[] for _ in range(10)]
    for l in range(depth):
        cmp_w, cmp_pe = _compress_weights(nsa_cmp_w[l], nsa_cmp_pe[l])
        lp = {'layer': l, 'norm': norm_w[l], 'w_in': w_in_b,
              'gla_gate_w': gla_gate_w[l], 'gla_gate_b': gla_gate_b[l], 'gla_norm': gla_norm[l],
              'nsa_cmp_pe': cmp_pe, 'nsa_cmp_w': cmp_w, 'diff_lambda': diff_lambda[l], 'diff_norm': diff_norm[l],
              'w_out': w_out_b, 'ffn_w_in': ffn_w_in_b, 'ffn_w_out': ffn_w_out_b}
        lam_init = 0.8 - 0.6 * math.exp(-0.3 * l)

        win0 = jnp.zeros((bp, NSA_WINDOW, kvw), BF16)
        gla0 = jnp.zeros((bp, hk, GLA_DV), F32)
        xp, cmp_p, slc_p, diff_p, win_p, gla_p = _layer(xp, mod_all[l, :bp], 0, None, win0, gla0, lp, lam_init)

        win_prev = state_nsa_win_kv[l].reshape(bs, wb, kvw)
        win_in = jnp.concatenate([jnp.zeros((bs, NSA_WINDOW - wb, kvw), F32), win_prev], axis=1)
        xs, cmp_s, slc_s, diff_s, win_s, gla_s = _layer(xs, mod_all[l, bp:bp + bs], past_len, past_views, win_in,
                                                        state_gla[l].reshape(bs, hk, GLA_DV), lp, lam_init)

        kv_shape = lambda b, n: (b, n, 2, NSA_KV_HEADS, HEAD_DIM)
        outs[0].append(cmp_p.reshape(kv_shape(bp, lp_len)))
        outs[1].append(cmp_s.reshape(kv_shape(bs, ls_len)))
        outs[2].append(slc_p.reshape(kv_shape(bp, lp_len)))
        outs[3].append(slc_s.reshape(kv_shape(bs, ls_len)))
        outs[4].append(diff_p.reshape(bp, lp_len, 2, DIFF_HEADS, DIFF_V))
        outs[5].append(diff_s.reshape(bs, ls_len, 2, DIFF_HEADS, DIFF_V))
        win_p3 = win_p.reshape(bp, lp_len, kvw)
        win_state_p = jnp.concatenate([jnp.zeros((bp, wb, kvw), F32), win_p3], axis=1)[:, -wb:]
        win_state_s = jnp.concatenate([win_prev, win_s.reshape(bs, ls_len, kvw)], axis=1)[:, -wb:]
        outs[6].append(win_state_p.reshape(kv_shape(bp, wb)))
        outs[7].append(win_state_s.reshape(kv_shape(bs, wb)))
        outs[8].append(gla_p.reshape(bp, GLA_HEADS, GLA_DK, GLA_DV))
        outs[9].append(gla_s.reshape(bs, GLA_HEADS, GLA_DK, GLA_DV))

    return (xp, xs) + tuple(jnp.stack(o) for o in outs)
```

```python
import functools
import math

import numpy as np
import jax
import jax.numpy as jnp
from jax import lax
from jax.experimental import pallas as pl
from jax.experimental.pallas import tpu as pltpu

F32 = jnp.float32
BF16 = jnp.bfloat16

PAGE_SIZE = 128
HEAD_DIM = 128
GLA_HEADS, GLA_DK, GLA_DV, GLA_GATE_RANK, GLA_TAU = 4, 64, 128, 16, 16.0
NSA_HEADS, NSA_KV_HEADS = 8, 2
NSA_CMP_BLOCK, NSA_CMP_STRIDE, NSA_SEL_BLOCK, NSA_TOP_N, NSA_WINDOW = 32, 16, 64, 16, 512
NSA_FORCE_BONUS = 1.0e4
DIFF_HEADS, DIFF_QK, DIFF_V = 4, 64, 128
ROPE_THETA = 10000.0
NORM_EPS = 1e-6
NEG = -1.0e30

LANES = 128
SUBLANES = 8
BF16_ROWS = 16
VMEM_LIMIT_BYTES = 56 * 1024 * 1024
PAGES_PER_STEP = 16

Z_NQ, Z_GQ, Z_GK, Z_GV, Z_GR = 0, 1024, 1280, 1536, 2048
Z_CMP, Z_SLC, Z_WIN, Z_DQ, Z_DK, Z_DV, Z_WIDTH = 2560, 3072, 3584, 4096, 4608, 5120, 5632
SMALL_GA, SMALL_GATE = 0, 16


def _cparams(sem):
    return pltpu.CompilerParams(dimension_semantics=sem, vmem_limit_bytes=VMEM_LIMIT_BYTES)


def _dot(a, b):
    return jnp.dot(a, b, preferred_element_type=F32)


def _dot_nt(a, b):
    return lax.dot_general(a, b, (((1,), (1,)), ((), ())), preferred_element_type=F32)


def _dot_tn(a, b):
    return lax.dot_general(a, b, (((0,), (0,)), ((), ())), preferred_element_type=F32)


def _split3(a):
    hi = a.astype(BF16)
    r = a - hi.astype(F32)
    mid = r.astype(BF16)
    lo = (r - mid.astype(F32)).astype(BF16)
    return hi, mid, lo


def _dot3_rhs_exact(a, b):
    hi, mid, lo = _split3(a)
    return _dot(hi, b) + _dot(mid, b) + _dot(lo, b)


def _dot3_lhs_exact(a, b):
    hi, mid, lo = _split3(b)
    return _dot(a, hi) + _dot(a, mid) + _dot(a, lo)


def _sigmoid(x):
    return 1.0 / (1.0 + jnp.exp(-x))


def _masked_softmax(s, mask):
    s = jnp.where(mask, s, NEG)
    m = jnp.max(s, axis=-1, keepdims=True)
    e = jnp.where(mask, jnp.exp(s - m), 0.0)
    den = jnp.sum(e, axis=-1, keepdims=True)
    return e / jnp.where(den > 0, den, 1.0)


def _mod_kernel(c_ref, w_ref, b_ref, o_ref):
    c = c_ref[...]
    a = (c * _sigmoid(c)).astype(BF16)
    o_ref[0] = _dot(a, w_ref[0].astype(BF16)) + b_ref[0]


def _modulation(c_all, ada_w, ada_b):
    depth, d, n6 = ada_w.shape
    rows = c_all.shape[0]
    tn = max(t for t in range(LANES, min(1024, n6) + 1, LANES) if n6 % t == 0)
    return pl.pallas_call(
        _mod_kernel,
        out_shape=jax.ShapeDtypeStruct((depth, rows, n6), F32),
        grid=(depth, n6 // tn),
        in_specs=[pl.BlockSpec((rows, d), lambda l, j: (0, 0)),
                  pl.BlockSpec((1, d, tn), lambda l, j: (l, 0, j)),
                  pl.BlockSpec((1, 1, tn), lambda l, j: (l, 0, j))],
        out_specs=pl.BlockSpec((1, rows, tn), lambda l, j: (l, 0, j)),
        compiler_params=_cparams(("arbitrary", "arbitrary")),
        name="adaln_modulation",
    )(c_all, ada_w, ada_b.reshape(depth, 1, n6))


def _norm_mod(x, nw, sc, sh):
    ms = jnp.mean(x * x, axis=-1, keepdims=True)
    return x * lax.rsqrt(ms + NORM_EPS) * nw * (1.0 + sc) + sh


def _in_proj_kernel(x_ref, nw_ref, sc_ref, sh_ref, w_ref, ws_ref, o_ref, os_ref, h_ref):
    @pl.when(pl.program_id(1) == 0)
    def _():
        h = _norm_mod(x_ref[...], nw_ref[...], sc_ref[0], sh_ref[0]).astype(BF16)
        h_ref[...] = h
        os_ref[...] = _dot(h, ws_ref[0])

    o_ref[...] = _dot(h_ref[...], w_ref[0])


def _mod_spec(seq_len, tm, d):
    if seq_len % tm == 0:
        per = seq_len // tm
        return pl.BlockSpec((1, 1, d), lambda i, *_: (i // per, 0, 0))
    return pl.BlockSpec((1, tm, d), lambda i, *_: (0, i, 0))


def _expand_mod(m, seq_len, tm):
    if seq_len % tm == 0:
        return m[:, None, :]
    return jnp.repeat(m, seq_len, axis=0)[None]


def _in_proj(x2, nw, sc, sh, w, w_small, layer, seq_len, tm, tn):
    m, d = x2.shape
    n = w.shape[2]
    return pl.pallas_call(
        _in_proj_kernel,
        out_shape=[jax.ShapeDtypeStruct((m, n), F32), jax.ShapeDtypeStruct((m, LANES), F32)],
        grid=(m // tm, n // tn),
        in_specs=[pl.BlockSpec((tm, d), lambda i, j: (i, 0)),
                  pl.BlockSpec((1, d), lambda i, j: (0, 0)),
                  _mod_spec(seq_len, tm, d), _mod_spec(seq_len, tm, d),
                  pl.BlockSpec((1, d, tn), lambda i, j: (layer, 0, j)),
                  pl.BlockSpec((1, d, LANES), lambda i, j: (layer, 0, 0))],
        out_specs=[pl.BlockSpec((tm, tn), lambda i, j: (i, j)), pl.BlockSpec((tm, LANES), lambda i, j: (i, 0))],
        scratch_shapes=[pltpu.VMEM((tm, d), BF16)],
        compiler_params=_cparams(("arbitrary", "arbitrary")),
        name="in_proj",
    )(x2, nw, _expand_mod(sc, seq_len, tm), _expand_mod(sh, seq_len, tm), w, w_small)


def _rope_tables(pos):
    pos = np.asarray(pos, np.float64)[:, None]

    def tab(half, reps):
        inv = ROPE_THETA ** (-np.arange(half, dtype=np.float64) / half)
        ang = pos * inv[None, :]
        c, s = np.cos(ang), np.sin(ang)
        return np.tile(np.concatenate([c, c], -1), (1, reps)), np.tile(np.concatenate([-s, s], -1), (1, reps))

    c128, s128 = tab(HEAD_DIM // 2, 1)
    c64, s64 = tab(DIFF_QK // 2, 2)
    return np.concatenate([c128, s128, c64, s64], axis=-1).astype(np.float32)


def _rope128(x, cos, sin):
    return x * cos + pltpu.roll(x, 64, 1) * sin


def _rope64(x, cos, sin, first_half):
    partner = jnp.where(first_half, pltpu.roll(x, 96, 1), pltpu.roll(x, 32, 1))
    return x * cos + partner * sin


def _rope_kernel(nq_ref, cmp_ref, slc_ref, win_ref, dq_ref, dk_ref, dv_ref, tab_ref,
                 qn_o, cmp_o, cmpb_o, slc_o, slcb_o, win_o, winb_o, dq_o, dkv_o, dkvb_o, *vt_outs):
    if vt_outs:
        slc_vt_o, diff_vt_o = vt_outs
        for g in range(NSA_KV_HEADS):
            slc_vt_o[0, g, 0] = slc_ref[:, (NSA_KV_HEADS + g) * LANES:(NSA_KV_HEADS + g + 1) * LANES].T.astype(BF16)
        for h in range(DIFF_HEADS):
            diff_vt_o[0, h, 0] = dv_ref[:, h * LANES:(h + 1) * LANES].T.astype(BF16)
    tab = tab_ref[...]
    rows = tab.shape[0]
    c128, s128, c64, s64 = (tab[:, i * LANES:(i + 1) * LANES] for i in range(4))
    lane = lax.broadcasted_iota(jnp.int32, (1, LANES), 1)
    first_half = (lane % DIFF_QK) < (DIFF_QK // 2)
    nsa_scale = HEAD_DIM ** -0.5
    diff_scale = DIFF_QK ** -0.5

    for h in range(NSA_HEADS):
        sl = slice(h * LANES, (h + 1) * LANES)
        qn_o[:, sl] = (_rope128(nq_ref[:, sl], c128, s128) * nsa_scale).astype(qn_o.dtype)

    for src, dst, dstb in ((cmp_ref, cmp_o, cmpb_o), (slc_ref, slc_o, slcb_o), (win_ref, win_o, winb_o)):
        for g in range(2 * NSA_KV_HEADS):
            sl = slice(g * LANES, (g + 1) * LANES)
            v = src[:, sl]
            if g < NSA_KV_HEADS:
                v = _rope128(v, c128, s128)
            dst[pl.ds(g, rows, stride=2 * NSA_KV_HEADS), :] = v
            dstb[:, sl] = v.astype(BF16)

    for h in range(DIFF_HEADS):
        sl = slice(h * LANES, (h + 1) * LANES)
        dq_o[:, sl] = (_rope64(dq_ref[:, sl], c64, s64, first_half) * diff_scale).astype(dq_o.dtype)
        k = _rope64(dk_ref[:, sl], c64, s64, first_half)
        dkv_o[pl.ds(h, rows, stride=2 * DIFF_HEADS), :] = k
        dkvb_o[:, sl] = k.astype(BF16)
        sv = slice((DIFF_HEADS + h) * LANES, (DIFF_HEADS + h + 1) * LANES)
        v = dv_ref[:, sl]
        dkv_o[pl.ds(DIFF_HEADS + h, rows, stride=2 * DIFF_HEADS), :] = v
        dkvb_o[:, sv] = v.astype(BF16)


def _rope_split(z, tab, seq_len, tm, act_dt, transposed_values):
    m = z.shape[0]
    per = max(seq_len // tm, 1)
    tab_spec = (pl.BlockSpec((tm, 4 * LANES), lambda i: (i % per, 0)) if seq_len % tm == 0
                else pl.BlockSpec((tm, 4 * LANES), lambda i: (i, 0)))

    def zs(width, col):
        return pl.BlockSpec((tm, width), lambda i: (i, col // width))

    def os(width):
        return pl.BlockSpec((tm, width), lambda i: (i, 0))

    kv = (512, F32, True)
    outs = [(1024, act_dt, False), kv, (512, BF16, False), kv, (512, BF16, False), kv, (512, BF16, False),
            (512, act_dt, False), (1024, F32, True), (1024, BF16, False)]
    out_shape = [jax.ShapeDtypeStruct((m * w // LANES, LANES) if il else (m, w), dt) for w, dt, il in outs]
    out_specs = [pl.BlockSpec((tm * w // LANES, LANES), lambda i: (i, 0)) if il else os(w) for w, dt, il in outs]
    if transposed_values:
        assert seq_len % tm == 0
        for heads in (NSA_KV_HEADS, DIFF_HEADS):
            out_shape.append(jax.ShapeDtypeStruct((m // seq_len, heads, per, LANES, tm), BF16))
            out_specs.append(pl.BlockSpec((1, heads, 1, LANES, tm), lambda i: (i // per, 0, i % per, 0, 0)))
    return pl.pallas_call(
        _rope_kernel,
        out_shape=out_shape,
        grid=(m // tm,),
        in_specs=[zs(1024, Z_NQ), zs(512, Z_CMP), zs(512, Z_SLC), zs(512, Z_WIN),
                  zs(512, Z_DQ), zs(512, Z_DK), zs(512, Z_DV), tab_spec],
        out_specs=out_specs,
        compiler_params=_cparams(("arbitrary",)),
        name="rope_split",
    )(z, z, z, z, z, z, z, tab)


def _page_view(cache):
    depth, n_pool, page, two, heads, width = cache.shape
    return cache.reshape(depth, n_pool, page * two * heads, width)


def _page_specs(layer, page_rows, pages_per_step):
    def spec(p):
        return pl.BlockSpec((1, 1, page_rows, LANES),
                            lambda b, j, pt: (layer, pt[b, j * pages_per_step + p], 0, 0))
    return [spec(p) for p in range(pages_per_step)]


def _page_slabs(page_refs, first_slot, n, n_slots):
    return jnp.concatenate(
        [jnp.concatenate([ref[0, 0, pl.ds(first_slot + s, PAGE_SIZE, stride=n_slots), :] for s in range(n)], axis=1)
         for ref in page_refs], axis=0).astype(BF16)


def _token_slabs(ref, first_slot, n, n_slots):
    tokens = ref.shape[0] // n_slots
    return jnp.concatenate([ref[pl.ds(first_slot + s, tokens, stride=n_slots), :] for s in range(n)], axis=1)


def _pad_rows(x, multiple):
    pad = -x.shape[0] % multiple
    return x if pad == 0 else jnp.concatenate([x, jnp.zeros((pad, x.shape[1]), x.dtype)], axis=0)


def _gla_constants(c):
    t = np.arange(c)
    sizes = []
    s = c // 2
    while s >= 1:
        sizes.append(s)
        s //= 2
    sel, masks = [], []
    for sz in sizes:
        ref_row = (t // (2 * sz)) * (2 * sz) + sz - 1
        sel.append((t[None, :] <= ref_row[:, None]).astype(np.float32))
        same = (t[:, None] // (2 * sz)) == (t[None, :] // (2 * sz))
        masks.append((same & ((t[:, None] // sz) % 2 == 1) & ((t[None, :] // sz) % 2 == 0)).astype(np.float32))
    sel.append((t[None, :] <= t[:, None]).astype(np.float32))
    masks.append(np.eye(c, dtype=np.float32))
    return np.concatenate(sel, 0), np.stack(masks, 0)


def _gla_kernel(chunk, levels, q_ref, k_ref, v_ref, r_ref, sm_ref, gw_ref, gb_ref, nw_ref, s0_ref,
                sel_ref, mask_ref, o_ref, st_ref, state_ref):
    c = chunk
    c_in = q_ref.shape[0]
    hk = GLA_HEADS * GLA_DK
    ci = pl.program_id(1)

    @pl.when(ci == 0)
    def _():
        state_ref[...] = s0_ref[0].T

    def rows(x):
        if c_in == c:
            return x
        return jnp.concatenate([x, jnp.zeros((c - c_in, x.shape[1]), x.dtype)], axis=0)

    ga = rows(sm_ref[:, SMALL_GA:SMALL_GA + GLA_GATE_RANK])
    pre = _dot3_lhs_exact_both(ga, gw_ref[...]) + gb_ref[...]
    log_a = (jnp.minimum(pre, 0.0) - jnp.log(1.0 + jnp.exp(-jnp.abs(pre)))) / GLA_TAU
    if c_in != c:
        log_a = jnp.where(lax.broadcasted_iota(jnp.int32, (c, 1), 0) < c_in, log_a, 0.0)
    refs = _dot3_lhs_exact(sel_ref[...], log_a)
    b = refs[(levels - 1) * c:levels * c]
    q = rows(q_ref[...]) * (GLA_DK ** -0.5)
    k = rows(k_ref[...])
    v_all = rows(v_ref[...])
    lane = lax.broadcasted_iota(jnp.int32, (1, hk), 1)
    head_of_lane = lane // GLA_DK

    def stack_heads(x):
        return jnp.concatenate([jnp.where(head_of_lane == h, x, 0.0) for h in range(GLA_HEADS)], axis=0).astype(BF16)

    attn = jnp.zeros((GLA_HEADS, c, c), F32)
    for lv in range(levels):
        r = refs[lv * c:(lv + 1) * c]
        qd = stack_heads(q * jnp.exp(jnp.minimum(b - r, 0.0)))
        kd = (k * jnp.exp(jnp.minimum(r - b, 0.0))).astype(BF16)
        attn = attn + _dot_nt(qd, kd).reshape(GLA_HEADS, c, c) * mask_ref[lv][None]
    state = state_ref[...]
    inter = _dot_nt(stack_heads(q * jnp.exp(b)), state.astype(BF16))

    nw = nw_ref[...]
    for h in range(GLA_HEADS):
        vh = v_all[:, h * GLA_DV:(h + 1) * GLA_DV]
        o = (_dot(attn[h].astype(BF16), vh.astype(BF16)) + inter[h * c:(h + 1) * c])[0:c_in]
        y = o * lax.rsqrt(jnp.mean(o * o, axis=-1, keepdims=True) + NORM_EPS) * nw
        rh = r_ref[:, h * GLA_DV:(h + 1) * GLA_DV]
        o_ref[:, h * GLA_DV:(h + 1) * GLA_DV] = (y * (rh * _sigmoid(rh))).astype(o_ref.dtype)

    b_end = b[c - 1:c]
    kd = (k * jnp.exp(b_end - b)).astype(BF16)
    new_state = state * jnp.exp(b_end)
    for h in range(GLA_HEADS):
        vh = v_all[:, h * GLA_DV:(h + 1) * GLA_DV].astype(BF16)
        new_state = new_state + jnp.where(head_of_lane == h, _dot_tn(vh, kd), 0.0)
    state_ref[...] = new_state

    @pl.when(ci == pl.num_programs(1) - 1)
    def _():
        st_ref[0] = new_state.T


def _dot3_lhs_exact_both(a, b):
    ah, am, al = _split3(a)
    bh, bm, bl = _split3(b)
    return (_dot(ah, bh) + _dot(ah, bm) + _dot(am, bh)) + (_dot(ah, bl) + _dot(am, bm) + _dot(al, bh))


def _gla(z, z_small, gate_w, gate_b, norm_w, s0, batch, seq_len, chunk, act_dt):
    m = z.shape[0]
    nc = seq_len // chunk
    hk, hv = GLA_HEADS * GLA_DK, GLA_HEADS * GLA_DV
    comp_rows = max(chunk, BF16_ROWS)
    sel, masks = _gla_constants(comp_rows)
    levels = masks.shape[0]

    def zs(width, col):
        return pl.BlockSpec((chunk, width), lambda b, c: (b * nc + c, col // width))

    const2 = lambda b, c: (0, 0)
    return pl.pallas_call(
        functools.partial(_gla_kernel, comp_rows, levels),
        out_shape=[jax.ShapeDtypeStruct((m, hv), act_dt), jax.ShapeDtypeStruct((batch, hk, GLA_DV), F32)],
        grid=(batch, nc),
        in_specs=[zs(hk, Z_GQ), zs(hk, Z_GK), zs(hv, Z_GV), zs(hv, Z_GR), zs(LANES, 0),
                  pl.BlockSpec((GLA_GATE_RANK, hk), const2), pl.BlockSpec((1, hk), const2),
                  pl.BlockSpec((1, GLA_DV), const2),
                  pl.BlockSpec((1, hk, GLA_DV), lambda b, c: (b, 0, 0)),
                  pl.BlockSpec((levels * comp_rows, comp_rows), const2),
                  pl.BlockSpec((levels, comp_rows, comp_rows), lambda b, c: (0, 0, 0))],
        out_specs=[pl.BlockSpec((chunk, hv), lambda b, c: (b * nc + c, 0)),
                   pl.BlockSpec((1, hk, GLA_DV), lambda b, c: (b, 0, 0))],
        scratch_shapes=[pltpu.VMEM((GLA_DV, hk), F32)],
        compiler_params=_cparams(("arbitrary", "arbitrary")),
        name="gla",
    )(z, z, z, z, z_small, gate_w, gate_b.reshape(1, hk), norm_w.reshape(1, GLA_DV), s0,
      jnp.asarray(sel, BF16), jnp.asarray(masks, F32))


def _online_update(s, mask, v, m_ref, l_ref, acc_ref):
    if mask is not None:
        s = jnp.where(mask, s, NEG)
    m_prev = m_ref[...]
    m_new = jnp.maximum(m_prev, jnp.max(s, axis=-1, keepdims=True))
    p = jnp.exp(s - m_new)
    if mask is not None:
        p = jnp.where(mask, p, 0.0)
    alpha = jnp.exp(m_prev - m_new)
    l_ref[...] = alpha * l_ref[...] + jnp.sum(p, axis=-1, keepdims=True)
    acc_ref[...] = alpha * acc_ref[...] + _dot(p.astype(BF16), v)
    m_ref[...] = m_new


def _online_update_keymajor(s, mask, vt, m_ref, l_ref, acc_ref):
    if mask is not None:
        s = jnp.where(mask, s, NEG)
    m_prev = m_ref[...]
    m_new = jnp.maximum(m_prev, jnp.max(s, axis=0, keepdims=True))
    p = jnp.exp(s - m_new)
    if mask is not None:
        p = jnp.where(mask, p, 0.0)
    alpha = jnp.exp(m_prev - m_new)
    l_ref[...] = alpha * l_ref[...] + jnp.sum(p, axis=0, keepdims=True)
    acc_ref[...] = alpha * acc_ref[...] + _dot(vt, p.astype(BF16))
    m_ref[...] = m_new


def _diff_kernel(tq, tk, qpos0, lam_init, qi_ref, kj_ref, q_ref, k_ref, vt_ref, lam_ref, nw_ref, o_ref,
                 qq_ref, m_ref, l_ref, acc_ref):
    step_id = pl.program_id(2)
    i, j = qi_ref[step_id], kj_ref[step_id]
    q_lo = qpos0 + i * tq
    last_j = (q_lo + tq - 1) // tk

    @pl.when(j == 0)
    def _():
        q = q_ref[...].astype(F32)
        lane = lax.broadcasted_iota(jnp.int32, (1, LANES), 1)
        qq_ref[...] = jnp.concatenate([jnp.where(lane < DIFF_QK, q, 0.0),
                                       jnp.where(lane >= DIFF_QK, q, 0.0)], axis=0).astype(BF16)
        m_ref[...] = jnp.full_like(m_ref, NEG)
        l_ref[...] = jnp.zeros_like(l_ref)
        acc_ref[...] = jnp.zeros_like(acc_ref)

    def step(masked):
        s = _dot_nt(k_ref[0], qq_ref[...])
        mask = None
        if masked:
            kpos = j * tk + lax.broadcasted_iota(jnp.int32, (tk, 1), 0)
            qpos = q_lo + lax.broadcasted_iota(jnp.int32, (1, tq), 1)
            mk = kpos <= qpos
            mask = jnp.concatenate([mk, mk], axis=1)
        _online_update_keymajor(s, mask, vt_ref[0, 0, 0], m_ref, l_ref, acc_ref)

    fully_visible = (j + 1) * tk - 1 <= q_lo

    @pl.when(fully_visible)
    def _():
        step(False)

    @pl.when(jnp.logical_not(fully_visible))
    def _():
        step(True)

    @pl.when(j == last_j)
    def _():
        l = l_ref[...]
        o12 = (acc_ref[...] / jnp.where(l > 0, l, 1.0)).T
        o_ref[...] = _diff_finish(o12, lam_ref, nw_ref, lam_init).astype(o_ref.dtype)


def _diff_attention(dq, kv, vt, lam_p, norm_w, batch, seq_len, qpos0, lam_init, tq, tk):
    m = dq.shape[0]
    nq = seq_len // tq
    nk = kv.shape[1] // tk
    assert vt.shape[2] == nk and vt.shape[4] == tk

    pairs = [(i, j) for i in range(nq) for j in range(min(nk - 1, (qpos0 + i * tq + tq - 1) // tk) + 1)]
    qi = jnp.asarray([p[0] for p in pairs], jnp.int32)
    kj = jnp.asarray([p[1] for p in pairs], jnp.int32)
    const = lambda b, h, s, qi, kj: (0, 0)
    return pl.pallas_call(
        functools.partial(_diff_kernel, tq, tk, qpos0, lam_init),
        out_shape=jax.ShapeDtypeStruct((m, DIFF_HEADS * DIFF_V), dq.dtype),
        grid_spec=pltpu.PrefetchScalarGridSpec(
            num_scalar_prefetch=2,
            grid=(batch, DIFF_HEADS, len(pairs)),
            in_specs=[pl.BlockSpec((tq, LANES), lambda b, h, s, qi, kj: (b * nq + qi[s], h)),
                      pl.BlockSpec((1, tk, LANES), lambda b, h, s, qi, kj: (b, kj[s], h)),
                      pl.BlockSpec((1, 1, 1, LANES, tk), lambda b, h, s, qi, kj: (b, h, kj[s], 0, 0)),
                      pl.BlockSpec((4, DIFF_QK), const), pl.BlockSpec((1, DIFF_V), const)],
            out_specs=pl.BlockSpec((tq, LANES), lambda b, h, s, qi, kj: (b * nq + qi[s], h)),
            scratch_shapes=[pltpu.VMEM((2 * tq, LANES), BF16), pltpu.VMEM((1, 2 * tq), F32),
                            pltpu.VMEM((1, 2 * tq), F32), pltpu.VMEM((DIFF_V, 2 * tq), F32)]),
        compiler_params=_cparams(("arbitrary",) * 3),
        name="diff_attention",
    )(qi, kj, dq, kv, vt, lam_p, norm_w.reshape(1, DIFF_V))


def _diff_finish(o12, lam_ref, nw_ref, lam_init):
    half = o12.shape[0] // 2
    lam_p = lam_ref[...]
    lam = (jnp.exp(jnp.sum(lam_p[0:1] * lam_p[1:2], axis=-1, keepdims=True))
           - jnp.exp(jnp.sum(lam_p[2:3] * lam_p[3:4], axis=-1, keepdims=True)) + lam_init)
    o = o12[0:half] - lam * o12[half:]
    y = o * lax.rsqrt(jnp.mean(o * o, axis=-1, keepdims=True) + NORM_EPS) * nw_ref[...]
    return y * (1.0 - lam_init)


def _diff_paged_kernel(pps, lam_init, pt_ref, q_ref, new_ref, lam_ref, nw_ref, *rest):
    page_refs, o_ref = rest[:pps], rest[pps]
    qbd_ref, m_ref, l_ref, acc_ref = rest[pps + 1:]
    j = pl.program_id(1)
    n_new = q_ref.shape[0]
    rows_h = 2 * n_new
    hw = DIFF_HEADS * LANES

    @pl.when(j == 0)
    def _():
        q = q_ref[...]
        lane = lax.broadcasted_iota(jnp.int32, (1, LANES), 1)
        blocks = []
        for h in range(DIFF_HEADS):
            qh = q[:, h * LANES:(h + 1) * LANES]
            q12 = jnp.concatenate([jnp.where(lane < DIFF_QK, qh, 0.0), jnp.where(lane >= DIFF_QK, qh, 0.0)], axis=0)
            blocks.append(jnp.concatenate(
                [q12 if hh == h else jnp.zeros((rows_h, LANES), F32) for hh in range(DIFF_HEADS)], axis=1))
        qbd_ref[...] = jnp.concatenate(blocks, axis=0).astype(BF16)
        m_ref[...] = jnp.full_like(m_ref, NEG)
        l_ref[...] = jnp.zeros_like(l_ref)
        acc_ref[...] = jnp.zeros_like(acc_ref)

    k = _page_slabs(page_refs, 0, DIFF_HEADS, 2 * DIFF_HEADS)
    v = _page_slabs(page_refs, DIFF_HEADS, DIFF_HEADS, 2 * DIFF_HEADS)
    _online_update(_dot_nt(qbd_ref[...], k), None, v, m_ref, l_ref, acc_ref)

    @pl.when(j == pl.num_programs(1) - 1)
    def _():
        kn = _pad_rows(_token_slabs(new_ref, 0, DIFF_HEADS, 2 * DIFF_HEADS), BF16_ROWS).astype(BF16)
        vn = _pad_rows(_token_slabs(new_ref, DIFF_HEADS, DIFF_HEADS, 2 * DIFF_HEADS), BF16_ROWS).astype(BF16)
        t = lax.broadcasted_iota(jnp.int32, (DIFF_HEADS * rows_h, 1), 0) % n_new
        u = lax.broadcasted_iota(jnp.int32, (1, kn.shape[0]), 1)
        _online_update(_dot_nt(qbd_ref[...], kn), u <= t, vn, m_ref, l_ref, acc_ref)
        l = l_ref[...]
        o12 = acc_ref[...] / jnp.where(l > 0, l, 1.0)
        for h in range(DIFF_HEADS):
            r = slice(h * rows_h, (h + 1) * rows_h)
            o_ref[:, h * LANES:(h + 1) * LANES] = _diff_finish(o12[r, h * LANES:(h + 1) * LANES], lam_ref, nw_ref,
                                                               lam_init)


def _diff_attention_paged(dq, new_kv, cache_view, layer, page_table, lam_p, norm_w, lam_init, pps):
    b, n_pages = page_table.shape
    n_new = dq.shape[0] // b
    hw = DIFF_HEADS * LANES
    rows = DIFF_HEADS * 2 * n_new
    assert n_pages % pps == 0 and n_new % SUBLANES == 0
    const = lambda i, j, pt: (0, 0)
    return pl.pallas_call(
        functools.partial(_diff_paged_kernel, pps, lam_init),
        out_shape=jax.ShapeDtypeStruct(dq.shape, F32),
        grid_spec=pltpu.PrefetchScalarGridSpec(
            num_scalar_prefetch=1,
            grid=(b, n_pages // pps),
            in_specs=[pl.BlockSpec((n_new, hw), lambda i, j, pt: (i, 0)),
                      pl.BlockSpec((n_new * 2 * DIFF_HEADS, LANES), lambda i, j, pt: (i, 0)),
                      pl.BlockSpec((4, DIFF_QK), const), pl.BlockSpec((1, DIFF_V), const)]
                     + _page_specs(layer, PAGE_SIZE * 2 * DIFF_HEADS, pps),
            out_specs=pl.BlockSpec((n_new, hw), lambda i, j, pt: (i, 0)),
            scratch_shapes=[pltpu.VMEM((rows, hw), BF16), pltpu.VMEM((rows, 1), F32), pltpu.VMEM((rows, 1), F32),
                            pltpu.VMEM((rows, hw), F32)]),
        compiler_params=_cparams(("arbitrary", "arbitrary")),
        name="diff_attention_paged",
    )(page_table, dq, new_kv, lam_p, norm_w.reshape(1, DIFF_V), *([cache_view] * pps))


def _compress_weights(w, pe):
    k = NSA_CMP_STRIDE * HEAD_DIM
    return (jnp.concatenate([w[:, :k], w[:, k:]], axis=2).astype(BF16),
            pe.reshape(2, NSA_CMP_BLOCK // NSA_CMP_STRIDE, k))


def _compress_blocks(segments, rows, w_ref, pe_ref, o_ref):
    for c in range(2):
        pe = _dot(_pad_rows(pe_ref[c], BF16_ROWS).astype(BF16), w_ref[c])
        pe_term = pe[0:1, 0:HEAD_DIM] + pe[1:2, HEAD_DIM:2 * HEAD_DIM]
        for g in range(NSA_KV_HEADS):
            both = _dot(segments(c * NSA_KV_HEADS + g), w_ref[c])
            o_ref[0, c, g] = (both[:, 0:HEAD_DIM] + pltpu.roll(both[:, HEAD_DIM:2 * HEAD_DIM], rows - 1, 0)
                              + pe_term).astype(BF16)


def _compress_kernel(x_ref, w_ref, pe_ref, o_ref):
    tok_w = 2 * NSA_KV_HEADS * HEAD_DIM

    def segments(slot):
        return jnp.concatenate([x_ref[0, :, t * tok_w + slot * HEAD_DIM:t * tok_w + (slot + 1) * HEAD_DIM]
                                for t in range(NSA_CMP_STRIDE)], axis=1)

    _compress_blocks(segments, x_ref.shape[1], w_ref, pe_ref, o_ref)


def _compress_paged_kernel(pps, n_past_seg, pt_ref, new_ref, w_ref, pe_ref, *rest):
    page_refs, o_ref, x_ref = rest[:pps], rest[pps], rest[pps + 1]
    j = pl.program_id(1)
    slots = 2 * NSA_KV_HEADS
    seg_per_page = PAGE_SIZE // NSA_CMP_STRIDE
    rows = x_ref.shape[1]

    for pp in range(0, pps, 2):
        row0 = pl.multiple_of((j * pps + pp) * seg_per_page, 2 * seg_per_page)
        for slot in range(slots):
            for t in range(NSA_CMP_STRIDE):
                pair = [page_refs[pp + q][0, 0, pl.ds(t * slots + slot, seg_per_page, stride=NSA_CMP_STRIDE * slots), :]
                        for q in range(2)]
                x_ref[slot, pl.ds(row0, 2 * seg_per_page), t * HEAD_DIM:(t + 1) * HEAD_DIM] = (
                    jnp.concatenate(pair, axis=0).astype(BF16))

    @pl.when(j == pl.num_programs(1) - 1)
    def _():
        n_new = new_ref.shape[0] // slots
        first = lax.broadcasted_iota(jnp.int32, (rows - n_past_seg, 1), 0) == 0
        for slot in range(slots):
            for t in range(NSA_CMP_STRIDE):
                if t < n_new:
                    tail = jnp.where(first, new_ref[t * slots + slot:t * slots + slot + 1, :], 0.0)
                else:
                    tail = jnp.zeros((rows - n_past_seg, HEAD_DIM), F32)
                x_ref[slot, n_past_seg:rows, t * HEAD_DIM:(t + 1) * HEAD_DIM] = tail.astype(BF16)
        _compress_blocks(lambda slot: x_ref[slot], rows, w_ref, pe_ref, o_ref)


def _compress_paged(cache_view, layer, page_table, new_rows, w, pe, pps):
    b, n_pages = page_table.shape
    slots = 2 * NSA_KV_HEADS
    n_new = new_rows.shape[0] // (b * slots)
    seg_per_page = PAGE_SIZE // NSA_CMP_STRIDE
    n_past_seg = n_pages * seg_per_page
    rows = n_past_seg + 2 * BF16_ROWS
    assert n_new <= NSA_CMP_STRIDE and pps % 2 == 0 and n_pages % pps == 0
    return pl.pallas_call(
        functools.partial(_compress_paged_kernel, pps, n_past_seg),
        out_shape=jax.ShapeDtypeStruct((b, 2, NSA_KV_HEADS, rows, HEAD_DIM), BF16),
        grid_spec=pltpu.PrefetchScalarGridSpec(
            num_scalar_prefetch=1,
            grid=(b, n_pages // pps),
            in_specs=[pl.BlockSpec((n_new * slots, HEAD_DIM), lambda i, j, pt: (i, 0)),
                      pl.BlockSpec(w.shape, lambda i, j, pt: (0, 0, 0)),
                      pl.BlockSpec(pe.shape, lambda i, j, pt: (0, 0, 0))]
                     + _page_specs(layer, PAGE_SIZE * slots, pps),
            out_specs=pl.BlockSpec((1, 2, NSA_KV_HEADS, rows, HEAD_DIM), lambda i, j, pt: (i, 0, 0, 0, 0)),
            scratch_shapes=[pltpu.VMEM((slots, rows, NSA_CMP_STRIDE * HEAD_DIM), BF16)]),
        compiler_params=_cparams(("arbitrary", "arbitrary")),
        name="nsa_compress_paged",
    )(page_table, new_rows, w, pe, *([cache_view] * pps))


def _compress(seg, w, pe):
    b, rows, width = seg.shape
    return pl.pallas_call(
        _compress_kernel,
        out_shape=jax.ShapeDtypeStruct((b, 2, NSA_KV_HEADS, rows, HEAD_DIM), BF16),
        grid=(b,),
        in_specs=[pl.BlockSpec((1, rows, width), lambda i: (i, 0, 0)),
                  pl.BlockSpec(w.shape, lambda i: (0, 0, 0)),
                  pl.BlockSpec(pe.shape, lambda i: (0, 0, 0))],
        out_specs=pl.BlockSpec((1, 2, NSA_KV_HEADS, rows, HEAD_DIM), lambda i: (i, 0, 0, 0, 0)),
        compiler_params=_cparams(("arbitrary",)),
        name="nsa_compress",
    )(seg, w, pe)


def _nsa_band(n_cmp_rows, n_slc_pad):
    ratio = NSA_SEL_BLOCK // NSA_CMP_STRIDE
    span = NSA_CMP_BLOCK // NSA_CMP_STRIDE
    n = np.arange(n_cmp_rows)[:, None]
    j = np.arange(n_slc_pad)[None, :]
    return ((n >= ratio * j - (span - 1)) & (n <= ratio * j + ratio - 1)).astype(np.float32)


def _nsa_compressed(qs, qpos_rows, ck, cv):
    cmp_end = lax.broadcasted_iota(jnp.int32, (1, ck.shape[0]), 1) * NSA_CMP_STRIDE + (NSA_CMP_BLOCK - 1)
    p_c = _masked_softmax(_dot_nt(qs, ck), cmp_end <= qpos_rows)
    return p_c, _dot(p_c.astype(BF16), cv)


def _nsa_block_scores(p_c, qpos, band_ref):
    tq = qpos.shape[0]
    imp = p_c[0:tq]
    for n in range(1, p_c.shape[0] // tq):
        imp = imp + p_c[n * tq:(n + 1) * tq]
    p_slc = _dot3_rhs_exact(_pad_rows(imp, BF16_ROWS), band_ref[...])[0:tq]
    blk = lax.broadcasted_iota(jnp.int32, (1, band_ref.shape[1]), 1)
    cur = qpos // NSA_SEL_BLOCK
    valid = blk * NSA_SEL_BLOCK <= qpos
    forced = (blk == 0) | (blk == cur) | (blk == cur - 1)
    return jnp.where(valid, p_slc + jnp.where(forced, NSA_FORCE_BONUS, 0.0), NEG)


def _nsa_select(score, score_ref, n_slc):
    score_t = score.T
    score_ref[...] = score_t
    blk_t = lax.broadcasted_iota(jnp.int32, (score_t.shape[0], 1), 0)

    def count(jp, cnt):
        row = score_ref[pl.ds(jp, 1), :]
        beats = (row > score_t) | ((row == score_t) & (jp < blk_t))
        return cnt + jnp.where(beats, 1.0, 0.0)

    cnt = lax.fori_loop(0, n_slc, count, jnp.zeros(score_t.shape, F32))
    return jnp.where((cnt < NSA_TOP_N) & (score_t > 0.5 * NEG), 1.0, 0.0)


def _nsa_window(qs, qpos_rows, kw, vw, first_pos):
    wpos = first_pos + lax.broadcasted_iota(jnp.int32, (1, kw.shape[0]), 1)
    mask_w = (wpos <= qpos_rows) & (wpos > qpos_rows - NSA_WINDOW) & (wpos >= 0)
    return _dot(_masked_softmax(_dot_nt(qs, kw), mask_w).astype(BF16), vw)


def _nsa_kernel(tq, tk, qpos0, n_slc, q_ref, sm_ref, ck_ref, cv_ref, sk_ref, svt_ref, wk_ref, wv_ref,
                band_ref, o_ref, score_ref, sel_ref, m_ref, l_ref, acc_ref):
    i = pl.program_id(1)
    hpg = NSA_HEADS // NSA_KV_HEADS
    q_lo = qpos0 + i * tq
    qpos = q_lo + lax.broadcasted_iota(jnp.int32, (tq, 1), 0)
    qpos_lanes = q_lo + lax.broadcasted_iota(jnp.int32, (1, tq), 1)
    qpos_rows = jnp.concatenate([qpos] * hpg, axis=0)
    gates = _sigmoid(sm_ref[:, SMALL_GATE:SMALL_GATE + 3 * NSA_HEADS])
    blocks_per_tile = tk // NSA_SEL_BLOCK

    for g in range(NSA_KV_HEADS):
        q = q_ref[:, g * hpg * LANES:(g + 1) * hpg * LANES]
        qs = jnp.concatenate([q[:, n * LANES:(n + 1) * LANES] for n in range(hpg)], axis=0).astype(BF16)

        p_c, o_c = _nsa_compressed(qs, qpos_rows, ck_ref[0, 0, g], cv_ref[0, 0, g])
        n_rank = jnp.minimum((q_lo + tq - 1) // NSA_SEL_BLOCK + 1, n_slc)
        sel_ref[...] = _nsa_select(_nsa_block_scores(p_c, qpos, band_ref), score_ref, n_rank)

        m_ref[...] = jnp.full_like(m_ref, NEG)
        l_ref[...] = jnp.zeros_like(l_ref)
        acc_ref[...] = jnp.zeros_like(acc_ref)

        def key_tile(kt, carry):
            start = pl.multiple_of(kt * tk, tk)
            k = sk_ref[0, pl.ds(start, tk), g * LANES:(g + 1) * LANES]
            blocks = sel_ref[pl.ds(pl.multiple_of(kt * blocks_per_tile, blocks_per_tile), blocks_per_tile), :]
            picked = jnp.concatenate([jnp.broadcast_to(blocks[r:r + 1], (NSA_SEL_BLOCK, tq))
                                      for r in range(blocks_per_tile)], axis=0) > 0.5
            kpos = kt * tk + lax.broadcasted_iota(jnp.int32, (tk, 1), 0)
            mk = picked & (kpos <= qpos_lanes)
            mask = jnp.concatenate([mk] * hpg, axis=1)
            _online_update_keymajor(_dot_nt(k, qs), mask, svt_ref[0, g, kt], m_ref, l_ref, acc_ref)
            return carry

        lax.fori_loop(0, (q_lo + tq - 1) // tk + 1, key_tile, 0)
        l = l_ref[...]
        o_st = acc_ref[...] / jnp.where(l > 0, l, 1.0)
        o_s = jnp.concatenate([o_st[:, n * tq:(n + 1) * tq].T for n in range(hpg)], axis=0)

        wrows = -(-(tq + NSA_WINDOW) // BF16_ROWS) * BF16_ROWS
        wstart = pl.multiple_of(i * tq, tq)
        o_w = _nsa_window(qs, qpos_rows, wk_ref[0, pl.ds(wstart, wrows), g * LANES:(g + 1) * LANES],
                          wv_ref[0, pl.ds(wstart, wrows), g * LANES:(g + 1) * LANES], q_lo - NSA_WINDOW)

        for n in range(hpg):
            h = g * hpg + n
            r = slice(n * tq, (n + 1) * tq)
            o = (gates[:, 3 * h:3 * h + 1] * o_c[r] + gates[:, 3 * h + 1:3 * h + 2] * o_s[r]
                 + gates[:, 3 * h + 2:3 * h + 3] * o_w[r])
            o_ref[:, h * LANES:(h + 1) * LANES] = o.astype(o_ref.dtype)


def _nsa_attention(qn, z_small, comp, slc_buf, slc_vt, win_all, batch, seq_len, qpos0, tq, tk):
    m = qn.shape[0]
    nq = seq_len // tq
    t_keys = slc_buf.shape[1]
    n_cmp_rows = comp.shape[3]
    n_slc = -(-(qpos0 + seq_len) // NSA_SEL_BLOCK)
    nsp = -(-n_slc // LANES) * LANES
    assert tq % LANES == 0 and tk % NSA_SEL_BLOCK == 0 and slc_vt.shape[2] * tk == t_keys and nsp * NSA_SEL_BLOCK >= t_keys
    band = _nsa_band(n_cmp_rows, nsp)
    hw = NSA_HEADS * HEAD_DIM
    kvw = NSA_KV_HEADS * HEAD_DIM
    rows = (NSA_HEADS // NSA_KV_HEADS) * tq
    wlen = win_all.shape[1]
    return pl.pallas_call(
        functools.partial(_nsa_kernel, tq, tk, qpos0, n_slc),
        out_shape=jax.ShapeDtypeStruct((m, hw), qn.dtype),
        grid=(batch, nq),
        in_specs=[pl.BlockSpec((tq, hw), lambda b, i: (b * nq + i, 0)),
                  pl.BlockSpec((tq, LANES), lambda b, i: (b * nq + i, 0)),
                  pl.BlockSpec((1, 1, NSA_KV_HEADS, n_cmp_rows, HEAD_DIM), lambda b, i: (b, 0, 0, 0, 0)),
                  pl.BlockSpec((1, 1, NSA_KV_HEADS, n_cmp_rows, HEAD_DIM), lambda b, i: (b, 1, 0, 0, 0)),
                  pl.BlockSpec((1, t_keys, kvw), lambda b, i: (b, 0, 0)),
                  pl.BlockSpec((1,) + slc_vt.shape[1:], lambda b, i: (b, 0, 0, 0, 0)),
                  pl.BlockSpec((1, wlen, kvw), lambda b, i: (b, 0, 0)),
                  pl.BlockSpec((1, wlen, kvw), lambda b, i: (b, 0, 1)),
                  pl.BlockSpec((n_cmp_rows, nsp), lambda b, i: (0, 0))],
        out_specs=pl.BlockSpec((tq, hw), lambda b, i: (b * nq + i, 0)),
        scratch_shapes=[pltpu.VMEM((nsp, tq), F32), pltpu.VMEM((nsp, tq), F32), pltpu.VMEM((1, rows), F32),
                        pltpu.VMEM((1, rows), F32), pltpu.VMEM((HEAD_DIM, rows), F32)],
        compiler_params=_cparams(("arbitrary", "arbitrary")),
        name="nsa_attention",
    )(qn, z_small, comp, comp, slc_buf, slc_vt, win_all, win_all, jnp.asarray(band, BF16))


def _nsa_paged_kernel(pps, qpos0, n_slc, pt_ref, q_ref, sm_ref, ck_ref, cv_ref, new_ref, wk_ref, wv_ref, band_ref,
                      *rest):
    page_refs, o_ref = rest[:pps], rest[pps]
    score_ref, qbd_ref, sel_ref, m_ref, l_ref, acc_ref, oc_ref, ow_ref = rest[pps + 1:]
    j = pl.program_id(1)
    n_new = q_ref.shape[0]
    groups = NSA_KV_HEADS
    hpg = NSA_HEADS // groups
    rows_g = hpg * n_new
    gw = groups * LANES
    nsp = band_ref.shape[1]
    qpos = qpos0 + lax.broadcasted_iota(jnp.int32, (n_new, 1), 0)

    @pl.when(j == 0)
    def _():
        q = q_ref[...]
        qpos_rows = jnp.concatenate([qpos] * hpg, axis=0)
        scores, blocks = [], []
        for g in range(groups):
            qf = jnp.concatenate([q[:, (g * hpg + n) * LANES:(g * hpg + n + 1) * LANES] for n in range(hpg)], axis=0)
            qs = qf.astype(BF16)
            p_c, o_c = _nsa_compressed(qs, qpos_rows, ck_ref[0, 0, g], cv_ref[0, 0, g])
            oc_ref[g * rows_g:(g + 1) * rows_g] = o_c
            scores.append(_nsa_block_scores(p_c, qpos, band_ref))
            ow_ref[g * rows_g:(g + 1) * rows_g] = _nsa_window(
                qs, qpos_rows, wk_ref[0, :, g * LANES:(g + 1) * LANES], wv_ref[0, :, g * LANES:(g + 1) * LANES],
                qpos0 - NSA_WINDOW)
            blocks.append(jnp.concatenate(
                [qf if gg == g else jnp.zeros((rows_g, LANES), F32) for gg in range(groups)], axis=1))
        qbd_ref[...] = jnp.concatenate(blocks, axis=0).astype(BF16)
        pad = jnp.full((score_ref.shape[1] - groups * n_new, nsp), NEG, F32)
        sel = _nsa_select(jnp.concatenate(scores + [pad], axis=0), score_ref, n_slc).T
        sel_ref[...] = jnp.concatenate(
            [sel[g * n_new:(g + 1) * n_new] for g in range(groups) for _ in range(hpg)], axis=0).astype(BF16)
        m_ref[...] = jnp.full_like(m_ref, NEG)
        l_ref[...] = jnp.zeros_like(l_ref)
        acc_ref[...] = jnp.zeros_like(acc_ref)

    k = _page_slabs(page_refs, 0, groups, 2 * groups)
    v = _page_slabs(page_refs, groups, groups, 2 * groups)
    n_keys = pps * PAGE_SIZE
    key_blk = (j * n_keys + lax.broadcasted_iota(jnp.int32, (1, n_keys), 1)) // NSA_SEL_BLOCK
    expand = jnp.where(lax.broadcasted_iota(jnp.int32, (nsp, 1), 0) == key_blk, 1.0, 0.0).astype(BF16)
    picked = _dot(sel_ref[...], expand) > 0.5
    _online_update(_dot_nt(qbd_ref[...], k), picked, v, m_ref, l_ref, acc_ref)

    @pl.when(j == pl.num_programs(1) - 1)
    def _():
        kn = _pad_rows(_token_slabs(new_ref, 0, groups, 2 * groups), BF16_ROWS).astype(BF16)
        vn = _pad_rows(_token_slabs(new_ref, groups, groups, 2 * groups), BF16_ROWS).astype(BF16)
        new_blk = qpos0 // NSA_SEL_BLOCK
        t = lax.broadcasted_iota(jnp.int32, (groups * rows_g, 1), 0) % n_new
        u = lax.broadcasted_iota(jnp.int32, (1, kn.shape[0]), 1)
        mask = (u <= t) & (sel_ref[:, new_blk:new_blk + 1] > 0.5)
        _online_update(_dot_nt(qbd_ref[...], kn), mask, vn, m_ref, l_ref, acc_ref)
        l = l_ref[...]
        o_sel = acc_ref[...] / jnp.where(l > 0, l, 1.0)
        gates = _sigmoid(sm_ref[:, SMALL_GATE:SMALL_GATE + 3 * NSA_HEADS])
        for g in range(groups):
            for n in range(hpg):
                h = g * hpg + n
                r = slice(g * rows_g + n * n_new, g * rows_g + (n + 1) * n_new)
                o_ref[:, h * LANES:(h + 1) * LANES] = (
                    gates[:, 3 * h:3 * h + 1] * oc_ref[r] + gates[:, 3 * h + 1:3 * h + 2] * o_sel[r, g * LANES:(g + 1) * LANES]
                    + gates[:, 3 * h + 2:3 * h + 3] * ow_ref[r])


def _nsa_attention_paged(qn, z_small, comp, new_kv, win_all, cache_view, layer, page_table, qpos0, pps):
    b, n_pages = page_table.shape
    n_new = qn.shape[0] // b
    groups = NSA_KV_HEADS
    hw = NSA_HEADS * HEAD_DIM
    gw = groups * HEAD_DIM
    rows = NSA_HEADS * n_new
    n_cmp_rows = comp.shape[3]
    n_slc = -(-(qpos0 + n_new) // NSA_SEL_BLOCK)
    nsp = -(-n_slc // LANES) * LANES
    wlen = win_all.shape[1]
    assert n_pages % pps == 0 and qpos0 == n_pages * PAGE_SIZE and n_new <= NSA_SEL_BLOCK
    assert qpos0 % NSA_SEL_BLOCK == 0 and groups * n_new <= LANES and n_new % SUBLANES == 0
    band = _nsa_band(n_cmp_rows, nsp)
    const = lambda i, j, pt: (0, 0)
    return pl.pallas_call(
        functools.partial(_nsa_paged_kernel, pps, qpos0, n_slc),
        out_shape=jax.ShapeDtypeStruct(qn.shape, F32),
        grid_spec=pltpu.PrefetchScalarGridSpec(
            num_scalar_prefetch=1,
            grid=(b, n_pages // pps),
            in_specs=[pl.BlockSpec((n_new, hw), lambda i, j, pt: (i, 0)),
                      pl.BlockSpec((n_new, LANES), lambda i, j, pt: (i, 0)),
                      pl.BlockSpec((1, 1, groups, n_cmp_rows, HEAD_DIM), lambda i, j, pt: (i, 0, 0, 0, 0)),
                      pl.BlockSpec((1, 1, groups, n_cmp_rows, HEAD_DIM), lambda i, j, pt: (i, 1, 0, 0, 0)),
                      pl.BlockSpec((n_new * 2 * groups, LANES), lambda i, j, pt: (i, 0)),
                      pl.BlockSpec((1, wlen, gw), lambda i, j, pt: (i, 0, 0)),
                      pl.BlockSpec((1, wlen, gw), lambda i, j, pt: (i, 0, 1)),
                      pl.BlockSpec((n_cmp_rows, nsp), const)]
                     + _page_specs(layer, PAGE_SIZE * 2 * groups, pps),
            out_specs=pl.BlockSpec((n_new, hw), lambda i, j, pt: (i, 0)),
            scratch_shapes=[pltpu.VMEM((nsp, LANES), F32), pltpu.VMEM((rows, gw), BF16), pltpu.VMEM((rows, nsp), BF16),
                            pltpu.VMEM((rows, 1), F32), pltpu.VMEM((rows, 1), F32), pltpu.VMEM((rows, gw), F32),
                            pltpu.VMEM((rows, HEAD_DIM), F32), pltpu.VMEM((rows, HEAD_DIM), F32)]),
        compiler_params=_cparams(("arbitrary", "arbitrary")),
        name="nsa_attention_paged",
    )(page_table, qn, z_small, comp, comp, new_kv, win_all, win_all, jnp.asarray(band, BF16),
      *([cache_view] * pps))


def _out_proj_kernel(x_ref, og_ref, on_ref, od_ref, w_ref, nw_ref, g_ref, o_ref):
    a = jnp.concatenate([og_ref[...].astype(BF16), on_ref[...].astype(BF16), od_ref[...].astype(BF16)], axis=-1)
    mix = _dot(a, w_ref[0])
    y = mix * lax.rsqrt(jnp.mean(mix * mix, axis=-1, keepdims=True) + NORM_EPS) * nw_ref[...]
    o_ref[...] = x_ref[...] + g_ref[0] * y


def _out_proj(x2, o_gla, o_nsa, o_d, w, layer, nw, gate, seq_len, tm):
    m, d = x2.shape
    row = lambda width: pl.BlockSpec((tm, width), lambda i: (i, 0))
    return pl.pallas_call(
        _out_proj_kernel,
        out_shape=jax.ShapeDtypeStruct((m, d), F32),
        grid=(m // tm,),
        in_specs=[row(d), row(o_gla.shape[1]), row(o_nsa.shape[1]), row(o_d.shape[1]),
                  pl.BlockSpec((1,) + w.shape[1:], lambda i: (layer, 0, 0)), pl.BlockSpec((1, d), lambda i: (0, 0)),
                  _mod_spec(seq_len, tm, d)],
        out_specs=row(d),
        compiler_params=_cparams(("arbitrary",)),
        name="out_proj",
    )(x2, o_gla, o_nsa, o_d, w, nw, _expand_mod(gate, seq_len, tm))


def _ffn_kernel(x_ref, nw2_ref, sc_ref, sh_ref, wg_ref, wu_ref, wo_ref, nw3_ref, g_ref, o_ref, h_ref, acc_ref):
    j = pl.program_id(1)

    @pl.when(j == 0)
    def _():
        h_ref[...] = _norm_mod(x_ref[...], nw2_ref[...], sc_ref[0], sh_ref[0]).astype(BF16)
        acc_ref[...] = jnp.zeros_like(acc_ref)

    h = h_ref[...]
    gate = _dot(h, wg_ref[0])
    up = _dot(h, wu_ref[0])
    acc_ref[...] += _dot((gate * _sigmoid(gate) * up).astype(BF16), wo_ref[0])

    @pl.when(j == pl.num_programs(1) - 1)
    def _():
        f = acc_ref[...]
        y = f * lax.rsqrt(jnp.mean(f * f, axis=-1, keepdims=True) + NORM_EPS) * nw3_ref[...]
        o_ref[...] = x_ref[...] + g_ref[0] * y


def _ffn(x2, nw2, sc, sh, w_in, w_out, layer, nw3, gate, seq_len, tm, tf):
    m, d = x2.shape
    d_ff = w_out.shape[1]
    nf = d_ff // tf
    row = pl.BlockSpec((tm, d), lambda i, j: (i, 0))
    vec = pl.BlockSpec((1, d), lambda i, j: (0, 0))
    return pl.pallas_call(
        _ffn_kernel,
        out_shape=jax.ShapeDtypeStruct((m, d), F32),
        grid=(m // tm, nf),
        in_specs=[row, vec, _mod_spec(seq_len, tm, d), _mod_spec(seq_len, tm, d),
                  pl.BlockSpec((1, d, tf), lambda i, j: (layer, 0, j)),
                  pl.BlockSpec((1, d, tf), lambda i, j: (layer, 0, nf + j)),
                  pl.BlockSpec((1, tf, d), lambda i, j: (layer, j, 0)),
                  vec, _mod_spec(seq_len, tm, d)],
        out_specs=row,
        scratch_shapes=[pltpu.VMEM((tm, d), BF16), pltpu.VMEM((tm, d), F32)],
        compiler_params=_cparams(("arbitrary", "arbitrary")),
        name="ffn",
    )(x2, nw2, _expand_mod(sc, seq_len, tm), _expand_mod(sh, seq_len, tm), w_in, w_in, w_out, nw3,
      _expand_mod(gate, seq_len, tm))


def _pick(n, target):
    if n <= target:
        return n
    for t in range(target, 7, -1):
        if n % t == 0 and t % SUBLANES == 0:
            return t
    return n


def _permute_w_in(w):
    depth, d = w.shape[:2]
    hk, hv = GLA_HEADS * GLA_DK, GLA_HEADS * GLA_DV
    kvw = NSA_KV_HEADS * HEAD_DIM
    sizes = [hk, hk, hv, hv, GLA_GATE_RANK, NSA_HEADS * HEAD_DIM, kvw, kvw, kvw, kvw, kvw, kvw, NSA_HEADS * 3,
             DIFF_HEADS * 2 * DIFF_QK, DIFF_HEADS * 2 * DIFF_QK, DIFF_HEADS * DIFF_V]
    offs = np.concatenate([[0], np.cumsum(sizes)])
    part = lambda k: w[:, :, offs[k]:offs[k + 1]]
    main = jnp.concatenate([part(k) for k in (5, 0, 1, 2, 3, 6, 7, 8, 9, 10, 11, 13, 14, 15)], axis=2)
    pad = LANES - GLA_GATE_RANK - NSA_HEADS * 3
    small = jnp.concatenate([part(4), part(12), jnp.zeros((depth, d, pad), w.dtype)], axis=2)
    return main.astype(BF16), small.astype(BF16)


def _layer(x, mod, qpos0, past, win_buf, gla_s0, lp, lam_init):
    b, seq_len, d = x.shape
    m = b * seq_len
    x2 = x.reshape(m, d)
    sh1, sc1, g1, sh2, sc2, g2 = jnp.split(mod, 6, axis=-1)
    tm = _pick(seq_len, 512) if seq_len >= 128 else m
    tm_in = _pick(seq_len, 1024) if seq_len >= 128 else m
    layer = lp['layer']
    z, z_small = _in_proj(x2, lp['norm'][0:1], sc1, sh1, *lp['w_in'], layer, seq_len, tm_in, _pick(Z_WIDTH, 512))

    tab = _rope_tables(qpos0 + np.arange(seq_len))
    if seq_len % tm != 0:
        tab = np.tile(tab, (m // seq_len, 1))
    tab = jnp.asarray(tab)
    act_dt = BF16 if seq_len % BF16_ROWS == 0 else F32
    (qn, cmp_f, cmp_b, slc_f, slc_b, win_f, win_b, dq, dkv_f, dkv_b, *vts) = _rope_split(
        z, tab, seq_len, tm, act_dt, transposed_values=past is None)

    chunk = 64 if seq_len % 64 == 0 else seq_len
    o_gla, gla_state = _gla(z, z_small, lp['gla_gate_w'], lp['gla_gate_b'], lp['gla_norm'], gla_s0, b, seq_len,
                            chunk, act_dt)

    kvw = 2 * NSA_KV_HEADS * HEAD_DIM
    dw = 2 * DIFF_HEADS * DIFF_V
    win_len = NSA_WINDOW + seq_len
    win_pad = jnp.zeros((b, -win_len % BF16_ROWS, kvw), BF16)
    win_all = jnp.concatenate([win_buf.astype(BF16), win_b.reshape(b, seq_len, kvw), win_pad], axis=1)

    if past is None:
        slc_vt, diff_vt = vts
        seg = cmp_b.reshape(b, seq_len // NSA_CMP_STRIDE, NSA_CMP_STRIDE * kvw)
        comp = _compress(seg, lp['nsa_cmp_w'], lp['nsa_cmp_pe'])
        o_nsa = _nsa_attention(qn, z_small, comp, slc_b.reshape(b, seq_len, kvw), slc_vt, win_all, b, seq_len, qpos0,
                               _pick(seq_len, 256), tm)
        o_d = _diff_attention(dq, dkv_b.reshape(b, seq_len, dw), diff_vt, lp['diff_lambda'], lp['diff_norm'], b,
                              seq_len, qpos0, lam_init, _pick(seq_len, 512), tm)
    else:
        pt = past['page_table']
        pps = max(p for p in (PAGES_PER_STEP, 8, 4, 2) if pt.shape[1] % p == 0)
        comp = _compress_paged(past['cmp'], layer, pt, cmp_f, lp['nsa_cmp_w'], lp['nsa_cmp_pe'], pps)
        o_nsa = _nsa_attention_paged(qn, z_small, comp, slc_f, win_all, past['slc'], layer, pt, qpos0, pps)
        o_d = _diff_attention_paged(dq, dkv_f, past['diff'], layer, pt, lp['diff_lambda'], lp['diff_norm'],
                                    lam_init, pps)

    x1 = _out_proj(x2, o_gla, o_nsa, o_d, lp['w_out'], layer, lp['norm'][1:2], g1, seq_len, tm)
    d_ff = lp['ffn_w_out'].shape[1]
    x2n = _ffn(x1, lp['norm'][2:3], sc2, sh2, lp['ffn_w_in'], lp['ffn_w_out'], layer, lp['norm'][3:4], g2, seq_len,
               tm, _pick(d_ff, 512) if d_ff % LANES == 0 else d_ff)

    return (x2n.reshape(b, seq_len, d), cmp_f, slc_f, dkv_f, win_f, gla_state)


def kernel(x_prompt, x_sample, c_prompt, c_sample, cache_nsa_cmp_kv, cache_nsa_slc_kv, cache_diff_kv,
           state_nsa_win_kv, state_gla, page_table, ada_w, ada_b, norm_w, w_in, gla_gate_w, gla_gate_b,
           gla_norm, nsa_cmp_pe, nsa_cmp_w, diff_lambda, diff_norm, w_out, ffn_w_in, ffn_w_out):
    depth = ada_w.shape[0]
    bp, lp_len, d = x_prompt.shape
    bs, ls_len, _ = x_sample.shape
    n_pages = page_table.shape[1]
    past_len = n_pages * PAGE_SIZE
    wb = state_nsa_win_kv.shape[2]
    kvw = 2 * NSA_KV_HEADS * HEAD_DIM
    hk = GLA_HEADS * GLA_DK

    n_c = bp + bs
    rows = -(-n_c // SUBLANES) * SUBLANES
    c_all = jnp.concatenate([c_prompt, c_sample, jnp.zeros((rows - n_c, d), F32)], axis=0)
    mod_all = _modulation(c_all, ada_w, ada_b)

    w_in_b, w_out_b = _permute_w_in(w_in), w_out.astype(BF16)
    ffn_w_in_b, ffn_w_out_b = ffn_w_in.astype(BF16), ffn_w_out.astype(BF16)
    past_views = {'cmp': _page_view(cache_nsa_cmp_kv), 'slc': _page_view(cache_nsa_slc_kv),
                  'diff': _page_view(cache_diff_kv), 'page_table': page_table}

    xp, xs = x_prompt, x_sample
    outs = [[] for _ in range(10)]
    for l in range(depth):
        cmp_w, cmp_pe = _compress_weights(nsa_cmp_w[l], nsa_cmp_pe[l])
        lp = {'layer': l, 'norm': norm_w[l], 'w_in': w_in_b,
              'gla_gate_w': gla_gate_w[l], 'gla_gate_b': gla_gate_b[l], 'gla_norm': gla_norm[l],
              'nsa_cmp_pe': cmp_pe, 'nsa_cmp_w': cmp_w, 'diff_lambda': diff_lambda[l], 'diff_norm': diff_norm[l],
              'w_out': w_out_b, 'ffn_w_in': ffn_w_in_b, 'ffn_w_out': ffn_w_out_b}
        lam_init = 0.8 - 0.6 * math.exp(-0.3 * l)

        win0 = jnp.zeros((bp, NSA_WINDOW, kvw), BF16)
        gla0 = jnp.zeros((bp, hk, GLA_DV), F32)
        xp, cmp_p, slc_p, diff_p, win_p, gla_p = _layer(xp, mod_all[l, :bp], 0, None, win0, gla0, lp, lam_init)

        win_prev = state_nsa_win_kv[l].reshape(bs, wb, kvw)
        win_in = jnp.concatenate([jnp.zeros((bs, NSA_WINDOW - wb, kvw), F32), win_prev], axis=1)
        xs, cmp_s, slc_s, diff_s, win_s, gla_s = _layer(xs, mod_all[l, bp:bp + bs], past_len, past_views, win_in,
                                                        state_gla[l].reshape(bs, hk, GLA_DV), lp, lam_init)

        kv_shape = lambda b, n: (b, n, 2, NSA_KV_HEADS, HEAD_DIM)
        outs[0].append(cmp_p.reshape(kv_shape(bp, lp_len)))
        outs[1].append(cmp_s.reshape(kv_shape(bs, ls_len)))
        outs[2].append(slc_p.reshape(kv_shape(bp, lp_len)))
        outs[3].append(slc_s.reshape(kv_shape(bs, ls_len)))
        outs[4].append(diff_p.reshape(bp, lp_len, 2, DIFF_HEADS, DIFF_V))
        outs[5].append(diff_s.reshape(bs, ls_len, 2, DIFF_HEADS, DIFF_V))
        win_p5 = win_p.reshape(kv_shape(bp, lp_len))
        outs[6].append(jnp.concatenate([jnp.zeros(kv_shape(bp, wb), F32), win_p5], axis=1)[:, -wb:])
        outs[7].append(jnp.concatenate([state_nsa_win_kv[l], win_s.reshape(kv_shape(bs, ls_len))], axis=1)[:, -wb:])
        outs[8].append(gla_p.reshape(bp, GLA_HEADS, GLA_DK, GLA_DV))
        outs[9].append(gla_s.reshape(bs, GLA_HEADS, GLA_DK, GLA_DV))

    return (xp, xs) + tuple(jnp.stack(o) for o in outs)
```

```python
import functools
import math

import numpy as np
import jax
import jax.numpy as jnp
from jax import lax
from jax.experimental import pallas as pl
from jax.experimental.pallas import tpu as pltpu

F32 = jnp.float32
BF16 = jnp.bfloat16

PAGE_SIZE = 128
HEAD_DIM = 128
GLA_HEADS, GLA_DK, GLA_DV, GLA_GATE_RANK, GLA_TAU = 4, 64, 128, 16, 16.0
NSA_HEADS, NSA_KV_HEADS = 8, 2
NSA_CMP_BLOCK, NSA_CMP_STRIDE, NSA_SEL_BLOCK, NSA_TOP_N, NSA_WINDOW = 32, 16, 64, 16, 512
NSA_FORCE_BONUS = 1.0e4
DIFF_HEADS, DIFF_QK, DIFF_V = 4, 64, 128
ROPE_THETA = 10000.0
NORM_EPS = 1e-6
NEG = -1.0e30

LANES = 128
SUBLANES = 8
BF16_ROWS = 16
VMEM_LIMIT_BYTES = 56 * 1024 * 1024
PAGES_PER_STEP = 16

Z_NQ, Z_GQ, Z_GK, Z_GV, Z_GR = 0, 1024, 1280, 1536, 2048
Z_CMP, Z_SLC, Z_WIN, Z_DQ, Z_DK, Z_DV, Z_WIDTH = 2560, 3072, 3584, 4096, 4608, 5120, 5632
SMALL_GA, SMALL_GATE = 0, 16


def _cparams(sem):
    return pltpu.CompilerParams(dimension_semantics=sem, vmem_limit_bytes=VMEM_LIMIT_BYTES)


def _dot(a, b):
    return jnp.dot(a, b, preferred_element_type=F32)


def _dot_nt(a, b):
    return lax.dot_general(a, b, (((1,), (1,)), ((), ())), preferred_element_type=F32)


def _dot_tn(a, b):
    return lax.dot_general(a, b, (((0,), (0,)), ((), ())), preferred_element_type=F32)


def _split3(a):
    hi = a.astype(BF16)
    r = a - hi.astype(F32)
    mid = r.astype(BF16)
    lo = (r - mid.astype(F32)).astype(BF16)
    return hi, mid, lo


def _dot3_rhs_exact(a, b):
    hi, mid, lo = _split3(a)
    return _dot(hi, b) + _dot(mid, b) + _dot(lo, b)


def _dot3_lhs_exact(a, b):
    hi, mid, lo = _split3(b)
    return _dot(a, hi) + _dot(a, mid) + _dot(a, lo)


def _sigmoid(x):
    return 1.0 / (1.0 + jnp.exp(-x))


def _masked_softmax(s, mask):
    s = jnp.where(mask, s, NEG)
    m = jnp.max(s, axis=-1, keepdims=True)
    e = jnp.where(mask, jnp.exp(s - m), 0.0)
    den = jnp.sum(e, axis=-1, keepdims=True)
    return e / jnp.where(den > 0, den, 1.0)


def _mod_kernel(c_ref, w_ref, b_ref, o_ref):
    c = c_ref[...]
    a = (c * _sigmoid(c)).astype(BF16)
    o_ref[0] = _dot(a, w_ref[0].astype(BF16)) + b_ref[0]


def _modulation(c_all, ada_w, ada_b):
    depth, d, n6 = ada_w.shape
    rows = c_all.shape[0]
    tn = max(t for t in range(LANES, min(1024, n6) + 1, LANES) if n6 % t == 0)
    return pl.pallas_call(
        _mod_kernel,
        out_shape=jax.ShapeDtypeStruct((depth, rows, n6), F32),
        grid=(depth, n6 // tn),
        in_specs=[pl.BlockSpec((rows, d), lambda l, j: (0, 0)),
                  pl.BlockSpec((1, d, tn), lambda l, j: (l, 0, j)),
                  pl.BlockSpec((1, 1, tn), lambda l, j: (l, 0, j))],
        out_specs=pl.BlockSpec((1, rows, tn), lambda l, j: (l, 0, j)),
        compiler_params=_cparams(("arbitrary", "arbitrary")),
        name="adaln_modulation",
    )(c_all, ada_w, ada_b.reshape(depth, 1, n6))


def _norm_mod(x, nw, sc, sh):
    ms = jnp.mean(x * x, axis=-1, keepdims=True)
    return x * lax.rsqrt(ms + NORM_EPS) * nw * (1.0 + sc) + sh


def _in_proj_kernel(x_ref, nw_ref, sc_ref, sh_ref, w_ref, ws_ref, o_ref, os_ref, h_ref):
    @pl.when(pl.program_id(1) == 0)
    def _():
        h = _norm_mod(x_ref[...], nw_ref[...], sc_ref[0], sh_ref[0]).astype(BF16)
        h_ref[...] = h
        os_ref[...] = _dot(h, ws_ref[0])

    o_ref[...] = _dot(h_ref[...], w_ref[0])


def _mod_spec(seq_len, tm, d):
    if seq_len % tm == 0:
        per = seq_len // tm
        return pl.BlockSpec((1, 1, d), lambda i, *_: (i // per, 0, 0))
    return pl.BlockSpec((1, tm, d), lambda i, *_: (0, i, 0))


def _expand_mod(m, seq_len, tm):
    if seq_len % tm == 0:
        return m[:, None, :]
    return jnp.repeat(m, seq_len, axis=0)[None]


def _in_proj(x2, nw, sc, sh, w, w_small, layer, seq_len, tm, tn):
    m, d = x2.shape
    n = w.shape[2]
    return pl.pallas_call(
        _in_proj_kernel,
        out_shape=[jax.ShapeDtypeStruct((m, n), F32), jax.ShapeDtypeStruct((m, LANES), F32)],
        grid=(m // tm, n // tn),
        in_specs=[pl.BlockSpec((tm, d), lambda i, j: (i, 0)),
                  pl.BlockSpec((1, d), lambda i, j: (0, 0)),
                  _mod_spec(seq_len, tm, d), _mod_spec(seq_len, tm, d),
                  pl.BlockSpec((1, d, tn), lambda i, j: (layer, 0, j)),
                  pl.BlockSpec((1, d, LANES), lambda i, j: (layer, 0, 0))],
        out_specs=[pl.BlockSpec((tm, tn), lambda i, j: (i, j)), pl.BlockSpec((tm, LANES), lambda i, j: (i, 0))],
        scratch_shapes=[pltpu.VMEM((tm, d), BF16)],
        compiler_params=_cparams(("arbitrary", "arbitrary")),
        name="in_proj",
    )(x2, nw, _expand_mod(sc, seq_len, tm), _expand_mod(sh, seq_len, tm), w, w_small)


def _rope_tables(pos):
    pos = np.asarray(pos, np.float64)[:, None]

    def tab(half, reps):
        inv = ROPE_THETA ** (-np.arange(half, dtype=np.float64) / half)
        ang = pos * inv[None, :]
        c, s = np.cos(ang), np.sin(ang)
        return np.tile(np.concatenate([c, c], -1), (1, reps)), np.tile(np.concatenate([-s, s], -1), (1, reps))

    c128, s128 = tab(HEAD_DIM // 2, 1)
    c64, s64 = tab(DIFF_QK // 2, 2)
    return np.concatenate([c128, s128, c64, s64], axis=-1).astype(np.float32)


def _rope128(x, cos, sin):
    return x * cos + pltpu.roll(x, 64, 1) * sin


def _rope64(x, cos, sin, first_half):
    partner = jnp.where(first_half, pltpu.roll(x, 96, 1), pltpu.roll(x, 32, 1))
    return x * cos + partner * sin


def _rope_kernel(nq_ref, cmp_ref, slc_ref, win_ref, dq_ref, dk_ref, dv_ref, tab_ref,
                 qn_o, cmp_o, cmpb_o, slc_o, slcb_o, win_o, winb_o, dq_o, dkv_o, dkvb_o, *vt_outs):
    if vt_outs:
        slc_vt_o, diff_vt_o = vt_outs
        for g in range(NSA_KV_HEADS):
            slc_vt_o[0, g, 0] = slc_ref[:, (NSA_KV_HEADS + g) * LANES:(NSA_KV_HEADS + g + 1) * LANES].T.astype(BF16)
        for h in range(DIFF_HEADS):
            diff_vt_o[0, h, 0] = dv_ref[:, h * LANES:(h + 1) * LANES].T.astype(BF16)
    tab = tab_ref[...]
    rows = tab.shape[0]
    c128, s128, c64, s64 = (tab[:, i * LANES:(i + 1) * LANES] for i in range(4))
    lane = lax.broadcasted_iota(jnp.int32, (1, LANES), 1)
    first_half = (lane % DIFF_QK) < (DIFF_QK // 2)
    nsa_scale = HEAD_DIM ** -0.5
    diff_scale = DIFF_QK ** -0.5

    for h in range(NSA_HEADS):
        sl = slice(h * LANES, (h + 1) * LANES)
        qn_o[:, sl] = (_rope128(nq_ref[:, sl], c128, s128) * nsa_scale).astype(qn_o.dtype)

    for src, dst, dstb in ((cmp_ref, cmp_o, cmpb_o), (slc_ref, slc_o, slcb_o), (win_ref, win_o, winb_o)):
        for g in range(2 * NSA_KV_HEADS):
            sl = slice(g * LANES, (g + 1) * LANES)
            v = src[:, sl]
            if g < NSA_KV_HEADS:
                v = _rope128(v, c128, s128)
            dst[pl.ds(g, rows, stride=2 * NSA_KV_HEADS), :] = v
            dstb[:, sl] = v.astype(BF16)

    for h in range(DIFF_HEADS):
        sl = slice(h * LANES, (h + 1) * LANES)
        dq_o[:, sl] = (_rope64(dq_ref[:, sl], c64, s64, first_half) * diff_scale).astype(dq_o.dtype)
        k = _rope64(dk_ref[:, sl], c64, s64, first_half)
        dkv_o[pl.ds(h, rows, stride=2 * DIFF_HEADS), :] = k
        dkvb_o[:, sl] = k.astype(BF16)
        sv = slice((DIFF_HEADS + h) * LANES, (DIFF_HEADS + h + 1) * LANES)
        v = dv_ref[:, sl]
        dkv_o[pl.ds(DIFF_HEADS + h, rows, stride=2 * DIFF_HEADS), :] = v
        dkvb_o[:, sv] = v.astype(BF16)


def _rope_split(z, tab, seq_len, tm, act_dt, transposed_values):
    m = z.shape[0]
    per = max(seq_len // tm, 1)
    tab_spec = (pl.BlockSpec((tm, 4 * LANES), lambda i: (i % per, 0)) if seq_len % tm == 0
                else pl.BlockSpec((tm, 4 * LANES), lambda i: (i, 0)))

    def zs(width, col):
        return pl.BlockSpec((tm, width), lambda i: (i, col // width))

    def os(width):
        return pl.BlockSpec((tm, width), lambda i: (i, 0))

    kv = (512, F32, True)
    outs = [(1024, act_dt, False), kv, (512, BF16, False), kv, (512, BF16, False), kv, (512, BF16, False),
            (512, act_dt, False), (1024, F32, True), (1024, BF16, False)]
    out_shape = [jax.ShapeDtypeStruct((m * w // LANES, LANES) if il else (m, w), dt) for w, dt, il in outs]
    out_specs = [pl.BlockSpec((tm * w // LANES, LANES), lambda i: (i, 0)) if il else os(w) for w, dt, il in outs]
    if transposed_values:
        assert seq_len % tm == 0
        for heads in (NSA_KV_HEADS, DIFF_HEADS):
            out_shape.append(jax.ShapeDtypeStruct((m // seq_len, heads, per, LANES, tm), BF16))
            out_specs.append(pl.BlockSpec((1, heads, 1, LANES, tm), lambda i: (i // per, 0, i % per, 0, 0)))
    return pl.pallas_call(
        _rope_kernel,
        out_shape=out_shape,
        grid=(m // tm,),
        in_specs=[zs(1024, Z_NQ), zs(512, Z_CMP), zs(512, Z_SLC), zs(512, Z_WIN),
                  zs(512, Z_DQ), zs(512, Z_DK), zs(512, Z_DV), tab_spec],
        out_specs=out_specs,
        compiler_params=_cparams(("arbitrary",)),
        name="rope_split",
    )(z, z, z, z, z, z, z, tab)


def _page_view(cache):
    depth, n_pool, page, two, heads, width = cache.shape
    return cache.reshape(depth, n_pool, page * two * heads, width)


def _page_specs(layer, page_rows, pages_per_step):
    def spec(p):
        return pl.BlockSpec((1, 1, page_rows, LANES),
                            lambda b, j, pt: (layer, pt[b, j * pages_per_step + p], 0, 0))
    return [spec(p) for p in range(pages_per_step)]


def _page_slabs(page_refs, first_slot, n, n_slots):
    return jnp.concatenate(
        [jnp.concatenate([ref[0, 0, pl.ds(first_slot + s, PAGE_SIZE, stride=n_slots), :] for s in range(n)], axis=1)
         for ref in page_refs], axis=0).astype(BF16)


def _token_slabs(ref, first_slot, n, n_slots):
    tokens = ref.shape[0] // n_slots
    return jnp.concatenate([ref[pl.ds(first_slot + s, tokens, stride=n_slots), :] for s in range(n)], axis=1)


def _pad_rows(x, multiple):
    pad = -x.shape[0] % multiple
    return x if pad == 0 else jnp.concatenate([x, jnp.zeros((pad, x.shape[1]), x.dtype)], axis=0)


def _gla_constants(c):
    t = np.arange(c)
    sizes = []
    s = c // 2
    while s >= 1:
        sizes.append(s)
        s //= 2
    sel, masks = [], []
    for sz in sizes:
        ref_row = (t // (2 * sz)) * (2 * sz) + sz - 1
        sel.append((t[None, :] <= ref_row[:, None]).astype(np.float32))
        same = (t[:, None] // (2 * sz)) == (t[None, :] // (2 * sz))
        masks.append((same & ((t[:, None] // sz) % 2 == 1) & ((t[None, :] // sz) % 2 == 0)).astype(np.float32))
    sel.append((t[None, :] <= t[:, None]).astype(np.float32))
    masks.append(np.eye(c, dtype=np.float32))
    return np.concatenate(sel, 0), np.stack(masks, 0)


def _gla_kernel(chunk, levels, q_ref, k_ref, v_ref, r_ref, sm_ref, gw_ref, gb_ref, nw_ref, s0_ref,
                sel_ref, mask_ref, o_ref, st_ref, state_ref):
    c = chunk
    c_in = q_ref.shape[0]
    hk = GLA_HEADS * GLA_DK
    ci = pl.program_id(1)

    @pl.when(ci == 0)
    def _():
        state_ref[...] = s0_ref[0].T

    def rows(x):
        if c_in == c:
            return x
        return jnp.concatenate([x, jnp.zeros((c - c_in, x.shape[1]), x.dtype)], axis=0)

    ga = rows(sm_ref[:, SMALL_GA:SMALL_GA + GLA_GATE_RANK])
    pre = _dot3_lhs_exact_both(ga, gw_ref[...]) + gb_ref[...]
    log_a = (jnp.minimum(pre, 0.0) - jnp.log(1.0 + jnp.exp(-jnp.abs(pre)))) / GLA_TAU
    if c_in != c:
        log_a = jnp.where(lax.broadcasted_iota(jnp.int32, (c, 1), 0) < c_in, log_a, 0.0)
    refs = _dot3_lhs_exact(sel_ref[...], log_a)
    b = refs[(levels - 1) * c:levels * c]
    q = rows(q_ref[...]) * (GLA_DK ** -0.5)
    k = rows(k_ref[...])
    v_all = rows(v_ref[...])
    lane = lax.broadcasted_iota(jnp.int32, (1, hk), 1)
    head_of_lane = lane // GLA_DK

    def stack_heads(x):
        return jnp.concatenate([jnp.where(head_of_lane == h, x, 0.0) for h in range(GLA_HEADS)], axis=0).astype(BF16)

    attn = jnp.zeros((GLA_HEADS, c, c), F32)
    for lv in range(levels):
        r = refs[lv * c:(lv + 1) * c]
        qd = stack_heads(q * jnp.exp(jnp.minimum(b - r, 0.0)))
        kd = (k * jnp.exp(jnp.minimum(r - b, 0.0))).astype(BF16)
        attn = attn + _dot_nt(qd, kd).reshape(GLA_HEADS, c, c) * mask_ref[lv][None]
    state = state_ref[...]
    inter = _dot_nt(stack_heads(q * jnp.exp(b)), state.astype(BF16))

    nw = nw_ref[...]
    for h in range(GLA_HEADS):
        vh = v_all[:, h * GLA_DV:(h + 1) * GLA_DV]
        o = (_dot(attn[h].astype(BF16), vh.astype(BF16)) + inter[h * c:(h + 1) * c])[0:c_in]
        y = o * lax.rsqrt(jnp.mean(o * o, axis=-1, keepdims=True) + NORM_EPS) * nw
        rh = r_ref[:, h * GLA_DV:(h + 1) * GLA_DV]
        o_ref[:, h * GLA_DV:(h + 1) * GLA_DV] = (y * (rh * _sigmoid(rh))).astype(o_ref.dtype)

    b_end = b[c - 1:c]
    kd = (k * jnp.exp(b_end - b)).astype(BF16)
    new_state = state * jnp.exp(b_end)
    for h in range(GLA_HEADS):
        vh = v_all[:, h * GLA_DV:(h + 1) * GLA_DV].astype(BF16)
        new_state = new_state + jnp.where(head_of_lane == h, _dot_tn(vh, kd), 0.0)
    state_ref[...] = new_state

    @pl.when(ci == pl.num_programs(1) - 1)
    def _():
        st_ref[0] = new_state.T


def _dot3_lhs_exact_both(a, b):
    ah, am, al = _split3(a)
    bh, bm, bl = _split3(b)
    return (_dot(ah, bh) + _dot(ah, bm) + _dot(am, bh)) + (_dot(ah, bl) + _dot(am, bm) + _dot(al, bh))


def _gla(z, z_small, gate_w, gate_b, norm_w, s0, batch, seq_len, chunk, act_dt):
    m = z.shape[0]
    nc = seq_len // chunk
    hk, hv = GLA_HEADS * GLA_DK, GLA_HEADS * GLA_DV
    comp_rows = max(chunk, BF16_ROWS)
    sel, masks = _gla_constants(comp_rows)
    levels = masks.shape[0]

    def zs(width, col):
        return pl.BlockSpec((chunk, width), lambda b, c: (b * nc + c, col // width))

    const2 = lambda b, c: (0, 0)
    return pl.pallas_call(
        functools.partial(_gla_kernel, comp_rows, levels),
        out_shape=[jax.ShapeDtypeStruct((m, hv), act_dt), jax.ShapeDtypeStruct((batch, hk, GLA_DV), F32)],
        grid=(batch, nc),
        in_specs=[zs(hk, Z_GQ), zs(hk, Z_GK), zs(hv, Z_GV), zs(hv, Z_GR), zs(LANES, 0),
                  pl.BlockSpec((GLA_GATE_RANK, hk), const2), pl.BlockSpec((1, hk), const2),
                  pl.BlockSpec((1, GLA_DV), const2),
                  pl.BlockSpec((1, hk, GLA_DV), lambda b, c: (b, 0, 0)),
                  pl.BlockSpec((levels * comp_rows, comp_rows), const2),
                  pl.BlockSpec((levels, comp_rows, comp_rows), lambda b, c: (0, 0, 0))],
        out_specs=[pl.BlockSpec((chunk, hv), lambda b, c: (b * nc + c, 0)),
                   pl.BlockSpec((1, hk, GLA_DV), lambda b, c: (b, 0, 0))],
        scratch_shapes=[pltpu.VMEM((GLA_DV, hk), F32)],
        compiler_params=_cparams(("arbitrary", "arbitrary")),
        name="gla",
    )(z, z, z, z, z_small, gate_w, gate_b.reshape(1, hk), norm_w.reshape(1, GLA_DV), s0,
      jnp.asarray(sel, BF16), jnp.asarray(masks, F32))


def _online_update(s, mask, v, m_ref, l_ref, acc_ref):
    if mask is not None:
        s = jnp.where(mask, s, NEG)
    m_prev = m_ref[...]
    m_new = jnp.maximum(m_prev, jnp.max(s, axis=-1, keepdims=True))
    p = jnp.exp(s - m_new)
    if mask is not None:
        p = jnp.where(mask, p, 0.0)
    alpha = jnp.exp(m_prev - m_new)
    l_ref[...] = alpha * l_ref[...] + jnp.sum(p, axis=-1, keepdims=True)
    acc_ref[...] = alpha * acc_ref[...] + _dot(p.astype(BF16), v)
    m_ref[...] = m_new


def _online_update_keymajor(s, mask, vt, m_ref, l_ref, acc_ref):
    if mask is not None:
        s = jnp.where(mask, s, NEG)
    m_prev = m_ref[...]
    m_new = jnp.maximum(m_prev, jnp.max(s, axis=0, keepdims=True))
    p = jnp.exp(s - m_new)
    if mask is not None:
        p = jnp.where(mask, p, 0.0)
    alpha = jnp.exp(m_prev - m_new)
    l_ref[...] = alpha * l_ref[...] + jnp.sum(p, axis=0, keepdims=True)
    acc_ref[...] = alpha * acc_ref[...] + _dot(vt, p.astype(BF16))
    m_ref[...] = m_new


def _diff_kernel(tq, tk, qpos0, lam_init, qi_ref, kj_ref, q_ref, k_ref, vt_ref, lam_ref, nw_ref, o_ref,
                 qq_ref, m_ref, l_ref, acc_ref):
    step_id = pl.program_id(2)
    i, j = qi_ref[step_id], kj_ref[step_id]
    q_lo = qpos0 + i * tq
    last_j = (q_lo + tq - 1) // tk

    @pl.when(j == 0)
    def _():
        q = q_ref[...].astype(F32)
        lane = lax.broadcasted_iota(jnp.int32, (1, LANES), 1)
        qq_ref[...] = jnp.concatenate([jnp.where(lane < DIFF_QK, q, 0.0),
                                       jnp.where(lane >= DIFF_QK, q, 0.0)], axis=0).astype(BF16)
        m_ref[...] = jnp.full_like(m_ref, NEG)
        l_ref[...] = jnp.zeros_like(l_ref)
        acc_ref[...] = jnp.zeros_like(acc_ref)

    def step(masked):
        s = _dot_nt(k_ref[0], qq_ref[...])
        mask = None
        if masked:
            kpos = j * tk + lax.broadcasted_iota(jnp.int32, (tk, 1), 0)
            qpos = q_lo + lax.broadcasted_iota(jnp.int32, (1, tq), 1)
            mk = kpos <= qpos
            mask = jnp.concatenate([mk, mk], axis=1)
        _online_update_keymajor(s, mask, vt_ref[0, 0, 0], m_ref, l_ref, acc_ref)

    fully_visible = (j + 1) * tk - 1 <= q_lo

    @pl.when(fully_visible)
    def _():
        step(False)

    @pl.when(jnp.logical_not(fully_visible))
    def _():
        step(True)

    @pl.when(j == last_j)
    def _():
        l = l_ref[...]
        o12 = (acc_ref[...] / jnp.where(l > 0, l, 1.0)).T
        o_ref[...] = _diff_finish(o12, lam_ref, nw_ref, lam_init).astype(o_ref.dtype)


def _diff_attention(dq, kv, vt, lam_p, norm_w, batch, seq_len, qpos0, lam_init, tq, tk):
    m = dq.shape[0]
    nq = seq_len // tq
    nk = kv.shape[1] // tk
    assert vt.shape[2] == nk and vt.shape[4] == tk

    pairs = [(i, j) for i in range(nq) for j in range(min(nk - 1, (qpos0 + i * tq + tq - 1) // tk) + 1)]
    qi = jnp.asarray([p[0] for p in pairs], jnp.int32)
    kj = jnp.asarray([p[1] for p in pairs], jnp.int32)
    const = lambda b, h, s, qi, kj: (0, 0)
    return pl.pallas_call(
        functools.partial(_diff_kernel, tq, tk, qpos0, lam_init),
        out_shape=jax.ShapeDtypeStruct((m, DIFF_HEADS * DIFF_V), dq.dtype),
        grid_spec=pltpu.PrefetchScalarGridSpec(
            num_scalar_prefetch=2,
            grid=(batch, DIFF_HEADS, len(pairs)),
            in_specs=[pl.BlockSpec((tq, LANES), lambda b, h, s, qi, kj: (b * nq + qi[s], h)),
                      pl.BlockSpec((1, tk, LANES), lambda b, h, s, qi, kj: (b, kj[s], h)),
                      pl.BlockSpec((1, 1, 1, LANES, tk), lambda b, h, s, qi, kj: (b, h, kj[s], 0, 0)),
                      pl.BlockSpec((4, DIFF_QK), const), pl.BlockSpec((1, DIFF_V), const)],
            out_specs=pl.BlockSpec((tq, LANES), lambda b, h, s, qi, kj: (b * nq + qi[s], h)),
            scratch_shapes=[pltpu.VMEM((2 * tq, LANES), BF16), pltpu.VMEM((1, 2 * tq), F32),
                            pltpu.VMEM((1, 2 * tq), F32), pltpu.VMEM((DIFF_V, 2 * tq), F32)]),
        compiler_params=_cparams(("arbitrary",) * 3),
        name="diff_attention",
    )(qi, kj, dq, kv, vt, lam_p, norm_w.reshape(1, DIFF_V))


def _diff_finish(o12, lam_ref, nw_ref, lam_init):
    half = o12.shape[0] // 2
    lam_p = lam_ref[...]
    lam = (jnp.exp(jnp.sum(lam_p[0:1] * lam_p[1:2], axis=-1, keepdims=True))
           - jnp.exp(jnp.sum(lam_p[2:3] * lam_p[3:4], axis=-1, keepdims=True)) + lam_init)
    o = o12[0:half] - lam * o12[half:]
    y = o * lax.rsqrt(jnp.mean(o * o, axis=-1, keepdims=True) + NORM_EPS) * nw_ref[...]
    return y * (1.0 - lam_init)


def _diff_paged_kernel(pps, lam_init, pt_ref, q_ref, new_ref, lam_ref, nw_ref, *rest):
    page_refs, o_ref = rest[:pps], rest[pps]
    qbd_ref, m_ref, l_ref, acc_ref = rest[pps + 1:]
    j = pl.program_id(1)
    n_new = q_ref.shape[0]
    rows_h = 2 * n_new
    hw = DIFF_HEADS * LANES

    @pl.when(j == 0)
    def _():
        q = q_ref[...]
        lane = lax.broadcasted_iota(jnp.int32, (1, LANES), 1)
        blocks = []
        for h in range(DIFF_HEADS):
            qh = q[:, h * LANES:(h + 1) * LANES]
            q12 = jnp.concatenate([jnp.where(lane < DIFF_QK, qh, 0.0), jnp.where(lane >= DIFF_QK, qh, 0.0)], axis=0)
            blocks.append(jnp.concatenate(
                [q12 if hh == h else jnp.zeros((rows_h, LANES), F32) for hh in range(DIFF_HEADS)], axis=1))
        qbd_ref[...] = jnp.concatenate(blocks, axis=0).astype(BF16)
        m_ref[...] = jnp.full_like(m_ref, NEG)
        l_ref[...] = jnp.zeros_like(l_ref)
        acc_ref[...] = jnp.zeros_like(acc_ref)

    k = _page_slabs(page_refs, 0, DIFF_HEADS, 2 * DIFF_HEADS)
    v = _page_slabs(page_refs, DIFF_HEADS, DIFF_HEADS, 2 * DIFF_HEADS)
    _online_update(_dot_nt(qbd_ref[...], k), None, v, m_ref, l_ref, acc_ref)

    @pl.when(j == pl.num_programs(1) - 1)
    def _():
        kn = _pad_rows(_token_slabs(new_ref, 0, DIFF_HEADS, 2 * DIFF_HEADS), BF16_ROWS).astype(BF16)
        vn = _pad_rows(_token_slabs(new_ref, DIFF_HEADS, DIFF_HEADS, 2 * DIFF_HEADS), BF16_ROWS).astype(BF16)
        t = lax.broadcasted_iota(jnp.int32, (DIFF_HEADS * rows_h, 1), 0) % n_new
        u = lax.broadcasted_iota(jnp.int32, (1, kn.shape[0]), 1)
        _online_update(_dot_nt(qbd_ref[...], kn), u <= t, vn, m_ref, l_ref, acc_ref)
        l = l_ref[...]
        o12 = acc_ref[...] / jnp.where(l > 0, l, 1.0)
        for h in range(DIFF_HEADS):
            r = slice(h * rows_h, (h + 1) * rows_h)
            o_ref[:, h * LANES:(h + 1) * LANES] = _diff_finish(o12[r, h * LANES:(h + 1) * LANES], lam_ref, nw_ref,
                                                               lam_init)


def _diff_attention_paged(dq, new_kv, cache_view, layer, page_table, lam_p, norm_w, lam_init, pps):
    b, n_pages = page_table.shape
    n_new = dq.shape[0] // b
    hw = DIFF_HEADS * LANES
    rows = DIFF_HEADS * 2 * n_new
    assert n_pages % pps == 0 and n_new % SUBLANES == 0
    const = lambda i, j, pt: (0, 0)
    return pl.pallas_call(
        functools.partial(_diff_paged_kernel, pps, lam_init),
        out_shape=jax.ShapeDtypeStruct(dq.shape, F32),
        grid_spec=pltpu.PrefetchScalarGridSpec(
            num_scalar_prefetch=1,
            grid=(b, n_pages // pps),
            in_specs=[pl.BlockSpec((n_new, hw), lambda i, j, pt: (i, 0)),
                      pl.BlockSpec((n_new * 2 * DIFF_HEADS, LANES), lambda i, j, pt: (i, 0)),
                      pl.BlockSpec((4, DIFF_QK), const), pl.BlockSpec((1, DIFF_V), const)]
                     + _page_specs(layer, PAGE_SIZE * 2 * DIFF_HEADS, pps),
            out_specs=pl.BlockSpec((n_new, hw), lambda i, j, pt: (i, 0)),
            scratch_shapes=[pltpu.VMEM((rows, hw), BF16), pltpu.VMEM((rows, 1), F32), pltpu.VMEM((rows, 1), F32),
                            pltpu.VMEM((rows, hw), F32)]),
        compiler_params=_cparams(("arbitrary", "arbitrary")),
        name="diff_attention_paged",
    )(page_table, dq, new_kv, lam_p, norm_w.reshape(1, DIFF_V), *([cache_view] * pps))


def _compress_weights(w, pe):
    k = NSA_CMP_STRIDE * HEAD_DIM
    return (jnp.concatenate([w[:, :k], w[:, k:]], axis=2).astype(BF16),
            pe.reshape(2, NSA_CMP_BLOCK // NSA_CMP_STRIDE, k))


def _compress_blocks(segments, rows, w_ref, pe_ref, o_ref):
    for c in range(2):
        pe = _dot(_pad_rows(pe_ref[c], BF16_ROWS).astype(BF16), w_ref[c])
        pe_term = pe[0:1, 0:HEAD_DIM] + pe[1:2, HEAD_DIM:2 * HEAD_DIM]
        for g in range(NSA_KV_HEADS):
            both = _dot(segments(c * NSA_KV_HEADS + g), w_ref[c])
            o_ref[0, c, g] = (both[:, 0:HEAD_DIM] + pltpu.roll(both[:, HEAD_DIM:2 * HEAD_DIM], rows - 1, 0)
                              + pe_term).astype(BF16)


def _compress_kernel(x_ref, w_ref, pe_ref, o_ref):
    tok_w = 2 * NSA_KV_HEADS * HEAD_DIM

    def segments(slot):
        return jnp.concatenate([x_ref[0, :, t * tok_w + slot * HEAD_DIM:t * tok_w + (slot + 1) * HEAD_DIM]
                                for t in range(NSA_CMP_STRIDE)], axis=1)

    _compress_blocks(segments, x_ref.shape[1], w_ref, pe_ref, o_ref)


def _compress_paged_kernel(pps, n_past_seg, pt_ref, new_ref, w_ref, pe_ref, *rest):
    page_refs, o_ref, x_ref = rest[:pps], rest[pps], rest[pps + 1]
    j = pl.program_id(1)
    slots = 2 * NSA_KV_HEADS
    seg_per_page = PAGE_SIZE // NSA_CMP_STRIDE
    rows = x_ref.shape[1]

    for pp in range(0, pps, 2):
        row0 = pl.multiple_of((j * pps + pp) * seg_per_page, 2 * seg_per_page)
        for slot in range(slots):
            for t in range(NSA_CMP_STRIDE):
                pair = [page_refs[pp + q][0, 0, pl.ds(t * slots + slot, seg_per_page, stride=NSA_CMP_STRIDE * slots), :]
                        for q in range(2)]
                x_ref[slot, pl.ds(row0, 2 * seg_per_page), t * HEAD_DIM:(t + 1) * HEAD_DIM] = (
                    jnp.concatenate(pair, axis=0).astype(BF16))

    @pl.when(j == pl.num_programs(1) - 1)
    def _():
        n_new = new_ref.shape[0] // slots
        first = lax.broadcasted_iota(jnp.int32, (rows - n_past_seg, 1), 0) == 0
        for slot in range(slots):
            for t in range(NSA_CMP_STRIDE):
                if t < n_new:
                    tail = jnp.where(first, new_ref[t * slots + slot:t * slots + slot + 1, :], 0.0)
                else:
                    tail = jnp.zeros((rows - n_past_seg, HEAD_DIM), F32)
                x_ref[slot, n_past_seg:rows, t * HEAD_DIM:(t + 1) * HEAD_DIM] = tail.astype(BF16)
        _compress_blocks(lambda slot: x_ref[slot], rows, w_ref, pe_ref, o_ref)


def _compress_paged(cache_view, layer, page_table, new_rows, w, pe, pps):
    b, n_pages = page_table.shape
    slots = 2 * NSA_KV_HEADS
    n_new = new_rows.shape[0] // (b * slots)
    seg_per_page = PAGE_SIZE // NSA_CMP_STRIDE
    n_past_seg = n_pages * seg_per_page
    rows = n_past_seg + 2 * BF16_ROWS
    assert n_new <= NSA_CMP_STRIDE and pps % 2 == 0 and n_pages % pps == 0
    return pl.pallas_call(
        functools.partial(_compress_paged_kernel, pps, n_past_seg),
        out_shape=jax.ShapeDtypeStruct((b, 2, NSA_KV_HEADS, rows, HEAD_DIM), BF16),
        grid_spec=pltpu.PrefetchScalarGridSpec(
            num_scalar_prefetch=1,
            grid=(b, n_pages // pps),
            in_specs=[pl.BlockSpec((n_new * slots, HEAD_DIM), lambda i, j, pt: (i, 0)),
                      pl.BlockSpec(w.shape, lambda i, j, pt: (0, 0, 0)),
                      pl.BlockSpec(pe.shape, lambda i, j, pt: (0, 0, 0))]
                     + _page_specs(layer, PAGE_SIZE * slots, pps),
            out_specs=pl.BlockSpec((1, 2, NSA_KV_HEADS, rows, HEAD_DIM), lambda i, j, pt: (i, 0, 0, 0, 0)),
            scratch_shapes=[pltpu.VMEM((slots, rows, NSA_CMP_STRIDE * HEAD_DIM), BF16)]),
        compiler_params=_cparams(("arbitrary", "arbitrary")),
        name="nsa_compress_paged",
    )(page_table, new_rows, w, pe, *([cache_view] * pps))


def _compress(seg, w, pe):
    b, rows, width = seg.shape
    return pl.pallas_call(
        _compress_kernel,
        out_shape=jax.ShapeDtypeStruct((b, 2, NSA_KV_HEADS, rows, HEAD_DIM), BF16),
        grid=(b,),
        in_specs=[pl.BlockSpec((1, rows, width), lambda i: (i, 0, 0)),
                  pl.BlockSpec(w.shape, lambda i: (0, 0, 0)),
                  pl.BlockSpec(pe.shape, lambda i: (0, 0, 0))],
        out_specs=pl.BlockSpec((1, 2, NSA_KV_HEADS, rows, HEAD_DIM), lambda i: (i, 0, 0, 0, 0)),
        compiler_params=_cparams(("arbitrary",)),
        name="nsa_compress",
    )(seg, w, pe)


def _nsa_band(n_cmp_rows, n_slc_pad):
    ratio = NSA_SEL_BLOCK // NSA_CMP_STRIDE
    span = NSA_CMP_BLOCK // NSA_CMP_STRIDE
    n = np.arange(n_cmp_rows)[:, None]
    j = np.arange(n_slc_pad)[None, :]
    return ((n >= ratio * j - (span - 1)) & (n <= ratio * j + ratio - 1)).astype(np.float32)


def _nsa_compressed(qs, qpos_rows, ck, cv):
    cmp_end = lax.broadcasted_iota(jnp.int32, (1, ck.shape[0]), 1) * NSA_CMP_STRIDE + (NSA_CMP_BLOCK - 1)
    p_c = _masked_softmax(_dot_nt(qs, ck), cmp_end <= qpos_rows)
    return p_c, _dot(p_c.astype(BF16), cv)


def _nsa_block_scores(p_c, qpos, band_ref):
    tq = qpos.shape[0]
    imp = p_c[0:tq]
    for n in range(1, p_c.shape[0] // tq):
        imp = imp + p_c[n * tq:(n + 1) * tq]
    p_slc = _dot3_rhs_exact(_pad_rows(imp, BF16_ROWS), band_ref[...])[0:tq]
    blk = lax.broadcasted_iota(jnp.int32, (1, band_ref.shape[1]), 1)
    cur = qpos // NSA_SEL_BLOCK
    valid = blk * NSA_SEL_BLOCK <= qpos
    forced = (blk == 0) | (blk == cur) | (blk == cur - 1)
    return jnp.where(valid, p_slc + jnp.where(forced, NSA_FORCE_BONUS, 0.0), NEG)


def _nsa_select(score, score_ref, n_slc):
    score_t = score.T
    score_ref[...] = score_t
    blk_t = lax.broadcasted_iota(jnp.int32, (score_t.shape[0], 1), 0)

    def count(jp, cnt):
        row = score_ref[pl.ds(jp, 1), :]
        beats = (row > score_t) | ((row == score_t) & (jp < blk_t))
        return cnt + jnp.where(beats, 1.0, 0.0)

    cnt = lax.fori_loop(0, n_slc, count, jnp.zeros(score_t.shape, F32))
    return jnp.where((cnt < NSA_TOP_N) & (score_t > 0.5 * NEG), 1.0, 0.0)


def _nsa_select_few(score, score_ref, n_slc):
    nq, nsp = score.shape
    fold = LANES // nq
    rows = nsp // fold
    score_t = _pad_rows(score, LANES).T[:, 0:nq]
    score_ref[...] = jnp.concatenate([score_t] * fold, axis=1)
    packed = jnp.concatenate([score_t[g * rows:(g + 1) * rows] for g in range(fold)], axis=1)
    blk = (lax.broadcasted_iota(jnp.int32, (1, LANES), 1) // nq) * rows + lax.broadcasted_iota(jnp.int32, (rows, 1), 0)

    def count(jp, cnt):
        row = score_ref[pl.ds(jp, 1), :]
        beats = (row > packed) | ((row == packed) & (jp < blk))
        return cnt + jnp.where(beats, 1.0, 0.0)

    cnt = lax.fori_loop(0, n_slc, count, jnp.zeros(packed.shape, F32))
    sel_p = jnp.where((cnt < NSA_TOP_N) & (packed > 0.5 * NEG), 1.0, 0.0)
    sel_t = _pad_rows(sel_p, LANES).T
    return jnp.concatenate([sel_t[g * nq:(g + 1) * nq, 0:rows] for g in range(fold)], axis=1)


def _nsa_window(qs, qpos_rows, kw, vw, first_pos):
    wpos = first_pos + lax.broadcasted_iota(jnp.int32, (1, kw.shape[0]), 1)
    mask_w = (wpos <= qpos_rows) & (wpos > qpos_rows - NSA_WINDOW) & (wpos >= 0)
    return _dot(_masked_softmax(_dot_nt(qs, kw), mask_w).astype(BF16), vw)


def _nsa_kernel(tq, tk, qpos0, n_slc, q_ref, sm_ref, ck_ref, cv_ref, sk_ref, svt_ref, wk_ref, wv_ref,
                band_ref, o_ref, score_ref, sel_ref, m_ref, l_ref, acc_ref):
    i = pl.program_id(1)
    hpg = NSA_HEADS // NSA_KV_HEADS
    q_lo = qpos0 + i * tq
    qpos = q_lo + lax.broadcasted_iota(jnp.int32, (tq, 1), 0)
    qpos_lanes = q_lo + lax.broadcasted_iota(jnp.int32, (1, tq), 1)
    qpos_rows = jnp.concatenate([qpos] * hpg, axis=0)
    gates = _sigmoid(sm_ref[:, SMALL_GATE:SMALL_GATE + 3 * NSA_HEADS])
    blocks_per_tile = tk // NSA_SEL_BLOCK

    for g in range(NSA_KV_HEADS):
        q = q_ref[:, g * hpg * LANES:(g + 1) * hpg * LANES]
        qs = jnp.concatenate([q[:, n * LANES:(n + 1) * LANES] for n in range(hpg)], axis=0).astype(BF16)

        p_c, o_c = _nsa_compressed(qs, qpos_rows, ck_ref[0, 0, g], cv_ref[0, 0, g])
        n_rank = jnp.minimum((q_lo + tq - 1) // NSA_SEL_BLOCK + 1, n_slc)
        sel_ref[...] = _nsa_select(_nsa_block_scores(p_c, qpos, band_ref), score_ref, n_rank)

        m_ref[...] = jnp.full_like(m_ref, NEG)
        l_ref[...] = jnp.zeros_like(l_ref)
        acc_ref[...] = jnp.zeros_like(acc_ref)

        def key_tile(kt, carry):
            start = pl.multiple_of(kt * tk, tk)
            k = sk_ref[0, pl.ds(start, tk), g * LANES:(g + 1) * LANES]
            blocks = sel_ref[pl.ds(pl.multiple_of(kt * blocks_per_tile, blocks_per_tile), blocks_per_tile), :]
            picked = jnp.concatenate([jnp.broadcast_to(blocks[r:r + 1], (NSA_SEL_BLOCK, tq))
                                      for r in range(blocks_per_tile)], axis=0) > 0.5
            kpos = kt * tk + lax.broadcasted_iota(jnp.int32, (tk, 1), 0)
            mk = picked & (kpos <= qpos_lanes)
            mask = jnp.concatenate([mk] * hpg, axis=1)
            _online_update_keymajor(_dot_nt(k, qs), mask, svt_ref[0, g, kt], m_ref, l_ref, acc_ref)
            return carry

        lax.fori_loop(0, (q_lo + tq - 1) // tk + 1, key_tile, 0)
        l = l_ref[...]
        o_st = acc_ref[...] / jnp.where(l > 0, l, 1.0)
        o_s = jnp.concatenate([o_st[:, n * tq:(n + 1) * tq].T for n in range(hpg)], axis=0)

        wrows = -(-(tq + NSA_WINDOW) // BF16_ROWS) * BF16_ROWS
        wstart = pl.multiple_of(i * tq, tq)
        o_w = _nsa_window(qs, qpos_rows, wk_ref[0, pl.ds(wstart, wrows), g * LANES:(g + 1) * LANES],
                          wv_ref[0, pl.ds(wstart, wrows), g * LANES:(g + 1) * LANES], q_lo - NSA_WINDOW)

        for n in range(hpg):
            h = g * hpg + n
            r = slice(n * tq, (n + 1) * tq)
            o = (gates[:, 3 * h:3 * h + 1] * o_c[r] + gates[:, 3 * h + 1:3 * h + 2] * o_s[r]
                 + gates[:, 3 * h + 2:3 * h + 3] * o_w[r])
            o_ref[:, h * LANES:(h + 1) * LANES] = o.astype(o_ref.dtype)


def _nsa_attention(qn, z_small, comp, slc_buf, slc_vt, win_all, batch, seq_len, qpos0, tq, tk):
    m = qn.shape[0]
    nq = seq_len // tq
    t_keys = slc_buf.shape[1]
    n_cmp_rows = comp.shape[3]
    n_slc = -(-(qpos0 + seq_len) // NSA_SEL_BLOCK)
    nsp = -(-n_slc // LANES) * LANES
    assert tq % LANES == 0 and tk % NSA_SEL_BLOCK == 0 and slc_vt.shape[2] * tk == t_keys and nsp * NSA_SEL_BLOCK >= t_keys
    band = _nsa_band(n_cmp_rows, nsp)
    hw = NSA_HEADS * HEAD_DIM
    kvw = NSA_KV_HEADS * HEAD_DIM
    rows = (NSA_HEADS // NSA_KV_HEADS) * tq
    wlen = win_all.shape[1]
    return pl.pallas_call(
        functools.partial(_nsa_kernel, tq, tk, qpos0, n_slc),
        out_shape=jax.ShapeDtypeStruct((m, hw), qn.dtype),
        grid=(batch, nq),
        in_specs=[pl.BlockSpec((tq, hw), lambda b, i: (b * nq + i, 0)),
                  pl.BlockSpec((tq, LANES), lambda b, i: (b * nq + i, 0)),
                  pl.BlockSpec((1, 1, NSA_KV_HEADS, n_cmp_rows, HEAD_DIM), lambda b, i: (b, 0, 0, 0, 0)),
                  pl.BlockSpec((1, 1, NSA_KV_HEADS, n_cmp_rows, HEAD_DIM), lambda b, i: (b, 1, 0, 0, 0)),
                  pl.BlockSpec((1, t_keys, kvw), lambda b, i: (b, 0, 0)),
                  pl.BlockSpec((1,) + slc_vt.shape[1:], lambda b, i: (b, 0, 0, 0, 0)),
                  pl.BlockSpec((1, wlen, kvw), lambda b, i: (b, 0, 0)),
                  pl.BlockSpec((1, wlen, kvw), lambda b, i: (b, 0, 1)),
                  pl.BlockSpec((n_cmp_rows, nsp), lambda b, i: (0, 0))],
        out_specs=pl.BlockSpec((tq, hw), lambda b, i: (b * nq + i, 0)),
        scratch_shapes=[pltpu.VMEM((nsp, tq), F32), pltpu.VMEM((nsp, tq), F32), pltpu.VMEM((1, rows), F32),
                        pltpu.VMEM((1, rows), F32), pltpu.VMEM((HEAD_DIM, rows), F32)],
        compiler_params=_cparams(("arbitrary", "arbitrary")),
        name="nsa_attention",
    )(qn, z_small, comp, comp, slc_buf, slc_vt, win_all, win_all, jnp.asarray(band, BF16))


def _nsa_paged_kernel(pps, qpos0, n_slc, pt_ref, q_ref, sm_ref, ck_ref, cv_ref, new_ref, wst_ref, wnew_ref, band_ref,
                      *rest):
    page_refs, o_ref = rest[:pps], rest[pps]
    score_ref, qbd_ref, sel_ref, m_ref, l_ref, acc_ref, oc_ref, ow_ref = rest[pps + 1:]
    j = pl.program_id(1)
    n_new = q_ref.shape[0]
    groups = NSA_KV_HEADS
    hpg = NSA_HEADS // groups
    rows_g = hpg * n_new
    nsp = band_ref.shape[1]
    wb = wst_ref.shape[2] // (2 * groups)
    qpos = qpos0 + lax.broadcasted_iota(jnp.int32, (n_new, 1), 0)

    @pl.when(j == 0)
    def _():
        q = q_ref[...]
        qpos_rows = jnp.concatenate([qpos] * hpg, axis=0)
        scores, blocks = [], []
        for g in range(groups):
            qf = jnp.concatenate([q[:, (g * hpg + n) * LANES:(g * hpg + n + 1) * LANES] for n in range(hpg)], axis=0)
            qs = qf.astype(BF16)
            p_c, o_c = _nsa_compressed(qs, qpos_rows, ck_ref[0, 0, g], cv_ref[0, 0, g])
            oc_ref[g * rows_g:(g + 1) * rows_g] = o_c
            scores.append(_nsa_block_scores(p_c, qpos, band_ref))
            kw, vw = (_pad_rows(jnp.concatenate(
                [wst_ref[0, 0, pl.ds(slot, wb, stride=2 * groups), :], wnew_ref[pl.ds(slot, n_new, stride=2 * groups), :]],
                axis=0), BF16_ROWS).astype(BF16) for slot in (g, groups + g))
            ow_ref[g * rows_g:(g + 1) * rows_g] = _nsa_window(qs, qpos_rows, kw, vw, qpos0 - wb)
            blocks.append(jnp.concatenate(
                [qf if gg == g else jnp.zeros((rows_g, LANES), F32) for gg in range(groups)], axis=1))
        qbd_ref[...] = jnp.concatenate(blocks, axis=0).astype(BF16)
        sel = _nsa_select_few(jnp.concatenate(scores, axis=0), score_ref, n_slc)
        sel_ref[...] = jnp.concatenate(
            [sel[g * n_new:(g + 1) * n_new] for g in range(groups) for _ in range(hpg)], axis=0).astype(BF16)
        m_ref[...] = jnp.full_like(m_ref, NEG)
        l_ref[...] = jnp.zeros_like(l_ref)
        acc_ref[...] = jnp.zeros_like(acc_ref)

    k = _page_slabs(page_refs, 0, groups, 2 * groups)
    v = _page_slabs(page_refs, groups, groups, 2 * groups)
    n_keys = pps * PAGE_SIZE
    key_blk = (j * n_keys + lax.broadcasted_iota(jnp.int32, (1, n_keys), 1)) // NSA_SEL_BLOCK
    expand = jnp.where(lax.broadcasted_iota(jnp.int32, (nsp, 1), 0) == key_blk, 1.0, 0.0).astype(BF16)
    picked = _dot(sel_ref[...], expand) > 0.5
    _online_update(_dot_nt(qbd_ref[...], k), picked, v, m_ref, l_ref, acc_ref)

    @pl.when(j == pl.num_programs(1) - 1)
    def _():
        kn = _pad_rows(_token_slabs(new_ref, 0, groups, 2 * groups), BF16_ROWS).astype(BF16)
        vn = _pad_rows(_token_slabs(new_ref, groups, groups, 2 * groups), BF16_ROWS).astype(BF16)
        new_blk = qpos0 // NSA_SEL_BLOCK
        t = lax.broadcasted_iota(jnp.int32, (groups * rows_g, 1), 0) % n_new
        u = lax.broadcasted_iota(jnp.int32, (1, kn.shape[0]), 1)
        mask = (u <= t) & (sel_ref[:, new_blk:new_blk + 1] > 0.5)
        _online_update(_dot_nt(qbd_ref[...], kn), mask, vn, m_ref, l_ref, acc_ref)
        l = l_ref[...]
        o_sel = acc_ref[...] / jnp.where(l > 0, l, 1.0)
        gates = _sigmoid(sm_ref[:, SMALL_GATE:SMALL_GATE + 3 * NSA_HEADS])
        for g in range(groups):
            for n in range(hpg):
                h = g * hpg + n
                r = slice(g * rows_g + n * n_new, g * rows_g + (n + 1) * n_new)
                o_ref[:, h * LANES:(h + 1) * LANES] = (
                    gates[:, 3 * h:3 * h + 1] * oc_ref[r] + gates[:, 3 * h + 1:3 * h + 2] * o_sel[r, g * LANES:(g + 1) * LANES]
                    + gates[:, 3 * h + 2:3 * h + 3] * ow_ref[r])


def _nsa_attention_paged(qn, z_small, comp, new_kv, win_state, win_new, cache_view, layer, page_table, qpos0, pps):
    b, n_pages = page_table.shape
    n_new = qn.shape[0] // b
    groups = NSA_KV_HEADS
    hw = NSA_HEADS * HEAD_DIM
    gw = groups * HEAD_DIM
    rows = NSA_HEADS * n_new
    n_cmp_rows = comp.shape[3]
    n_slc = -(-(qpos0 + n_new) // NSA_SEL_BLOCK)
    nsp = -(-n_slc // LANES) * LANES
    wrows = win_state.shape[2]
    assert wrows == 2 * groups * min(NSA_WINDOW, qpos0)
    assert n_pages % pps == 0 and qpos0 == n_pages * PAGE_SIZE and n_new <= NSA_SEL_BLOCK
    assert qpos0 % NSA_SEL_BLOCK == 0 and LANES % (groups * n_new) == 0 and n_new % SUBLANES == 0
    assert nsp % (SUBLANES * LANES // (groups * n_new)) == 0
    band = _nsa_band(n_cmp_rows, nsp)
    const = lambda i, j, pt: (0, 0)
    return pl.pallas_call(
        functools.partial(_nsa_paged_kernel, pps, qpos0, n_slc),
        out_shape=jax.ShapeDtypeStruct(qn.shape, F32),
        grid_spec=pltpu.PrefetchScalarGridSpec(
            num_scalar_prefetch=1,
            grid=(b, n_pages // pps),
            in_specs=[pl.BlockSpec((n_new, hw), lambda i, j, pt: (i, 0)),
                      pl.BlockSpec((n_new, LANES), lambda i, j, pt: (i, 0)),
                      pl.BlockSpec((1, 1, groups, n_cmp_rows, HEAD_DIM), lambda i, j, pt: (i, 0, 0, 0, 0)),
                      pl.BlockSpec((1, 1, groups, n_cmp_rows, HEAD_DIM), lambda i, j, pt: (i, 1, 0, 0, 0)),
                      pl.BlockSpec((n_new * 2 * groups, LANES), lambda i, j, pt: (i, 0)),
                      pl.BlockSpec((1, 1, wrows, LANES), lambda i, j, pt: (layer, i, 0, 0)),
                      pl.BlockSpec((n_new * 2 * groups, LANES), lambda i, j, pt: (i, 0)),
                      pl.BlockSpec((n_cmp_rows, nsp), const)]
                     + _page_specs(layer, PAGE_SIZE * 2 * groups, pps),
            out_specs=pl.BlockSpec((n_new, hw), lambda i, j, pt: (i, 0)),
            scratch_shapes=[pltpu.VMEM((nsp, LANES), F32), pltpu.VMEM((rows, gw), BF16), pltpu.VMEM((rows, nsp), BF16),
                            pltpu.VMEM((rows, 1), F32), pltpu.VMEM((rows, 1), F32), pltpu.VMEM((rows, gw), F32),
                            pltpu.VMEM((rows, HEAD_DIM), F32), pltpu.VMEM((rows, HEAD_DIM), F32)]),
        compiler_params=_cparams(("arbitrary", "arbitrary")),
        name="nsa_attention_paged",
    )(page_table, qn, z_small, comp, comp, new_kv, win_state, win_new, jnp.asarray(band, BF16),
      *([cache_view] * pps))


def _out_proj_kernel(x_ref, og_ref, on_ref, od_ref, w_ref, nw_ref, g_ref, o_ref):
    a = jnp.concatenate([og_ref[...].astype(BF16), on_ref[...].astype(BF16), od_ref[...].astype(BF16)], axis=-1)
    mix = _dot(a, w_ref[0])
    y = mix * lax.rsqrt(jnp.mean(mix * mix, axis=-1, keepdims=True) + NORM_EPS) * nw_ref[...]
    o_ref[...] = x_ref[...] + g_ref[0] * y


def _out_proj(x2, o_gla, o_nsa, o_d, w, layer, nw, gate, seq_len, tm):
    m, d = x2.shape
    row = lambda width: pl.BlockSpec((tm, width), lambda i: (i, 0))
    return pl.pallas_call(
        _out_proj_kernel,
        out_shape=jax.ShapeDtypeStruct((m, d), F32),
        grid=(m // tm,),
        in_specs=[row(d), row(o_gla.shape[1]), row(o_nsa.shape[1]), row(o_d.shape[1]),
                  pl.BlockSpec((1,) + w.shape[1:], lambda i: (layer, 0, 0)), pl.BlockSpec((1, d), lambda i: (0, 0)),
                  _mod_spec(seq_len, tm, d)],
        out_specs=row(d),
        compiler_params=_cparams(("arbitrary",)),
        name="out_proj",
    )(x2, o_gla, o_nsa, o_d, w, nw, _expand_mod(gate, seq_len, tm))


def _ffn_kernel(x_ref, nw2_ref, sc_ref, sh_ref, wg_ref, wu_ref, wo_ref, nw3_ref, g_ref, o_ref, h_ref, acc_ref):
    j = pl.program_id(1)

    @pl.when(j == 0)
    def _():
        h_ref[...] = _norm_mod(x_ref[...], nw2_ref[...], sc_ref[0], sh_ref[0]).astype(BF16)
        acc_ref[...] = jnp.zeros_like(acc_ref)

    h = h_ref[...]
    gate = _dot(h, wg_ref[0])
    up = _dot(h, wu_ref[0])
    acc_ref[...] += _dot((gate * _sigmoid(gate) * up).astype(BF16), wo_ref[0])

    @pl.when(j == pl.num_programs(1) - 1)
    def _():
        f = acc_ref[...]
        y = f * lax.rsqrt(jnp.mean(f * f, axis=-1, keepdims=True) + NORM_EPS) * nw3_ref[...]
        o_ref[...] = x_ref[...] + g_ref[0] * y


def _ffn(x2, nw2, sc, sh, w_in, w_out, layer, nw3, gate, seq_len, tm, tf):
    m, d = x2.shape
    d_ff = w_out.shape[1]
    nf = d_ff // tf
    row = pl.BlockSpec((tm, d), lambda i, j: (i, 0))
    vec = pl.BlockSpec((1, d), lambda i, j: (0, 0))
    return pl.pallas_call(
        _ffn_kernel,
        out_shape=jax.ShapeDtypeStruct((m, d), F32),
        grid=(m // tm, nf),
        in_specs=[row, vec, _mod_spec(seq_len, tm, d), _mod_spec(seq_len, tm, d),
                  pl.BlockSpec((1, d, tf), lambda i, j: (layer, 0, j)),
                  pl.BlockSpec((1, d, tf), lambda i, j: (layer, 0, nf + j)),
                  pl.BlockSpec((1, tf, d), lambda i, j: (layer, j, 0)),
                  vec, _mod_spec(seq_len, tm, d)],
        out_specs=row,
        scratch_shapes=[pltpu.VMEM((tm, d), BF16), pltpu.VMEM((tm, d), F32)],
        compiler_params=_cparams(("arbitrary", "arbitrary")),
        name="ffn",
    )(x2, nw2, _expand_mod(sc, seq_len, tm), _expand_mod(sh, seq_len, tm), w_in, w_in, w_out, nw3,
      _expand_mod(gate, seq_len, tm))


def _pick(n, target):
    if n <= target:
        return n
    for t in range(target, 7, -1):
        if n % t == 0 and t % SUBLANES == 0:
            return t
    return n


def _permute_w_in(w):
    depth, d = w.shape[:2]
    hk, hv = GLA_HEADS * GLA_DK, GLA_HEADS * GLA_DV
    kvw = NSA_KV_HEADS * HEAD_DIM
    sizes = [hk, hk, hv, hv, GLA_GATE_RANK, NSA_HEADS * HEAD_DIM, kvw, kvw, kvw, kvw, kvw, kvw, NSA_HEADS * 3,
             DIFF_HEADS * 2 * DIFF_QK, DIFF_HEADS * 2 * DIFF_QK, DIFF_HEADS * DIFF_V]
    offs = np.concatenate([[0], np.cumsum(sizes)])
    part = lambda k: w[:, :, offs[k]:offs[k + 1]]
    main = jnp.concatenate([part(k) for k in (5, 0, 1, 2, 3, 6, 7, 8, 9, 10, 11, 13, 14, 15)], axis=2)
    pad = LANES - GLA_GATE_RANK - NSA_HEADS * 3
    small = jnp.concatenate([part(4), part(12), jnp.zeros((depth, d, pad), w.dtype)], axis=2)
    return main.astype(BF16), small.astype(BF16)


def _layer(x, mod, qpos0, past, gla_s0, lp, lam_init):
    b, seq_len, d = x.shape
    m = b * seq_len
    x2 = x.reshape(m, d)
    sh1, sc1, g1, sh2, sc2, g2 = jnp.split(mod, 6, axis=-1)
    tm = _pick(seq_len, 512) if seq_len >= 128 else m
    tm_in = _pick(seq_len, 1024) if seq_len >= 128 else m
    layer = lp['layer']
    z, z_small = _in_proj(x2, lp['norm'][0:1], sc1, sh1, *lp['w_in'], layer, seq_len, tm_in, _pick(Z_WIDTH, 512))

    tab = _rope_tables(qpos0 + np.arange(seq_len))
    if seq_len % tm != 0:
        tab = np.tile(tab, (m // seq_len, 1))
    tab = jnp.asarray(tab)
    act_dt = BF16 if seq_len % BF16_ROWS == 0 else F32
    (qn, cmp_f, cmp_b, slc_f, slc_b, win_f, win_b, dq, dkv_f, dkv_b, *vts) = _rope_split(
        z, tab, seq_len, tm, act_dt, transposed_values=past is None)

    chunk = 64 if seq_len % 64 == 0 else seq_len
    o_gla, gla_state = _gla(z, z_small, lp['gla_gate_w'], lp['gla_gate_b'], lp['gla_norm'], gla_s0, b, seq_len,
                            chunk, act_dt)

    kvw = 2 * NSA_KV_HEADS * HEAD_DIM
    dw = 2 * DIFF_HEADS * DIFF_V

    if past is None:
        slc_vt, diff_vt = vts
        win_pad = jnp.zeros((b, -(NSA_WINDOW + seq_len) % BF16_ROWS, kvw), BF16)
        win_all = jnp.concatenate([jnp.zeros((b, NSA_WINDOW, kvw), BF16), win_b.reshape(b, seq_len, kvw), win_pad],
                                  axis=1)
        seg = cmp_b.reshape(b, seq_len // NSA_CMP_STRIDE, NSA_CMP_STRIDE * kvw)
        comp = _compress(seg, lp['nsa_cmp_w'], lp['nsa_cmp_pe'])
        o_nsa = _nsa_attention(qn, z_small, comp, slc_b.reshape(b, seq_len, kvw), slc_vt, win_all, b, seq_len, qpos0,
                               _pick(seq_len, 256), tm)
        o_d = _diff_attention(dq, dkv_b.reshape(b, seq_len, dw), diff_vt, lp['diff_lambda'], lp['diff_norm'], b,
                              seq_len, qpos0, lam_init, _pick(seq_len, 512), tm)
    else:
        pt = past['page_table']
        pps = max(p for p in (PAGES_PER_STEP, 8, 4, 2) if pt.shape[1] % p == 0)
        comp = _compress_paged(past['cmp'], layer, pt, cmp_f, lp['nsa_cmp_w'], lp['nsa_cmp_pe'], pps)
        o_nsa = _nsa_attention_paged(qn, z_small, comp, slc_f, past['win'], win_f, past['slc'], layer, pt, qpos0, pps)
        o_d = _diff_attention_paged(dq, dkv_f, past['diff'], layer, pt, lp['diff_lambda'], lp['diff_norm'],
                                    lam_init, pps)

    x1 = _out_proj(x2, o_gla, o_nsa, o_d, lp['w_out'], layer, lp['norm'][1:2], g1, seq_len, tm)
    d_ff = lp['ffn_w_out'].shape[1]
    x2n = _ffn(x1, lp['norm'][2:3], sc2, sh2, lp['ffn_w_in'], lp['ffn_w_out'], layer, lp['norm'][3:4], g2, seq_len,
               tm, _pick(d_ff, 512) if d_ff % LANES == 0 else d_ff)

    return (x2n.reshape(b, seq_len, d), cmp_f, slc_f, dkv_f, win_f, gla_state)


def kernel(x_prompt, x_sample, c_prompt, c_sample, cache_nsa_cmp_kv, cache_nsa_slc_kv, cache_diff_kv,
           state_nsa_win_kv, state_gla, page_table, ada_w, ada_b, norm_w, w_in, gla_gate_w, gla_gate_b,
           gla_norm, nsa_cmp_pe, nsa_cmp_w, diff_lambda, diff_norm, w_out, ffn_w_in, ffn_w_out):
    depth = ada_w.shape[0]
    bp, lp_len, d = x_prompt.shape
    bs, ls_len, _ = x_sample.shape
    n_pages = page_table.shape[1]
    past_len = n_pages * PAGE_SIZE
    wb = state_nsa_win_kv.shape[2]
    kvw = 2 * NSA_KV_HEADS * HEAD_DIM
    hk = GLA_HEADS * GLA_DK

    n_c = bp + bs
    rows = -(-n_c // SUBLANES) * SUBLANES
    c_all = jnp.concatenate([c_prompt, c_sample, jnp.zeros((rows - n_c, d), F32)], axis=0)
    mod_all = _modulation(c_all, ada_w, ada_b)

    w_in_b, w_out_b = _permute_w_in(w_in), w_out.astype(BF16)
    ffn_w_in_b, ffn_w_out_b = ffn_w_in.astype(BF16), ffn_w_out.astype(BF16)
    past_views = {'cmp': _page_view(cache_nsa_cmp_kv), 'slc': _page_view(cache_nsa_slc_kv),
                  'diff': _page_view(cache_diff_kv), 'win': _page_view(state_nsa_win_kv), 'page_table': page_table}

    xp, xs = x_prompt, x_sample
    outs = [[] for _ in range(10)]
    for l in range(depth):
        cmp_w, cmp_pe = _compress_weights(nsa_cmp_w[l], nsa_cmp_pe[l])
        lp = {'layer': l, 'norm': norm_w[l], 'w_in': w_in_b,
              'gla_gate_w': gla_gate_w[l], 'gla_gate_b': gla_gate_b[l], 'gla_norm': gla_norm[l],
              'nsa_cmp_pe': cmp_pe, 'nsa_cmp_w': cmp_w, 'diff_lambda': diff_lambda[l], 'diff_norm': diff_norm[l],
              'w_out': w_out_b, 'ffn_w_in': ffn_w_in_b, 'ffn_w_out': ffn_w_out_b}
        lam_init = 0.8 - 0.6 * math.exp(-0.3 * l)

        gla0 = jnp.zeros((bp, hk, GLA_DV), F32)
        xp, cmp_p, slc_p, diff_p, win_p, gla_p = _layer(xp, mod_all[l, :bp], 0, None, gla0, lp, lam_init)

        xs, cmp_s, slc_s, diff_s, win_s, gla_s = _layer(xs, mod_all[l, bp:bp + bs], past_len, past_views,
                                                        state_gla[l].reshape(bs, hk, GLA_DV), lp, lam_init)

        kv_shape = lambda b, n: (b, n, 2, NSA_KV_HEADS, HEAD_DIM)
        outs[0].append(cmp_p.reshape(kv_shape(bp, lp_len)))
        outs[1].append(cmp_s.reshape(kv_shape(bs, ls_len)))
        outs[2].append(slc_p.reshape(kv_shape(bp, lp_len)))
        outs[3].append(slc_s.reshape(kv_shape(bs, ls_len)))
        outs[4].append(diff_p.reshape(bp, lp_len, 2, DIFF_HEADS, DIFF_V))
        outs[5].append(diff_s.reshape(bs, ls_len, 2, DIFF_HEADS, DIFF_V))
        win_p5 = win_p.reshape(kv_shape(bp, lp_len))
        outs[6].append(jnp.concatenate([jnp.zeros(kv_shape(bp, wb), F32), win_p5], axis=1)[:, -wb:])
        outs[7].append(jnp.concatenate([state_nsa_win_kv[l], win_s.reshape(kv_shape(bs, ls_len))], axis=1)[:, -wb:])
        outs[8].append(gla_p.reshape(bp, GLA_HEADS, GLA_DK, GLA_DV))
        outs[9].append(gla_s.reshape(bs, GLA_HEADS, GLA_DK, GLA_DV))

    return (xp, xs) + tuple(jnp.stack(o) for o in outs)
```

```python
import functools
import math

import numpy as np
import jax
import jax.numpy as jnp
from jax import lax
from jax.experimental import pallas as pl
from jax.experimental.pallas import tpu as pltpu

F32 = jnp.float32
BF16 = jnp.bfloat16

PAGE_SIZE = 128
HEAD_DIM = 128
GLA_HEADS, GLA_DK, GLA_DV, GLA_GATE_RANK, GLA_TAU = 4, 64, 128, 16, 16.0
NSA_HEADS, NSA_KV_HEADS = 8, 2
NSA_CMP_BLOCK, NSA_CMP_STRIDE, NSA_SEL_BLOCK, NSA_TOP_N, NSA_WINDOW = 32, 16, 64, 16, 512
NSA_FORCE_BONUS = 1.0e4
DIFF_HEADS, DIFF_QK, DIFF_V = 4, 64, 128
ROPE_THETA = 10000.0
NORM_EPS = 1e-6
NEG = -1.0e30

LANES = 128
SUBLANES = 8
BF16_ROWS = 16
VMEM_LIMIT_BYTES = 56 * 1024 * 1024
GLA_CHUNK = 128
PAGES_PER_STEP = 16

Z_NQ, Z_GQ, Z_GK, Z_GV, Z_GR = 0, 1024, 1280, 1536, 2048
Z_CMP, Z_SLC, Z_WIN, Z_DQ, Z_DK, Z_DV, Z_WIDTH = 2560, 3072, 3584, 4096, 4608, 5120, 5632
SMALL_GA, SMALL_GATE = 0, 16


def _cparams(sem):
    return pltpu.CompilerParams(dimension_semantics=sem, vmem_limit_bytes=VMEM_LIMIT_BYTES)


def _dot(a, b):
    return jnp.dot(a, b, preferred_element_type=F32)


def _dot_nt(a, b):
    return lax.dot_general(a, b, (((1,), (1,)), ((), ())), preferred_element_type=F32)


def _dot_tn(a, b):
    return lax.dot_general(a, b, (((0,), (0,)), ((), ())), preferred_element_type=F32)


def _split3(a):
    hi = a.astype(BF16)
    r = a - hi.astype(F32)
    mid = r.astype(BF16)
    lo = (r - mid.astype(F32)).astype(BF16)
    return hi, mid, lo


def _dot3_rhs_exact(a, b):
    hi, mid, lo = _split3(a)
    return _dot(hi, b) + _dot(mid, b) + _dot(lo, b)


def _dot3_lhs_exact(a, b):
    hi, mid, lo = _split3(b)
    return _dot(a, hi) + _dot(a, mid) + _dot(a, lo)


def _sigmoid(x):
    return 1.0 / (1.0 + jnp.exp(-x))


def _masked_softmax(s, mask):
    s = jnp.where(mask, s, NEG)
    m = jnp.max(s, axis=-1, keepdims=True)
    e = jnp.where(mask, jnp.exp(s - m), 0.0)
    den = jnp.sum(e, axis=-1, keepdims=True)
    return e / jnp.where(den > 0, den, 1.0)


def _mod_kernel(c_ref, w_ref, b_ref, o_ref):
    c = c_ref[...]
    a = (c * _sigmoid(c)).astype(BF16)
    o_ref[0] = _dot(a, w_ref[0].astype(BF16)) + b_ref[0]


def _modulation(c_all, ada_w, ada_b):
    depth, d, n6 = ada_w.shape
    rows = c_all.shape[0]
    tn = max(t for t in range(LANES, min(1024, n6) + 1, LANES) if n6 % t == 0)
    return pl.pallas_call(
        _mod_kernel,
        out_shape=jax.ShapeDtypeStruct((depth, rows, n6), F32),
        grid=(depth, n6 // tn),
        in_specs=[pl.BlockSpec((rows, d), lambda l, j: (0, 0)),
                  pl.BlockSpec((1, d, tn), lambda l, j: (l, 0, j)),
                  pl.BlockSpec((1, 1, tn), lambda l, j: (l, 0, j))],
        out_specs=pl.BlockSpec((1, rows, tn), lambda l, j: (l, 0, j)),
        compiler_params=_cparams(("arbitrary", "arbitrary")),
        name="adaln_modulation",
    )(c_all, ada_w, ada_b.reshape(depth, 1, n6))


def _norm_mod(x, nw, sc, sh):
    ms = jnp.mean(x * x, axis=-1, keepdims=True)
    return x * lax.rsqrt(ms + NORM_EPS) * nw * (1.0 + sc) + sh


def _in_proj_kernel(x_ref, nw_ref, sc_ref, sh_ref, w_ref, ws_ref, o_ref, os_ref, h_ref):
    @pl.when(pl.program_id(1) == 0)
    def _():
        h = _norm_mod(x_ref[...], nw_ref[...], sc_ref[0], sh_ref[0]).astype(BF16)
        h_ref[...] = h
        os_ref[...] = _dot(h, ws_ref[0])

    o_ref[...] = _dot(h_ref[...], w_ref[0])


def _mod_spec(seq_len, tm, d):
    if seq_len % tm == 0:
        per = seq_len // tm
        return pl.BlockSpec((1, 1, d), lambda i, *_: (i // per, 0, 0))
    return pl.BlockSpec((1, tm, d), lambda i, *_: (0, i, 0))


def _expand_mod(m, seq_len, tm):
    if seq_len % tm == 0:
        return m[:, None, :]
    return jnp.repeat(m, seq_len, axis=0)[None]


def _in_proj(x2, nw, sc, sh, w, w_small, layer, seq_len, tm, tn):
    m, d = x2.shape
    n = w.shape[2]
    return pl.pallas_call(
        _in_proj_kernel,
        out_shape=[jax.ShapeDtypeStruct((m, n), F32), jax.ShapeDtypeStruct((m, LANES), F32)],
        grid=(m // tm, n // tn),
        in_specs=[pl.BlockSpec((tm, d), lambda i, j: (i, 0)),
                  pl.BlockSpec((1, d), lambda i, j: (0, 0)),
                  _mod_spec(seq_len, tm, d), _mod_spec(seq_len, tm, d),
                  pl.BlockSpec((1, d, tn), lambda i, j: (layer, 0, j)),
                  pl.BlockSpec((1, d, LANES), lambda i, j: (layer, 0, 0))],
        out_specs=[pl.BlockSpec((tm, tn), lambda i, j: (i, j)), pl.BlockSpec((tm, LANES), lambda i, j: (i, 0))],
        scratch_shapes=[pltpu.VMEM((tm, d), BF16)],
        compiler_params=_cparams(("arbitrary", "arbitrary")),
        name="in_proj",
    )(x2, nw, _expand_mod(sc, seq_len, tm), _expand_mod(sh, seq_len, tm), w, w_small)


def _rope_tables(pos):
    pos = np.asarray(pos, np.float64)[:, None]

    def tab(half, reps):
        inv = ROPE_THETA ** (-np.arange(half, dtype=np.float64) / half)
        ang = pos * inv[None, :]
        c, s = np.cos(ang), np.sin(ang)
        return np.tile(np.concatenate([c, c], -1), (1, reps)), np.tile(np.concatenate([-s, s], -1), (1, reps))

    c128, s128 = tab(HEAD_DIM // 2, 1)
    c64, s64 = tab(DIFF_QK // 2, 2)
    return np.concatenate([c128, s128, c64, s64], axis=-1).astype(np.float32)


def _rope128(x, cos, sin):
    return x * cos + pltpu.roll(x, 64, 1) * sin


def _rope64(x, cos, sin, first_half):
    partner = jnp.where(first_half, pltpu.roll(x, 96, 1), pltpu.roll(x, 32, 1))
    return x * cos + partner * sin


def _rope_kernel(nq_ref, cmp_ref, slc_ref, win_ref, dq_ref, dk_ref, dv_ref, tab_ref,
                 qn_o, cmp_o, cmpb_o, slc_o, slcb_o, win_o, winb_o, dq_o, dkv_o, dkvb_o, *vt_outs):
    if vt_outs:
        slc_vt_o, diff_vt_o = vt_outs
        for g in range(NSA_KV_HEADS):
            slc_vt_o[0, g, 0] = slc_ref[:, (NSA_KV_HEADS + g) * LANES:(NSA_KV_HEADS + g + 1) * LANES].T.astype(BF16)
        for h in range(DIFF_HEADS):
            diff_vt_o[0, h, 0] = dv_ref[:, h * LANES:(h + 1) * LANES].T.astype(BF16)
    tab = tab_ref[...]
    rows = tab.shape[0]
    c128, s128, c64, s64 = (tab[:, i * LANES:(i + 1) * LANES] for i in range(4))
    lane = lax.broadcasted_iota(jnp.int32, (1, LANES), 1)
    first_half = (lane % DIFF_QK) < (DIFF_QK // 2)
    nsa_scale = HEAD_DIM ** -0.5
    diff_scale = DIFF_QK ** -0.5

    for h in range(NSA_HEADS):
        sl = slice(h * LANES, (h + 1) * LANES)
        qn_o[:, sl] = (_rope128(nq_ref[:, sl], c128, s128) * nsa_scale).astype(qn_o.dtype)

    for src, dst, dstb in ((cmp_ref, cmp_o, cmpb_o), (slc_ref, slc_o, slcb_o), (win_ref, win_o, winb_o)):
        for g in range(2 * NSA_KV_HEADS):
            sl = slice(g * LANES, (g + 1) * LANES)
            v = src[:, sl]
            if g < NSA_KV_HEADS:
                v = _rope128(v, c128, s128)
            dst[pl.ds(g, rows, stride=2 * NSA_KV_HEADS), :] = v
            dstb[:, sl] = v.astype(BF16)

    for h in range(DIFF_HEADS):
        sl = slice(h * LANES, (h + 1) * LANES)
        dq_o[:, sl] = (_rope64(dq_ref[:, sl], c64, s64, first_half) * diff_scale).astype(dq_o.dtype)
        k = _rope64(dk_ref[:, sl], c64, s64, first_half)
        dkv_o[pl.ds(h, rows, stride=2 * DIFF_HEADS), :] = k
        dkvb_o[:, sl] = k.astype(BF16)
        sv = slice((DIFF_HEADS + h) * LANES, (DIFF_HEADS + h + 1) * LANES)
        v = dv_ref[:, sl]
        dkv_o[pl.ds(DIFF_HEADS + h, rows, stride=2 * DIFF_HEADS), :] = v
        dkvb_o[:, sv] = v.astype(BF16)


def _rope_split(z, tab, seq_len, tm, act_dt, transposed_values):
    m = z.shape[0]
    per = max(seq_len // tm, 1)
    tab_spec = (pl.BlockSpec((tm, 4 * LANES), lambda i: (i % per, 0)) if seq_len % tm == 0
                else pl.BlockSpec((tm, 4 * LANES), lambda i: (i, 0)))

    def zs(width, col):
        return pl.BlockSpec((tm, width), lambda i: (i, col // width))

    def os(width):
        return pl.BlockSpec((tm, width), lambda i: (i, 0))

    kv = (512, F32, True)
    outs = [(1024, act_dt, False), kv, (512, BF16, False), kv, (512, BF16, False), kv, (512, BF16, False),
            (512, act_dt, False), (1024, F32, True), (1024, BF16, False)]
    out_shape = [jax.ShapeDtypeStruct((m * w // LANES, LANES) if il else (m, w), dt) for w, dt, il in outs]
    out_specs = [pl.BlockSpec((tm * w // LANES, LANES), lambda i: (i, 0)) if il else os(w) for w, dt, il in outs]
    if transposed_values:
        assert seq_len % tm == 0
        for heads in (NSA_KV_HEADS, DIFF_HEADS):
            out_shape.append(jax.ShapeDtypeStruct((m // seq_len, heads, per, LANES, tm), BF16))
            out_specs.append(pl.BlockSpec((1, heads, 1, LANES, tm), lambda i: (i // per, 0, i % per, 0, 0)))
    return pl.pallas_call(
        _rope_kernel,
        out_shape=out_shape,
        grid=(m // tm,),
        in_specs=[zs(1024, Z_NQ), zs(512, Z_CMP), zs(512, Z_SLC), zs(512, Z_WIN),
                  zs(512, Z_DQ), zs(512, Z_DK), zs(512, Z_DV), tab_spec],
        out_specs=out_specs,
        compiler_params=_cparams(("arbitrary",)),
        name="rope_split",
    )(z, z, z, z, z, z, z, tab)


def _page_view(cache):
    depth, n_pool, page, two, heads, width = cache.shape
    return cache.reshape(depth, n_pool, page * two * heads, width)


def _page_specs(layer, page_rows, pages_per_step):
    def spec(p):
        return pl.BlockSpec((1, 1, page_rows, LANES),
                            lambda b, j, pt: (layer, pt[b, j * pages_per_step + p], 0, 0))
    return [spec(p) for p in range(pages_per_step)]


def _page_slabs(page_refs, first_slot, n, n_slots):
    return jnp.concatenate(
        [jnp.concatenate([ref[0, 0, pl.ds(first_slot + s, PAGE_SIZE, stride=n_slots), :] for s in range(n)], axis=1)
         for ref in page_refs], axis=0).astype(BF16)


def _token_slabs(ref, first_slot, n, n_slots):
    tokens = ref.shape[0] // n_slots
    return jnp.concatenate([ref[pl.ds(first_slot + s, tokens, stride=n_slots), :] for s in range(n)], axis=1)


def _pad_rows(x, multiple):
    pad = -x.shape[0] % multiple
    return x if pad == 0 else jnp.concatenate([x, jnp.zeros((pad, x.shape[1]), x.dtype)], axis=0)


def _gla_constants(c):
    t = np.arange(c)
    sizes = []
    s = c // 2
    while s >= 1:
        sizes.append(s)
        s //= 2
    sel, masks = [], []
    for sz in sizes:
        ref_row = (t // (2 * sz)) * (2 * sz) + sz - 1
        sel.append((t[None, :] <= ref_row[:, None]).astype(np.float32))
        same = (t[:, None] // (2 * sz)) == (t[None, :] // (2 * sz))
        masks.append((same & ((t[:, None] // sz) % 2 == 1) & ((t[None, :] // sz) % 2 == 0)).astype(np.float32))
    sel.append((t[None, :] <= t[:, None]).astype(np.float32))
    masks.append(np.eye(c, dtype=np.float32))
    return np.concatenate(sel, 0), np.stack(masks, 0)


def _gla_kernel(chunk, levels, q_ref, k_ref, v_ref, r_ref, sm_ref, gw_ref, gb_ref, nw_ref, s0_ref,
                sel_ref, mask_ref, o_ref, st_ref, state_ref):
    c = chunk
    c_in = q_ref.shape[0]
    hk = GLA_HEADS * GLA_DK
    ci = pl.program_id(1)

    @pl.when(ci == 0)
    def _():
        state_ref[...] = s0_ref[0].T

    def rows(x):
        if c_in == c:
            return x
        return jnp.concatenate([x, jnp.zeros((c - c_in, x.shape[1]), x.dtype)], axis=0)

    ga = rows(sm_ref[:, SMALL_GA:SMALL_GA + GLA_GATE_RANK])
    pre = _dot3_lhs_exact_both(ga, gw_ref[...]) + gb_ref[...]
    log_a = (jnp.minimum(pre, 0.0) - jnp.log(1.0 + jnp.exp(-jnp.abs(pre)))) / GLA_TAU
    if c_in != c:
        log_a = jnp.where(lax.broadcasted_iota(jnp.int32, (c, 1), 0) < c_in, log_a, 0.0)
    refs = _dot3_lhs_exact(sel_ref[...], log_a)
    b = refs[(levels - 1) * c:levels * c]
    q = rows(q_ref[...]) * (GLA_DK ** -0.5)
    k = rows(k_ref[...])
    v_all = rows(v_ref[...])
    lane = lax.broadcasted_iota(jnp.int32, (1, hk), 1)
    head_of_lane = lane // GLA_DK

    def stack_heads(x):
        return jnp.concatenate([jnp.where(head_of_lane == h, x, 0.0) for h in range(GLA_HEADS)], axis=0).astype(BF16)

    attn = jnp.zeros((GLA_HEADS, c, c), F32)
    for lv in range(levels):
        r = refs[lv * c:(lv + 1) * c]
        qd = stack_heads(q * jnp.exp(jnp.minimum(b - r, 0.0)))
        kd = (k * jnp.exp(jnp.minimum(r - b, 0.0))).astype(BF16)
        attn = attn + _dot_nt(qd, kd).reshape(GLA_HEADS, c, c) * mask_ref[lv][None]
    state = state_ref[...]
    inter = _dot_nt(stack_heads(q * jnp.exp(b)), state.astype(BF16))

    nw = nw_ref[...]
    for h in range(GLA_HEADS):
        vh = v_all[:, h * GLA_DV:(h + 1) * GLA_DV]
        o = (_dot(attn[h].astype(BF16), vh.astype(BF16)) + inter[h * c:(h + 1) * c])[0:c_in]
        y = o * lax.rsqrt(jnp.mean(o * o, axis=-1, keepdims=True) + NORM_EPS) * nw
        rh = r_ref[:, h * GLA_DV:(h + 1) * GLA_DV]
        o_ref[:, h * GLA_DV:(h + 1) * GLA_DV] = (y * (rh * _sigmoid(rh))).astype(o_ref.dtype)

    b_end = b[c - 1:c]
    kd = (k * jnp.exp(b_end - b)).astype(BF16)
    new_state = state * jnp.exp(b_end)
    for h in range(GLA_HEADS):
        vh = v_all[:, h * GLA_DV:(h + 1) * GLA_DV].astype(BF16)
        new_state = new_state + jnp.where(head_of_lane == h, _dot_tn(vh, kd), 0.0)
    state_ref[...] = new_state

    @pl.when(ci == pl.num_programs(1) - 1)
    def _():
        st_ref[0] = new_state.T


def _dot3_lhs_exact_both(a, b):
    ah, am, al = _split3(a)
    bh, bm, bl = _split3(b)
    return (_dot(ah, bh) + _dot(ah, bm) + _dot(am, bh)) + (_dot(ah, bl) + _dot(am, bm) + _dot(al, bh))


def _gla(z, z_small, gate_w, gate_b, norm_w, s0, batch, seq_len, chunk, act_dt):
    m = z.shape[0]
    nc = seq_len // chunk
    hk, hv = GLA_HEADS * GLA_DK, GLA_HEADS * GLA_DV
    comp_rows = max(chunk, BF16_ROWS)
    sel, masks = _gla_constants(comp_rows)
    levels = masks.shape[0]

    def zs(width, col):
        return pl.BlockSpec((chunk, width), lambda b, c: (b * nc + c, col // width))

    const2 = lambda b, c: (0, 0)
    return pl.pallas_call(
        functools.partial(_gla_kernel, comp_rows, levels),
        out_shape=[jax.ShapeDtypeStruct((m, hv), act_dt), jax.ShapeDtypeStruct((batch, hk, GLA_DV), F32)],
        grid=(batch, nc),
        in_specs=[zs(hk, Z_GQ), zs(hk, Z_GK), zs(hv, Z_GV), zs(hv, Z_GR), zs(LANES, 0),
                  pl.BlockSpec((GLA_GATE_RANK, hk), const2), pl.BlockSpec((1, hk), const2),
                  pl.BlockSpec((1, GLA_DV), const2),
                  pl.BlockSpec((1, hk, GLA_DV), lambda b, c: (b, 0, 0)),
                  pl.BlockSpec((levels * comp_rows, comp_rows), const2),
                  pl.BlockSpec((levels, comp_rows, comp_rows), lambda b, c: (0, 0, 0))],
        out_specs=[pl.BlockSpec((chunk, hv), lambda b, c: (b * nc + c, 0)),
                   pl.BlockSpec((1, hk, GLA_DV), lambda b, c: (b, 0, 0))],
        scratch_shapes=[pltpu.VMEM((GLA_DV, hk), F32)],
        compiler_params=_cparams(("arbitrary", "arbitrary")),
        name="gla",
    )(z, z, z, z, z_small, gate_w, gate_b.reshape(1, hk), norm_w.reshape(1, GLA_DV), s0,
      jnp.asarray(sel, BF16), jnp.asarray(masks, F32))


def _online_update(s, mask, v, m_ref, l_ref, acc_ref):
    if mask is not None:
        s = jnp.where(mask, s, NEG)
    m_prev = m_ref[...]
    m_new = jnp.maximum(m_prev, jnp.max(s, axis=-1, keepdims=True))
    p = jnp.exp(s - m_new)
    if mask is not None:
        p = jnp.where(mask, p, 0.0)
    alpha = jnp.exp(m_prev - m_new)
    l_ref[...] = alpha * l_ref[...] + jnp.sum(p, axis=-1, keepdims=True)
    acc_ref[...] = alpha * acc_ref[...] + _dot(p.astype(BF16), v)
    m_ref[...] = m_new


def _online_update_keymajor(s, mask, vt, m_ref, l_ref, acc_ref):
    if mask is not None:
        s = jnp.where(mask, s, NEG)
    m_prev = m_ref[...]
    m_new = jnp.maximum(m_prev, jnp.max(s, axis=0, keepdims=True))
    p = jnp.exp(s - m_new)
    if mask is not None:
        p = jnp.where(mask, p, 0.0)
    alpha = jnp.exp(m_prev - m_new)
    l_ref[...] = alpha * l_ref[...] + jnp.sum(p, axis=0, keepdims=True)
    acc_ref[...] = alpha * acc_ref[...] + _dot(vt, p.astype(BF16))
    m_ref[...] = m_new


def _diff_kernel(tq, tk, qpos0, lam_init, qi_ref, kj_ref, q_ref, k_ref, vt_ref, lam_ref, nw_ref, o_ref,
                 qq_ref, m_ref, l_ref, acc_ref):
    step_id = pl.program_id(2)
    i, j = qi_ref[step_id], kj_ref[step_id]
    q_lo = qpos0 + i * tq
    last_j = (q_lo + tq - 1) // tk

    @pl.when(j == 0)
    def _():
        q = q_ref[...].astype(F32)
        lane = lax.broadcasted_iota(jnp.int32, (1, LANES), 1)
        qq_ref[...] = jnp.concatenate([jnp.where(lane < DIFF_QK, q, 0.0),
                                       jnp.where(lane >= DIFF_QK, q, 0.0)], axis=0).astype(BF16)
        m_ref[...] = jnp.full_like(m_ref, NEG)
        l_ref[...] = jnp.zeros_like(l_ref)
        acc_ref[...] = jnp.zeros_like(acc_ref)

    def step(masked):
        s = _dot_nt(k_ref[0], qq_ref[...])
        mask = None
        if masked:
            kpos = j * tk + lax.broadcasted_iota(jnp.int32, (tk, 1), 0)
            qpos = q_lo + lax.broadcasted_iota(jnp.int32, (1, tq), 1)
            mk = kpos <= qpos
            mask = jnp.concatenate([mk, mk], axis=1)
        _online_update_keymajor(s, mask, vt_ref[0, 0, 0], m_ref, l_ref, acc_ref)

    fully_visible = (j + 1) * tk - 1 <= q_lo

    @pl.when(fully_visible)
    def _():
        step(False)

    @pl.when(jnp.logical_not(fully_visible))
    def _():
        step(True)

    @pl.when(j == last_j)
    def _():
        l = l_ref[...]
        o12 = (acc_ref[...] / jnp.where(l > 0, l, 1.0)).T
        o_ref[...] = _diff_finish(o12, lam_ref, nw_ref, lam_init).astype(o_ref.dtype)


def _diff_attention(dq, kv, vt, lam_p, norm_w, batch, seq_len, qpos0, lam_init, tq, tk):
    m = dq.shape[0]
    nq = seq_len // tq
    nk = kv.shape[1] // tk
    assert vt.shape[2] == nk and vt.shape[4] == tk

    pairs = [(i, j) for i in range(nq) for j in range(min(nk - 1, (qpos0 + i * tq + tq - 1) // tk) + 1)]
    qi = jnp.asarray([p[0] for p in pairs], jnp.int32)
    kj = jnp.asarray([p[1] for p in pairs], jnp.int32)
    const = lambda b, h, s, qi, kj: (0, 0)
    return pl.pallas_call(
        functools.partial(_diff_kernel, tq, tk, qpos0, lam_init),
        out_shape=jax.ShapeDtypeStruct((m, DIFF_HEADS * DIFF_V), dq.dtype),
        grid_spec=pltpu.PrefetchScalarGridSpec(
            num_scalar_prefetch=2,
            grid=(batch, DIFF_HEADS, len(pairs)),
            in_specs=[pl.BlockSpec((tq, LANES), lambda b, h, s, qi, kj: (b * nq + qi[s], h)),
                      pl.BlockSpec((1, tk, LANES), lambda b, h, s, qi, kj: (b, kj[s], h)),
                      pl.BlockSpec((1, 1, 1, LANES, tk), lambda b, h, s, qi, kj: (b, h, kj[s], 0, 0)),
                      pl.BlockSpec((4, DIFF_QK), const), pl.BlockSpec((1, DIFF_V), const)],
            out_specs=pl.BlockSpec((tq, LANES), lambda b, h, s, qi, kj: (b * nq + qi[s], h)),
            scratch_shapes=[pltpu.VMEM((2 * tq, LANES), BF16), pltpu.VMEM((1, 2 * tq), F32),
                            pltpu.VMEM((1, 2 * tq), F32), pltpu.VMEM((DIFF_V, 2 * tq), F32)]),
        compiler_params=_cparams(("arbitrary",) * 3),
        name="diff_attention",
    )(qi, kj, dq, kv, vt, lam_p, norm_w.reshape(1, DIFF_V))


def _diff_finish(o12, lam_ref, nw_ref, lam_init):
    half = o12.shape[0] // 2
    lam_p = lam_ref[...]
    lam = (jnp.exp(jnp.sum(lam_p[0:1] * lam_p[1:2], axis=-1, keepdims=True))
           - jnp.exp(jnp.sum(lam_p[2:3] * lam_p[3:4], axis=-1, keepdims=True)) + lam_init)
    o = o12[0:half] - lam * o12[half:]
    y = o * lax.rsqrt(jnp.mean(o * o, axis=-1, keepdims=True) + NORM_EPS) * nw_ref[...]
    return y * (1.0 - lam_init)


def _diff_paged_kernel(pps, lam_init, pt_ref, q_ref, new_ref, lam_ref, nw_ref, *rest):
    page_refs, o_ref = rest[:pps], rest[pps]
    qbd_ref, m_ref, l_ref, acc_ref = rest[pps + 1:]
    j = pl.program_id(1)
    n_new = q_ref.shape[0]
    rows_h = 2 * n_new
    hw = DIFF_HEADS * LANES

    @pl.when(j == 0)
    def _():
        q = q_ref[...]
        lane = lax.broadcasted_iota(jnp.int32, (1, LANES), 1)
        blocks = []
        for h in range(DIFF_HEADS):
            qh = q[:, h * LANES:(h + 1) * LANES]
            q12 = jnp.concatenate([jnp.where(lane < DIFF_QK, qh, 0.0), jnp.where(lane >= DIFF_QK, qh, 0.0)], axis=0)
            blocks.append(jnp.concatenate(
                [q12 if hh == h else jnp.zeros((rows_h, LANES), F32) for hh in range(DIFF_HEADS)], axis=1))
        qbd_ref[...] = jnp.concatenate(blocks, axis=0).astype(BF16)
        m_ref[...] = jnp.full_like(m_ref, NEG)
        l_ref[...] = jnp.zeros_like(l_ref)
        acc_ref[...] = jnp.zeros_like(acc_ref)

    k = _page_slabs(page_refs, 0, DIFF_HEADS, 2 * DIFF_HEADS)
    v = _page_slabs(page_refs, DIFF_HEADS, DIFF_HEADS, 2 * DIFF_HEADS)
    _online_update(_dot_nt(qbd_ref[...], k), None, v, m_ref, l_ref, acc_ref)

    @pl.when(j == pl.num_programs(1) - 1)
    def _():
        kn = _pad_rows(_token_slabs(new_ref, 0, DIFF_HEADS, 2 * DIFF_HEADS), BF16_ROWS).astype(BF16)
        vn = _pad_rows(_token_slabs(new_ref, DIFF_HEADS, DIFF_HEADS, 2 * DIFF_HEADS), BF16_ROWS).astype(BF16)
        t = lax.broadcasted_iota(jnp.int32, (DIFF_HEADS * rows_h, 1), 0) % n_new
        u = lax.broadcasted_iota(jnp.int32, (1, kn.shape[0]), 1)
        _online_update(_dot_nt(qbd_ref[...], kn), u <= t, vn, m_ref, l_ref, acc_ref)
        l = l_ref[...]
        o12 = acc_ref[...] / jnp.where(l > 0, l, 1.0)
        for h in range(DIFF_HEADS):
            r = slice(h * rows_h, (h + 1) * rows_h)
            o_ref[:, h * LANES:(h + 1) * LANES] = _diff_finish(o12[r, h * LANES:(h + 1) * LANES], lam_ref, nw_ref,
                                                               lam_init)


def _diff_attention_paged(dq, new_kv, cache_view, layer, page_table, lam_p, norm_w, lam_init, pps):
    b, n_pages = page_table.shape
    n_new = dq.shape[0] // b
    hw = DIFF_HEADS * LANES
    rows = DIFF_HEADS * 2 * n_new
    assert n_pages % pps == 0 and n_new % SUBLANES == 0
    const = lambda i, j, pt: (0, 0)
    return pl.pallas_call(
        functools.partial(_diff_paged_kernel, pps, lam_init),
        out_shape=jax.ShapeDtypeStruct(dq.shape, F32),
        grid_spec=pltpu.PrefetchScalarGridSpec(
            num_scalar_prefetch=1,
            grid=(b, n_pages // pps),
            in_specs=[pl.BlockSpec((n_new, hw), lambda i, j, pt: (i, 0)),
                      pl.BlockSpec((n_new * 2 * DIFF_HEADS, LANES), lambda i, j, pt: (i, 0)),
                      pl.BlockSpec((4, DIFF_QK), const), pl.BlockSpec((1, DIFF_V), const)]
                     + _page_specs(layer, PAGE_SIZE * 2 * DIFF_HEADS, pps),
            out_specs=pl.BlockSpec((n_new, hw), lambda i, j, pt: (i, 0)),
            scratch_shapes=[pltpu.VMEM((rows, hw), BF16), pltpu.VMEM((rows, 1), F32), pltpu.VMEM((rows, 1), F32),
                            pltpu.VMEM((rows, hw), F32)]),
        compiler_params=_cparams(("arbitrary", "arbitrary")),
        name="diff_attention_paged",
    )(page_table, dq, new_kv, lam_p, norm_w.reshape(1, DIFF_V), *([cache_view] * pps))


def _compress_weights(w, pe):
    k = NSA_CMP_STRIDE * HEAD_DIM
    return (jnp.concatenate([w[:, :k], w[:, k:]], axis=2).astype(BF16),
            pe.reshape(2, NSA_CMP_BLOCK // NSA_CMP_STRIDE, k))


def _compress_blocks(segments, rows, w_ref, pe_ref, o_ref):
    for c in range(2):
        pe = _dot(_pad_rows(pe_ref[c], BF16_ROWS).astype(BF16), w_ref[c])
        pe_term = pe[0:1, 0:HEAD_DIM] + pe[1:2, HEAD_DIM:2 * HEAD_DIM]
        for g in range(NSA_KV_HEADS):
            both = _dot(segments(c * NSA_KV_HEADS + g), w_ref[c])
            o_ref[0, c, g] = (both[:, 0:HEAD_DIM] + pltpu.roll(both[:, HEAD_DIM:2 * HEAD_DIM], rows - 1, 0)
                              + pe_term).astype(BF16)


def _compress_kernel(x_ref, w_ref, pe_ref, o_ref):
    tok_w = 2 * NSA_KV_HEADS * HEAD_DIM

    def segments(slot):
        return jnp.concatenate([x_ref[0, :, t * tok_w + slot * HEAD_DIM:t * tok_w + (slot + 1) * HEAD_DIM]
                                for t in range(NSA_CMP_STRIDE)], axis=1)

    _compress_blocks(segments, x_ref.shape[1], w_ref, pe_ref, o_ref)


def _compress_paged_kernel(pps, n_past_seg, pt_ref, new_ref, w_ref, pe_ref, *rest):
    page_refs, o_ref, x_ref = rest[:pps], rest[pps], rest[pps + 1]
    j = pl.program_id(1)
    slots = 2 * NSA_KV_HEADS
    seg_per_page = PAGE_SIZE // NSA_CMP_STRIDE
    rows = x_ref.shape[1]

    for pp in range(0, pps, 2):
        row0 = pl.multiple_of((j * pps + pp) * seg_per_page, 2 * seg_per_page)
        for slot in range(slots):
            for t in range(NSA_CMP_STRIDE):
                pair = [page_refs[pp + q][0, 0, pl.ds(t * slots + slot, seg_per_page, stride=NSA_CMP_STRIDE * slots), :]
                        for q in range(2)]
                x_ref[slot, pl.ds(row0, 2 * seg_per_page), t * HEAD_DIM:(t + 1) * HEAD_DIM] = (
                    jnp.concatenate(pair, axis=0).astype(BF16))

    @pl.when(j == pl.num_programs(1) - 1)
    def _():
        n_new = new_ref.shape[0] // slots
        first = lax.broadcasted_iota(jnp.int32, (rows - n_past_seg, 1), 0) == 0
        for slot in range(slots):
            for t in range(NSA_CMP_STRIDE):
                if t < n_new:
                    tail = jnp.where(first, new_ref[t * slots + slot:t * slots + slot + 1, :], 0.0)
                else:
                    tail = jnp.zeros((rows - n_past_seg, HEAD_DIM), F32)
                x_ref[slot, n_past_seg:rows, t * HEAD_DIM:(t + 1) * HEAD_DIM] = tail.astype(BF16)
        _compress_blocks(lambda slot: x_ref[slot], rows, w_ref, pe_ref, o_ref)


def _compress_paged(cache_view, layer, page_table, new_rows, w, pe, pps):
    b, n_pages = page_table.shape
    slots = 2 * NSA_KV_HEADS
    n_new = new_rows.shape[0] // (b * slots)
    seg_per_page = PAGE_SIZE // NSA_CMP_STRIDE
    n_past_seg = n_pages * seg_per_page
    rows = n_past_seg + 2 * BF16_ROWS
    assert n_new <= NSA_CMP_STRIDE and pps % 2 == 0 and n_pages % pps == 0
    return pl.pallas_call(
        functools.partial(_compress_paged_kernel, pps, n_past_seg),
        out_shape=jax.ShapeDtypeStruct((b, 2, NSA_KV_HEADS, rows, HEAD_DIM), BF16),
        grid_spec=pltpu.PrefetchScalarGridSpec(
            num_scalar_prefetch=1,
            grid=(b, n_pages // pps),
            in_specs=[pl.BlockSpec((n_new * slots, HEAD_DIM), lambda i, j, pt: (i, 0)),
                      pl.BlockSpec(w.shape, lambda i, j, pt: (0, 0, 0)),
                      pl.BlockSpec(pe.shape, lambda i, j, pt: (0, 0, 0))]
                     + _page_specs(layer, PAGE_SIZE * slots, pps),
            out_specs=pl.BlockSpec((1, 2, NSA_KV_HEADS, rows, HEAD_DIM), lambda i, j, pt: (i, 0, 0, 0, 0)),
            scratch_shapes=[pltpu.VMEM((slots, rows, NSA_CMP_STRIDE * HEAD_DIM), BF16)]),
        compiler_params=_cparams(("arbitrary", "arbitrary")),
        name="nsa_compress_paged",
    )(page_table, new_rows, w, pe, *([cache_view] * pps))


def _compress(seg, w, pe):
    b, rows, width = seg.shape
    return pl.pallas_call(
        _compress_kernel,
        out_shape=jax.ShapeDtypeStruct((b, 2, NSA_KV_HEADS, rows, HEAD_DIM), BF16),
        grid=(b,),
        in_specs=[pl.BlockSpec((1, rows, width), lambda i: (i, 0, 0)),
                  pl.BlockSpec(w.shape, lambda i: (0, 0, 0)),
                  pl.BlockSpec(pe.shape, lambda i: (0, 0, 0))],
        out_specs=pl.BlockSpec((1, 2, NSA_KV_HEADS, rows, HEAD_DIM), lambda i: (i, 0, 0, 0, 0)),
        compiler_params=_cparams(("arbitrary",)),
        name="nsa_compress",
    )(seg, w, pe)


def _nsa_band(n_cmp_rows, n_slc_pad):
    ratio = NSA_SEL_BLOCK // NSA_CMP_STRIDE
    span = NSA_CMP_BLOCK // NSA_CMP_STRIDE
    n = np.arange(n_cmp_rows)[:, None]
    j = np.arange(n_slc_pad)[None, :]
    return ((n >= ratio * j - (span - 1)) & (n <= ratio * j + ratio - 1)).astype(np.float32)


def _nsa_compressed(qs, qpos_rows, ck, cv):
    cmp_end = lax.broadcasted_iota(jnp.int32, (1, ck.shape[0]), 1) * NSA_CMP_STRIDE + (NSA_CMP_BLOCK - 1)
    p_c = _masked_softmax(_dot_nt(qs, ck), cmp_end <= qpos_rows)
    return p_c, _dot(p_c.astype(BF16), cv)


def _nsa_block_scores(p_c, qpos, band_ref):
    tq = qpos.shape[0]
    imp = p_c[0:tq]
    for n in range(1, p_c.shape[0] // tq):
        imp = imp + p_c[n * tq:(n + 1) * tq]
    p_slc = _dot3_rhs_exact(_pad_rows(imp, BF16_ROWS), band_ref[...])[0:tq]
    blk = lax.broadcasted_iota(jnp.int32, (1, band_ref.shape[1]), 1)
    cur = qpos // NSA_SEL_BLOCK
    valid = blk * NSA_SEL_BLOCK <= qpos
    forced = (blk == 0) | (blk == cur) | (blk == cur - 1)
    return jnp.where(valid, p_slc + jnp.where(forced, NSA_FORCE_BONUS, 0.0), NEG)


def _nsa_select(score, score_ref, n_slc):
    score_t = score.T
    score_ref[...] = score_t
    blk_t = lax.broadcasted_iota(jnp.int32, (score_t.shape[0], 1), 0)

    def count(jp, cnt):
        row = score_ref[pl.ds(jp, 1), :]
        beats = (row > score_t) | ((row == score_t) & (jp < blk_t))
        return cnt + jnp.where(beats, 1.0, 0.0)

    cnt = lax.fori_loop(0, n_slc, count, jnp.zeros(score_t.shape, F32))
    return jnp.where((cnt < NSA_TOP_N) & (score_t > 0.5 * NEG), 1.0, 0.0)


def _nsa_select_few(score, score_ref, n_slc):
    nq, nsp = score.shape
    fold = LANES // nq
    rows = nsp // fold
    score_t = _pad_rows(score, LANES).T[:, 0:nq]
    score_ref[...] = jnp.concatenate([score_t] * fold, axis=1)
    packed = jnp.concatenate([score_t[g * rows:(g + 1) * rows] for g in range(fold)], axis=1)
    blk = (lax.broadcasted_iota(jnp.int32, (1, LANES), 1) // nq) * rows + lax.broadcasted_iota(jnp.int32, (rows, 1), 0)

    def count(jp, cnt):
        row = score_ref[pl.ds(jp, 1), :]
        beats = (row > packed) | ((row == packed) & (jp < blk))
        return cnt + jnp.where(beats, 1.0, 0.0)

    cnt = lax.fori_loop(0, n_slc, count, jnp.zeros(packed.shape, F32))
    sel_p = jnp.where((cnt < NSA_TOP_N) & (packed > 0.5 * NEG), 1.0, 0.0)
    sel_t = _pad_rows(sel_p, LANES).T
    return jnp.concatenate([sel_t[g * nq:(g + 1) * nq, 0:rows] for g in range(fold)], axis=1)


def _nsa_window(qs, qpos_rows, kw, vw, first_pos):
    wpos = first_pos + lax.broadcasted_iota(jnp.int32, (1, kw.shape[0]), 1)
    mask_w = (wpos <= qpos_rows) & (wpos > qpos_rows - NSA_WINDOW) & (wpos >= 0)
    return _dot(_masked_softmax(_dot_nt(qs, kw), mask_w).astype(BF16), vw)


def _nsa_kernel(tq, tk, qpos0, n_slc, q_ref, sm_ref, ck_ref, cv_ref, sk_ref, svt_ref, wk_ref, wv_ref,
                band_ref, o_ref, score_ref, sel_ref, m_ref, l_ref, acc_ref):
    i = pl.program_id(1)
    hpg = NSA_HEADS // NSA_KV_HEADS
    q_lo = qpos0 + i * tq
    qpos = q_lo + lax.broadcasted_iota(jnp.int32, (tq, 1), 0)
    qpos_lanes = q_lo + lax.broadcasted_iota(jnp.int32, (1, tq), 1)
    qpos_rows = jnp.concatenate([qpos] * hpg, axis=0)
    gates = _sigmoid(sm_ref[:, SMALL_GATE:SMALL_GATE + 3 * NSA_HEADS])
    blocks_per_tile = tk // NSA_SEL_BLOCK

    for g in range(NSA_KV_HEADS):
        q = q_ref[:, g * hpg * LANES:(g + 1) * hpg * LANES]
        qs = jnp.concatenate([q[:, n * LANES:(n + 1) * LANES] for n in range(hpg)], axis=0).astype(BF16)

        p_c, o_c = _nsa_compressed(qs, qpos_rows, ck_ref[0, 0, g], cv_ref[0, 0, g])
        n_rank = jnp.minimum((q_lo + tq - 1) // NSA_SEL_BLOCK + 1, n_slc)
        sel_ref[...] = _nsa_select(_nsa_block_scores(p_c, qpos, band_ref), score_ref, n_rank)

        m_ref[...] = jnp.full_like(m_ref, NEG)
        l_ref[...] = jnp.zeros_like(l_ref)
        acc_ref[...] = jnp.zeros_like(acc_ref)

        def key_tile(kt, carry):
            start = pl.multiple_of(kt * tk, tk)
            k = sk_ref[0, pl.ds(start, tk), g * LANES:(g + 1) * LANES]
            blocks = sel_ref[pl.ds(pl.multiple_of(kt * blocks_per_tile, blocks_per_tile), blocks_per_tile), :]
            picked = jnp.concatenate([jnp.broadcast_to(blocks[r:r + 1], (NSA_SEL_BLOCK, tq))
                                      for r in range(blocks_per_tile)], axis=0) > 0.5
            kpos = kt * tk + lax.broadcasted_iota(jnp.int32, (tk, 1), 0)
            mk = picked & (kpos <= qpos_lanes)
            mask = jnp.concatenate([mk] * hpg, axis=1)
            _online_update_keymajor(_dot_nt(k, qs), mask, svt_ref[0, g, kt], m_ref, l_ref, acc_ref)
            return carry

        lax.fori_loop(0, (q_lo + tq - 1) // tk + 1, key_tile, 0)
        l = l_ref[...]
        o_st = acc_ref[...] / jnp.where(l > 0, l, 1.0)
        o_s = jnp.concatenate([o_st[:, n * tq:(n + 1) * tq].T for n in range(hpg)], axis=0)

        wrows = -(-(tq + NSA_WINDOW) // BF16_ROWS) * BF16_ROWS
        wstart = pl.multiple_of(i * tq, tq)
        o_w = _nsa_window(qs, qpos_rows, wk_ref[0, pl.ds(wstart, wrows), g * LANES:(g + 1) * LANES],
                          wv_ref[0, pl.ds(wstart, wrows), g * LANES:(g + 1) * LANES], q_lo - NSA_WINDOW)

        for n in range(hpg):
            h = g * hpg + n
            r = slice(n * tq, (n + 1) * tq)
            o = (gates[:, 3 * h:3 * h + 1] * o_c[r] + gates[:, 3 * h + 1:3 * h + 2] * o_s[r]
                 + gates[:, 3 * h + 2:3 * h + 3] * o_w[r])
            o_ref[:, h * LANES:(h + 1) * LANES] = o.astype(o_ref.dtype)


def _nsa_attention(qn, z_small, comp, slc_buf, slc_vt, win_all, batch, seq_len, qpos0, tq, tk):
    m = qn.shape[0]
    nq = seq_len // tq
    t_keys = slc_buf.shape[1]
    n_cmp_rows = comp.shape[3]
    n_slc = -(-(qpos0 + seq_len) // NSA_SEL_BLOCK)
    nsp = -(-n_slc // LANES) * LANES
    assert tq % LANES == 0 and tk % NSA_SEL_BLOCK == 0 and slc_vt.shape[2] * tk == t_keys and nsp * NSA_SEL_BLOCK >= t_keys
    band = _nsa_band(n_cmp_rows, nsp)
    hw = NSA_HEADS * HEAD_DIM
    kvw = NSA_KV_HEADS * HEAD_DIM
    rows = (NSA_HEADS // NSA_KV_HEADS) * tq
    wlen = win_all.shape[1]
    return pl.pallas_call(
        functools.partial(_nsa_kernel, tq, tk, qpos0, n_slc),
        out_shape=jax.ShapeDtypeStruct((m, hw), qn.dtype),
        grid=(batch, nq),
        in_specs=[pl.BlockSpec((tq, hw), lambda b, i: (b * nq + i, 0)),
                  pl.BlockSpec((tq, LANES), lambda b, i: (b * nq + i, 0)),
                  pl.BlockSpec((1, 1, NSA_KV_HEADS, n_cmp_rows, HEAD_DIM), lambda b, i: (b, 0, 0, 0, 0)),
                  pl.BlockSpec((1, 1, NSA_KV_HEADS, n_cmp_rows, HEAD_DIM), lambda b, i: (b, 1, 0, 0, 0)),
                  pl.BlockSpec((1, t_keys, kvw), lambda b, i: (b, 0, 0)),
                  pl.BlockSpec((1,) + slc_vt.shape[1:], lambda b, i: (b, 0, 0, 0, 0)),
                  pl.BlockSpec((1, wlen, kvw), lambda b, i: (b, 0, 0)),
                  pl.BlockSpec((1, wlen, kvw), lambda b, i: (b, 0, 1)),
                  pl.BlockSpec((n_cmp_rows, nsp), lambda b, i: (0, 0))],
        out_specs=pl.BlockSpec((tq, hw), lambda b, i: (b * nq + i, 0)),
        scratch_shapes=[pltpu.VMEM((nsp, tq), F32), pltpu.VMEM((nsp, tq), F32), pltpu.VMEM((1, rows), F32),
                        pltpu.VMEM((1, rows), F32), pltpu.VMEM((HEAD_DIM, rows), F32)],
        compiler_params=_cparams(("arbitrary", "arbitrary")),
        name="nsa_attention",
    )(qn, z_small, comp, comp, slc_buf, slc_vt, win_all, win_all, jnp.asarray(band, BF16))


def _nsa_paged_kernel(pps, qpos0, n_slc, pt_ref, q_ref, sm_ref, ck_ref, cv_ref, new_ref, wst_ref, wnew_ref, band_ref,
                      *rest):
    page_refs, o_ref = rest[:pps], rest[pps]
    score_ref, qbd_ref, sel_ref, m_ref, l_ref, acc_ref, oc_ref, ow_ref = rest[pps + 1:]
    j = pl.program_id(1)
    n_new = q_ref.shape[0]
    groups = NSA_KV_HEADS
    hpg = NSA_HEADS // groups
    rows_g = hpg * n_new
    nsp = band_ref.shape[1]
    wb = wst_ref.shape[2] // (2 * groups)
    qpos = qpos0 + lax.broadcasted_iota(jnp.int32, (n_new, 1), 0)

    @pl.when(j == 0)
    def _():
        q = q_ref[...]
        qpos_rows = jnp.concatenate([qpos] * hpg, axis=0)
        scores, blocks = [], []
        for g in range(groups):
            qf = jnp.concatenate([q[:, (g * hpg + n) * LANES:(g * hpg + n + 1) * LANES] for n in range(hpg)], axis=0)
            qs = qf.astype(BF16)
            p_c, o_c = _nsa_compressed(qs, qpos_rows, ck_ref[0, 0, g], cv_ref[0, 0, g])
            oc_ref[g * rows_g:(g + 1) * rows_g] = o_c
            scores.append(_nsa_block_scores(p_c, qpos, band_ref))
            kw, vw = (_pad_rows(jnp.concatenate(
                [wst_ref[0, 0, pl.ds(slot, wb, stride=2 * groups), :], wnew_ref[pl.ds(slot, n_new, stride=2 * groups), :]],
                axis=0), BF16_ROWS).astype(BF16) for slot in (g, groups + g))
            ow_ref[g * rows_g:(g + 1) * rows_g] = _nsa_window(qs, qpos_rows, kw, vw, qpos0 - wb)
            blocks.append(jnp.concatenate(
                [qf if gg == g else jnp.zeros((rows_g, LANES), F32) for gg in range(groups)], axis=1))
        qbd_ref[...] = jnp.concatenate(blocks, axis=0).astype(BF16)
        sel = _nsa_select_few(jnp.concatenate(scores, axis=0), score_ref, n_slc)
        sel_ref[...] = jnp.concatenate(
            [sel[g * n_new:(g + 1) * n_new] for g in range(groups) for _ in range(hpg)], axis=0).astype(BF16)
        m_ref[...] = jnp.full_like(m_ref, NEG)
        l_ref[...] = jnp.zeros_like(l_ref)
        acc_ref[...] = jnp.zeros_like(acc_ref)

    k = _page_slabs(page_refs, 0, groups, 2 * groups)
    v = _page_slabs(page_refs, groups, groups, 2 * groups)
    n_keys = pps * PAGE_SIZE
    key_blk = (j * n_keys + lax.broadcasted_iota(jnp.int32, (1, n_keys), 1)) // NSA_SEL_BLOCK
    expand = jnp.where(lax.broadcasted_iota(jnp.int32, (nsp, 1), 0) == key_blk, 1.0, 0.0).astype(BF16)
    picked = _dot(sel_ref[...], expand) > 0.5
    _online_update(_dot_nt(qbd_ref[...], k), picked, v, m_ref, l_ref, acc_ref)

    @pl.when(j == pl.num_programs(1) - 1)
    def _():
        kn = _pad_rows(_token_slabs(new_ref, 0, groups, 2 * groups), BF16_ROWS).astype(BF16)
        vn = _pad_rows(_token_slabs(new_ref, groups, groups, 2 * groups), BF16_ROWS).astype(BF16)
        new_blk = qpos0 // NSA_SEL_BLOCK
        t = lax.broadcasted_iota(jnp.int32, (groups * rows_g, 1), 0) % n_new
        u = lax.broadcasted_iota(jnp.int32, (1, kn.shape[0]), 1)
        mask = (u <= t) & (sel_ref[:, new_blk:new_blk + 1] > 0.5)
        _online_update(_dot_nt(qbd_ref[...], kn), mask, vn, m_ref, l_ref, acc_ref)
        l = l_ref[...]
        o_sel = acc_ref[...] / jnp.where(l > 0, l, 1.0)
        gates = _sigmoid(sm_ref[:, SMALL_GATE:SMALL_GATE + 3 * NSA_HEADS])
        for g in range(groups):
            for n in range(hpg):
                h = g * hpg + n
                r = slice(g * rows_g + n * n_new, g * rows_g + (n + 1) * n_new)
                o_ref[:, h * LANES:(h + 1) * LANES] = (
                    gates[:, 3 * h:3 * h + 1] * oc_ref[r] + gates[:, 3 * h + 1:3 * h + 2] * o_sel[r, g * LANES:(g + 1) * LANES]
                    + gates[:, 3 * h + 2:3 * h + 3] * ow_ref[r])


def _nsa_attention_paged(qn, z_small, comp, new_kv, win_state, win_new, cache_view, layer, page_table, qpos0, pps):
    b, n_pages = page_table.shape
    n_new = qn.shape[0] // b
    groups = NSA_KV_HEADS
    hw = NSA_HEADS * HEAD_DIM
    gw = groups * HEAD_DIM
    rows = NSA_HEADS * n_new
    n_cmp_rows = comp.shape[3]
    n_slc = -(-(qpos0 + n_new) // NSA_SEL_BLOCK)
    nsp = -(-n_slc // LANES) * LANES
    wrows = win_state.shape[2]
    assert wrows == 2 * groups * min(NSA_WINDOW, qpos0)
    assert n_pages % pps == 0 and qpos0 == n_pages * PAGE_SIZE and n_new <= NSA_SEL_BLOCK
    assert qpos0 % NSA_SEL_BLOCK == 0 and LANES % (groups * n_new) == 0 and n_new % SUBLANES == 0
    assert nsp % (SUBLANES * LANES // (groups * n_new)) == 0
    band = _nsa_band(n_cmp_rows, nsp)
    const = lambda i, j, pt: (0, 0)
    return pl.pallas_call(
        functools.partial(_nsa_paged_kernel, pps, qpos0, n_slc),
        out_shape=jax.ShapeDtypeStruct(qn.shape, F32),
        grid_spec=pltpu.PrefetchScalarGridSpec(
            num_scalar_prefetch=1,
            grid=(b, n_pages // pps),
            in_specs=[pl.BlockSpec((n_new, hw), lambda i, j, pt: (i, 0)),
                      pl.BlockSpec((n_new, LANES), lambda i, j, pt: (i, 0)),
                      pl.BlockSpec((1, 1, groups, n_cmp_rows, HEAD_DIM), lambda i, j, pt: (i, 0, 0, 0, 0)),
                      pl.BlockSpec((1, 1, groups, n_cmp_rows, HEAD_DIM), lambda i, j, pt: (i, 1, 0, 0, 0)),
                      pl.BlockSpec((n_new * 2 * groups, LANES), lambda i, j, pt: (i, 0)),
                      pl.BlockSpec((1, 1, wrows, LANES), lambda i, j, pt: (layer, i, 0, 0)),
                      pl.BlockSpec((n_new * 2 * groups, LANES), lambda i, j, pt: (i, 0)),
                      pl.BlockSpec((n_cmp_rows, nsp), const)]
                     + _page_specs(layer, PAGE_SIZE * 2 * groups, pps),
            out_specs=pl.BlockSpec((n_new, hw), lambda i, j, pt: (i, 0)),
            scratch_shapes=[pltpu.VMEM((nsp, LANES), F32), pltpu.VMEM((rows, gw), BF16), pltpu.VMEM((rows, nsp), BF16),
                            pltpu.VMEM((rows, 1), F32), pltpu.VMEM((rows, 1), F32), pltpu.VMEM((rows, gw), F32),
                            pltpu.VMEM((rows, HEAD_DIM), F32), pltpu.VMEM((rows, HEAD_DIM), F32)]),
        compiler_params=_cparams(("arbitrary", "arbitrary")),
        name="nsa_attention_paged",
    )(page_table, qn, z_small, comp, comp, new_kv, win_state, win_new, jnp.asarray(band, BF16),
      *([cache_view] * pps))


def _out_proj_kernel(x_ref, og_ref, on_ref, od_ref, w_ref, nw_ref, g_ref, o_ref):
    a = jnp.concatenate([og_ref[...].astype(BF16), on_ref[...].astype(BF16), od_ref[...].astype(BF16)], axis=-1)
    mix = _dot(a, w_ref[0])
    y = mix * lax.rsqrt(jnp.mean(mix * mix, axis=-1, keepdims=True) + NORM_EPS) * nw_ref[...]
    o_ref[...] = x_ref[...] + g_ref[0] * y


def _out_proj(x2, o_gla, o_nsa, o_d, w, layer, nw, gate, seq_len, tm):
    m, d = x2.shape
    row = lambda width: pl.BlockSpec((tm, width), lambda i: (i, 0))
    return pl.pallas_call(
        _out_proj_kernel,
        out_shape=jax.ShapeDtypeStruct((m, d), F32),
        grid=(m // tm,),
        in_specs=[row(d), row(o_gla.shape[1]), row(o_nsa.shape[1]), row(o_d.shape[1]),
                  pl.BlockSpec((1,) + w.shape[1:], lambda i: (layer, 0, 0)), pl.BlockSpec((1, d), lambda i: (0, 0)),
                  _mod_spec(seq_len, tm, d)],
        out_specs=row(d),
        compiler_params=_cparams(("arbitrary",)),
        name="out_proj",
    )(x2, o_gla, o_nsa, o_d, w, nw, _expand_mod(gate, seq_len, tm))


def _ffn_kernel(x_ref, nw2_ref, sc_ref, sh_ref, wg_ref, wu_ref, wo_ref, nw3_ref, g_ref, o_ref, h_ref, acc_ref):
    j = pl.program_id(1)

    @pl.when(j == 0)
    def _():
        h_ref[...] = _norm_mod(x_ref[...], nw2_ref[...], sc_ref[0], sh_ref[0]).astype(BF16)
        acc_ref[...] = jnp.zeros_like(acc_ref)

    h = h_ref[...]
    gate = _dot(h, wg_ref[0])
    up = _dot(h, wu_ref[0])
    acc_ref[...] += _dot((gate * _sigmoid(gate) * up).astype(BF16), wo_ref[0])

    @pl.when(j == pl.num_programs(1) - 1)
    def _():
        f = acc_ref[...]
        y = f * lax.rsqrt(jnp.mean(f * f, axis=-1, keepdims=True) + NORM_EPS) * nw3_ref[...]
        o_ref[...] = x_ref[...] + g_ref[0] * y


def _ffn(x2, nw2, sc, sh, w_in, w_out, layer, nw3, gate, seq_len, tm, tf):
    m, d = x2.shape
    d_ff = w_out.shape[1]
    nf = d_ff // tf
    row = pl.BlockSpec((tm, d), lambda i, j: (i, 0))
    vec = pl.BlockSpec((1, d), lambda i, j: (0, 0))
    return pl.pallas_call(
        _ffn_kernel,
        out_shape=jax.ShapeDtypeStruct((m, d), F32),
        grid=(m // tm, nf),
        in_specs=[row, vec, _mod_spec(seq_len, tm, d), _mod_spec(seq_len, tm, d),
                  pl.BlockSpec((1, d, tf), lambda i, j: (layer, 0, j)),
                  pl.BlockSpec((1, d, tf), lambda i, j: (layer, 0, nf + j)),
                  pl.BlockSpec((1, tf, d), lambda i, j: (layer, j, 0)),
                  vec, _mod_spec(seq_len, tm, d)],
        out_specs=row,
        scratch_shapes=[pltpu.VMEM((tm, d), BF16), pltpu.VMEM((tm, d), F32)],
        compiler_params=_cparams(("arbitrary", "arbitrary")),
        name="ffn",
    )(x2, nw2, _expand_mod(sc, seq_len, tm), _expand_mod(sh, seq_len, tm), w_in, w_in, w_out, nw3,
      _expand_mod(gate, seq_len, tm))


def _pick(n, target):
    if n <= target:
        return n
    for t in range(target, 7, -1):
        if n % t == 0 and t % SUBLANES == 0:
            return t
    return n


def _permute_w_in(w):
    depth, d = w.shape[:2]
    hk, hv = GLA_HEADS * GLA_DK, GLA_HEADS * GLA_DV
    kvw = NSA_KV_HEADS * HEAD_DIM
    sizes = [hk, hk, hv, hv, GLA_GATE_RANK, NSA_HEADS * HEAD_DIM, kvw, kvw, kvw, kvw, kvw, kvw, NSA_HEADS * 3,
             DIFF_HEADS * 2 * DIFF_QK, DIFF_HEADS * 2 * DIFF_QK, DIFF_HEADS * DIFF_V]
    offs = np.concatenate([[0], np.cumsum(sizes)])
    part = lambda k: w[:, :, offs[k]:offs[k + 1]]
    main = jnp.concatenate([part(k) for k in (5, 0, 1, 2, 3, 6, 7, 8, 9, 10, 11, 13, 14, 15)], axis=2)
    pad = LANES - GLA_GATE_RANK - NSA_HEADS * 3
    small = jnp.concatenate([part(4), part(12), jnp.zeros((depth, d, pad), w.dtype)], axis=2)
    return main.astype(BF16), small.astype(BF16)


def _layer(x, mod, qpos0, past, gla_s0, lp, lam_init):
    b, seq_len, d = x.shape
    m = b * seq_len
    x2 = x.reshape(m, d)
    sh1, sc1, g1, sh2, sc2, g2 = jnp.split(mod, 6, axis=-1)
    tm = _pick(seq_len, 512) if seq_len >= 128 else m
    tm_in = _pick(seq_len, 1024) if seq_len >= 128 else m
    layer = lp['layer']
    z, z_small = _in_proj(x2, lp['norm'][0:1], sc1, sh1, *lp['w_in'], layer, seq_len, tm_in, _pick(Z_WIDTH, 512))

    tab = _rope_tables(qpos0 + np.arange(seq_len))
    if seq_len % tm != 0:
        tab = np.tile(tab, (m // seq_len, 1))
    tab = jnp.asarray(tab)
    act_dt = BF16 if seq_len % BF16_ROWS == 0 else F32
    (qn, cmp_f, cmp_b, slc_f, slc_b, win_f, win_b, dq, dkv_f, dkv_b, *vts) = _rope_split(
        z, tab, seq_len, tm, act_dt, transposed_values=past is None)

    chunk = GLA_CHUNK if seq_len % GLA_CHUNK == 0 else seq_len
    o_gla, gla_state = _gla(z, z_small, lp['gla_gate_w'], lp['gla_gate_b'], lp['gla_norm'], gla_s0, b, seq_len,
                            chunk, act_dt)

    kvw = 2 * NSA_KV_HEADS * HEAD_DIM
    dw = 2 * DIFF_HEADS * DIFF_V

    if past is None:
        slc_vt, diff_vt = vts
        win_pad = jnp.zeros((b, -(NSA_WINDOW + seq_len) % BF16_ROWS, kvw), BF16)
        win_all = jnp.concatenate([jnp.zeros((b, NSA_WINDOW, kvw), BF16), win_b.reshape(b, seq_len, kvw), win_pad],
                                  axis=1)
        seg = cmp_b.reshape(b, seq_len // NSA_CMP_STRIDE, NSA_CMP_STRIDE * kvw)
        comp = _compress(seg, lp['nsa_cmp_w'], lp['nsa_cmp_pe'])
        o_nsa = _nsa_attention(qn, z_small, comp, slc_b.reshape(b, seq_len, kvw), slc_vt, win_all, b, seq_len, qpos0,
                               _pick(seq_len, 256), tm)
        o_d = _diff_attention(dq, dkv_b.reshape(b, seq_len, dw), diff_vt, lp['diff_lambda'], lp['diff_norm'], b,
                              seq_len, qpos0, lam_init, _pick(seq_len, 512), tm)
    else:
        pt = past['page_table']
        pps = max(p for p in (PAGES_PER_STEP, 8, 4, 2) if pt.shape[1] % p == 0)
        comp = _compress_paged(past['cmp'], layer, pt, cmp_f, lp['nsa_cmp_w'], lp['nsa_cmp_pe'], pps)
        o_nsa = _nsa_attention_paged(qn, z_small, comp, slc_f, past['win'], win_f, past['slc'], layer, pt, qpos0, pps)
        o_d = _diff_attention_paged(dq, dkv_f, past['diff'], layer, pt, lp['diff_lambda'], lp['diff_norm'],
                                    lam_init, pps)

    x1 = _out_proj(x2, o_gla, o_nsa, o_d, lp['w_out'], layer, lp['norm'][1:2], g1, seq_len, tm)
    d_ff = lp['ffn_w_out'].shape[1]
    x2n = _ffn(x1, lp['norm'][2:3], sc2, sh2, lp['ffn_w_in'], lp['ffn_w_out'], layer, lp['norm'][3:4], g2, seq_len,
               tm, _pick(d_ff, 512) if d_ff % LANES == 0 else d_ff)

    return (x2n.reshape(b, seq_len, d), cmp_f, slc_f, dkv_f, win_f, gla_state)


def kernel(x_prompt, x_sample, c_prompt, c_sample, cache_nsa_cmp_kv, cache_nsa_slc_kv, cache_diff_kv,
           state_nsa_win_kv, state_gla, page_table, ada_w, ada_b, norm_w, w_in, gla_gate_w, gla_gate_b,
           gla_norm, nsa_cmp_pe, nsa_cmp_w, diff_lambda, diff_norm, w_out, ffn_w_in, ffn_w_out):
    depth = ada_w.shape[0]
    bp, lp_len, d = x_prompt.shape
    bs, ls_len, _ = x_sample.shape
    n_pages = page_table.shape[1]
    past_len = n_pages * PAGE_SIZE
    wb = state_nsa_win_kv.shape[2]
    hk = GLA_HEADS * GLA_DK

    n_c = bp + bs
    rows = -(-n_c // SUBLANES) * SUBLANES
    c_all = jnp.concatenate([c_prompt, c_sample, jnp.zeros((rows - n_c, d), F32)], axis=0)
    mod_all = _modulation(c_all, ada_w, ada_b)

    w_in_b, w_out_b = _permute_w_in(w_in), w_out.astype(BF16)
    ffn_w_in_b, ffn_w_out_b = ffn_w_in.astype(BF16), ffn_w_out.astype(BF16)
    past_views = {'cmp': _page_view(cache_nsa_cmp_kv), 'slc': _page_view(cache_nsa_slc_kv),
                  'diff': _page_view(cache_diff_kv), 'win': _page_view(state_nsa_win_kv), 'page_table': page_table}

    xp, xs = x_prompt, x_sample
    outs = [[] for _ in range(10)]
    for l in range(depth):
        cmp_w, cmp_pe = _compress_weights(nsa_cmp_w[l], nsa_cmp_pe[l])
        lp = {'layer': l, 'norm': norm_w[l], 'w_in': w_in_b,
              'gla_gate_w': gla_gate_w[l], 'gla_gate_b': gla_gate_b[l], 'gla_norm': gla_norm[l],
              'nsa_cmp_pe': cmp_pe, 'nsa_cmp_w': cmp_w, 'diff_lambda': diff_lambda[l], 'diff_norm': diff_norm[l],
              'w_out': w_out_b, 'ffn_w_in': ffn_w_in_b, 'ffn_w_out': ffn_w_out_b}
        lam_init = 0.8 - 0.6 * math.exp(-0.3 * l)

        gla0 = jnp.zeros((bp, hk, GLA_DV), F32)
        xp, cmp_p, slc_p, diff_p, win_p, gla_p = _layer(xp, mod_all[l, :bp], 0, None, gla0, lp, lam_init)

        xs, cmp_s, slc_s, diff_s, win_s, gla_s = _layer(xs, mod_all[l, bp:bp + bs], past_len, past_views,
                                                        state_gla[l].reshape(bs, hk, GLA_DV), lp, lam_init)

        kv_shape = lambda b, n: (b, n, 2, NSA_KV_HEADS, HEAD_DIM)
        outs[0].append(cmp_p.reshape(kv_shape(bp, lp_len)))
        outs[1].append(cmp_s.reshape(kv_shape(bs, ls_len)))
        outs[2].append(slc_p.reshape(kv_shape(bp, lp_len)))
        outs[3].append(slc_s.reshape(kv_shape(bs, ls_len)))
        outs[4].append(diff_p.reshape(bp, lp_len, 2, DIFF_HEADS, DIFF_V))
        outs[5].append(diff_s.reshape(bs, ls_len, 2, DIFF_HEADS, DIFF_V))
        win_p5 = win_p.reshape(kv_shape(bp, lp_len))
        outs[6].append(win_p5[:, lp_len - wb:] if lp_len >= wb else
                       jnp.concatenate([jnp.zeros(kv_shape(bp, wb - lp_len), F32), win_p5], axis=1))
        outs[7].append(win_s.reshape(kv_shape(bs, ls_len)))
        outs[8].append(gla_p.reshape(bp, GLA_HEADS, GLA_DK, GLA_DV))
        outs[9].append(gla_s.reshape(bs, GLA_HEADS, GLA_DK, GLA_DV))

    stacked = [jnp.stack(o) for o in outs]
    stacked[7] = jnp.concatenate([state_nsa_win_kv, stacked[7]], axis=2)[:, :, -wb:]
    return (xp, xs) + tuple(stacked)
```

```python
import functools
import math

import numpy as np
import jax
import jax.numpy as jnp
from jax import lax
from jax.experimental import pallas as pl
from jax.experimental.pallas import tpu as pltpu

F32 = jnp.float32
BF16 = jnp.bfloat16

PAGE_SIZE = 128
HEAD_DIM = 128
GLA_HEADS, GLA_DK, GLA_DV, GLA_GATE_RANK, GLA_TAU = 4, 64, 128, 16, 16.0
NSA_HEADS, NSA_KV_HEADS = 8, 2
NSA_CMP_BLOCK, NSA_CMP_STRIDE, NSA_SEL_BLOCK, NSA_TOP_N, NSA_WINDOW = 32, 16, 64, 16, 512
NSA_FORCE_BONUS = 1.0e4
DIFF_HEADS, DIFF_QK, DIFF_V = 4, 64, 128
ROPE_THETA = 10000.0
NORM_EPS = 1e-6
NEG = -1.0e30

LANES = 128
SUBLANES = 8
BF16_ROWS = 16
VMEM_LIMIT_BYTES = 56 * 1024 * 1024
GLA_CHUNK = 128
PAGES_PER_STEP = 16

Z_NQ, Z_GQ, Z_GK, Z_GV, Z_GR = 0, 1024, 1280, 1536, 2048
Z_CMP, Z_SLC, Z_WIN, Z_DQ, Z_DK, Z_DV, Z_WIDTH = 2560, 3072, 3584, 4096, 4608, 5120, 5632
SMALL_GA, SMALL_GATE = 0, 16


def _cparams(sem):
    return pltpu.CompilerParams(dimension_semantics=sem, vmem_limit_bytes=VMEM_LIMIT_BYTES)


def _dot(a, b):
    return jnp.dot(a, b, preferred_element_type=F32)


def _dot_nt(a, b):
    return lax.dot_general(a, b, (((1,), (1,)), ((), ())), preferred_element_type=F32)


def _dot_tn(a, b):
    return lax.dot_general(a, b, (((0,), (0,)), ((), ())), preferred_element_type=F32)


def _split3(a):
    hi = a.astype(BF16)
    r = a - hi.astype(F32)
    mid = r.astype(BF16)
    lo = (r - mid.astype(F32)).astype(BF16)
    return hi, mid, lo


def _dot3_rhs_exact(a, b):
    hi, mid, lo = _split3(a)
    return _dot(hi, b) + _dot(mid, b) + _dot(lo, b)


def _dot3_lhs_exact(a, b):
    hi, mid, lo = _split3(b)
    return _dot(a, hi) + _dot(a, mid) + _dot(a, lo)


def _sigmoid(x):
    return 1.0 / (1.0 + jnp.exp(-x))


def _masked_softmax(s, mask):
    s = jnp.where(mask, s, NEG)
    m = jnp.max(s, axis=-1, keepdims=True)
    e = jnp.where(mask, jnp.exp(s - m), 0.0)
    den = jnp.sum(e, axis=-1, keepdims=True)
    return e / jnp.where(den > 0, den, 1.0)


def _mod_kernel(c_ref, w_ref, b_ref, o_ref):
    c = c_ref[...]
    a = (c * _sigmoid(c)).astype(BF16)
    o_ref[0] = _dot(a, w_ref[0].astype(BF16)) + b_ref[0]


def _modulation(c_all, ada_w, ada_b):
    depth, d, n6 = ada_w.shape
    rows = c_all.shape[0]
    tn = max(t for t in range(LANES, min(1024, n6) + 1, LANES) if n6 % t == 0)
    return pl.pallas_call(
        _mod_kernel,
        out_shape=jax.ShapeDtypeStruct((depth, rows, n6), F32),
        grid=(depth, n6 // tn),
        in_specs=[pl.BlockSpec((rows, d), lambda l, j: (0, 0)),
                  pl.BlockSpec((1, d, tn), lambda l, j: (l, 0, j)),
                  pl.BlockSpec((1, 1, tn), lambda l, j: (l, 0, j))],
        out_specs=pl.BlockSpec((1, rows, tn), lambda l, j: (l, 0, j)),
        compiler_params=_cparams(("arbitrary", "arbitrary")),
        name="adaln_modulation",
    )(c_all, ada_w, ada_b.reshape(depth, 1, n6))


def _norm_mod(x, nw, sc, sh):
    ms = jnp.mean(x * x, axis=-1, keepdims=True)
    return x * lax.rsqrt(ms + NORM_EPS) * nw * (1.0 + sc) + sh


def _in_proj_kernel(x_ref, nw_ref, sc_ref, sh_ref, w_ref, ws_ref, o_ref, os_ref, h_ref):
    @pl.when(pl.program_id(1) == 0)
    def _():
        h = _norm_mod(x_ref[...], nw_ref[...], sc_ref[0], sh_ref[0]).astype(BF16)
        h_ref[...] = h
        os_ref[...] = _dot(h, ws_ref[0])

    o_ref[...] = _dot(h_ref[...], w_ref[0])


def _mod_spec(seq_len, tm, d):
    if seq_len % tm == 0:
        per = seq_len // tm
        return pl.BlockSpec((1, 1, d), lambda i, *_: (i // per, 0, 0))
    return pl.BlockSpec((1, tm, d), lambda i, *_: (0, i, 0))


def _expand_mod(m, seq_len, tm):
    if seq_len % tm == 0:
        return m[:, None, :]
    return jnp.repeat(m, seq_len, axis=0)[None]


def _in_proj(x2, nw, sc, sh, w, w_small, layer, seq_len, tm, tn):
    m, d = x2.shape
    n = w.shape[2]
    return pl.pallas_call(
        _in_proj_kernel,
        out_shape=[jax.ShapeDtypeStruct((m, n), F32), jax.ShapeDtypeStruct((m, LANES), F32)],
        grid=(m // tm, n // tn),
        in_specs=[pl.BlockSpec((tm, d), lambda i, j: (i, 0)),
                  pl.BlockSpec((1, d), lambda i, j: (0, 0)),
                  _mod_spec(seq_len, tm, d), _mod_spec(seq_len, tm, d),
                  pl.BlockSpec((1, d, tn), lambda i, j: (layer, 0, j)),
                  pl.BlockSpec((1, d, LANES), lambda i, j: (layer, 0, 0))],
        out_specs=[pl.BlockSpec((tm, tn), lambda i, j: (i, j)), pl.BlockSpec((tm, LANES), lambda i, j: (i, 0))],
        scratch_shapes=[pltpu.VMEM((tm, d), BF16)],
        compiler_params=_cparams(("arbitrary", "arbitrary")),
        name="in_proj",
    )(x2, nw, _expand_mod(sc, seq_len, tm), _expand_mod(sh, seq_len, tm), w, w_small)


def _rope_tables(pos):
    pos = np.asarray(pos, np.float64)[:, None]

    def tab(half, reps):
        inv = ROPE_THETA ** (-np.arange(half, dtype=np.float64) / half)
        ang = pos * inv[None, :]
        c, s = np.cos(ang), np.sin(ang)
        return np.tile(np.concatenate([c, c], -1), (1, reps)), np.tile(np.concatenate([-s, s], -1), (1, reps))

    c128, s128 = tab(HEAD_DIM // 2, 1)
    c64, s64 = tab(DIFF_QK // 2, 2)
    return np.concatenate([c128, s128, c64, s64], axis=-1).astype(np.float32)


def _rope128(x, cos, sin):
    return x * cos + pltpu.roll(x, 64, 1) * sin


def _rope64(x, cos, sin, first_half):
    partner = jnp.where(first_half, pltpu.roll(x, 96, 1), pltpu.roll(x, 32, 1))
    return x * cos + partner * sin


def _rope_kernel(nq_ref, cmp_ref, slc_ref, win_ref, dq_ref, dk_ref, dv_ref, tab_ref,
                 qn_o, cmp_o, cmpb_o, slc_o, slcb_o, win_o, winb_o, dq_o, dkv_o, dkvb_o, *vt_outs):
    if vt_outs:
        slc_vt_o, diff_vt_o = vt_outs
        for g in range(NSA_KV_HEADS):
            slc_vt_o[0, g, 0] = slc_ref[:, (NSA_KV_HEADS + g) * LANES:(NSA_KV_HEADS + g + 1) * LANES].T.astype(BF16)
        for h in range(DIFF_HEADS):
            diff_vt_o[0, h, 0] = dv_ref[:, h * LANES:(h + 1) * LANES].T.astype(BF16)
    tab = tab_ref[...]
    rows = tab.shape[0]
    c128, s128, c64, s64 = (tab[:, i * LANES:(i + 1) * LANES] for i in range(4))
    lane = lax.broadcasted_iota(jnp.int32, (1, LANES), 1)
    first_half = (lane % DIFF_QK) < (DIFF_QK // 2)
    nsa_scale = HEAD_DIM ** -0.5
    diff_scale = DIFF_QK ** -0.5

    for h in range(NSA_HEADS):
        sl = slice(h * LANES, (h + 1) * LANES)
        qn_o[:, sl] = (_rope128(nq_ref[:, sl], c128, s128) * nsa_scale).astype(qn_o.dtype)

    for src, dst, dstb in ((cmp_ref, cmp_o, cmpb_o), (slc_ref, slc_o, slcb_o), (win_ref, win_o, winb_o)):
        for g in range(2 * NSA_KV_HEADS):
            sl = slice(g * LANES, (g + 1) * LANES)
            v = src[:, sl]
            if g < NSA_KV_HEADS:
                v = _rope128(v, c128, s128)
            dst[pl.ds(g, rows, stride=2 * NSA_KV_HEADS), :] = v
            dstb[:, sl] = v.astype(BF16)

    for h in range(DIFF_HEADS):
        sl = slice(h * LANES, (h + 1) * LANES)
        dq_o[:, sl] = (_rope64(dq_ref[:, sl], c64, s64, first_half) * diff_scale).astype(dq_o.dtype)
        k = _rope64(dk_ref[:, sl], c64, s64, first_half)
        dkv_o[pl.ds(h, rows, stride=2 * DIFF_HEADS), :] = k
        dkvb_o[:, sl] = k.astype(BF16)
        sv = slice((DIFF_HEADS + h) * LANES, (DIFF_HEADS + h + 1) * LANES)
        v = dv_ref[:, sl]
        dkv_o[pl.ds(DIFF_HEADS + h, rows, stride=2 * DIFF_HEADS), :] = v
        dkvb_o[:, sv] = v.astype(BF16)


def _rope_split(z, tab, seq_len, tm, act_dt, transposed_values):
    m = z.shape[0]
    per = max(seq_len // tm, 1)
    tab_spec = (pl.BlockSpec((tm, 4 * LANES), lambda i: (i % per, 0)) if seq_len % tm == 0
                else pl.BlockSpec((tm, 4 * LANES), lambda i: (i, 0)))

    def zs(width, col):
        return pl.BlockSpec((tm, width), lambda i: (i, col // width))

    def os(width):
        return pl.BlockSpec((tm, width), lambda i: (i, 0))

    kv = (512, F32, True)
    outs = [(1024, act_dt, False), kv, (512, BF16, False), kv, (512, BF16, False), kv, (512, BF16, False),
            (512, act_dt, False), (1024, F32, True), (1024, BF16, False)]
    out_shape = [jax.ShapeDtypeStruct((m * w // LANES, LANES) if il else (m, w), dt) for w, dt, il in outs]
    out_specs = [pl.BlockSpec((tm * w // LANES, LANES), lambda i: (i, 0)) if il else os(w) for w, dt, il in outs]
    if transposed_values:
        assert seq_len % tm == 0
        for heads in (NSA_KV_HEADS, DIFF_HEADS):
            out_shape.append(jax.ShapeDtypeStruct((m // seq_len, heads, per, LANES, tm), BF16))
            out_specs.append(pl.BlockSpec((1, heads, 1, LANES, tm), lambda i: (i // per, 0, i % per, 0, 0)))
    return pl.pallas_call(
        _rope_kernel,
        out_shape=out_shape,
        grid=(m // tm,),
        in_specs=[zs(1024, Z_NQ), zs(512, Z_CMP), zs(512, Z_SLC), zs(512, Z_WIN),
                  zs(512, Z_DQ), zs(512, Z_DK), zs(512, Z_DV), tab_spec],
        out_specs=out_specs,
        compiler_params=_cparams(("arbitrary",)),
        name="rope_split",
    )(z, z, z, z, z, z, z, tab)


def _page_view(cache):
    depth, n_pool, page, two, heads, width = cache.shape
    return cache.reshape(depth, n_pool, page * two * heads, width)


def _page_specs(layer, page_rows, pages_per_step):
    def spec(p):
        return pl.BlockSpec((1, 1, page_rows, LANES),
                            lambda b, j, pt: (layer, pt[b, j * pages_per_step + p], 0, 0))
    return [spec(p) for p in range(pages_per_step)]


def _page_slabs(page_refs, first_slot, n, n_slots):
    return jnp.concatenate(
        [jnp.concatenate([ref[0, 0, pl.ds(first_slot + s, PAGE_SIZE, stride=n_slots), :] for s in range(n)], axis=1)
         for ref in page_refs], axis=0).astype(BF16)


def _token_slabs(ref, first_slot, n, n_slots):
    tokens = ref.shape[0] // n_slots
    return jnp.concatenate([ref[pl.ds(first_slot + s, tokens, stride=n_slots), :] for s in range(n)], axis=1)


def _pad_rows(x, multiple):
    pad = -x.shape[0] % multiple
    return x if pad == 0 else jnp.concatenate([x, jnp.zeros((pad, x.shape[1]), x.dtype)], axis=0)


def _gla_constants(c):
    t = np.arange(c)
    sizes = []
    s = c // 2
    while s >= 1:
        sizes.append(s)
        s //= 2
    sel, masks = [], []
    for sz in sizes:
        ref_row = (t // (2 * sz)) * (2 * sz) + sz - 1
        sel.append((t[None, :] <= ref_row[:, None]).astype(np.float32))
        same = (t[:, None] // (2 * sz)) == (t[None, :] // (2 * sz))
        masks.append((same & ((t[:, None] // sz) % 2 == 1) & ((t[None, :] // sz) % 2 == 0)).astype(np.float32))
    sel.append((t[None, :] <= t[:, None]).astype(np.float32))
    masks.append(np.eye(c, dtype=np.float32))
    return np.concatenate(sel, 0), np.stack(masks, 0)


def _gla_kernel(chunk, levels, q_ref, k_ref, v_ref, r_ref, sm_ref, gw_ref, gb_ref, nw_ref, s0_ref,
                sel_ref, mask_ref, o_ref, st_ref, state_ref):
    c = chunk
    c_in = q_ref.shape[0]
    hk = GLA_HEADS * GLA_DK
    ci = pl.program_id(1)

    @pl.when(ci == 0)
    def _():
        state_ref[...] = s0_ref[0].T

    def rows(x):
        if c_in == c:
            return x
        return jnp.concatenate([x, jnp.zeros((c - c_in, x.shape[1]), x.dtype)], axis=0)

    ga = rows(sm_ref[:, SMALL_GA:SMALL_GA + GLA_GATE_RANK])
    pre = _dot3_lhs_exact_both(ga, gw_ref[...]) + gb_ref[...]
    log_a = (jnp.minimum(pre, 0.0) - jnp.log(1.0 + jnp.exp(-jnp.abs(pre)))) / GLA_TAU
    if c_in != c:
        log_a = jnp.where(lax.broadcasted_iota(jnp.int32, (c, 1), 0) < c_in, log_a, 0.0)
    refs = _dot3_lhs_exact(sel_ref[...], log_a)
    b = refs[(levels - 1) * c:levels * c]
    q = rows(q_ref[...]) * (GLA_DK ** -0.5)
    k = rows(k_ref[...])
    v_all = rows(v_ref[...])
    lane = lax.broadcasted_iota(jnp.int32, (1, hk), 1)
    head_of_lane = lane // GLA_DK

    def stack_heads(x):
        return jnp.concatenate([jnp.where(head_of_lane == h, x, 0.0) for h in range(GLA_HEADS)], axis=0).astype(BF16)

    attn = jnp.zeros((GLA_HEADS, c, c), F32)
    for lv in range(levels):
        r = refs[lv * c:(lv + 1) * c]
        qd = stack_heads(q * jnp.exp(jnp.minimum(b - r, 0.0)))
        kd = (k * jnp.exp(jnp.minimum(r - b, 0.0))).astype(BF16)
        attn = attn + _dot_nt(qd, kd).reshape(GLA_HEADS, c, c) * mask_ref[lv][None]
    state = state_ref[...]
    inter = _dot_nt(stack_heads(q * jnp.exp(b)), state.astype(BF16))

    nw = nw_ref[...]
    for h in range(GLA_HEADS):
        vh = v_all[:, h * GLA_DV:(h + 1) * GLA_DV]
        o = (_dot(attn[h].astype(BF16), vh.astype(BF16)) + inter[h * c:(h + 1) * c])[0:c_in]
        y = o * lax.rsqrt(jnp.mean(o * o, axis=-1, keepdims=True) + NORM_EPS) * nw
        rh = r_ref[:, h * GLA_DV:(h + 1) * GLA_DV]
        o_ref[:, h * GLA_DV:(h + 1) * GLA_DV] = (y * (rh * _sigmoid(rh))).astype(o_ref.dtype)

    b_end = b[c - 1:c]
    kd = (k * jnp.exp(b_end - b)).astype(BF16)
    new_state = state * jnp.exp(b_end)
    for h in range(GLA_HEADS):
        vh = v_all[:, h * GLA_DV:(h + 1) * GLA_DV].astype(BF16)
        new_state = new_state + jnp.where(head_of_lane == h, _dot_tn(vh, kd), 0.0)
    state_ref[...] = new_state

    @pl.when(ci == pl.num_programs(1) - 1)
    def _():
        st_ref[0] = new_state.T


def _dot3_lhs_exact_both(a, b):
    ah, am, al = _split3(a)
    bh, bm, bl = _split3(b)
    return (_dot(ah, bh) + _dot(ah, bm) + _dot(am, bh)) + (_dot(ah, bl) + _dot(am, bm) + _dot(al, bh))


def _gla(z, z_small, gate_w, gate_b, norm_w, s0, batch, seq_len, chunk, act_dt):
    m = z.shape[0]
    nc = seq_len // chunk
    hk, hv = GLA_HEADS * GLA_DK, GLA_HEADS * GLA_DV
    comp_rows = max(chunk, BF16_ROWS)
    sel, masks = _gla_constants(comp_rows)
    levels = masks.shape[0]

    def zs(width, col):
        return pl.BlockSpec((chunk, width), lambda b, c: (b * nc + c, col // width))

    const2 = lambda b, c: (0, 0)
    return pl.pallas_call(
        functools.partial(_gla_kernel, comp_rows, levels),
        out_shape=[jax.ShapeDtypeStruct((m, hv), act_dt), jax.ShapeDtypeStruct((batch, hk, GLA_DV), F32)],
        grid=(batch, nc),
        in_specs=[zs(hk, Z_GQ), zs(hk, Z_GK), zs(hv, Z_GV), zs(hv, Z_GR), zs(LANES, 0),
                  pl.BlockSpec((GLA_GATE_RANK, hk), const2), pl.BlockSpec((1, hk), const2),
                  pl.BlockSpec((1, GLA_DV), const2),
                  pl.BlockSpec((1, hk, GLA_DV), lambda b, c: (b, 0, 0)),
                  pl.BlockSpec((levels * comp_rows, comp_rows), const2),
                  pl.BlockSpec((levels, comp_rows, comp_rows), lambda b, c: (0, 0, 0))],
        out_specs=[pl.BlockSpec((chunk, hv), lambda b, c: (b * nc + c, 0)),
                   pl.BlockSpec((1, hk, GLA_DV), lambda b, c: (b, 0, 0))],
        scratch_shapes=[pltpu.VMEM((GLA_DV, hk), F32)],
        compiler_params=_cparams(("arbitrary", "arbitrary")),
        name="gla",
    )(z, z, z, z, z_small, gate_w, gate_b.reshape(1, hk), norm_w.reshape(1, GLA_DV), s0,
      jnp.asarray(sel, BF16), jnp.asarray(masks, F32))


def _online_update(s, mask, v, m_ref, l_ref, acc_ref):
    if mask is not None:
        s = jnp.where(mask, s, NEG)
    m_prev = m_ref[...]
    m_new = jnp.maximum(m_prev, jnp.max(s, axis=-1, keepdims=True))
    p = jnp.exp(s - m_new)
    if mask is not None:
        p = jnp.where(mask, p, 0.0)
    alpha = jnp.exp(m_prev - m_new)
    l_ref[...] = alpha * l_ref[...] + jnp.sum(p, axis=-1, keepdims=True)
    acc_ref[...] = alpha * acc_ref[...] + _dot(p.astype(BF16), v)
    m_ref[...] = m_new


def _online_update_keymajor(s, mask, vt, m_ref, l_ref, acc_ref):
    if mask is not None:
        s = jnp.where(mask, s, NEG)
    m_prev = m_ref[...]
    m_new = jnp.maximum(m_prev, jnp.max(s, axis=0, keepdims=True))
    p = jnp.exp(s - m_new)
    if mask is not None:
        p = jnp.where(mask, p, 0.0)
    alpha = jnp.exp(m_prev - m_new)
    l_ref[...] = alpha * l_ref[...] + jnp.sum(p, axis=0, keepdims=True)
    acc_ref[...] = alpha * acc_ref[...] + _dot(vt, p.astype(BF16))
    m_ref[...] = m_new


def _diff_kernel(tq, tk, qpos0, lam_init, qi_ref, kj_ref, q_ref, k_ref, vt_ref, lam_ref, nw_ref, o_ref,
                 qq_ref, m_ref, l_ref, acc_ref):
    step_id = pl.program_id(2)
    i, j = qi_ref[step_id], kj_ref[step_id]
    q_lo = qpos0 + i * tq
    last_j = (q_lo + tq - 1) // tk

    @pl.when(j == 0)
    def _():
        q = q_ref[...].astype(F32)
        lane = lax.broadcasted_iota(jnp.int32, (1, LANES), 1)
        qq_ref[...] = jnp.concatenate([jnp.where(lane < DIFF_QK, q, 0.0),
                                       jnp.where(lane >= DIFF_QK, q, 0.0)], axis=0).astype(BF16)
        m_ref[...] = jnp.full_like(m_ref, NEG)
        l_ref[...] = jnp.zeros_like(l_ref)
        acc_ref[...] = jnp.zeros_like(acc_ref)

    def step(masked):
        s = _dot_nt(k_ref[0], qq_ref[...])
        mask = None
        if masked:
            kpos = j * tk + lax.broadcasted_iota(jnp.int32, (tk, 1), 0)
            qpos = q_lo + lax.broadcasted_iota(jnp.int32, (1, tq), 1)
            mk = kpos <= qpos
            mask = jnp.concatenate([mk, mk], axis=1)
        _online_update_keymajor(s, mask, vt_ref[0, 0, 0], m_ref, l_ref, acc_ref)

    fully_visible = (j + 1) * tk - 1 <= q_lo

    @pl.when(fully_visible)
    def _():
        step(False)

    @pl.when(jnp.logical_not(fully_visible))
    def _():
        step(True)

    @pl.when(j == last_j)
    def _():
        l = l_ref[...]
        o12 = (acc_ref[...] / jnp.where(l > 0, l, 1.0)).T
        o_ref[...] = _diff_finish(o12, lam_ref, nw_ref, lam_init).astype(o_ref.dtype)


def _diff_attention(dq, kv, vt, lam_p, norm_w, batch, seq_len, qpos0, lam_init, tq, tk):
    m = dq.shape[0]
    nq = seq_len // tq
    nk = kv.shape[1] // tk
    assert vt.shape[2] == nk and vt.shape[4] == tk

    pairs = [(i, j) for i in range(nq) for j in range(min(nk - 1, (qpos0 + i * tq + tq - 1) // tk) + 1)]
    qi = jnp.asarray([p[0] for p in pairs], jnp.int32)
    kj = jnp.asarray([p[1] for p in pairs], jnp.int32)
    const = lambda b, h, s, qi, kj: (0, 0)
    return pl.pallas_call(
        functools.partial(_diff_kernel, tq, tk, qpos0, lam_init),
        out_shape=jax.ShapeDtypeStruct((m, DIFF_HEADS * DIFF_V), dq.dtype),
        grid_spec=pltpu.PrefetchScalarGridSpec(
            num_scalar_prefetch=2,
            grid=(batch, DIFF_HEADS, len(pairs)),
            in_specs=[pl.BlockSpec((tq, LANES), lambda b, h, s, qi, kj: (b * nq + qi[s], h)),
                      pl.BlockSpec((1, tk, LANES), lambda b, h, s, qi, kj: (b, kj[s], h)),
                      pl.BlockSpec((1, 1, 1, LANES, tk), lambda b, h, s, qi, kj: (b, h, kj[s], 0, 0)),
                      pl.BlockSpec((4, DIFF_QK), const), pl.BlockSpec((1, DIFF_V), const)],
            out_specs=pl.BlockSpec((tq, LANES), lambda b, h, s, qi, kj: (b * nq + qi[s], h)),
            scratch_shapes=[pltpu.VMEM((2 * tq, LANES), BF16), pltpu.VMEM((1, 2 * tq), F32),
                            pltpu.VMEM((1, 2 * tq), F32), pltpu.VMEM((DIFF_V, 2 * tq), F32)]),
        compiler_params=_cparams(("arbitrary",) * 3),
        name="diff_attention",
    )(qi, kj, dq, kv, vt, lam_p, norm_w.reshape(1, DIFF_V))


def _diff_finish(o12, lam_ref, nw_ref, lam_init):
    half = o12.shape[0] // 2
    lam_p = lam_ref[...]
    lam = (jnp.exp(jnp.sum(lam_p[0:1] * lam_p[1:2], axis=-1, keepdims=True))
           - jnp.exp(jnp.sum(lam_p[2:3] * lam_p[3:4], axis=-1, keepdims=True)) + lam_init)
    o = o12[0:half] - lam * o12[half:]
    y = o * lax.rsqrt(jnp.mean(o * o, axis=-1, keepdims=True) + NORM_EPS) * nw_ref[...]
    return y * (1.0 - lam_init)


def _diff_paged_kernel(pps, lam_init, pt_ref, q_ref, new_ref, lam_ref, nw_ref, *rest):
    page_refs, o_ref = rest[:pps], rest[pps]
    qbd_ref, m_ref, l_ref, acc_ref = rest[pps + 1:]
    j = pl.program_id(1)
    n_new = q_ref.shape[0]
    rows_h = 2 * n_new
    hw = DIFF_HEADS * LANES

    @pl.when(j == 0)
    def _():
        q = q_ref[...]
        lane = lax.broadcasted_iota(jnp.int32, (1, LANES), 1)
        blocks = []
        for h in range(DIFF_HEADS):
            qh = q[:, h * LANES:(h + 1) * LANES]
            q12 = jnp.concatenate([jnp.where(lane < DIFF_QK, qh, 0.0), jnp.where(lane >= DIFF_QK, qh, 0.0)], axis=0)
            blocks.append(jnp.concatenate(
                [q12 if hh == h else jnp.zeros((rows_h, LANES), F32) for hh in range(DIFF_HEADS)], axis=1))
        qbd_ref[...] = jnp.concatenate(blocks, axis=0).astype(BF16)
        m_ref[...] = jnp.full_like(m_ref, NEG)
        l_ref[...] = jnp.zeros_like(l_ref)
        acc_ref[...] = jnp.zeros_like(acc_ref)

    k = _page_slabs(page_refs, 0, DIFF_HEADS, 2 * DIFF_HEADS)
    v = _page_slabs(page_refs, DIFF_HEADS, DIFF_HEADS, 2 * DIFF_HEADS)
    _online_update(_dot_nt(qbd_ref[...], k), None, v, m_ref, l_ref, acc_ref)

    @pl.when(j == pl.num_programs(1) - 1)
    def _():
        kn = _pad_rows(_token_slabs(new_ref, 0, DIFF_HEADS, 2 * DIFF_HEADS), BF16_ROWS).astype(BF16)
        vn = _pad_rows(_token_slabs(new_ref, DIFF_HEADS, DIFF_HEADS, 2 * DIFF_HEADS), BF16_ROWS).astype(BF16)
        t = lax.broadcasted_iota(jnp.int32, (DIFF_HEADS * rows_h, 1), 0) % n_new
        u = lax.broadcasted_iota(jnp.int32, (1, kn.shape[0]), 1)
        _online_update(_dot_nt(qbd_ref[...], kn), u <= t, vn, m_ref, l_ref, acc_ref)
        l = l_ref[...]
        o12 = acc_ref[...] / jnp.where(l > 0, l, 1.0)
        for h in range(DIFF_HEADS):
            r = slice(h * rows_h, (h + 1) * rows_h)
            o_ref[:, h * LANES:(h + 1) * LANES] = _diff_finish(o12[r, h * LANES:(h + 1) * LANES], lam_ref, nw_ref,
                                                               lam_init)


def _diff_attention_paged(dq, new_kv, cache_view, layer, page_table, lam_p, norm_w, lam_init, pps):
    b, n_pages = page_table.shape
    n_new = dq.shape[0] // b
    hw = DIFF_HEADS * LANES
    rows = DIFF_HEADS * 2 * n_new
    assert n_pages % pps == 0 and n_new % SUBLANES == 0
    const = lambda i, j, pt: (0, 0)
    return pl.pallas_call(
        functools.partial(_diff_paged_kernel, pps, lam_init),
        out_shape=jax.ShapeDtypeStruct(dq.shape, F32),
        grid_spec=pltpu.PrefetchScalarGridSpec(
            num_scalar_prefetch=1,
            grid=(b, n_pages // pps),
            in_specs=[pl.BlockSpec((n_new, hw), lambda i, j, pt: (i, 0)),
                      pl.BlockSpec((n_new * 2 * DIFF_HEADS, LANES), lambda i, j, pt: (i, 0)),
                      pl.BlockSpec((4, DIFF_QK), const), pl.BlockSpec((1, DIFF_V), const)]
                     + _page_specs(layer, PAGE_SIZE * 2 * DIFF_HEADS, pps),
            out_specs=pl.BlockSpec((n_new, hw), lambda i, j, pt: (i, 0)),
            scratch_shapes=[pltpu.VMEM((rows, hw), BF16), pltpu.VMEM((rows, 1), F32), pltpu.VMEM((rows, 1), F32),
                            pltpu.VMEM((rows, hw), F32)]),
        compiler_params=_cparams(("arbitrary", "arbitrary")),
        name="diff_attention_paged",
    )(page_table, dq, new_kv, lam_p, norm_w.reshape(1, DIFF_V), *([cache_view] * pps))


def _compress_weights(w, pe):
    k = NSA_CMP_STRIDE * HEAD_DIM
    return (jnp.concatenate([w[:, :k], w[:, k:]], axis=2).astype(BF16),
            pe.reshape(2, NSA_CMP_BLOCK // NSA_CMP_STRIDE, k))


def _compress_blocks(segments, rows, w_ref, pe_ref, o_ref):
    for c in range(2):
        pe = _dot(_pad_rows(pe_ref[c], BF16_ROWS).astype(BF16), w_ref[c])
        pe_term = pe[0:1, 0:HEAD_DIM] + pe[1:2, HEAD_DIM:2 * HEAD_DIM]
        for g in range(NSA_KV_HEADS):
            both = _dot(segments(c * NSA_KV_HEADS + g), w_ref[c])
            o_ref[0, c, g] = (both[:, 0:HEAD_DIM] + pltpu.roll(both[:, HEAD_DIM:2 * HEAD_DIM], rows - 1, 0)
                              + pe_term).astype(BF16)


def _compress_kernel(x_ref, w_ref, pe_ref, o_ref):
    tok_w = 2 * NSA_KV_HEADS * HEAD_DIM

    def segments(slot):
        return jnp.concatenate([x_ref[0, :, t * tok_w + slot * HEAD_DIM:t * tok_w + (slot + 1) * HEAD_DIM]
                                for t in range(NSA_CMP_STRIDE)], axis=1)

    _compress_blocks(segments, x_ref.shape[1], w_ref, pe_ref, o_ref)


def _compress_paged_kernel(pps, n_past_seg, pt_ref, new_ref, w_ref, pe_ref, *rest):
    page_refs, o_ref, x_ref = rest[:pps], rest[pps], rest[pps + 1]
    j = pl.program_id(1)
    slots = 2 * NSA_KV_HEADS
    seg_per_page = PAGE_SIZE // NSA_CMP_STRIDE
    rows = x_ref.shape[1]

    for pp in range(0, pps, 2):
        row0 = pl.multiple_of((j * pps + pp) * seg_per_page, 2 * seg_per_page)
        for slot in range(slots):
            for t in range(NSA_CMP_STRIDE):
                pair = [page_refs[pp + q][0, 0, pl.ds(t * slots + slot, seg_per_page, stride=NSA_CMP_STRIDE * slots), :]
                        for q in range(2)]
                x_ref[slot, pl.ds(row0, 2 * seg_per_page), t * HEAD_DIM:(t + 1) * HEAD_DIM] = (
                    jnp.concatenate(pair, axis=0).astype(BF16))

    @pl.when(j == pl.num_programs(1) - 1)
    def _():
        n_new = new_ref.shape[0] // slots
        first = lax.broadcasted_iota(jnp.int32, (rows - n_past_seg, 1), 0) == 0
        for slot in range(slots):
            for t in range(NSA_CMP_STRIDE):
                if t < n_new:
                    tail = jnp.where(first, new_ref[t * slots + slot:t * slots + slot + 1, :], 0.0)
                else:
                    tail = jnp.zeros((rows - n_past_seg, HEAD_DIM), F32)
                x_ref[slot, n_past_seg:rows, t * HEAD_DIM:(t + 1) * HEAD_DIM] = tail.astype(BF16)
        _compress_blocks(lambda slot: x_ref[slot], rows, w_ref, pe_ref, o_ref)


def _compress_paged(cache_view, layer, page_table, new_rows, w, pe, pps):
    b, n_pages = page_table.shape
    slots = 2 * NSA_KV_HEADS
    n_new = new_rows.shape[0] // (b * slots)
    seg_per_page = PAGE_SIZE // NSA_CMP_STRIDE
    n_past_seg = n_pages * seg_per_page
    rows = n_past_seg + 2 * BF16_ROWS
    assert n_new <= NSA_CMP_STRIDE and pps % 2 == 0 and n_pages % pps == 0
    return pl.pallas_call(
        functools.partial(_compress_paged_kernel, pps, n_past_seg),
        out_shape=jax.ShapeDtypeStruct((b, 2, NSA_KV_HEADS, rows, HEAD_DIM), BF16),
        grid_spec=pltpu.PrefetchScalarGridSpec(
            num_scalar_prefetch=1,
            grid=(b, n_pages // pps),
            in_specs=[pl.BlockSpec((n_new * slots, HEAD_DIM), lambda i, j, pt: (i, 0)),
                      pl.BlockSpec(w.shape, lambda i, j, pt: (0, 0, 0)),
                      pl.BlockSpec(pe.shape, lambda i, j, pt: (0, 0, 0))]
                     + _page_specs(layer, PAGE_SIZE * slots, pps),
            out_specs=pl.BlockSpec((1, 2, NSA_KV_HEADS, rows, HEAD_DIM), lambda i, j, pt: (i, 0, 0, 0, 0)),
            scratch_shapes=[pltpu.VMEM((slots, rows, NSA_CMP_STRIDE * HEAD_DIM), BF16)]),
        compiler_params=_cparams(("arbitrary", "arbitrary")),
        name="nsa_compress_paged",
    )(page_table, new_rows, w, pe, *([cache_view] * pps))


def _compress(seg, w, pe):
    b, rows, width = seg.shape
    return pl.pallas_call(
        _compress_kernel,
        out_shape=jax.ShapeDtypeStruct((b, 2, NSA_KV_HEADS, rows, HEAD_DIM), BF16),
        grid=(b,),
        in_specs=[pl.BlockSpec((1, rows, width), lambda i: (i, 0, 0)),
                  pl.BlockSpec(w.shape, lambda i: (0, 0, 0)),
                  pl.BlockSpec(pe.shape, lambda i: (0, 0, 0))],
        out_specs=pl.BlockSpec((1, 2, NSA_KV_HEADS, rows, HEAD_DIM), lambda i: (i, 0, 0, 0, 0)),
        compiler_params=_cparams(("arbitrary",)),
        name="nsa_compress",
    )(seg, w, pe)


def _nsa_band(n_cmp_rows, n_slc_pad):
    ratio = NSA_SEL_BLOCK // NSA_CMP_STRIDE
    span = NSA_CMP_BLOCK // NSA_CMP_STRIDE
    n = np.arange(n_cmp_rows)[:, None]
    j = np.arange(n_slc_pad)[None, :]
    return ((n >= ratio * j - (span - 1)) & (n <= ratio * j + ratio - 1)).astype(np.float32)


def _nsa_compressed(qs, qpos_rows, ck, cv):
    cmp_end = lax.broadcasted_iota(jnp.int32, (1, ck.shape[0]), 1) * NSA_CMP_STRIDE + (NSA_CMP_BLOCK - 1)
    p_c = _masked_softmax(_dot_nt(qs, ck), cmp_end <= qpos_rows)
    return p_c, _dot(p_c.astype(BF16), cv)


def _nsa_block_scores(p_c, qpos, band_ref):
    tq = qpos.shape[0]
    imp = p_c[0:tq]
    for n in range(1, p_c.shape[0] // tq):
        imp = imp + p_c[n * tq:(n + 1) * tq]
    p_slc = _dot3_rhs_exact(_pad_rows(imp, BF16_ROWS), band_ref[...])[0:tq]
    blk = lax.broadcasted_iota(jnp.int32, (1, band_ref.shape[1]), 1)
    cur = qpos // NSA_SEL_BLOCK
    valid = blk * NSA_SEL_BLOCK <= qpos
    forced = (blk == 0) | (blk == cur) | (blk == cur - 1)
    return jnp.where(valid, p_slc + jnp.where(forced, NSA_FORCE_BONUS, 0.0), NEG)


def _nsa_select(score, score_ref, n_slc):
    score_t = score.T
    score_ref[...] = score_t
    blk_t = lax.broadcasted_iota(jnp.int32, (score_t.shape[0], 1), 0)

    def count(jp, cnt):
        row = score_ref[pl.ds(jp, 1), :]
        beats = (row > score_t) | ((row == score_t) & (jp < blk_t))
        return cnt + jnp.where(beats, 1.0, 0.0)

    cnt = lax.fori_loop(0, n_slc, count, jnp.zeros(score_t.shape, F32))
    return jnp.where((cnt < NSA_TOP_N) & (score_t > 0.5 * NEG), 1.0, 0.0)


def _nsa_select_few(score, score_ref, n_slc):
    nq, nsp = score.shape
    fold = LANES // nq
    rows = nsp // fold
    score_t = _pad_rows(score, LANES).T[:, 0:nq]
    score_ref[...] = jnp.concatenate([score_t] * fold, axis=1)
    packed = jnp.concatenate([score_t[g * rows:(g + 1) * rows] for g in range(fold)], axis=1)
    blk = (lax.broadcasted_iota(jnp.int32, (1, LANES), 1) // nq) * rows + lax.broadcasted_iota(jnp.int32, (rows, 1), 0)

    def count(jp, cnt):
        row = score_ref[pl.ds(jp, 1), :]
        beats = (row > packed) | ((row == packed) & (jp < blk))
        return cnt + jnp.where(beats, 1.0, 0.0)

    cnt = lax.fori_loop(0, n_slc, count, jnp.zeros(packed.shape, F32))
    sel_p = jnp.where((cnt < NSA_TOP_N) & (packed > 0.5 * NEG), 1.0, 0.0)
    sel_t = _pad_rows(sel_p, LANES).T
    return jnp.concatenate([sel_t[g * nq:(g + 1) * nq, 0:rows] for g in range(fold)], axis=1)


def _nsa_window(qs, qpos_rows, kw, vw, first_pos):
    wpos = first_pos + lax.broadcasted_iota(jnp.int32, (1, kw.shape[0]), 1)
    mask_w = (wpos <= qpos_rows) & (wpos > qpos_rows - NSA_WINDOW) & (wpos >= 0)
    return _dot(_masked_softmax(_dot_nt(qs, kw), mask_w).astype(BF16), vw)


def _nsa_kernel(tq, tk, qpos0, n_slc, q_ref, sm_ref, ck_ref, cv_ref, sk_ref, svt_ref, wk_ref, wv_ref,
                band_ref, o_ref, score_ref, sel_ref, m_ref, l_ref, acc_ref):
    i = pl.program_id(1)
    hpg = NSA_HEADS // NSA_KV_HEADS
    q_lo = qpos0 + i * tq
    qpos = q_lo + lax.broadcasted_iota(jnp.int32, (tq, 1), 0)
    qpos_lanes = q_lo + lax.broadcasted_iota(jnp.int32, (1, tq), 1)
    qpos_rows = jnp.concatenate([qpos] * hpg, axis=0)
    gates = _sigmoid(sm_ref[:, SMALL_GATE:SMALL_GATE + 3 * NSA_HEADS])
    blocks_per_tile = tk // NSA_SEL_BLOCK

    for g in range(NSA_KV_HEADS):
        q = q_ref[:, g * hpg * LANES:(g + 1) * hpg * LANES]
        qs = jnp.concatenate([q[:, n * LANES:(n + 1) * LANES] for n in range(hpg)], axis=0).astype(BF16)

        p_c, o_c = _nsa_compressed(qs, qpos_rows, ck_ref[0, 0, g], cv_ref[0, 0, g])
        n_rank = jnp.minimum((q_lo + tq - 1) // NSA_SEL_BLOCK + 1, n_slc)
        sel_ref[...] = _nsa_select(_nsa_block_scores(p_c, qpos, band_ref), score_ref, n_rank)

        m_ref[...] = jnp.full_like(m_ref, NEG)
        l_ref[...] = jnp.zeros_like(l_ref)
        acc_ref[...] = jnp.zeros_like(acc_ref)

        def key_tile(kt, carry):
            start = pl.multiple_of(kt * tk, tk)
            k = sk_ref[0, pl.ds(start, tk), g * LANES:(g + 1) * LANES]
            blocks = sel_ref[pl.ds(pl.multiple_of(kt * blocks_per_tile, blocks_per_tile), blocks_per_tile), :]
            picked = jnp.concatenate([jnp.broadcast_to(blocks[r:r + 1], (NSA_SEL_BLOCK, tq))
                                      for r in range(blocks_per_tile)], axis=0) > 0.5
            kpos = kt * tk + lax.broadcasted_iota(jnp.int32, (tk, 1), 0)
            mk = picked & (kpos <= qpos_lanes)
            mask = jnp.concatenate([mk] * hpg, axis=1)
            _online_update_keymajor(_dot_nt(k, qs), mask, svt_ref[0, g, kt], m_ref, l_ref, acc_ref)
            return carry

        lax.fori_loop(0, (q_lo + tq - 1) // tk + 1, key_tile, 0)
        l = l_ref[...]
        o_st = acc_ref[...] / jnp.where(l > 0, l, 1.0)
        o_s = jnp.concatenate([o_st[:, n * tq:(n + 1) * tq].T for n in range(hpg)], axis=0)

        wrows = -(-(tq + NSA_WINDOW) // BF16_ROWS) * BF16_ROWS
        wstart = pl.multiple_of(i * tq, tq)
        o_w = _nsa_window(qs, qpos_rows, wk_ref[0, pl.ds(wstart, wrows), g * LANES:(g + 1) * LANES],
                          wv_ref[0, pl.ds(wstart, wrows), g * LANES:(g + 1) * LANES], q_lo - NSA_WINDOW)

        for n in range(hpg):
            h = g * hpg + n
            r = slice(n * tq, (n + 1) * tq)
            o = (gates[:, 3 * h:3 * h + 1] * o_c[r] + gates[:, 3 * h + 1:3 * h + 2] * o_s[r]
                 + gates[:, 3 * h + 2:3 * h + 3] * o_w[r])
            o_ref[:, h * LANES:(h + 1) * LANES] = o.astype(o_ref.dtype)


def _nsa_attention(qn, z_small, comp, slc_buf, slc_vt, win_all, batch, seq_len, qpos0, tq, tk):
    m = qn.shape[0]
    nq = seq_len // tq
    t_keys = slc_buf.shape[1]
    n_cmp_rows = comp.shape[3]
    n_slc = -(-(qpos0 + seq_len) // NSA_SEL_BLOCK)
    nsp = -(-n_slc // LANES) * LANES
    assert tq % LANES == 0 and tk % NSA_SEL_BLOCK == 0 and slc_vt.shape[2] * tk == t_keys and nsp * NSA_SEL_BLOCK >= t_keys
    band = _nsa_band(n_cmp_rows, nsp)
    hw = NSA_HEADS * HEAD_DIM
    kvw = NSA_KV_HEADS * HEAD_DIM
    rows = (NSA_HEADS // NSA_KV_HEADS) * tq
    wlen = win_all.shape[1]
    return pl.pallas_call(
        functools.partial(_nsa_kernel, tq, tk, qpos0, n_slc),
        out_shape=jax.ShapeDtypeStruct((m, hw), qn.dtype),
        grid=(batch, nq),
        in_specs=[pl.BlockSpec((tq, hw), lambda b, i: (b * nq + i, 0)),
                  pl.BlockSpec((tq, LANES), lambda b, i: (b * nq + i, 0)),
                  pl.BlockSpec((1, 1, NSA_KV_HEADS, n_cmp_rows, HEAD_DIM), lambda b, i: (b, 0, 0, 0, 0)),
                  pl.BlockSpec((1, 1, NSA_KV_HEADS, n_cmp_rows, HEAD_DIM), lambda b, i: (b, 1, 0, 0, 0)),
                  pl.BlockSpec((1, t_keys, kvw), lambda b, i: (b, 0, 0)),
                  pl.BlockSpec((1,) + slc_vt.shape[1:], lambda b, i: (b, 0, 0, 0, 0)),
                  pl.BlockSpec((1, wlen, kvw), lambda b, i: (b, 0, 0)),
                  pl.BlockSpec((1, wlen, kvw), lambda b, i: (b, 0, 1)),
                  pl.BlockSpec((n_cmp_rows, nsp), lambda b, i: (0, 0))],
        out_specs=pl.BlockSpec((tq, hw), lambda b, i: (b * nq + i, 0)),
        scratch_shapes=[pltpu.VMEM((nsp, tq), F32), pltpu.VMEM((nsp, tq), F32), pltpu.VMEM((1, rows), F32),
                        pltpu.VMEM((1, rows), F32), pltpu.VMEM((HEAD_DIM, rows), F32)],
        compiler_params=_cparams(("arbitrary", "arbitrary")),
        name="nsa_attention",
    )(qn, z_small, comp, comp, slc_buf, slc_vt, win_all, win_all, jnp.asarray(band, BF16))


def _nsa_paged_kernel(pps, qpos0, n_slc, pt_ref, q_ref, sm_ref, ck_ref, cv_ref, new_ref, wst_ref, wnew_ref, band_ref,
                      *rest):
    page_refs, o_ref = rest[:pps], rest[pps]
    score_ref, qbd_ref, sel_ref, m_ref, l_ref, acc_ref, oc_ref, ow_ref = rest[pps + 1:]
    j = pl.program_id(1)
    n_new = q_ref.shape[0]
    groups = NSA_KV_HEADS
    hpg = NSA_HEADS // groups
    rows_g = hpg * n_new
    nsp = band_ref.shape[1]
    wb = wst_ref.shape[2] // (2 * groups)
    qpos = qpos0 + lax.broadcasted_iota(jnp.int32, (n_new, 1), 0)

    @pl.when(j == 0)
    def _():
        q = q_ref[...]
        qpos_rows = jnp.concatenate([qpos] * hpg, axis=0)
        scores, blocks = [], []
        for g in range(groups):
            qf = jnp.concatenate([q[:, (g * hpg + n) * LANES:(g * hpg + n + 1) * LANES] for n in range(hpg)], axis=0)
            qs = qf.astype(BF16)
            p_c, o_c = _nsa_compressed(qs, qpos_rows, ck_ref[0, 0, g], cv_ref[0, 0, g])
            oc_ref[g * rows_g:(g + 1) * rows_g] = o_c
            scores.append(_nsa_block_scores(p_c, qpos, band_ref))
            kw, vw = (_pad_rows(jnp.concatenate(
                [wst_ref[0, 0, pl.ds(slot, wb, stride=2 * groups), :], wnew_ref[pl.ds(slot, n_new, stride=2 * groups), :]],
                axis=0), BF16_ROWS).astype(BF16) for slot in (g, groups + g))
            ow_ref[g * rows_g:(g + 1) * rows_g] = _nsa_window(qs, qpos_rows, kw, vw, qpos0 - wb)
            blocks.append(jnp.concatenate(
                [qf if gg == g else jnp.zeros((rows_g, LANES), F32) for gg in range(groups)], axis=1))
        qbd_ref[...] = jnp.concatenate(blocks, axis=0).astype(BF16)
        sel = _nsa_select_few(jnp.concatenate(scores, axis=0), score_ref, n_slc)
        sel_ref[...] = jnp.concatenate(
            [sel[g * n_new:(g + 1) * n_new] for g in range(groups) for _ in range(hpg)], axis=0).astype(BF16)
        m_ref[...] = jnp.full_like(m_ref, NEG)
        l_ref[...] = jnp.zeros_like(l_ref)
        acc_ref[...] = jnp.zeros_like(acc_ref)

    k = _page_slabs(page_refs, 0, groups, 2 * groups)
    v = _page_slabs(page_refs, groups, groups, 2 * groups)
    n_keys = pps * PAGE_SIZE
    key_blk = (j * n_keys + lax.broadcasted_iota(jnp.int32, (1, n_keys), 1)) // NSA_SEL_BLOCK
    expand = jnp.where(lax.broadcasted_iota(jnp.int32, (nsp, 1), 0) == key_blk, 1.0, 0.0).astype(BF16)
    picked = _dot(sel_ref[...], expand) > 0.5
    _online_update(_dot_nt(qbd_ref[...], k), picked, v, m_ref, l_ref, acc_ref)

    @pl.when(j == pl.num_programs(1) - 1)
    def _():
        kn = _pad_rows(_token_slabs(new_ref, 0, groups, 2 * groups), BF16_ROWS).astype(BF16)
        vn = _pad_rows(_token_slabs(new_ref, groups, groups, 2 * groups), BF16_ROWS).astype(BF16)
        new_blk = qpos0 // NSA_SEL_BLOCK
        t = lax.broadcasted_iota(jnp.int32, (groups * rows_g, 1), 0) % n_new
        u = lax.broadcasted_iota(jnp.int32, (1, kn.shape[0]), 1)
        mask = (u <= t) & (sel_ref[:, new_blk:new_blk + 1] > 0.5)
        _online_update(_dot_nt(qbd_ref[...], kn), mask, vn, m_ref, l_ref, acc_ref)
        l = l_ref[...]
        o_sel = acc_ref[...] / jnp.where(l > 0, l, 1.0)
        gates = _sigmoid(sm_ref[:, SMALL_GATE:SMALL_GATE + 3 * NSA_HEADS])
        for g in range(groups):
            for n in range(hpg):
                h = g * hpg + n
                r = slice(g * rows_g + n * n_new, g * rows_g + (n + 1) * n_new)
                o_ref[:, h * LANES:(h + 1) * LANES] = (
                    gates[:, 3 * h:3 * h + 1] * oc_ref[r] + gates[:, 3 * h + 1:3 * h + 2] * o_sel[r, g * LANES:(g + 1) * LANES]
                    + gates[:, 3 * h + 2:3 * h + 3] * ow_ref[r])


def _nsa_attention_paged(qn, z_small, comp, new_kv, win_state, win_new, cache_view, layer, page_table, qpos0, pps):
    b, n_pages = page_table.shape
    n_new = qn.shape[0] // b
    groups = NSA_KV_HEADS
    hw = NSA_HEADS * HEAD_DIM
    gw = groups * HEAD_DIM
    rows = NSA_HEADS * n_new
    n_cmp_rows = comp.shape[3]
    n_slc = -(-(qpos0 + n_new) // NSA_SEL_BLOCK)
    nsp = -(-n_slc // LANES) * LANES
    wrows = win_state.shape[2]
    assert wrows == 2 * groups * min(NSA_WINDOW, qpos0)
    assert n_pages % pps == 0 and qpos0 == n_pages * PAGE_SIZE and n_new <= NSA_SEL_BLOCK
    assert qpos0 % NSA_SEL_BLOCK == 0 and LANES % (groups * n_new) == 0 and n_new % SUBLANES == 0
    assert nsp % (SUBLANES * LANES // (groups * n_new)) == 0
    band = _nsa_band(n_cmp_rows, nsp)
    const = lambda i, j, pt: (0, 0)
    return pl.pallas_call(
        functools.partial(_nsa_paged_kernel, pps, qpos0, n_slc),
        out_shape=jax.ShapeDtypeStruct(qn.shape, F32),
        grid_spec=pltpu.PrefetchScalarGridSpec(
            num_scalar_prefetch=1,
            grid=(b, n_pages // pps),
            in_specs=[pl.BlockSpec((n_new, hw), lambda i, j, pt: (i, 0)),
                      pl.BlockSpec((n_new, LANES), lambda i, j, pt: (i, 0)),
                      pl.BlockSpec((1, 1, groups, n_cmp_rows, HEAD_DIM), lambda i, j, pt: (i, 0, 0, 0, 0)),
                      pl.BlockSpec((1, 1, groups, n_cmp_rows, HEAD_DIM), lambda i, j, pt: (i, 1, 0, 0, 0)),
                      pl.BlockSpec((n_new * 2 * groups, LANES), lambda i, j, pt: (i, 0)),
                      pl.BlockSpec((1, 1, wrows, LANES), lambda i, j, pt: (layer, i, 0, 0)),
                      pl.BlockSpec((n_new * 2 * groups, LANES), lambda i, j, pt: (i, 0)),
                      pl.BlockSpec((n_cmp_rows, nsp), const)]
                     + _page_specs(layer, PAGE_SIZE * 2 * groups, pps),
            out_specs=pl.BlockSpec((n_new, hw), lambda i, j, pt: (i, 0)),
            scratch_shapes=[pltpu.VMEM((nsp, LANES), F32), pltpu.VMEM((rows, gw), BF16), pltpu.VMEM((rows, nsp), BF16),
                            pltpu.VMEM((rows, 1), F32), pltpu.VMEM((rows, 1), F32), pltpu.VMEM((rows, gw), F32),
                            pltpu.VMEM((rows, HEAD_DIM), F32), pltpu.VMEM((rows, HEAD_DIM), F32)]),
        compiler_params=_cparams(("arbitrary", "arbitrary")),
        name="nsa_attention_paged",
    )(page_table, qn, z_small, comp, comp, new_kv, win_state, win_new, jnp.asarray(band, BF16),
      *([cache_view] * pps))


def _out_proj_kernel(x_ref, og_ref, on_ref, od_ref, w_ref, nw_ref, g_ref, o_ref):
    a = jnp.concatenate([og_ref[...].astype(BF16), on_ref[...].astype(BF16), od_ref[...].astype(BF16)], axis=-1)
    mix = _dot(a, w_ref[0])
    y = mix * lax.rsqrt(jnp.mean(mix * mix, axis=-1, keepdims=True) + NORM_EPS) * nw_ref[...]
    o_ref[...] = x_ref[...] + g_ref[0] * y


def _out_proj(x2, o_gla, o_nsa, o_d, w, layer, nw, gate, seq_len, tm):
    m, d = x2.shape
    row = lambda width: pl.BlockSpec((tm, width), lambda i: (i, 0))
    return pl.pallas_call(
        _out_proj_kernel,
        out_shape=jax.ShapeDtypeStruct((m, d), F32),
        grid=(m // tm,),
        in_specs=[row(d), row(o_gla.shape[1]), row(o_nsa.shape[1]), row(o_d.shape[1]),
                  pl.BlockSpec((1,) + w.shape[1:], lambda i: (layer, 0, 0)), pl.BlockSpec((1, d), lambda i: (0, 0)),
                  _mod_spec(seq_len, tm, d)],
        out_specs=row(d),
        compiler_params=_cparams(("arbitrary",)),
        name="out_proj",
    )(x2, o_gla, o_nsa, o_d, w, nw, _expand_mod(gate, seq_len, tm))


def _ffn_kernel(x_ref, nw2_ref, sc_ref, sh_ref, wg_ref, wu_ref, wo_ref, nw3_ref, g_ref, o_ref, h_ref, acc_ref):
    j = pl.program_id(1)

    @pl.when(j == 0)
    def _():
        h_ref[...] = _norm_mod(x_ref[...], nw2_ref[...], sc_ref[0], sh_ref[0]).astype(BF16)
        acc_ref[...] = jnp.zeros_like(acc_ref)

    h = h_ref[...]
    gate = _dot(h, wg_ref[0])
    up = _dot(h, wu_ref[0])
    acc_ref[...] += _dot((gate * _sigmoid(gate) * up).astype(BF16), wo_ref[0])

    @pl.when(j == pl.num_programs(1) - 1)
    def _():
        f = acc_ref[...]
        y = f * lax.rsqrt(jnp.mean(f * f, axis=-1, keepdims=True) + NORM_EPS) * nw3_ref[...]
        o_ref[...] = x_ref[...] + g_ref[0] * y


def _ffn(x2, nw2, sc, sh, w_in, w_out, layer, nw3, gate, seq_len, tm, tf):
    m, d = x2.shape
    d_ff = w_out.shape[1]
    nf = d_ff // tf
    row = pl.BlockSpec((tm, d), lambda i, j: (i, 0))
    vec = pl.BlockSpec((1, d), lambda i, j: (0, 0))
    return pl.pallas_call(
        _ffn_kernel,
        out_shape=jax.ShapeDtypeStruct((m, d), F32),
        grid=(m // tm, nf),
        in_specs=[row, vec, _mod_spec(seq_len, tm, d), _mod_spec(seq_len, tm, d),
                  pl.BlockSpec((1, d, tf), lambda i, j: (layer, 0, j)),
                  pl.BlockSpec((1, d, tf), lambda i, j: (layer, 0, nf + j)),
                  pl.BlockSpec((1, tf, d), lambda i, j: (layer, j, 0)),
                  vec, _mod_spec(seq_len, tm, d)],
        out_specs=row,
        scratch_shapes=[pltpu.VMEM((tm, d), BF16), pltpu.VMEM((tm, d), F32)],
        compiler_params=_cparams(("arbitrary", "arbitrary")),
        name="ffn",
    )(x2, nw2, _expand_mod(sc, seq_len, tm), _expand_mod(sh, seq_len, tm), w_in, w_in, w_out, nw3,
      _expand_mod(gate, seq_len, tm))


def _pick(n, target):
    if n <= target:
        return n
    for t in range(target, 7, -1):
        if n % t == 0 and t % SUBLANES == 0:
            return t
    return n


def _permute_w_in(w):
    depth, d = w.shape[:2]
    hk, hv = GLA_HEADS * GLA_DK, GLA_HEADS * GLA_DV
    kvw = NSA_KV_HEADS * HEAD_DIM
    sizes = [hk, hk, hv, hv, GLA_GATE_RANK, NSA_HEADS * HEAD_DIM, kvw, kvw, kvw, kvw, kvw, kvw, NSA_HEADS * 3,
             DIFF_HEADS * 2 * DIFF_QK, DIFF_HEADS * 2 * DIFF_QK, DIFF_HEADS * DIFF_V]
    offs = np.concatenate([[0], np.cumsum(sizes)])
    part = lambda k: w[:, :, offs[k]:offs[k + 1]]
    main = jnp.concatenate([part(k) for k in (5, 0, 1, 2, 3, 6, 7, 8, 9, 10, 11, 13, 14, 15)], axis=2)
    pad = LANES - GLA_GATE_RANK - NSA_HEADS * 3
    small = jnp.concatenate([part(4), part(12), jnp.zeros((depth, d, pad), w.dtype)], axis=2)
    return main.astype(BF16), small.astype(BF16)


def _layer(x, mod, qpos0, past, gla_s0, lp, lam_init):
    b, seq_len, d = x.shape
    m = b * seq_len
    x2 = x.reshape(m, d)
    sh1, sc1, g1, sh2, sc2, g2 = jnp.split(mod, 6, axis=-1)
    tm = _pick(seq_len, 512) if seq_len >= 128 else m
    tm_in = _pick(seq_len, 1024) if seq_len >= 128 else m
    layer = lp['layer']
    z, z_small = _in_proj(x2, lp['norm'][0:1], sc1, sh1, *lp['w_in'], layer, seq_len, tm_in, _pick(Z_WIDTH, 512))

    tab = _rope_tables(qpos0 + np.arange(seq_len))
    if seq_len % tm != 0:
        tab = np.tile(tab, (m // seq_len, 1))
    tab = jnp.asarray(tab)
    act_dt = BF16 if seq_len % BF16_ROWS == 0 else F32
    (qn, cmp_f, cmp_b, slc_f, slc_b, win_f, win_b, dq, dkv_f, dkv_b, *vts) = _rope_split(
        z, tab, seq_len, tm, act_dt, transposed_values=past is None)

    chunk = GLA_CHUNK if seq_len % GLA_CHUNK == 0 else seq_len
    o_gla, gla_state = _gla(z, z_small, lp['gla_gate_w'], lp['gla_gate_b'], lp['gla_norm'], gla_s0, b, seq_len,
                            chunk, act_dt)

    kvw = 2 * NSA_KV_HEADS * HEAD_DIM
    dw = 2 * DIFF_HEADS * DIFF_V

    if past is None:
        slc_vt, diff_vt = vts
        win_pad = jnp.zeros((b, -(NSA_WINDOW + seq_len) % BF16_ROWS, kvw), BF16)
        win_all = jnp.concatenate([jnp.zeros((b, NSA_WINDOW, kvw), BF16), win_b.reshape(b, seq_len, kvw), win_pad],
                                  axis=1)
        seg = cmp_b.reshape(b, seq_len // NSA_CMP_STRIDE, NSA_CMP_STRIDE * kvw)
        comp = _compress(seg, lp['nsa_cmp_w'], lp['nsa_cmp_pe'])
        o_nsa = _nsa_attention(qn, z_small, comp, slc_b.reshape(b, seq_len, kvw), slc_vt, win_all, b, seq_len, qpos0,
                               _pick(seq_len, 256), tm)
        o_d = _diff_attention(dq, dkv_b.reshape(b, seq_len, dw), diff_vt, lp['diff_lambda'], lp['diff_norm'], b,
                              seq_len, qpos0, lam_init, _pick(seq_len, 512), tm)
    else:
        pt = past['page_table']
        pps = max(p for p in (PAGES_PER_STEP, 8, 4, 2) if pt.shape[1] % p == 0)
        pps_attn = max(p for p in (2 * PAGES_PER_STEP, pps) if pt.shape[1] % p == 0)
        comp = _compress_paged(past['cmp'], layer, pt, cmp_f, lp['nsa_cmp_w'], lp['nsa_cmp_pe'], pps)
        o_nsa = _nsa_attention_paged(qn, z_small, comp, slc_f, past['win'], win_f, past['slc'], layer, pt, qpos0,
                                     pps_attn)
        o_d = _diff_attention_paged(dq, dkv_f, past['diff'], layer, pt, lp['diff_lambda'], lp['diff_norm'],
                                    lam_init, pps_attn)

    x1 = _out_proj(x2, o_gla, o_nsa, o_d, lp['w_out'], layer, lp['norm'][1:2], g1, seq_len, tm)
    d_ff = lp['ffn_w_out'].shape[1]
    x2n = _ffn(x1, lp['norm'][2:3], sc2, sh2, lp['ffn_w_in'], lp['ffn_w_out'], layer, lp['norm'][3:4], g2, seq_len,
               tm, _pick(d_ff, 512) if d_ff % LANES == 0 else d_ff)

    return (x2n.reshape(b, seq_len, d), cmp_f, slc_f, dkv_f, win_f, gla_state)


def kernel(x_prompt, x_sample, c_prompt, c_sample, cache_nsa_cmp_kv, cache_nsa_slc_kv, cache_diff_kv,
           state_nsa_win_kv, state_gla, page_table, ada_w, ada_b, norm_w, w_in, gla_gate_w, gla_gate_b,
           gla_norm, nsa_cmp_pe, nsa_cmp_w, diff_lambda, diff_norm, w_out, ffn_w_in, ffn_w_out):
    depth = ada_w.shape[0]
    bp, lp_len, d = x_prompt.shape
    bs, ls_len, _ = x_sample.shape
    n_pages = page_table.shape[1]
    past_len = n_pages * PAGE_SIZE
    wb = state_nsa_win_kv.shape[2]
    hk = GLA_HEADS * GLA_DK

    n_c = bp + bs
    rows = -(-n_c // SUBLANES) * SUBLANES
    c_all = jnp.concatenate([c_prompt, c_sample, jnp.zeros((rows - n_c, d), F32)], axis=0)
    mod_all = _modulation(c_all, ada_w, ada_b)

    w_in_b, w_out_b = _permute_w_in(w_in), w_out.astype(BF16)
    ffn_w_in_b, ffn_w_out_b = ffn_w_in.astype(BF16), ffn_w_out.astype(BF16)
    past_views = {'cmp': _page_view(cache_nsa_cmp_kv), 'slc': _page_view(cache_nsa_slc_kv),
                  'diff': _page_view(cache_diff_kv), 'win': _page_view(state_nsa_win_kv), 'page_table': page_table}

    xp, xs = x_prompt, x_sample
    outs = [[] for _ in range(10)]
    for l in range(depth):
        cmp_w, cmp_pe = _compress_weights(nsa_cmp_w[l], nsa_cmp_pe[l])
        lp = {'layer': l, 'norm': norm_w[l], 'w_in': w_in_b,
              'gla_gate_w': gla_gate_w[l], 'gla_gate_b': gla_gate_b[l], 'gla_norm': gla_norm[l],
              'nsa_cmp_pe': cmp_pe, 'nsa_cmp_w': cmp_w, 'diff_lambda': diff_lambda[l], 'diff_norm': diff_norm[l],
              'w_out': w_out_b, 'ffn_w_in': ffn_w_in_b, 'ffn_w_out': ffn_w_out_b}
        lam_init = 0.8 - 0.6 * math.exp(-0.3 * l)

        gla0 = jnp.zeros((bp, hk, GLA_DV), F32)
        xp, cmp_p, slc_p, diff_p, win_p, gla_p = _layer(xp, mod_all[l, :bp], 0, None, gla0, lp, lam_init)

        xs, cmp_s, slc_s, diff_s, win_s, gla_s = _layer(xs, mod_all[l, bp:bp + bs], past_len, past_views,
                                                        state_gla[l].reshape(bs, hk, GLA_DV), lp, lam_init)

        kv_shape = lambda b, n: (b, n, 2, NSA_KV_HEADS, HEAD_DIM)
        outs[0].append(cmp_p.reshape(kv_shape(bp, lp_len)))
        outs[1].append(cmp_s.reshape(kv_shape(bs, ls_len)))
        outs[2].append(slc_p.reshape(kv_shape(bp, lp_len)))
        outs[3].append(slc_s.reshape(kv_shape(bs, ls_len)))
        outs[4].append(diff_p.reshape(bp, lp_len, 2, DIFF_HEADS, DIFF_V))
        outs[5].append(diff_s.reshape(bs, ls_len, 2, DIFF_HEADS, DIFF_V))
        win_p5 = win_p.reshape(kv_shape(bp, lp_len))
        outs[6].append(win_p5[:, lp_len - wb:] if lp_len >= wb else
                       jnp.concatenate([jnp.zeros(kv_shape(bp, wb - lp_len), F32), win_p5], axis=1))
        outs[7].append(win_s.reshape(kv_shape(bs, ls_len)))
        outs[8].append(gla_p.reshape(bp, GLA_HEADS, GLA_DK, GLA_DV))
        outs[9].append(gla_s.reshape(bs, GLA_HEADS, GLA_DK, GLA_DV))

    stacked = [jnp.stack(o) for o in outs]
    stacked[7] = jnp.concatenate([state_nsa_win_kv, stacked[7]], axis=2)[:, :, -wb:]
    return (xp, xs) + tuple(stacked)
```

```python
import functools
import math

import numpy as np
import jax
import jax.numpy as jnp
from jax import lax
from jax.experimental import pallas as pl
from jax.experimental.pallas import tpu as pltpu

F32 = jnp.float32
BF16 = jnp.bfloat16

PAGE_SIZE = 128
HEAD_DIM = 128
GLA_HEADS, GLA_DK, GLA_DV, GLA_GATE_RANK, GLA_TAU = 4, 64, 128, 16, 16.0
NSA_HEADS, NSA_KV_HEADS = 8, 2
NSA_CMP_BLOCK, NSA_CMP_STRIDE, NSA_SEL_BLOCK, NSA_TOP_N, NSA_WINDOW = 32, 16, 64, 16, 512
NSA_FORCE_BONUS = 1.0e4
DIFF_HEADS, DIFF_QK, DIFF_V = 4, 64, 128
ROPE_THETA = 10000.0
NORM_EPS = 1e-6
NEG = -1.0e30
LOG2_E = math.log2(math.e)

LANES = 128
SUBLANES = 8
BF16_ROWS = 16
VMEM_LIMIT_BYTES = 56 * 1024 * 1024
GLA_CHUNK = 128
PAGES_PER_STEP = 16

Z_NQ, Z_GQ, Z_GK, Z_GV, Z_GR = 0, 1024, 1280, 1536, 2048
Z_CMP, Z_SLC, Z_WIN, Z_DQ, Z_DK, Z_DV, Z_WIDTH = 2560, 3072, 3584, 4096, 4608, 5120, 5632
SMALL_GA, SMALL_GATE = 0, 16


def _cparams(sem):
    return pltpu.CompilerParams(dimension_semantics=sem, vmem_limit_bytes=VMEM_LIMIT_BYTES)


def _dot(a, b):
    return jnp.dot(a, b, preferred_element_type=F32)


def _dot_nt(a, b):
    return lax.dot_general(a, b, (((1,), (1,)), ((), ())), preferred_element_type=F32)


def _dot_tn(a, b):
    return lax.dot_general(a, b, (((0,), (0,)), ((), ())), preferred_element_type=F32)


def _split3(a):
    hi = a.astype(BF16)
    r = a - hi.astype(F32)
    mid = r.astype(BF16)
    lo = (r - mid.astype(F32)).astype(BF16)
    return hi, mid, lo


def _dot3_rhs_exact(a, b):
    hi, mid, lo = _split3(a)
    return _dot(hi, b) + _dot(mid, b) + _dot(lo, b)


def _dot3_lhs_exact(a, b):
    hi, mid, lo = _split3(b)
    return _dot(a, hi) + _dot(a, mid) + _dot(a, lo)


def _sigmoid(x):
    return 1.0 / (1.0 + jnp.exp(-x))


def _masked_softmax(s, mask):
    s = jnp.where(mask, s, NEG)
    m = jnp.max(s, axis=-1, keepdims=True)
    e = jnp.where(mask, jnp.exp2(s - m), 0.0)
    den = jnp.sum(e, axis=-1, keepdims=True)
    return e / jnp.where(den > 0, den, 1.0)


def _mod_kernel(c_ref, w_ref, b_ref, o_ref):
    c = c_ref[...]
    a = (c * _sigmoid(c)).astype(BF16)
    o_ref[0] = _dot(a, w_ref[0].astype(BF16)) + b_ref[0]


def _modulation(c_all, ada_w, ada_b):
    depth, d, n6 = ada_w.shape
    rows = c_all.shape[0]
    tn = max(t for t in range(LANES, min(1024, n6) + 1, LANES) if n6 % t == 0)
    return pl.pallas_call(
        _mod_kernel,
        out_shape=jax.ShapeDtypeStruct((depth, rows, n6), F32),
        grid=(depth, n6 // tn),
        in_specs=[pl.BlockSpec((rows, d), lambda l, j: (0, 0)),
                  pl.BlockSpec((1, d, tn), lambda l, j: (l, 0, j)),
                  pl.BlockSpec((1, 1, tn), lambda l, j: (l, 0, j))],
        out_specs=pl.BlockSpec((1, rows, tn), lambda l, j: (l, 0, j)),
        compiler_params=_cparams(("arbitrary", "arbitrary")),
        name="adaln_modulation",
    )(c_all, ada_w, ada_b.reshape(depth, 1, n6))


def _norm_mod(x, nw, sc, sh):
    ms = jnp.mean(x * x, axis=-1, keepdims=True)
    return x * lax.rsqrt(ms + NORM_EPS) * nw * (1.0 + sc) + sh


def _in_proj_kernel(x_ref, nw_ref, sc_ref, sh_ref, w_ref, ws_ref, o_ref, os_ref, h_ref):
    @pl.when(pl.program_id(1) == 0)
    def _():
        h = _norm_mod(x_ref[...], nw_ref[...], sc_ref[0], sh_ref[0]).astype(BF16)
        h_ref[...] = h
        os_ref[...] = _dot(h, ws_ref[0])

    o_ref[...] = _dot(h_ref[...], w_ref[0])


def _mod_spec(seq_len, tm, d):
    if seq_len % tm == 0:
        per = seq_len // tm
        return pl.BlockSpec((1, 1, d), lambda i, *_: (i // per, 0, 0))
    return pl.BlockSpec((1, tm, d), lambda i, *_: (0, i, 0))


def _expand_mod(m, seq_len, tm):
    if seq_len % tm == 0:
        return m[:, None, :]
    return jnp.repeat(m, seq_len, axis=0)[None]


def _in_proj(x2, nw, sc, sh, w, w_small, layer, seq_len, tm, tn):
    m, d = x2.shape
    n = w.shape[2]
    return pl.pallas_call(
        _in_proj_kernel,
        out_shape=[jax.ShapeDtypeStruct((m, n), F32), jax.ShapeDtypeStruct((m, LANES), F32)],
        grid=(m // tm, n // tn),
        in_specs=[pl.BlockSpec((tm, d), lambda i, j: (i, 0)),
                  pl.BlockSpec((1, d), lambda i, j: (0, 0)),
                  _mod_spec(seq_len, tm, d), _mod_spec(seq_len, tm, d),
                  pl.BlockSpec((1, d, tn), lambda i, j: (layer, 0, j)),
                  pl.BlockSpec((1, d, LANES), lambda i, j: (layer, 0, 0))],
        out_specs=[pl.BlockSpec((tm, tn), lambda i, j: (i, j)), pl.BlockSpec((tm, LANES), lambda i, j: (i, 0))],
        scratch_shapes=[pltpu.VMEM((tm, d), BF16)],
        compiler_params=_cparams(("arbitrary", "arbitrary")),
        name="in_proj",
    )(x2, nw, _expand_mod(sc, seq_len, tm), _expand_mod(sh, seq_len, tm), w, w_small)


def _rope_tables(pos):
    pos = np.asarray(pos, np.float64)[:, None]

    def tab(half, reps):
        inv = ROPE_THETA ** (-np.arange(half, dtype=np.float64) / half)
        ang = pos * inv[None, :]
        c, s = np.cos(ang), np.sin(ang)
        return np.tile(np.concatenate([c, c], -1), (1, reps)), np.tile(np.concatenate([-s, s], -1), (1, reps))

    c128, s128 = tab(HEAD_DIM // 2, 1)
    c64, s64 = tab(DIFF_QK // 2, 2)
    return np.concatenate([c128, s128, c64, s64], axis=-1).astype(np.float32)


def _rope128(x, cos, sin):
    return x * cos + pltpu.roll(x, 64, 1) * sin


def _rope64(x, cos, sin, first_half):
    partner = jnp.where(first_half, pltpu.roll(x, 96, 1), pltpu.roll(x, 32, 1))
    return x * cos + partner * sin


def _rope_kernel(nq_ref, cmp_ref, slc_ref, win_ref, dq_ref, dk_ref, dv_ref, tab_ref,
                 qn_o, cmp_o, cmpb_o, slc_o, slcb_o, win_o, winb_o, dq_o, dkv_o, dkvb_o, *vt_outs):
    if vt_outs:
        slc_vt_o, diff_vt_o = vt_outs
        for g in range(NSA_KV_HEADS):
            slc_vt_o[0, g, 0] = slc_ref[:, (NSA_KV_HEADS + g) * LANES:(NSA_KV_HEADS + g + 1) * LANES].T.astype(BF16)
        for h in range(DIFF_HEADS):
            diff_vt_o[0, h, 0] = dv_ref[:, h * LANES:(h + 1) * LANES].T.astype(BF16)
    tab = tab_ref[...]
    rows = tab.shape[0]
    c128, s128, c64, s64 = (tab[:, i * LANES:(i + 1) * LANES] for i in range(4))
    lane = lax.broadcasted_iota(jnp.int32, (1, LANES), 1)
    first_half = (lane % DIFF_QK) < (DIFF_QK // 2)
    nsa_scale = HEAD_DIM ** -0.5 * LOG2_E
    diff_scale = DIFF_QK ** -0.5 * LOG2_E

    for h in range(NSA_HEADS):
        sl = slice(h * LANES, (h + 1) * LANES)
        qn_o[:, sl] = (_rope128(nq_ref[:, sl], c128, s128) * nsa_scale).astype(qn_o.dtype)

    for src, dst, dstb in ((cmp_ref, cmp_o, cmpb_o), (slc_ref, slc_o, slcb_o), (win_ref, win_o, winb_o)):
        for g in range(2 * NSA_KV_HEADS):
            sl = slice(g * LANES, (g + 1) * LANES)
            v = src[:, sl]
            if g < NSA_KV_HEADS:
                v = _rope128(v, c128, s128)
            dst[pl.ds(g, rows, stride=2 * NSA_KV_HEADS), :] = v
            dstb[:, sl] = v.astype(BF16)

    for h in range(DIFF_HEADS):
        sl = slice(h * LANES, (h + 1) * LANES)
        dq_o[:, sl] = (_rope64(dq_ref[:, sl], c64, s64, first_half) * diff_scale).astype(dq_o.dtype)
        k = _rope64(dk_ref[:, sl], c64, s64, first_half)
        dkv_o[pl.ds(h, rows, stride=2 * DIFF_HEADS), :] = k
        dkvb_o[:, sl] = k.astype(BF16)
        sv = slice((DIFF_HEADS + h) * LANES, (DIFF_HEADS + h + 1) * LANES)
        v = dv_ref[:, sl]
        dkv_o[pl.ds(DIFF_HEADS + h, rows, stride=2 * DIFF_HEADS), :] = v
        dkvb_o[:, sv] = v.astype(BF16)


def _rope_split(z, tab, seq_len, tm, act_dt, transposed_values):
    m = z.shape[0]
    per = max(seq_len // tm, 1)
    tab_spec = (pl.BlockSpec((tm, 4 * LANES), lambda i: (i % per, 0)) if seq_len % tm == 0
                else pl.BlockSpec((tm, 4 * LANES), lambda i: (i, 0)))

    def zs(width, col):
        return pl.BlockSpec((tm, width), lambda i: (i, col // width))

    def os(width):
        return pl.BlockSpec((tm, width), lambda i: (i, 0))

    kv = (512, F32, True)
    outs = [(1024, act_dt, False), kv, (512, BF16, False), kv, (512, BF16, False), kv, (512, BF16, False),
            (512, act_dt, False), (1024, F32, True), (1024, BF16, False)]
    out_shape = [jax.ShapeDtypeStruct((m * w // LANES, LANES) if il else (m, w), dt) for w, dt, il in outs]
    out_specs = [pl.BlockSpec((tm * w // LANES, LANES), lambda i: (i, 0)) if il else os(w) for w, dt, il in outs]
    if transposed_values:
        assert seq_len % tm == 0
        for heads in (NSA_KV_HEADS, DIFF_HEADS):
            out_shape.append(jax.ShapeDtypeStruct((m // seq_len, heads, per, LANES, tm), BF16))
            out_specs.append(pl.BlockSpec((1, heads, 1, LANES, tm), lambda i: (i // per, 0, i % per, 0, 0)))
    return pl.pallas_call(
        _rope_kernel,
        out_shape=out_shape,
        grid=(m // tm,),
        in_specs=[zs(1024, Z_NQ), zs(512, Z_CMP), zs(512, Z_SLC), zs(512, Z_WIN),
                  zs(512, Z_DQ), zs(512, Z_DK), zs(512, Z_DV), tab_spec],
        out_specs=out_specs,
        compiler_params=_cparams(("arbitrary",)),
        name="rope_split",
    )(z, z, z, z, z, z, z, tab)


def _page_view(cache):
    depth, n_pool, page, two, heads, width = cache.shape
    return cache.reshape(depth, n_pool, page * two * heads, width)


def _page_specs(layer, page_rows, pages_per_step):
    def spec(p):
        return pl.BlockSpec((1, 1, page_rows, LANES),
                            lambda b, j, pt: (layer, pt[b, j * pages_per_step + p], 0, 0))
    return [spec(p) for p in range(pages_per_step)]


def _page_slabs(page_refs, first_slot, n, n_slots):
    return jnp.concatenate(
        [jnp.concatenate([ref[0, 0, pl.ds(first_slot + s, PAGE_SIZE, stride=n_slots), :] for s in range(n)], axis=1)
         for ref in page_refs], axis=0).astype(BF16)


def _token_slabs(ref, first_slot, n, n_slots):
    tokens = ref.shape[0] // n_slots
    return jnp.concatenate([ref[pl.ds(first_slot + s, tokens, stride=n_slots), :] for s in range(n)], axis=1)


def _pad_rows(x, multiple):
    pad = -x.shape[0] % multiple
    return x if pad == 0 else jnp.concatenate([x, jnp.zeros((pad, x.shape[1]), x.dtype)], axis=0)


def _gla_constants(c):
    t = np.arange(c)
    sizes = []
    s = c // 2
    while s >= 1:
        sizes.append(s)
        s //= 2
    sel, masks = [], []
    for sz in sizes:
        ref_row = (t // (2 * sz)) * (2 * sz) + sz - 1
        sel.append((t[None, :] <= ref_row[:, None]).astype(np.float32))
        same = (t[:, None] // (2 * sz)) == (t[None, :] // (2 * sz))
        masks.append((same & ((t[:, None] // sz) % 2 == 1) & ((t[None, :] // sz) % 2 == 0)).astype(np.float32))
    sel.append((t[None, :] <= t[:, None]).astype(np.float32))
    masks.append(np.eye(c, dtype=np.float32))
    return np.concatenate(sel, 0), np.stack(masks, 0)


def _gla_kernel(chunk, levels, q_ref, k_ref, v_ref, r_ref, sm_ref, gw_ref, gb_ref, nw_ref, s0_ref,
                sel_ref, mask_ref, o_ref, st_ref, state_ref):
    c = chunk
    c_in = q_ref.shape[0]
    hk = GLA_HEADS * GLA_DK
    ci = pl.program_id(1)

    @pl.when(ci == 0)
    def _():
        state_ref[...] = s0_ref[0].T

    def rows(x):
        if c_in == c:
            return x
        return jnp.concatenate([x, jnp.zeros((c - c_in, x.shape[1]), x.dtype)], axis=0)

    ga = rows(sm_ref[:, SMALL_GA:SMALL_GA + GLA_GATE_RANK])
    pre = _dot3_lhs_exact_both(ga, gw_ref[...]) + gb_ref[...]
    log_a = (jnp.minimum(pre, 0.0) - jnp.log(1.0 + jnp.exp(-jnp.abs(pre)))) / GLA_TAU
    if c_in != c:
        log_a = jnp.where(lax.broadcasted_iota(jnp.int32, (c, 1), 0) < c_in, log_a, 0.0)
    refs = _dot3_lhs_exact(sel_ref[...], log_a)
    b = refs[(levels - 1) * c:levels * c]
    q = rows(q_ref[...]) * (GLA_DK ** -0.5)
    k = rows(k_ref[...])
    v_all = rows(v_ref[...])
    lane = lax.broadcasted_iota(jnp.int32, (1, hk), 1)
    head_of_lane = lane // GLA_DK

    def stack_heads(x):
        return jnp.concatenate([jnp.where(head_of_lane == h, x, 0.0) for h in range(GLA_HEADS)], axis=0).astype(BF16)

    attn = jnp.zeros((GLA_HEADS, c, c), F32)
    for lv in range(levels):
        r = refs[lv * c:(lv + 1) * c]
        qd = stack_heads(q * jnp.exp(jnp.minimum(b - r, 0.0)))
        kd = (k * jnp.exp(jnp.minimum(r - b, 0.0))).astype(BF16)
        attn = attn + _dot_nt(qd, kd).reshape(GLA_HEADS, c, c) * mask_ref[lv][None]
    state = state_ref[...]
    inter = _dot_nt(stack_heads(q * jnp.exp(b)), state.astype(BF16))

    nw = nw_ref[...]
    for h in range(GLA_HEADS):
        vh = v_all[:, h * GLA_DV:(h + 1) * GLA_DV]
        o = (_dot(attn[h].astype(BF16), vh.astype(BF16)) + inter[h * c:(h + 1) * c])[0:c_in]
        y = o * lax.rsqrt(jnp.mean(o * o, axis=-1, keepdims=True) + NORM_EPS) * nw
        rh = r_ref[:, h * GLA_DV:(h + 1) * GLA_DV]
        o_ref[:, h * GLA_DV:(h + 1) * GLA_DV] = (y * (rh * _sigmoid(rh))).astype(o_ref.dtype)

    b_end = b[c - 1:c]
    kd = (k * jnp.exp(b_end - b)).astype(BF16)
    new_state = state * jnp.exp(b_end)
    for h in range(GLA_HEADS):
        vh = v_all[:, h * GLA_DV:(h + 1) * GLA_DV].astype(BF16)
        new_state = new_state + jnp.where(head_of_lane == h, _dot_tn(vh, kd), 0.0)
    state_ref[...] = new_state

    @pl.when(ci == pl.num_programs(1) - 1)
    def _():
        st_ref[0] = new_state.T


def _dot3_lhs_exact_both(a, b):
    ah, am, al = _split3(a)
    bh, bm, bl = _split3(b)
    return (_dot(ah, bh) + _dot(ah, bm) + _dot(am, bh)) + (_dot(ah, bl) + _dot(am, bm) + _dot(al, bh))


def _gla(z, z_small, gate_w, gate_b, norm_w, s0, batch, seq_len, chunk, act_dt):
    m = z.shape[0]
    nc = seq_len // chunk
    hk, hv = GLA_HEADS * GLA_DK, GLA_HEADS * GLA_DV
    comp_rows = max(chunk, BF16_ROWS)
    sel, masks = _gla_constants(comp_rows)
    levels = masks.shape[0]

    def zs(width, col):
        return pl.BlockSpec((chunk, width), lambda b, c: (b * nc + c, col // width))

    const2 = lambda b, c: (0, 0)
    return pl.pallas_call(
        functools.partial(_gla_kernel, comp_rows, levels),
        out_shape=[jax.ShapeDtypeStruct((m, hv), act_dt), jax.ShapeDtypeStruct((batch, hk, GLA_DV), F32)],
        grid=(batch, nc),
        in_specs=[zs(hk, Z_GQ), zs(hk, Z_GK), zs(hv, Z_GV), zs(hv, Z_GR), zs(LANES, 0),
                  pl.BlockSpec((GLA_GATE_RANK, hk), const2), pl.BlockSpec((1, hk), const2),
                  pl.BlockSpec((1, GLA_DV), const2),
                  pl.BlockSpec((1, hk, GLA_DV), lambda b, c: (b, 0, 0)),
                  pl.BlockSpec((levels * comp_rows, comp_rows), const2),
                  pl.BlockSpec((levels, comp_rows, comp_rows), lambda b, c: (0, 0, 0))],
        out_specs=[pl.BlockSpec((chunk, hv), lambda b, c: (b * nc + c, 0)),
                   pl.BlockSpec((1, hk, GLA_DV), lambda b, c: (b, 0, 0))],
        scratch_shapes=[pltpu.VMEM((GLA_DV, hk), F32)],
        compiler_params=_cparams(("arbitrary", "arbitrary")),
        name="gla",
    )(z, z, z, z, z_small, gate_w, gate_b.reshape(1, hk), norm_w.reshape(1, GLA_DV), s0,
      jnp.asarray(sel, BF16), jnp.asarray(masks, F32))


def _online_update(s, mask, v, m_ref, l_ref, acc_ref):
    if mask is not None:
        s = jnp.where(mask, s, NEG)
    m_prev = m_ref[...]
    m_new = jnp.maximum(m_prev, jnp.max(s, axis=-1, keepdims=True))
    p = jnp.exp2(s - m_new)
    if mask is not None:
        p = jnp.where(mask, p, 0.0)
    alpha = jnp.exp2(m_prev - m_new)
    l_ref[...] = alpha * l_ref[...] + jnp.sum(p, axis=-1, keepdims=True)
    acc_ref[...] = alpha * acc_ref[...] + _dot(p.astype(BF16), v)
    m_ref[...] = m_new


def _online_update_keymajor(s, mask, vt, m_ref, l_ref, acc_ref):
    if mask is not None:
        s = jnp.where(mask, s, NEG)
    m_prev = m_ref[...]
    m_new = jnp.maximum(m_prev, jnp.max(s, axis=0, keepdims=True))
    p = jnp.exp2(s - m_new)
    if mask is not None:
        p = jnp.where(mask, p, 0.0)
    alpha = jnp.exp2(m_prev - m_new)
    l_ref[...] = alpha * l_ref[...] + jnp.sum(p, axis=0, keepdims=True)
    acc_ref[...] = alpha * acc_ref[...] + _dot(vt, p.astype(BF16))
    m_ref[...] = m_new


def _diff_kernel(tq, tk, qpos0, lam_init, qi_ref, kj_ref, q_ref, k_ref, vt_ref, lam_ref, nw_ref, o_ref,
                 qq_ref, m_ref, l_ref, acc_ref):
    step_id = pl.program_id(2)
    i, j = qi_ref[step_id], kj_ref[step_id]
    q_lo = qpos0 + i * tq
    last_j = (q_lo + tq - 1) // tk

    @pl.when(j == 0)
    def _():
        q = q_ref[...].astype(F32)
        lane = lax.broadcasted_iota(jnp.int32, (1, LANES), 1)
        qq_ref[...] = jnp.concatenate([jnp.where(lane < DIFF_QK, q, 0.0),
                                       jnp.where(lane >= DIFF_QK, q, 0.0)], axis=0).astype(BF16)
        m_ref[...] = jnp.full_like(m_ref, NEG)
        l_ref[...] = jnp.zeros_like(l_ref)
        acc_ref[...] = jnp.zeros_like(acc_ref)

    def step(masked):
        s = _dot_nt(k_ref[0], qq_ref[...])
        mask = None
        if masked:
            kpos = j * tk + lax.broadcasted_iota(jnp.int32, (tk, 1), 0)
            qpos = q_lo + lax.broadcasted_iota(jnp.int32, (1, tq), 1)
            mk = kpos <= qpos
            mask = jnp.concatenate([mk, mk], axis=1)
        _online_update_keymajor(s, mask, vt_ref[0, 0, 0], m_ref, l_ref, acc_ref)

    fully_visible = (j + 1) * tk - 1 <= q_lo

    @pl.when(fully_visible)
    def _():
        step(False)

    @pl.when(jnp.logical_not(fully_visible))
    def _():
        step(True)

    @pl.when(j == last_j)
    def _():
        l = l_ref[...]
        o12 = (acc_ref[...] / jnp.where(l > 0, l, 1.0)).T
        o_ref[...] = _diff_finish(o12, lam_ref, nw_ref, lam_init).astype(o_ref.dtype)


def _diff_attention(dq, kv, vt, lam_p, norm_w, batch, seq_len, qpos0, lam_init, tq, tk):
    m = dq.shape[0]
    nq = seq_len // tq
    nk = kv.shape[1] // tk
    assert vt.shape[2] == nk and vt.shape[4] == tk

    pairs = [(i, j) for i in range(nq) for j in range(min(nk - 1, (qpos0 + i * tq + tq - 1) // tk) + 1)]
    qi = jnp.asarray([p[0] for p in pairs], jnp.int32)
    kj = jnp.asarray([p[1] for p in pairs], jnp.int32)
    const = lambda b, h, s, qi, kj: (0, 0)
    return pl.pallas_call(
        functools.partial(_diff_kernel, tq, tk, qpos0, lam_init),
        out_shape=jax.ShapeDtypeStruct((m, DIFF_HEADS * DIFF_V), dq.dtype),
        grid_spec=pltpu.PrefetchScalarGridSpec(
            num_scalar_prefetch=2,
            grid=(batch, DIFF_HEADS, len(pairs)),
            in_specs=[pl.BlockSpec((tq, LANES), lambda b, h, s, qi, kj: (b * nq + qi[s], h)),
                      pl.BlockSpec((1, tk, LANES), lambda b, h, s, qi, kj: (b, kj[s], h)),
                      pl.BlockSpec((1, 1, 1, LANES, tk), lambda b, h, s, qi, kj: (b, h, kj[s], 0, 0)),
                      pl.BlockSpec((4, DIFF_QK), const), pl.BlockSpec((1, DIFF_V), const)],
            out_specs=pl.BlockSpec((tq, LANES), lambda b, h, s, qi, kj: (b * nq + qi[s], h)),
            scratch_shapes=[pltpu.VMEM((2 * tq, LANES), BF16), pltpu.VMEM((1, 2 * tq), F32),
                            pltpu.VMEM((1, 2 * tq), F32), pltpu.VMEM((DIFF_V, 2 * tq), F32)]),
        compiler_params=_cparams(("arbitrary",) * 3),
        name="diff_attention",
    )(qi, kj, dq, kv, vt, lam_p, norm_w.reshape(1, DIFF_V))


def _diff_finish(o12, lam_ref, nw_ref, lam_init):
    half = o12.shape[0] // 2
    lam_p = lam_ref[...]
    lam = (jnp.exp(jnp.sum(lam_p[0:1] * lam_p[1:2], axis=-1, keepdims=True))
           - jnp.exp(jnp.sum(lam_p[2:3] * lam_p[3:4], axis=-1, keepdims=True)) + lam_init)
    o = o12[0:half] - lam * o12[half:]
    y = o * lax.rsqrt(jnp.mean(o * o, axis=-1, keepdims=True) + NORM_EPS) * nw_ref[...]
    return y * (1.0 - lam_init)


def _diff_paged_kernel(pps, lam_init, pt_ref, q_ref, new_ref, lam_ref, nw_ref, *rest):
    page_refs, o_ref = rest[:pps], rest[pps]
    qbd_ref, m_ref, l_ref, acc_ref = rest[pps + 1:]
    j = pl.program_id(1)
    n_new = q_ref.shape[0]
    rows_h = 2 * n_new
    hw = DIFF_HEADS * LANES

    @pl.when(j == 0)
    def _():
        q = q_ref[...]
        lane = lax.broadcasted_iota(jnp.int32, (1, LANES), 1)
        blocks = []
        for h in range(DIFF_HEADS):
            qh = q[:, h * LANES:(h + 1) * LANES]
            q12 = jnp.concatenate([jnp.where(lane < DIFF_QK, qh, 0.0), jnp.where(lane >= DIFF_QK, qh, 0.0)], axis=0)
            blocks.append(jnp.concatenate(
                [q12 if hh == h else jnp.zeros((rows_h, LANES), F32) for hh in range(DIFF_HEADS)], axis=1))
        qbd_ref[...] = jnp.concatenate(blocks, axis=0).astype(BF16)
        m_ref[...] = jnp.full_like(m_ref, NEG)
        l_ref[...] = jnp.zeros_like(l_ref)
        acc_ref[...] = jnp.zeros_like(acc_ref)

    k = _page_slabs(page_refs, 0, DIFF_HEADS, 2 * DIFF_HEADS)
    v = _page_slabs(page_refs, DIFF_HEADS, DIFF_HEADS, 2 * DIFF_HEADS)
    _online_update(_dot_nt(qbd_ref[...], k), None, v, m_ref, l_ref, acc_ref)

    @pl.when(j == pl.num_programs(1) - 1)
    def _():
        kn = _pad_rows(_token_slabs(new_ref, 0, DIFF_HEADS, 2 * DIFF_HEADS), BF16_ROWS).astype(BF16)
        vn = _pad_rows(_token_slabs(new_ref, DIFF_HEADS, DIFF_HEADS, 2 * DIFF_HEADS), BF16_ROWS).astype(BF16)
        t = lax.broadcasted_iota(jnp.int32, (DIFF_HEADS * rows_h, 1), 0) % n_new
        u = lax.broadcasted_iota(jnp.int32, (1, kn.shape[0]), 1)
        _online_update(_dot_nt(qbd_ref[...], kn), u <= t, vn, m_ref, l_ref, acc_ref)
        l = l_ref[...]
        o12 = acc_ref[...] / jnp.where(l > 0, l, 1.0)
        for h in range(DIFF_HEADS):
            r = slice(h * rows_h, (h + 1) * rows_h)
            o_ref[:, h * LANES:(h + 1) * LANES] = _diff_finish(o12[r, h * LANES:(h + 1) * LANES], lam_ref, nw_ref,
                                                               lam_init)


def _diff_attention_paged(dq, new_kv, cache_view, layer, page_table, lam_p, norm_w, lam_init, pps):
    b, n_pages = page_table.shape
    n_new = dq.shape[0] // b
    hw = DIFF_HEADS * LANES
    rows = DIFF_HEADS * 2 * n_new
    assert n_pages % pps == 0 and n_new % SUBLANES == 0
    const = lambda i, j, pt: (0, 0)
    return pl.pallas_call(
        functools.partial(_diff_paged_kernel, pps, lam_init),
        out_shape=jax.ShapeDtypeStruct(dq.shape, F32),
        grid_spec=pltpu.PrefetchScalarGridSpec(
            num_scalar_prefetch=1,
            grid=(b, n_pages // pps),
            in_specs=[pl.BlockSpec((n_new, hw), lambda i, j, pt: (i, 0)),
                      pl.BlockSpec((n_new * 2 * DIFF_HEADS, LANES), lambda i, j, pt: (i, 0)),
                      pl.BlockSpec((4, DIFF_QK), const), pl.BlockSpec((1, DIFF_V), const)]
                     + _page_specs(layer, PAGE_SIZE * 2 * DIFF_HEADS, pps),
            out_specs=pl.BlockSpec((n_new, hw), lambda i, j, pt: (i, 0)),
            scratch_shapes=[pltpu.VMEM((rows, hw), BF16), pltpu.VMEM((rows, 1), F32), pltpu.VMEM((rows, 1), F32),
                            pltpu.VMEM((rows, hw), F32)]),
        compiler_params=_cparams(("arbitrary", "arbitrary")),
        name="diff_attention_paged",
    )(page_table, dq, new_kv, lam_p, norm_w.reshape(1, DIFF_V), *([cache_view] * pps))


def _compress_weights(w, pe):
    k = NSA_CMP_STRIDE * HEAD_DIM
    return (jnp.concatenate([w[:, :k], w[:, k:]], axis=2).astype(BF16),
            pe.reshape(2, NSA_CMP_BLOCK // NSA_CMP_STRIDE, k))


def _compress_blocks(segments, rows, w_ref, pe_ref, o_ref):
    for c in range(2):
        pe = _dot(_pad_rows(pe_ref[c], BF16_ROWS).astype(BF16), w_ref[c])
        pe_term = pe[0:1, 0:HEAD_DIM] + pe[1:2, HEAD_DIM:2 * HEAD_DIM]
        for g in range(NSA_KV_HEADS):
            both = _dot(segments(c * NSA_KV_HEADS + g), w_ref[c])
            o_ref[0, c, g] = (both[:, 0:HEAD_DIM] + pltpu.roll(both[:, HEAD_DIM:2 * HEAD_DIM], rows - 1, 0)
                              + pe_term).astype(BF16)


def _compress_kernel(x_ref, w_ref, pe_ref, o_ref):
    tok_w = 2 * NSA_KV_HEADS * HEAD_DIM

    def segments(slot):
        return jnp.concatenate([x_ref[0, :, t * tok_w + slot * HEAD_DIM:t * tok_w + (slot + 1) * HEAD_DIM]
                                for t in range(NSA_CMP_STRIDE)], axis=1)

    _compress_blocks(segments, x_ref.shape[1], w_ref, pe_ref, o_ref)


def _compress_paged_kernel(pps, n_past_seg, pt_ref, new_ref, w_ref, pe_ref, *rest):
    page_refs, o_ref, x_ref = rest[:pps], rest[pps], rest[pps + 1]
    j = pl.program_id(1)
    slots = 2 * NSA_KV_HEADS
    seg_per_page = PAGE_SIZE // NSA_CMP_STRIDE
    rows = x_ref.shape[1]

    for pp in range(0, pps, 2):
        row0 = pl.multiple_of((j * pps + pp) * seg_per_page, 2 * seg_per_page)
        for slot in range(slots):
            for t in range(NSA_CMP_STRIDE):
                pair = [page_refs[pp + q][0, 0, pl.ds(t * slots + slot, seg_per_page, stride=NSA_CMP_STRIDE * slots), :]
                        for q in range(2)]
                x_ref[slot, pl.ds(row0, 2 * seg_per_page), t * HEAD_DIM:(t + 1) * HEAD_DIM] = (
                    jnp.concatenate(pair, axis=0).astype(BF16))

    @pl.when(j == pl.num_programs(1) - 1)
    def _():
        n_new = new_ref.shape[0] // slots
        first = lax.broadcasted_iota(jnp.int32, (rows - n_past_seg, 1), 0) == 0
        for slot in range(slots):
            for t in range(NSA_CMP_STRIDE):
                if t < n_new:
                    tail = jnp.where(first, new_ref[t * slots + slot:t * slots + slot + 1, :], 0.0)
                else:
                    tail = jnp.zeros((rows - n_past_seg, HEAD_DIM), F32)
                x_ref[slot, n_past_seg:rows, t * HEAD_DIM:(t + 1) * HEAD_DIM] = tail.astype(BF16)
        _compress_blocks(lambda slot: x_ref[slot], rows, w_ref, pe_ref, o_ref)


def _compress_paged(cache_view, layer, page_table, new_rows, w, pe, pps):
    b, n_pages = page_table.shape
    slots = 2 * NSA_KV_HEADS
    n_new = new_rows.shape[0] // (b * slots)
    seg_per_page = PAGE_SIZE // NSA_CMP_STRIDE
    n_past_seg = n_pages * seg_per_page
    rows = n_past_seg + 2 * BF16_ROWS
    assert n_new <= NSA_CMP_STRIDE and pps % 2 == 0 and n_pages % pps == 0
    return pl.pallas_call(
        functools.partial(_compress_paged_kernel, pps, n_past_seg),
        out_shape=jax.ShapeDtypeStruct((b, 2, NSA_KV_HEADS, rows, HEAD_DIM), BF16),
        grid_spec=pltpu.PrefetchScalarGridSpec(
            num_scalar_prefetch=1,
            grid=(b, n_pages // pps),
            in_specs=[pl.BlockSpec((n_new * slots, HEAD_DIM), lambda i, j, pt: (i, 0)),
                      pl.BlockSpec(w.shape, lambda i, j, pt: (0, 0, 0)),
                      pl.BlockSpec(pe.shape, lambda i, j, pt: (0, 0, 0))]
                     + _page_specs(layer, PAGE_SIZE * slots, pps),
            out_specs=pl.BlockSpec((1, 2, NSA_KV_HEADS, rows, HEAD_DIM), lambda i, j, pt: (i, 0, 0, 0, 0)),
            scratch_shapes=[pltpu.VMEM((slots, rows, NSA_CMP_STRIDE * HEAD_DIM), BF16)]),
        compiler_params=_cparams(("arbitrary", "arbitrary")),
        name="nsa_compress_paged",
    )(page_table, new_rows, w, pe, *([cache_view] * pps))


def _compress(seg, w, pe):
    b, rows, width = seg.shape
    return pl.pallas_call(
        _compress_kernel,
        out_shape=jax.ShapeDtypeStruct((b, 2, NSA_KV_HEADS, rows, HEAD_DIM), BF16),
        grid=(b,),
        in_specs=[pl.BlockSpec((1, rows, width), lambda i: (i, 0, 0)),
                  pl.BlockSpec(w.shape, lambda i: (0, 0, 0)),
                  pl.BlockSpec(pe.shape, lambda i: (0, 0, 0))],
        out_specs=pl.BlockSpec((1, 2, NSA_KV_HEADS, rows, HEAD_DIM), lambda i: (i, 0, 0, 0, 0)),
        compiler_params=_cparams(("arbitrary",)),
        name="nsa_compress",
    )(seg, w, pe)


def _nsa_band(n_cmp_rows, n_slc_pad):
    ratio = NSA_SEL_BLOCK // NSA_CMP_STRIDE
    span = NSA_CMP_BLOCK // NSA_CMP_STRIDE
    n = np.arange(n_cmp_rows)[:, None]
    j = np.arange(n_slc_pad)[None, :]
    return ((n >= ratio * j - (span - 1)) & (n <= ratio * j + ratio - 1)).astype(np.float32)


def _nsa_compressed(qs, qpos_rows, ck, cv):
    cmp_end = lax.broadcasted_iota(jnp.int32, (1, ck.shape[0]), 1) * NSA_CMP_STRIDE + (NSA_CMP_BLOCK - 1)
    p_c = _masked_softmax(_dot_nt(qs, ck), cmp_end <= qpos_rows)
    return p_c, _dot(p_c.astype(BF16), cv)


def _nsa_block_scores(p_c, qpos, band_ref):
    tq = qpos.shape[0]
    imp = p_c[0:tq]
    for n in range(1, p_c.shape[0] // tq):
        imp = imp + p_c[n * tq:(n + 1) * tq]
    p_slc = _dot3_rhs_exact(_pad_rows(imp, BF16_ROWS), band_ref[...])[0:tq]
    blk = lax.broadcasted_iota(jnp.int32, (1, band_ref.shape[1]), 1)
    cur = qpos // NSA_SEL_BLOCK
    valid = blk * NSA_SEL_BLOCK <= qpos
    forced = (blk == 0) | (blk == cur) | (blk == cur - 1)
    return jnp.where(valid, p_slc + jnp.where(forced, NSA_FORCE_BONUS, 0.0), NEG)


def _nsa_select(score, score_ref, n_slc):
    score_t = score.T
    score_ref[...] = score_t
    blk_t = lax.broadcasted_iota(jnp.int32, (score_t.shape[0], 1), 0)

    def count(jp, cnt):
        row = score_ref[pl.ds(jp, 1), :]
        beats = (row > score_t) | ((row == score_t) & (jp < blk_t))
        return cnt + jnp.where(beats, 1.0, 0.0)

    cnt = lax.fori_loop(0, n_slc, count, jnp.zeros(score_t.shape, F32))
    return jnp.where((cnt < NSA_TOP_N) & (score_t > 0.5 * NEG), 1.0, 0.0)


def _nsa_select_few(score, score_ref, n_slc):
    nq, nsp = score.shape
    fold = LANES // nq
    rows = nsp // fold
    score_t = _pad_rows(score, LANES).T[:, 0:nq]
    score_ref[...] = jnp.concatenate([score_t] * fold, axis=1)
    packed = jnp.concatenate([score_t[g * rows:(g + 1) * rows] for g in range(fold)], axis=1)
    blk = (lax.broadcasted_iota(jnp.int32, (1, LANES), 1) // nq) * rows + lax.broadcasted_iota(jnp.int32, (rows, 1), 0)

    def count(jp, cnt):
        row = score_ref[pl.ds(jp, 1), :]
        beats = (row > packed) | ((row == packed) & (jp < blk))
        return cnt + jnp.where(beats, 1.0, 0.0)

    cnt = lax.fori_loop(0, n_slc, count, jnp.zeros(packed.shape, F32))
    sel_p = jnp.where((cnt < NSA_TOP_N) & (packed > 0.5 * NEG), 1.0, 0.0)
    sel_t = _pad_rows(sel_p, LANES).T
    return jnp.concatenate([sel_t[g * nq:(g + 1) * nq, 0:rows] for g in range(fold)], axis=1)


def _nsa_window(qs, qpos_rows, kw, vw, first_pos):
    wpos = first_pos + lax.broadcasted_iota(jnp.int32, (1, kw.shape[0]), 1)
    mask_w = (wpos <= qpos_rows) & (wpos > qpos_rows - NSA_WINDOW) & (wpos >= 0)
    return _dot(_masked_softmax(_dot_nt(qs, kw), mask_w).astype(BF16), vw)


def _nsa_kernel(tq, tk, qpos0, n_slc, q_ref, sm_ref, ck_ref, cv_ref, sk_ref, svt_ref, wk_ref, wv_ref,
                band_ref, o_ref, score_ref, sel_ref, m_ref, l_ref, acc_ref):
    i = pl.program_id(1)
    hpg = NSA_HEADS // NSA_KV_HEADS
    q_lo = qpos0 + i * tq
    qpos = q_lo + lax.broadcasted_iota(jnp.int32, (tq, 1), 0)
    qpos_lanes = q_lo + lax.broadcasted_iota(jnp.int32, (1, tq), 1)
    qpos_rows = jnp.concatenate([qpos] * hpg, axis=0)
    gates = _sigmoid(sm_ref[:, SMALL_GATE:SMALL_GATE + 3 * NSA_HEADS])
    blocks_per_tile = tk // NSA_SEL_BLOCK

    for g in range(NSA_KV_HEADS):
        q = q_ref[:, g * hpg * LANES:(g + 1) * hpg * LANES]
        qs = jnp.concatenate([q[:, n * LANES:(n + 1) * LANES] for n in range(hpg)], axis=0).astype(BF16)

        p_c, o_c = _nsa_compressed(qs, qpos_rows, ck_ref[0, 0, g], cv_ref[0, 0, g])
        n_rank = jnp.minimum((q_lo + tq - 1) // NSA_SEL_BLOCK + 1, n_slc)
        sel_ref[...] = _nsa_select(_nsa_block_scores(p_c, qpos, band_ref), score_ref, n_rank)

        m_ref[...] = jnp.full_like(m_ref, NEG)
        l_ref[...] = jnp.zeros_like(l_ref)
        acc_ref[...] = jnp.zeros_like(acc_ref)

        def key_tile(kt, carry):
            start = pl.multiple_of(kt * tk, tk)
            k = sk_ref[0, pl.ds(start, tk), g * LANES:(g + 1) * LANES]
            blocks = sel_ref[pl.ds(pl.multiple_of(kt * blocks_per_tile, blocks_per_tile), blocks_per_tile), :]
            picked = jnp.concatenate([jnp.broadcast_to(blocks[r:r + 1], (NSA_SEL_BLOCK, tq))
                                      for r in range(blocks_per_tile)], axis=0) > 0.5
            kpos = kt * tk + lax.broadcasted_iota(jnp.int32, (tk, 1), 0)
            mk = picked & (kpos <= qpos_lanes)
            mask = jnp.concatenate([mk] * hpg, axis=1)
            _online_update_keymajor(_dot_nt(k, qs), mask, svt_ref[0, g, kt], m_ref, l_ref, acc_ref)
            return carry

        lax.fori_loop(0, (q_lo + tq - 1) // tk + 1, key_tile, 0)
        l = l_ref[...]
        o_st = acc_ref[...] / jnp.where(l > 0, l, 1.0)
        o_s = jnp.concatenate([o_st[:, n * tq:(n + 1) * tq].T for n in range(hpg)], axis=0)

        wrows = -(-(tq + NSA_WINDOW) // BF16_ROWS) * BF16_ROWS
        wstart = pl.multiple_of(i * tq, tq)
        o_w = _nsa_window(qs, qpos_rows, wk_ref[0, pl.ds(wstart, wrows), g * LANES:(g + 1) * LANES],
                          wv_ref[0, pl.ds(wstart, wrows), g * LANES:(g + 1) * LANES], q_lo - NSA_WINDOW)

        for n in range(hpg):
            h = g * hpg + n
            r = slice(n * tq, (n + 1) * tq)
            o = (gates[:, 3 * h:3 * h + 1] * o_c[r] + gates[:, 3 * h + 1:3 * h + 2] * o_s[r]
                 + gates[:, 3 * h + 2:3 * h + 3] * o_w[r])
            o_ref[:, h * LANES:(h + 1) * LANES] = o.astype(o_ref.dtype)


def _nsa_attention(qn, z_small, comp, slc_buf, slc_vt, win_all, batch, seq_len, qpos0, tq, tk):
    m = qn.shape[0]
    nq = seq_len // tq
    t_keys = slc_buf.shape[1]
    n_cmp_rows = comp.shape[3]
    n_slc = -(-(qpos0 + seq_len) // NSA_SEL_BLOCK)
    nsp = -(-n_slc // LANES) * LANES
    assert tq % LANES == 0 and tk % NSA_SEL_BLOCK == 0 and slc_vt.shape[2] * tk == t_keys and nsp * NSA_SEL_BLOCK >= t_keys
    band = _nsa_band(n_cmp_rows, nsp)
    hw = NSA_HEADS * HEAD_DIM
    kvw = NSA_KV_HEADS * HEAD_DIM
    rows = (NSA_HEADS // NSA_KV_HEADS) * tq
    wlen = win_all.shape[1]
    return pl.pallas_call(
        functools.partial(_nsa_kernel, tq, tk, qpos0, n_slc),
        out_shape=jax.ShapeDtypeStruct((m, hw), qn.dtype),
        grid=(batch, nq),
        in_specs=[pl.BlockSpec((tq, hw), lambda b, i: (b * nq + i, 0)),
                  pl.BlockSpec((tq, LANES), lambda b, i: (b * nq + i, 0)),
                  pl.BlockSpec((1, 1, NSA_KV_HEADS, n_cmp_rows, HEAD_DIM), lambda b, i: (b, 0, 0, 0, 0)),
                  pl.BlockSpec((1, 1, NSA_KV_HEADS, n_cmp_rows, HEAD_DIM), lambda b, i: (b, 1, 0, 0, 0)),
                  pl.BlockSpec((1, t_keys, kvw), lambda b, i: (b, 0, 0)),
                  pl.BlockSpec((1,) + slc_vt.shape[1:], lambda b, i: (b, 0, 0, 0, 0)),
                  pl.BlockSpec((1, wlen, kvw), lambda b, i: (b, 0, 0)),
                  pl.BlockSpec((1, wlen, kvw), lambda b, i: (b, 0, 1)),
                  pl.BlockSpec((n_cmp_rows, nsp), lambda b, i: (0, 0))],
        out_specs=pl.BlockSpec((tq, hw), lambda b, i: (b * nq + i, 0)),
        scratch_shapes=[pltpu.VMEM((nsp, tq), F32), pltpu.VMEM((nsp, tq), F32), pltpu.VMEM((1, rows), F32),
                        pltpu.VMEM((1, rows), F32), pltpu.VMEM((HEAD_DIM, rows), F32)],
        compiler_params=_cparams(("arbitrary", "arbitrary")),
        name="nsa_attention",
    )(qn, z_small, comp, comp, slc_buf, slc_vt, win_all, win_all, jnp.asarray(band, BF16))


def _nsa_paged_kernel(pps, qpos0, n_slc, pt_ref, q_ref, sm_ref, ck_ref, cv_ref, new_ref, wst_ref, wnew_ref, band_ref,
                      *rest):
    page_refs, o_ref = rest[:pps], rest[pps]
    score_ref, qbd_ref, sel_ref, m_ref, l_ref, acc_ref, oc_ref, ow_ref = rest[pps + 1:]
    j = pl.program_id(1)
    n_new = q_ref.shape[0]
    groups = NSA_KV_HEADS
    hpg = NSA_HEADS // groups
    rows_g = hpg * n_new
    nsp = band_ref.shape[1]
    wb = wst_ref.shape[2] // (2 * groups)
    qpos = qpos0 + lax.broadcasted_iota(jnp.int32, (n_new, 1), 0)

    @pl.when(j == 0)
    def _():
        q = q_ref[...]
        qpos_rows = jnp.concatenate([qpos] * hpg, axis=0)
        scores, blocks = [], []
        for g in range(groups):
            qf = jnp.concatenate([q[:, (g * hpg + n) * LANES:(g * hpg + n + 1) * LANES] for n in range(hpg)], axis=0)
            qs = qf.astype(BF16)
            p_c, o_c = _nsa_compressed(qs, qpos_rows, ck_ref[0, 0, g], cv_ref[0, 0, g])
            oc_ref[g * rows_g:(g + 1) * rows_g] = o_c
            scores.append(_nsa_block_scores(p_c, qpos, band_ref))
            kw, vw = (_pad_rows(jnp.concatenate(
                [wst_ref[0, 0, pl.ds(slot, wb, stride=2 * groups), :], wnew_ref[pl.ds(slot, n_new, stride=2 * groups), :]],
                axis=0), BF16_ROWS).astype(BF16) for slot in (g, groups + g))
            ow_ref[g * rows_g:(g + 1) * rows_g] = _nsa_window(qs, qpos_rows, kw, vw, qpos0 - wb)
            blocks.append(jnp.concatenate(
                [qf if gg == g else jnp.zeros((rows_g, LANES), F32) for gg in range(groups)], axis=1))
        qbd_ref[...] = jnp.concatenate(blocks, axis=0).astype(BF16)
        sel = _nsa_select_few(jnp.concatenate(scores, axis=0), score_ref, n_slc)
        sel_ref[...] = jnp.concatenate(
            [sel[g * n_new:(g + 1) * n_new] for g in range(groups) for _ in range(hpg)], axis=0).astype(BF16)
        m_ref[...] = jnp.full_like(m_ref, NEG)
        l_ref[...] = jnp.zeros_like(l_ref)
        acc_ref[...] = jnp.zeros_like(acc_ref)

    k = _page_slabs(page_refs, 0, groups, 2 * groups)
    v = _page_slabs(page_refs, groups, groups, 2 * groups)
    n_keys = pps * PAGE_SIZE
    key_blk = (j * n_keys + lax.broadcasted_iota(jnp.int32, (1, n_keys), 1)) // NSA_SEL_BLOCK
    expand = jnp.where(lax.broadcasted_iota(jnp.int32, (nsp, 1), 0) == key_blk, 1.0, 0.0).astype(BF16)
    picked = _dot(sel_ref[...], expand) > 0.5
    _online_update(_dot_nt(qbd_ref[...], k), picked, v, m_ref, l_ref, acc_ref)

    @pl.when(j == pl.num_programs(1) - 1)
    def _():
        kn = _pad_rows(_token_slabs(new_ref, 0, groups, 2 * groups), BF16_ROWS).astype(BF16)
        vn = _pad_rows(_token_slabs(new_ref, groups, groups, 2 * groups), BF16_ROWS).astype(BF16)
        new_blk = qpos0 // NSA_SEL_BLOCK
        t = lax.broadcasted_iota(jnp.int32, (groups * rows_g, 1), 0) % n_new
        u = lax.broadcasted_iota(jnp.int32, (1, kn.shape[0]), 1)
        mask = (u <= t) & (sel_ref[:, new_blk:new_blk + 1] > 0.5)
        _online_update(_dot_nt(qbd_ref[...], kn), mask, vn, m_ref, l_ref, acc_ref)
        l = l_ref[...]
        o_sel = acc_ref[...] / jnp.where(l > 0, l, 1.0)
        gates = _sigmoid(sm_ref[:, SMALL_GATE:SMALL_GATE + 3 * NSA_HEADS])
        for g in range(groups):
            for n in range(hpg):
                h = g * hpg + n
                r = slice(g * rows_g + n * n_new, g * rows_g + (n + 1) * n_new)
                o_ref[:, h * LANES:(h + 1) * LANES] = (
                    gates[:, 3 * h:3 * h + 1] * oc_ref[r] + gates[:, 3 * h + 1:3 * h + 2] * o_sel[r, g * LANES:(g + 1) * LANES]
                    + gates[:, 3 * h + 2:3 * h + 3] * ow_ref[r])


def _nsa_attention_paged(qn, z_small, comp, new_kv, win_state, win_new, cache_view, layer, page_table, qpos0, pps):
    b, n_pages = page_table.shape
    n_new = qn.shape[0] // b
    groups = NSA_KV_HEADS
    hw = NSA_HEADS * HEAD_DIM
    gw = groups * HEAD_DIM
    rows = NSA_HEADS * n_new
    n_cmp_rows = comp.shape[3]
    n_slc = -(-(qpos0 + n_new) // NSA_SEL_BLOCK)
    nsp = -(-n_slc // LANES) * LANES
    wrows = win_state.shape[2]
    assert wrows == 2 * groups * min(NSA_WINDOW, qpos0)
    assert n_pages % pps == 0 and qpos0 == n_pages * PAGE_SIZE and n_new <= NSA_SEL_BLOCK
    assert qpos0 % NSA_SEL_BLOCK == 0 and LANES % (groups * n_new) == 0 and n_new % SUBLANES == 0
    assert nsp % (SUBLANES * LANES // (groups * n_new)) == 0
    band = _nsa_band(n_cmp_rows, nsp)
    const = lambda i, j, pt: (0, 0)
    return pl.pallas_call(
        functools.partial(_nsa_paged_kernel, pps, qpos0, n_slc),
        out_shape=jax.ShapeDtypeStruct(qn.shape, F32),
        grid_spec=pltpu.PrefetchScalarGridSpec(
            num_scalar_prefetch=1,
            grid=(b, n_pages // pps),
            in_specs=[pl.BlockSpec((n_new, hw), lambda i, j, pt: (i, 0)),
                      pl.BlockSpec((n_new, LANES), lambda i, j, pt: (i, 0)),
                      pl.BlockSpec((1, 1, groups, n_cmp_rows, HEAD_DIM), lambda i, j, pt: (i, 0, 0, 0, 0)),
                      pl.BlockSpec((1, 1, groups, n_cmp_rows, HEAD_DIM), lambda i, j, pt: (i, 1, 0, 0, 0)),
                      pl.BlockSpec((n_new * 2 * groups, LANES), lambda i, j, pt: (i, 0)),
                      pl.BlockSpec((1, 1, wrows, LANES), lambda i, j, pt: (layer, i, 0, 0)),
                      pl.BlockSpec((n_new * 2 * groups, LANES), lambda i, j, pt: (i, 0)),
                      pl.BlockSpec((n_cmp_rows, nsp), const)]
                     + _page_specs(layer, PAGE_SIZE * 2 * groups, pps),
            out_specs=pl.BlockSpec((n_new, hw), lambda i, j, pt: (i, 0)),
            scratch_shapes=[pltpu.VMEM((nsp, LANES), F32), pltpu.VMEM((rows, gw), BF16), pltpu.VMEM((rows, nsp), BF16),
                            pltpu.VMEM((rows, 1), F32), pltpu.VMEM((rows, 1), F32), pltpu.VMEM((rows, gw), F32),
                            pltpu.VMEM((rows, HEAD_DIM), F32), pltpu.VMEM((rows, HEAD_DIM), F32)]),
        compiler_params=_cparams(("arbitrary", "arbitrary")),
        name="nsa_attention_paged",
    )(page_table, qn, z_small, comp, comp, new_kv, win_state, win_new, jnp.asarray(band, BF16),
      *([cache_view] * pps))


def _out_proj_kernel(x_ref, og_ref, on_ref, od_ref, w_ref, nw_ref, g_ref, o_ref):
    a = jnp.concatenate([og_ref[...].astype(BF16), on_ref[...].astype(BF16), od_ref[...].astype(BF16)], axis=-1)
    mix = _dot(a, w_ref[0])
    y = mix * lax.rsqrt(jnp.mean(mix * mix, axis=-1, keepdims=True) + NORM_EPS) * nw_ref[...]
    o_ref[...] = x_ref[...] + g_ref[0] * y


def _out_proj(x2, o_gla, o_nsa, o_d, w, layer, nw, gate, seq_len, tm):
    m, d = x2.shape
    row = lambda width: pl.BlockSpec((tm, width), lambda i: (i, 0))
    return pl.pallas_call(
        _out_proj_kernel,
        out_shape=jax.ShapeDtypeStruct((m, d), F32),
        grid=(m // tm,),
        in_specs=[row(d), row(o_gla.shape[1]), row(o_nsa.shape[1]), row(o_d.shape[1]),
                  pl.BlockSpec((1,) + w.shape[1:], lambda i: (layer, 0, 0)), pl.BlockSpec((1, d), lambda i: (0, 0)),
                  _mod_spec(seq_len, tm, d)],
        out_specs=row(d),
        compiler_params=_cparams(("arbitrary",)),
        name="out_proj",
    )(x2, o_gla, o_nsa, o_d, w, nw, _expand_mod(gate, seq_len, tm))


def _ffn_kernel(x_ref, nw2_ref, sc_ref, sh_ref, wg_ref, wu_ref, wo_ref, nw3_ref, g_ref, o_ref, h_ref, acc_ref):
    j = pl.program_id(1)

    @pl.when(j == 0)
    def _():
        h_ref[...] = _norm_mod(x_ref[...], nw2_ref[...], sc_ref[0], sh_ref[0]).astype(BF16)
        acc_ref[...] = jnp.zeros_like(acc_ref)

    h = h_ref[...]
    gate = _dot(h, wg_ref[0])
    up = _dot(h, wu_ref[0])
    acc_ref[...] += _dot((gate * _sigmoid(gate) * up).astype(BF16), wo_ref[0])

    @pl.when(j == pl.num_programs(1) - 1)
    def _():
        f = acc_ref[...]
        y = f * lax.rsqrt(jnp.mean(f * f, axis=-1, keepdims=True) + NORM_EPS) * nw3_ref[...]
        o_ref[...] = x_ref[...] + g_ref[0] * y


def _ffn(x2, nw2, sc, sh, w_in, w_out, layer, nw3, gate, seq_len, tm, tf):
    m, d = x2.shape
    d_ff = w_out.shape[1]
    nf = d_ff // tf
    row = pl.BlockSpec((tm, d), lambda i, j: (i, 0))
    vec = pl.BlockSpec((1, d), lambda i, j: (0, 0))
    return pl.pallas_call(
        _ffn_kernel,
        out_shape=jax.ShapeDtypeStruct((m, d), F32),
        grid=(m // tm, nf),
        in_specs=[row, vec, _mod_spec(seq_len, tm, d), _mod_spec(seq_len, tm, d),
                  pl.BlockSpec((1, d, tf), lambda i, j: (layer, 0, j)),
                  pl.BlockSpec((1, d, tf), lambda i, j: (layer, 0, nf + j)),
                  pl.BlockSpec((1, tf, d), lambda i, j: (layer, j, 0)),
                  vec, _mod_spec(seq_len, tm, d)],
        out_specs=row,
        scratch_shapes=[pltpu.VMEM((tm, d), BF16), pltpu.VMEM((tm, d), F32)],
        compiler_params=_cparams(("arbitrary", "arbitrary")),
        name="ffn",
    )(x2, nw2, _expand_mod(sc, seq_len, tm), _expand_mod(sh, seq_len, tm), w_in, w_in, w_out, nw3,
      _expand_mod(gate, seq_len, tm))


def _pick(n, target):
    if n <= target:
        return n
    for t in range(target, 7, -1):
        if n % t == 0 and t % SUBLANES == 0:
            return t
    return n


def _permute_w_in(w):
    depth, d = w.shape[:2]
    hk, hv = GLA_HEADS * GLA_DK, GLA_HEADS * GLA_DV
    kvw = NSA_KV_HEADS * HEAD_DIM
    sizes = [hk, hk, hv, hv, GLA_GATE_RANK, NSA_HEADS * HEAD_DIM, kvw, kvw, kvw, kvw, kvw, kvw, NSA_HEADS * 3,
             DIFF_HEADS * 2 * DIFF_QK, DIFF_HEADS * 2 * DIFF_QK, DIFF_HEADS * DIFF_V]
    offs = np.concatenate([[0], np.cumsum(sizes)])
    part = lambda k: w[:, :, offs[k]:offs[k + 1]]
    main = jnp.concatenate([part(k) for k in (5, 0, 1, 2, 3, 6, 7, 8, 9, 10, 11, 13, 14, 15)], axis=2)
    pad = LANES - GLA_GATE_RANK - NSA_HEADS * 3
    small = jnp.concatenate([part(4), part(12), jnp.zeros((depth, d, pad), w.dtype)], axis=2)
    return main.astype(BF16), small.astype(BF16)


def _layer(x, mod, qpos0, past, gla_s0, lp, lam_init):
    b, seq_len, d = x.shape
    m = b * seq_len
    x2 = x.reshape(m, d)
    sh1, sc1, g1, sh2, sc2, g2 = jnp.split(mod, 6, axis=-1)
    tm = _pick(seq_len, 512) if seq_len >= 128 else m
    tm_in = _pick(seq_len, 1024) if seq_len >= 128 else m
    layer = lp['layer']
    z, z_small = _in_proj(x2, lp['norm'][0:1], sc1, sh1, *lp['w_in'], layer, seq_len, tm_in, _pick(Z_WIDTH, 512))

    tab = _rope_tables(qpos0 + np.arange(seq_len))
    if seq_len % tm != 0:
        tab = np.tile(tab, (m // seq_len, 1))
    tab = jnp.asarray(tab)
    act_dt = BF16 if seq_len % BF16_ROWS == 0 else F32
    (qn, cmp_f, cmp_b, slc_f, slc_b, win_f, win_b, dq, dkv_f, dkv_b, *vts) = _rope_split(
        z, tab, seq_len, tm, act_dt, transposed_values=past is None)

    chunk = GLA_CHUNK if seq_len % GLA_CHUNK == 0 else seq_len
    o_gla, gla_state = _gla(z, z_small, lp['gla_gate_w'], lp['gla_gate_b'], lp['gla_norm'], gla_s0, b, seq_len,
                            chunk, act_dt)

    kvw = 2 * NSA_KV_HEADS * HEAD_DIM
    dw = 2 * DIFF_HEADS * DIFF_V

    if past is None:
        slc_vt, diff_vt = vts
        win_pad = jnp.zeros((b, -(NSA_WINDOW + seq_len) % BF16_ROWS, kvw), BF16)
        win_all = jnp.concatenate([jnp.zeros((b, NSA_WINDOW, kvw), BF16), win_b.reshape(b, seq_len, kvw), win_pad],
                                  axis=1)
        seg = cmp_b.reshape(b, seq_len // NSA_CMP_STRIDE, NSA_CMP_STRIDE * kvw)
        comp = _compress(seg, lp['nsa_cmp_w'], lp['nsa_cmp_pe'])
        o_nsa = _nsa_attention(qn, z_small, comp, slc_b.reshape(b, seq_len, kvw), slc_vt, win_all, b, seq_len, qpos0,
                               _pick(seq_len, 256), tm)
        o_d = _diff_attention(dq, dkv_b.reshape(b, seq_len, dw), diff_vt, lp['diff_lambda'], lp['diff_norm'], b,
                              seq_len, qpos0, lam_init, _pick(seq_len, 512), tm)
    else:
        pt = past['page_table']
        pps = max(p for p in (PAGES_PER_STEP, 8, 4, 2) if pt.shape[1] % p == 0)
        pps_attn = max(p for p in (2 * PAGES_PER_STEP, pps) if pt.shape[1] % p == 0)
        comp = _compress_paged(past['cmp'], layer, pt, cmp_f, lp['nsa_cmp_w'], lp['nsa_cmp_pe'], pps)
        o_nsa = _nsa_attention_paged(qn, z_small, comp, slc_f, past['win'], win_f, past['slc'], layer, pt, qpos0,
                                     pps_attn)
        o_d = _diff_attention_paged(dq, dkv_f, past['diff'], layer, pt, lp['diff_lambda'], lp['diff_norm'],
                                    lam_init, pps_attn)

    x1 = _out_proj(x2, o_gla, o_nsa, o_d, lp['w_out'], layer, lp['norm'][1:2], g1, seq_len, tm)
    d_ff = lp['ffn_w_out'].shape[1]
    x2n = _ffn(x1, lp['norm'][2:3], sc2, sh2, lp['ffn_w_in'], lp['ffn_w_out'], layer, lp['norm'][3:4], g2, seq_len,
               tm, _pick(d_ff, 512) if d_ff % LANES == 0 else d_ff)

    return (x2n.reshape(b, seq_len, d), cmp_f, slc_f, dkv_f, win_f, gla_state)


def kernel(x_prompt, x_sample, c_prompt, c_sample, cache_nsa_cmp_kv, cache_nsa_slc_kv, cache_diff_kv,
           state_nsa_win_kv, state_gla, page_table, ada_w, ada_b, norm_w, w_in, gla_gate_w, gla_gate_b,
           gla_norm, nsa_cmp_pe, nsa_cmp_w, diff_lambda, diff_norm, w_out, ffn_w_in, ffn_w_out):
    depth = ada_w.shape[0]
    bp, lp_len, d = x_prompt.shape
    bs, ls_len, _ = x_sample.shape
    n_pages = page_table.shape[1]
    past_len = n_pages * PAGE_SIZE
    wb = state_nsa_win_kv.shape[2]
    hk = GLA_HEADS * GLA_DK

    n_c = bp + bs
    rows = -(-n_c // SUBLANES) * SUBLANES
    c_all = jnp.concatenate([c_prompt, c_sample, jnp.zeros((rows - n_c, d), F32)], axis=0)
    mod_all = _modulation(c_all, ada_w, ada_b)

    w_in_b, w_out_b = _permute_w_in(w_in), w_out.astype(BF16)
    ffn_w_in_b, ffn_w_out_b = ffn_w_in.astype(BF16), ffn_w_out.astype(BF16)
    past_views = {'cmp': _page_view(cache_nsa_cmp_kv), 'slc': _page_view(cache_nsa_slc_kv),
                  'diff': _page_view(cache_diff_kv), 'win': _page_view(state_nsa_win_kv), 'page_table': page_table}

    xp, xs = x_prompt, x_sample
    outs = [[] for _ in range(10)]
    for l in range(depth):
        cmp_w, cmp_pe = _compress_weights(nsa_cmp_w[l], nsa_cmp_pe[l])
        lp = {'layer': l, 'norm': norm_w[l], 'w_in': w_in_b,
              'gla_gate_w': gla_gate_w[l], 'gla_gate_b': gla_gate_b[l], 'gla_norm': gla_norm[l],
              'nsa_cmp_pe': cmp_pe, 'nsa_cmp_w': cmp_w, 'diff_lambda': diff_lambda[l], 'diff_norm': diff_norm[l],
              'w_out': w_out_b, 'ffn_w_in': ffn_w_in_b, 'ffn_w_out': ffn_w_out_b}
        lam_init = 0.8 - 0.6 * math.exp(-0.3 * l)

        gla0 = jnp.zeros((bp, hk, GLA_DV), F32)
        xp, cmp_p, slc_p, diff_p, win_p, gla_p = _layer(xp, mod_all[l, :bp], 0, None, gla0, lp, lam_init)

        xs, cmp_s, slc_s, diff_s, win_s, gla_s = _layer(xs, mod_all[l, bp:bp + bs], past_len, past_views,
                                                        state_gla[l].reshape(bs, hk, GLA_DV), lp, lam_init)

        kv_shape = lambda b, n: (b, n, 2, NSA_KV_HEADS, HEAD_DIM)
        outs[0].append(cmp_p.reshape(kv_shape(bp, lp_len)))
        outs[1].append(cmp_s.reshape(kv_shape(bs, ls_len)))
        outs[2].append(slc_p.reshape(kv_shape(bp, lp_len)))
        outs[3].append(slc_s.reshape(kv_shape(bs, ls_len)))
        outs[4].append(diff_p.reshape(bp, lp_len, 2, DIFF_HEADS, DIFF_V))
        outs[5].append(diff_s.reshape(bs, ls_len, 2, DIFF_HEADS, DIFF_V))
        win_p5 = win_p.reshape(kv_shape(bp, lp_len))
        outs[6].append(win_p5[:, lp_len - wb:] if lp_len >= wb else
                       jnp.concatenate([jnp.zeros(kv_shape(bp, wb - lp_len), F32), win_p5], axis=1))
        outs[7].append(win_s.reshape(kv_shape(bs, ls_len)))
        outs[8].append(gla_p.reshape(bp, GLA_HEADS, GLA_DK, GLA_DV))
        outs[9].append(gla_s.reshape(bs, GLA_HEADS, GLA_DK, GLA_DV))

    stacked = [jnp.stack(o) for o in outs]
    stacked[7] = jnp.concatenate([state_nsa_win_kv, stacked[7]], axis=2)[:, :, -wb:]
    return (xp, xs) + tuple(stacked)
```
